```python
import jax, jax.numpy as jnp
from jax import lax
import numpy as np

D_MODEL = 2048
BATCH = 4
SEQ = 2048
DEPTH = 1
DEC_BATCH = 32
DEC_SEQ = 1
PAST_LEN = 8192
PAGE_SIZE = 128

D_RNN = D_MODEL
RNN_BLOCKS = 16
RNN_BLOCK_DIM = D_RNN // RNN_BLOCKS
CONV_W = 4
LRU_C = 8.0
N_HEADS = 16
HEAD_DIM = 128
N_KV_HEADS = 4
GROUP = N_HEADS // N_KV_HEADS
KV_DIM = N_KV_HEADS * HEAD_DIM
CMP_STRIDE = 16
CMP_BLK = 2 * CMP_STRIDE
CMP_HIDDEN = 2 * HEAD_DIM
SEL_BLK = 64
N_SEL = 16
WINDOW = 512
WIN_QBLK = 128
SEL_QBLK = 32
ATTN_SCALE = HEAD_DIM ** -0.5
FORCED_SCORE = 1e4
N_GROUPS = 4
EXP_PER_GROUP = 8
N_EXPERTS = N_GROUPS * EXP_PER_GROUP
TOP_K = 2
D_EXPERT = 512
EPS = 1e-6
MASK_VALUE = -1e30
IN_SPLITS = (D_RNN, D_RNN, N_HEADS * HEAD_DIM, 6 * KV_DIM, 3 * N_HEADS, 2 * D_MODEL)
D_IN = D_RNN + D_RNN + N_HEADS * HEAD_DIM + 6 * KV_DIM + 3 * N_HEADS + 2 * D_MODEL

kernel_name = "hawk_nsa_hmoe_step"


def rmsnorm(x, g):
    xf = x.astype(jnp.float32)
    var = jnp.mean(xf * xf, axis=-1, keepdims=True)
    return (xf * lax.rsqrt(var + EPS)).astype(x.dtype) * g


def masked_softmax(s, mask):
    s = jnp.where(mask, s.astype(jnp.float32), MASK_VALUE)
    return jax.nn.softmax(s, axis=-1) * mask


def causal_conv(x, prev, w, b):
    T = x.shape[1]
    xp = jnp.concatenate([prev, x], axis=1)
    y = sum(xp[:, k:k + T] * w[k] for k in range(CONV_W)) + b
    return y, xp[:, -(CONV_W - 1):]


def rg_lru(x, h0, w_a, b_a, w_x, b_x, lam):
    B, T, _ = x.shape
    xb = x.reshape(B, T, RNN_BLOCKS, RNN_BLOCK_DIM)
    r = jax.nn.sigmoid(jnp.einsum('btnd,nde->btne', xb, w_a).reshape(B, T, D_RNN) + b_a)
    i = jax.nn.sigmoid(jnp.einsum('btnd,nde->btne', xb, w_x).reshape(B, T, D_RNN) + b_x)
    log_a = (-LRU_C * r * jax.nn.softplus(-lam)).astype(jnp.float32)
    a = jnp.exp(log_a)
    mult = jnp.sqrt(-jnp.expm1(2.0 * log_a))
    bterm = mult * (i * x).astype(jnp.float32)
    bterm = bterm.at[:, 0].add(a[:, 0] * h0.astype(jnp.float32))

    def combine(e1, e2):
        a1, b1 = e1
        a2, b2 = e2
        return a1 * a2, a2 * b1 + b2

    _, h = lax.associative_scan(combine, (a, bterm), axis=1)
    return h.astype(x.dtype), h[:, -1].astype(x.dtype)


def gather_pages(pool, page_table):
    rows = pool[page_table]
    B, n_pages = page_table.shape
    return rows.reshape(B, n_pages * PAGE_SIZE, N_KV_HEADS, 2, HEAD_DIM)


def compress(rows, pe, w1, b1, w2, b2):
    B, L = rows.shape[:2]
    Lp = -(-L // CMP_STRIDE) * CMP_STRIDE
    rows = jnp.pad(rows, ((0, 0), (0, Lp - L), (0, 0), (0, 0), (0, 0)))
    chunks = rows.reshape(B, Lp // CMP_STRIDE, CMP_STRIDE, N_KV_HEADS, 2, HEAD_DIM)
    w1r = w1.reshape(2, 2, CMP_STRIDE, HEAD_DIM, CMP_HIDDEN)
    per = pe.reshape(2, 2, CMP_STRIDE, HEAD_DIM)
    half = jnp.einsum('bjshkd,kasde->bjhkae', chunks, w1r) + jnp.einsum('kasd,kasde->kae', per, w1r)
    pre = half[:, :-1, :, :, 0] + half[:, 1:, :, :, 1] + b1
    return jnp.einsum('bnhke,ked->bnhkd', jax.nn.gelu(pre), w2) + b2


def cmp_attention(q, qpos, kv_c):
    n = kv_c.shape[1]
    last = jnp.arange(n) * CMP_STRIDE + (CMP_BLK - 1)
    mask = last[None, :] <= qpos[:, None]
    s = jnp.einsum('bqhgd,bnhd->bhgqn', q, kv_c[..., 0, :]) * ATTN_SCALE
    p = masked_softmax(s, mask)
    o = jnp.einsum('bhgqn,bnhd->bqhgd', p.astype(kv_c.dtype), kv_c[..., 1, :])
    return o, p


def selection_importance(p_cmp, n_sel):
    n_cmp = p_cmp.shape[-1]
    cs = jnp.arange(n_cmp) * CMP_STRIDE
    ss = jnp.arange(n_sel) * SEL_BLK
    ov = jnp.minimum(cs[:, None] + CMP_BLK, ss[None, :] + SEL_BLK) - jnp.maximum(cs[:, None], ss[None, :])
    m = jnp.clip(ov, 0, CMP_BLK).astype(jnp.float32) / CMP_BLK
    return jnp.einsum('bhgqn,nj->bhqj', p_cmp, m)


def select_blocks(imp, qpos, n_sel):
    cur = qpos // SEL_BLK
    j = jnp.arange(n_sel)
    valid = j[None, :] <= cur[:, None]
    forced = (j[None, :] == 0) | (j[None, :] >= cur[:, None] - 1)
    score = jnp.where(forced, FORCED_SCORE, imp)
    score = jnp.where(valid, score, -1.0)
    _, idx = lax.top_k(score, min(N_SEL, n_sel))
    sel_valid = jnp.take_along_axis(jnp.broadcast_to(valid, score.shape), idx, axis=-1)
    return idx, sel_valid


def sel_attention(q, qpos, idx, sel_valid, rows_at):
    B, Tq = q.shape[:2]
    k = idx.shape[-1]
    pos = idx[..., None] * SEL_BLK + jnp.arange(SEL_BLK)
    kv = rows_at(pos.reshape(B, N_KV_HEADS, Tq * k * SEL_BLK))
    kv = kv.reshape(B, N_KV_HEADS, Tq, k * SEL_BLK, 2, HEAD_DIM)
    mask = (sel_valid[..., None] & (pos <= qpos[:, None, None])).reshape(B, N_KV_HEADS, Tq, k * SEL_BLK)
    s = jnp.einsum('bqhgd,bhqnd->bhgqn', q, kv[..., 0, :]) * ATTN_SCALE
    p = masked_softmax(s, mask[:, :, None])
    return jnp.einsum('bhgqn,bhqnd->bqhgd', p.astype(kv.dtype), kv[..., 1, :])


def sel_attention_blocked(q, qpos, idx, sel_valid, rows_at):
    B, Tq = q.shape[:2]
    if Tq <= SEL_QBLK or Tq % SEL_QBLK:
        return sel_attention(q, qpos, idx, sel_valid, rows_at)
    nb = Tq // SEL_QBLK
    k = idx.shape[-1]
    qb = q.reshape(B, nb, SEL_QBLK, N_KV_HEADS, GROUP, HEAD_DIM).transpose(1, 0, 2, 3, 4, 5)
    idxb = idx.reshape(B, N_KV_HEADS, nb, SEL_QBLK, k).transpose(2, 0, 1, 3, 4)
    valb = sel_valid.reshape(B, N_KV_HEADS, nb, SEL_QBLK, k).transpose(2, 0, 1, 3, 4)
    qposb = qpos.reshape(nb, SEL_QBLK)
    out = lax.map(lambda a: sel_attention(a[0], a[1], a[2], a[3], rows_at), (qb, qposb, idxb, valb))
    return out.transpose(1, 0, 2, 3, 4, 5).reshape(B, Tq, N_KV_HEADS, GROUP, HEAD_DIM)


def band_attend(q, qpos, kv, kpos):
    d = qpos[:, :, None] - kpos[:, None, :]
    mask = (d >= 0) & (d < WINDOW) & (kpos[:, None, :] >= 0)
    s = jnp.einsum('bcqhgd,bckhd->bhgcqk', q, kv[..., 0, :]) * ATTN_SCALE
    p = masked_softmax(s, mask)
    return jnp.einsum('bhgcqk,bckhd->bcqhgd', p.astype(kv.dtype), kv[..., 1, :])


def window_prompt(q, kv):
    B, T = q.shape[:2]
    nb = T // WIN_QBLK
    nw = WINDOW // WIN_QBLK
    kvp = jnp.pad(kv, ((0, 0), (WINDOW, 0), (0, 0), (0, 0), (0, 0)))
    kvb = kvp.reshape(B, nb + nw, WIN_QBLK, N_KV_HEADS, 2, HEAD_DIM)
    band = jnp.concatenate([kvb[:, o:o + nb] for o in range(nw + 1)], axis=2)
    qb = q.reshape(B, nb, WIN_QBLK, N_KV_HEADS, GROUP, HEAD_DIM)
    base = jnp.arange(nb) * WIN_QBLK
    qpos = base[:, None] + jnp.arange(WIN_QBLK)
    kpos = base[:, None] - WINDOW + jnp.arange((nw + 1) * WIN_QBLK)
    return band_attend(qb, qpos, band, kpos).reshape(B, T, N_KV_HEADS, GROUP, HEAD_DIM)


def hier_moe(x, w_grp, b_grp, w_exp, b_exp, w_gate, w_up, w_down):
    N = x.shape[0]
    lg = (x @ w_grp + b_grp).astype(jnp.float32)
    g = jnp.argmax(lg, axis=-1)
    pg = jnp.take_along_axis(jax.nn.softmax(lg, axis=-1), g[:, None], axis=-1)
    le = (x @ w_exp + b_exp).astype(jnp.float32).reshape(N, N_GROUPS, EXP_PER_GROUP)
    pe = jax.nn.softmax(le[jnp.arange(N), g], axis=-1)
    top_p, top_i = lax.top_k(pe, TOP_K)
    top_p = top_p / jnp.sum(top_p, axis=-1, keepdims=True)
    eid = g[:, None] * EXP_PER_GROUP + top_i
    comb = jnp.zeros((N, N_EXPERTS), jnp.float32).at[jnp.arange(N)[:, None], eid].add(pg * top_p)
    comb = comb.astype(x.dtype)

    def expert_step(acc, e):
        wg, wu, wd, c = e
        h = jax.nn.silu(x @ wg) * (x @ wu)
        return acc + c[:, None] * (h @ wd), None

    y, _ = lax.scan(expert_step, jnp.zeros_like(x), (w_gate, w_up, w_down, comb.T))
    return y


def trunk_layer(x, w, past):
    B, T, _ = x.shape
    prompt = past is None
    pos0 = 0 if prompt else PAST_LEN
    qpos = pos0 + jnp.arange(T)
    xn = rmsnorm(x, w['g_mix'])
    z = xn @ w['w_in']
    u_rnn, u_gate, u_q, u_kv, u_bg, u_mg = jnp.split(z, np.cumsum(IN_SPLITS)[:-1].tolist(), axis=-1)

    conv_prev = jnp.zeros((B, CONV_W - 1, D_RNN), x.dtype) if prompt else past['conv']
    h_prev = jnp.zeros((B, D_RNN), x.dtype) if prompt else past['h']
    u_c, conv_new = causal_conv(u_rnn, conv_prev, w['conv_w'], w['conv_b'])
    h_seq, h_new = rg_lru(u_c, h_prev, w['lru_w_a'], w['lru_b_a'], w['lru_w_x'], w['lru_b_x'], w['lru_lambda'])
    o_rnn = h_seq * jax.nn.gelu(u_gate)

    q = u_q.reshape(B, T, N_KV_HEADS, GROUP, HEAD_DIM)
    kv = u_kv.reshape(B, T, 3, N_KV_HEADS, 2, HEAD_DIM)
    kv_cmp, kv_sel, kv_win = kv[:, :, 0], kv[:, :, 1], kv[:, :, 2]
    bidx = jnp.arange(B)[:, None, None]
    hidx = jnp.arange(N_KV_HEADS)[None, :, None]
    if prompt:
        cmp_rows = kv_cmp

        def rows_at(pos):
            return kv_sel[bidx, jnp.minimum(pos, T - 1), hidx]

        o_win = window_prompt(q, kv_win)
        win_new = kv_win[:, T - min(WINDOW, T):]
    else:
        pt = past['page_table']
        pool_sel = past['sel']
        cmp_rows = jnp.concatenate([gather_pages(past['cmp'], pt), kv_cmp], axis=1)

        def rows_at(pos):
            p_old = jnp.minimum(pos, PAST_LEN - 1)
            old = pool_sel[pt[bidx, p_old // PAGE_SIZE], p_old % PAGE_SIZE, hidx]
            new = kv_sel[bidx, jnp.clip(pos - PAST_LEN, 0, T - 1), hidx]
            return jnp.where((pos >= PAST_LEN)[..., None, None], new, old)

        w_buf = past['win'].shape[1]
        kvw = jnp.concatenate([past['win'], kv_win], axis=1)
        kpos = PAST_LEN - w_buf + jnp.arange(w_buf + T)
        o_win = band_attend(q[:, None], qpos[None], kvw[:, None], kpos[None])[:, 0]
        win_new = kvw[:, T:]

    L = cmp_rows.shape[1]
    kv_c = compress(cmp_rows, w['cmp_pe'], w['cmp_w1'], w['cmp_b1'], w['cmp_w2'], w['cmp_b2'])
    o_cmp, p_cmp = cmp_attention(q, qpos, kv_c)
    n_sel = -(-L // SEL_BLK)
    imp = selection_importance(p_cmp, n_sel)
    idx, sel_valid = select_blocks(imp, qpos, n_sel)
    o_sel = sel_attention_blocked(q, qpos, idx, sel_valid, rows_at)
    bg = jax.nn.sigmoid(u_bg).reshape(B, T, 3, N_KV_HEADS, GROUP)[..., None]
    o_nsa = (bg[:, :, 0] * o_cmp + bg[:, :, 1] * o_sel + bg[:, :, 2] * o_win).reshape(B, T, N_HEADS * HEAD_DIM)

    mg = jax.nn.sigmoid(u_mg).reshape(B, T, 2, D_MODEL)
    x = x + (mg[:, :, 0] * o_rnn + mg[:, :, 1] * o_nsa) @ w['w_out']

    xn2 = rmsnorm(x, w['g_ffn']).reshape(B * T, D_MODEL)
    x = x + hier_moe(xn2, w['moe_w_group'], w['moe_b_group'], w['moe_w_expert'], w['moe_b_expert'],
                     w['moe_w_gate'], w['moe_w_up'], w['moe_w_down']).reshape(B, T, D_MODEL)
    return x, kv_cmp, kv_sel, win_new, conv_new, h_new


def setup_inputs(seed: int = 0) -> dict:
    key = jax.random.key(seed)
    ks = jax.random.split(key, 32)

    def nrm(k, shape, scale):
        return jax.random.normal(k, shape, jnp.float32) * scale

    n_pages = PAST_LEN // PAGE_SIZE
    n_used = DEC_BATCH * n_pages
    n_pool = n_used + n_used // 4
    w_buf = min(WINDOW, PAST_LEN)
    page_table = jax.random.permutation(ks[0], n_pool)[:n_used].reshape(DEC_BATCH, n_pages).astype(jnp.int32)
    a0 = jax.random.uniform(ks[1], (D_RNN,), jnp.float32, 0.9, 0.999)
    return {
        'x_prompt': nrm(ks[2], (BATCH, SEQ, D_MODEL), 1.0),
        'x_sample': nrm(ks[3], (DEC_BATCH, DEC_SEQ, D_MODEL), 1.0),
        'cache_cmp_kv': nrm(ks[4], (n_pool, PAGE_SIZE, N_KV_HEADS, 2, HEAD_DIM), 1.0),
        'cache_sel_kv': nrm(ks[5], (n_pool, PAGE_SIZE, N_KV_HEADS, 2, HEAD_DIM), 1.0),
        'cache_win_kv': nrm(ks[6], (DEC_BATCH, w_buf, N_KV_HEADS, 2, HEAD_DIM), 1.0),
        'state_conv': nrm(ks[7], (DEC_BATCH, CONV_W - 1, D_RNN), 1.0),
        'state_rglru': nrm(ks[8], (DEC_BATCH, D_RNN), 0.5),
        'page_table': page_table,
        'g_mix': 1.0 + nrm(ks[9], (D_MODEL,), 0.01),
        'w_in': nrm(ks[10], (D_MODEL, D_IN), D_MODEL ** -0.5),
        'conv_w': nrm(ks[11], (CONV_W, D_RNN), CONV_W ** -0.5),
        'conv_b': nrm(ks[12], (D_RNN,), 0.01),
        'lru_w_a': nrm(ks[13], (RNN_BLOCKS, RNN_BLOCK_DIM, RNN_BLOCK_DIM), RNN_BLOCK_DIM ** -0.5),
        'lru_b_a': nrm(ks[14], (D_RNN,), 0.01),
        'lru_w_x': nrm(ks[15], (RNN_BLOCKS, RNN_BLOCK_DIM, RNN_BLOCK_DIM), RNN_BLOCK_DIM ** -0.5),
        'lru_b_x': nrm(ks[16], (D_RNN,), 0.01),
        'lru_lambda': jnp.log(a0) - jnp.log1p(-a0),
        'cmp_pe': nrm(ks[17], (2, CMP_BLK, HEAD_DIM), 0.1),
        'cmp_w1': nrm(ks[18], (2, CMP_BLK * HEAD_DIM, CMP_HIDDEN), (CMP_BLK * HEAD_DIM) ** -0.5),
        'cmp_b1': nrm(ks[19], (2, CMP_HIDDEN), 0.01),
        'cmp_w2': nrm(ks[20], (2, CMP_HIDDEN, HEAD_DIM), CMP_HIDDEN ** -0.5),
        'cmp_b2': nrm(ks[21], (2, HEAD_DIM), 0.01),
        'w_out': nrm(ks[22], (D_MODEL, D_MODEL), D_MODEL ** -0.5),
        'g_ffn': 1.0 + nrm(ks[23], (D_MODEL,), 0.01),
        'moe_w_group': nrm(ks[24], (D_MODEL, N_GROUPS), D_MODEL ** -0.5),
        'moe_b_group': nrm(ks[25], (N_GROUPS,), 0.01),
        'moe_w_expert': nrm(ks[26], (D_MODEL, N_EXPERTS), D_MODEL ** -0.5),
        'moe_b_expert': nrm(ks[27], (N_EXPERTS,), 0.01),
        'moe_w_gate': nrm(ks[28], (N_EXPERTS, D_MODEL, D_EXPERT), D_MODEL ** -0.5),
        'moe_w_up': nrm(ks[29], (N_EXPERTS, D_MODEL, D_EXPERT), D_MODEL ** -0.5),
        'moe_w_down': nrm(ks[30], (N_EXPERTS, D_EXPERT, D_MODEL), D_EXPERT ** -0.5),
        'g_final': 1.0 + nrm(ks[31], (D_MODEL,), 0.01),
    }


def reference(x_prompt, x_sample, cache_cmp_kv, cache_sel_kv, cache_win_kv, state_conv, state_rglru,
              page_table, g_mix, w_in, conv_w, conv_b, lru_w_a, lru_b_a, lru_w_x, lru_b_x, lru_lambda,
              cmp_pe, cmp_w1, cmp_b1, cmp_w2, cmp_b2, w_out, g_ffn, moe_w_group, moe_b_group,
              moe_w_expert, moe_b_expert, moe_w_gate, moe_w_up, moe_w_down, g_final):
    w = dict(g_mix=g_mix, w_in=w_in, conv_w=conv_w, conv_b=conv_b, lru_w_a=lru_w_a, lru_b_a=lru_b_a,
             lru_w_x=lru_w_x, lru_b_x=lru_b_x, lru_lambda=lru_lambda, cmp_pe=cmp_pe, cmp_w1=cmp_w1,
             cmp_b1=cmp_b1, cmp_w2=cmp_w2, cmp_b2=cmp_b2, w_out=w_out, g_ffn=g_ffn,
             moe_w_group=moe_w_group, moe_b_group=moe_b_group, moe_w_expert=moe_w_expert,
             moe_b_expert=moe_b_expert, moe_w_gate=moe_w_gate, moe_w_up=moe_w_up, moe_w_down=moe_w_down)
    past = dict(cmp=cache_cmp_kv, sel=cache_sel_kv, win=cache_win_kv, conv=state_conv, h=state_rglru,
                page_table=page_table)
    xp = x_prompt
    xs = x_sample
    for _ in range(DEPTH):
        xp, p_cmp, p_sel, p_win, p_conv, p_h = trunk_layer(xp, w, None)
        xs, s_cmp, s_sel, s_win, s_conv, s_h = trunk_layer(xs, w, past)
    y_prompt = rmsnorm(xp, g_final)
    y_sample = rmsnorm(xs, g_final)
    return (y_prompt, y_sample, p_cmp, s_cmp, p_sel, s_sel, p_win, s_win, p_conv, s_conv, p_h, s_h)
```

```python
import functools

import jax
import jax.numpy as jnp
import numpy as np
from jax import lax
from jax.experimental import pallas as pl
from jax.experimental.pallas import tpu as pltpu

F32 = jnp.float32
BF16 = jnp.bfloat16
I32 = jnp.int32

D_MODEL = 2048
D_RNN = 2048
RNN_BLOCKS = 16
RNN_BLOCK_DIM = 128
CONV_W = 4
LRU_C = 8.0
N_HEADS = 16
HEAD_DIM = 128
N_KV_HEADS = 4
GROUP = 4
KV_DIM = N_KV_HEADS * HEAD_DIM
HK = 2 * N_KV_HEADS
CMP_STRIDE = 16
CMP_BLK = 32
CMP_HIDDEN = 256
SEL_BLK = 64
SEL_SHIFT = 6
N_SEL = 16
WINDOW = 512
PAGE_SIZE = 128
ATTN_SCALE = HEAD_DIM ** -0.5
FORCED_SCORE = 1e4
N_GROUPS = 4
EXP_PER_GROUP = 8
N_EXPERTS = 32
D_EXPERT = 512
EPS = 1e-6
MASK_VALUE = -1e30
LANES = 128
MIB = 1024 * 1024

_NT = (((1,), (1,)), ((), ()))


def _params(sem, vmem_mib):
    return pltpu.CompilerParams(dimension_semantics=sem, vmem_limit_bytes=vmem_mib * MIB)


def _msoftmax(s, mask):
    sm = jnp.where(mask, s, MASK_VALUE)
    m = jnp.max(sm, axis=-1, keepdims=True)
    e = jnp.where(mask, jnp.exp(sm - m), 0.0)
    den = jnp.sum(e, axis=-1, keepdims=True)
    return e * jnp.where(den > 0.0, 1.0 / den, 0.0)


def _gelu_tanh(x):
    return 0.5 * x * (1.0 + jnp.tanh(0.7978845608028654 * (x + 0.044715 * (x * x * x))))


def _rmsnorm_body(x_ref, g_ref, o_ref):
    x = x_ref[...]
    var = jnp.mean(x * x, axis=-1, keepdims=True)
    o_ref[...] = ((x * lax.rsqrt(var + EPS)) * g_ref[...]).astype(o_ref.dtype)


def rmsnorm(x, g, tm, out_dtype):
    m, d = x.shape
    return pl.pallas_call(
        _rmsnorm_body,
        grid=(m // tm,),
        in_specs=[pl.BlockSpec((tm, d), lambda i: (i, 0)), pl.BlockSpec((1, d), lambda i: (0, 0))],
        out_specs=pl.BlockSpec((tm, d), lambda i: (i, 0)),
        out_shape=jax.ShapeDtypeStruct((m, d), out_dtype),
        compiler_params=_params(("arbitrary",), 40),
        name="rmsnorm",
    )(x, g.reshape(1, d))


def _mm_body(x_ref, w_ref, o_ref, wbf_ref):
    @pl.when(pl.program_id(1) == 0)
    def _():
        wbf_ref[...] = w_ref[...].astype(BF16)

    o_ref[...] = jnp.dot(x_ref[...], wbf_ref[...], preferred_element_type=F32).astype(o_ref.dtype)


def _mm_res_body(x_ref, w_ref, r_ref, o_ref, wbf_ref):
    @pl.when(pl.program_id(1) == 0)
    def _():
        wbf_ref[...] = w_ref[...].astype(BF16)

    o_ref[...] = r_ref[...] + jnp.dot(x_ref[...], wbf_ref[...], preferred_element_type=F32)


def matmul_cols(x, w, col0, ncols, tm, tn, out_dtype, residual=None, name="matmul"):
    m, k = x.shape
    cb0 = col0 // tn
    assert col0 % tn == 0 and ncols % tn == 0 and m % tm == 0
    in_specs = [pl.BlockSpec((tm, k), lambda j, i: (i, 0)), pl.BlockSpec((k, tn), lambda j, i: (0, cb0 + j))]
    args = [x, w]
    body = _mm_body
    if residual is not None:
        in_specs.append(pl.BlockSpec((tm, tn), lambda j, i: (i, j)))
        args.append(residual)
        body = _mm_res_body
    return pl.pallas_call(
        body,
        grid=(ncols // tn, m // tm),
        in_specs=in_specs,
        out_specs=pl.BlockSpec((tm, tn), lambda j, i: (i, j)),
        out_shape=jax.ShapeDtypeStruct((m, ncols), out_dtype),
        scratch_shapes=[pltpu.VMEM((k, tn), BF16)],
        compiler_params=_params(("arbitrary", "arbitrary"), 48),
        name=name,
    )(*args)


def _merge_body(mg_ref, rnn_ref, nsa_ref, o_ref):
    d = rnn_ref.shape[-1]
    ga = jax.nn.sigmoid(mg_ref[:, :d])
    gb = jax.nn.sigmoid(mg_ref[:, d:])
    o_ref[...] = (ga * rnn_ref[...] + gb * nsa_ref[...]).astype(o_ref.dtype)


def merge(u_mg, o_rnn, o_nsa, tm):
    m, d = o_rnn.shape
    return pl.pallas_call(
        _merge_body,
        grid=(m // tm,),
        in_specs=[pl.BlockSpec((tm, 2 * d), lambda i: (i, 0)), pl.BlockSpec((tm, d), lambda i: (i, 0)),
                  pl.BlockSpec((tm, d), lambda i: (i, 0))],
        out_specs=pl.BlockSpec((tm, d), lambda i: (i, 0)),
        out_shape=jax.ShapeDtypeStruct((m, d), BF16),
        compiler_params=_params(("arbitrary",), 48),
        name="merge",
    )(u_mg, o_rnn, o_nsa)


def _lru_gates(uc, wa_ref, wx_ref, ba_ref, bx_ref, lam_ref):
    ucb = uc.astype(BF16)
    ra, ri = [], []
    for n in range(RNN_BLOCKS):
        blk = ucb[:, n * RNN_BLOCK_DIM:(n + 1) * RNN_BLOCK_DIM]
        ra.append(jnp.dot(blk, wa_ref[n], preferred_element_type=F32))
        ri.append(jnp.dot(blk, wx_ref[n], preferred_element_type=F32))
    r = jax.nn.sigmoid(jnp.concatenate(ra, axis=-1) + ba_ref[...])
    i = jax.nn.sigmoid(jnp.concatenate(ri, axis=-1) + bx_ref[...])
    z = -lam_ref[...]
    softplus = jnp.maximum(z, 0.0) + jnp.log1p(jnp.exp(-jnp.abs(z)))
    log_a = (-LRU_C * r) * softplus
    a = jnp.exp(log_a)
    mult = jnp.sqrt(-jnp.tanh(log_a) * (a * a + 1.0))
    return a, mult * (i * uc)


def _lru_prompt_body(u_ref, gate_ref, cw_ref, cb_ref, wa_ref, wx_ref, ba_ref, bx_ref, lam_ref,
                     o_ref, conv_ref, hlast_ref, xbuf, hcar, acum, bcum):
    t = pl.program_id(1)
    nt = pl.num_programs(1)
    tt, d = u_ref.shape[1], u_ref.shape[2]

    @pl.when(t == 0)
    def _():
        xbuf[0:8, :] = jnp.zeros((8, d), F32)
        hcar[...] = jnp.zeros((8, d), F32)

    x = u_ref[0]
    xbuf[8:8 + tt, :] = x
    uc = (cw_ref[0:1, :] * xbuf[5:5 + tt, :] + cw_ref[1:2, :] * xbuf[6:6 + tt, :]
          + cw_ref[2:3, :] * xbuf[7:7 + tt, :] + cw_ref[3:4, :] * x) + cb_ref[...]
    xbuf[0:8, :] = xbuf[tt:tt + 8, :]

    a, b = _lru_gates(uc, wa_ref, wx_ref, ba_ref, bx_ref, lam_ref)
    g = tt // 8
    a3 = a.reshape(g, 8, d)
    b3 = b.reshape(g, 8, d)
    row = lax.broadcasted_iota(I32, (g, 8, d), 1)
    for s in (1, 2, 4):
        a_sh = pltpu.roll(a3, s, axis=1)
        b_sh = pltpu.roll(b3, s, axis=1)
        keep = row >= s
        b3 = jnp.where(keep, a3 * b_sh + b3, b3)
        a3 = jnp.where(keep, a3 * a_sh, a3)
    acum[...] = a3
    bcum[...] = b3

    def body(gi, h):
        hg = bcum[gi] + acum[gi] * h
        bcum[gi] = hg
        return jnp.broadcast_to(hg[7:8, :], (8, d))

    hfin = lax.fori_loop(0, g, body, hcar[...])
    hcar[...] = hfin
    o_ref[0] = bcum[...].reshape(tt, d) * _gelu_tanh(gate_ref[0])

    @pl.when(t == nt - 1)
    def _():
        conv_ref[0] = x[tt - (CONV_W - 1):tt, :]
        hlast_ref[0] = hfin[0:1, :]


def lru_prompt(u_rnn, u_gate, conv_w, conv_b, wa, wx, ba, bx, lam, tt):
    b, t, d = u_rnn.shape
    vec = lambda: pl.BlockSpec((1, d), lambda i, j: (0, 0))
    wspec = lambda: pl.BlockSpec((RNN_BLOCKS, RNN_BLOCK_DIM, RNN_BLOCK_DIM), lambda i, j: (0, 0, 0))
    return pl.pallas_call(
        _lru_prompt_body,
        grid=(b, t // tt),
        in_specs=[pl.BlockSpec((1, tt, d), lambda i, j: (i, j, 0)), pl.BlockSpec((1, tt, d), lambda i, j: (i, j, 0)),
                  pl.BlockSpec((CONV_W, d), lambda i, j: (0, 0)), vec(), wspec(), wspec(), vec(), vec(), vec()],
        out_specs=[pl.BlockSpec((1, tt, d), lambda i, j: (i, j, 0)),
                   pl.BlockSpec((1, CONV_W - 1, d), lambda i, j: (i, 0, 0)),
                   pl.BlockSpec((1, 1, d), lambda i, j: (i, 0, 0))],
        out_shape=[jax.ShapeDtypeStruct((b, t, d), F32), jax.ShapeDtypeStruct((b, CONV_W - 1, d), F32),
                   jax.ShapeDtypeStruct((b, 1, d), F32)],
        scratch_shapes=[pltpu.VMEM((tt + 8, d), F32), pltpu.VMEM((8, d), F32),
                        pltpu.VMEM((tt // 8, 8, d), F32), pltpu.VMEM((tt // 8, 8, d), F32)],
        compiler_params=_params(("arbitrary", "arbitrary"), 56),
        name="lru_prompt",
    )(u_rnn, u_gate, conv_w, conv_b.reshape(1, d), wa, wx, ba.reshape(1, d), bx.reshape(1, d), lam.reshape(1, d))


def _lru_sample_body(u_ref, gate_ref, cp_ref, h0_ref, cw_ref, cb_ref, wa_ref, wx_ref, ba_ref, bx_ref, lam_ref,
                     o_ref, cn_ref, h_ref):
    x = u_ref[...]
    uc = (cw_ref[0:1, :] * cp_ref[0] + cw_ref[1:2, :] * cp_ref[1] + cw_ref[2:3, :] * cp_ref[2]
          + cw_ref[3:4, :] * x) + cb_ref[...]
    a, b = _lru_gates(uc, wa_ref, wx_ref, ba_ref, bx_ref, lam_ref)
    h = a * h0_ref[...] + b
    o_ref[...] = h * _gelu_tanh(gate_ref[...])
    h_ref[...] = h
    cn_ref[0] = cp_ref[1]
    cn_ref[1] = cp_ref[2]
    cn_ref[2] = x


def lru_sample(u_rnn, u_gate, conv_prev_t, h0, conv_w, conv_b, wa, wx, ba, bx, lam):
    n, d = u_rnn.shape
    return pl.pallas_call(
        _lru_sample_body,
        out_shape=[jax.ShapeDtypeStruct((n, d), F32), jax.ShapeDtypeStruct((CONV_W - 1, n, d), F32),
                   jax.ShapeDtypeStruct((n, d), F32)],
        compiler_params=pltpu.CompilerParams(vmem_limit_bytes=40 * MIB),
        name="lru_sample",
    )(u_rnn, u_gate, conv_prev_t, h0, conv_w, conv_b.reshape(1, d), wa, wx, ba.reshape(1, d), bx.reshape(1, d),
      lam.reshape(1, d))


COMPRESS_PAGES = 16
CHUNKS_PER_PAGE = PAGE_SIZE // CMP_STRIDE


def _compress_body(pt_ref, *refs, pp, n_steps):
    page_refs = refs[:pp]
    nxt_ref, extra_ref, wc_ref, pe_ref, b1_ref, w2_ref, b2_ref, o_ref, res_scr = refs[pp:]
    del pt_ref
    is_last = pl.program_id(1) == n_steps - 1
    m_rows = pp * CHUNKS_PER_PAGE * N_KV_HEADS
    lo4 = lax.broadcasted_iota(I32, (4, 8, 8, LANES), 2) < 4
    lo3 = lax.broadcasted_iota(I32, (8, 8, LANES), 1) < 4

    for k in range(2):
        lhs_sp = [[] for _ in range(8)]
        for pr in page_refs:
            xk = pr[pl.ds(k, PAGE_SIZE * N_KV_HEADS, stride=2), :]
            x5 = xk.reshape(4, 2, 8, 8, LANES)
            a0 = x5[:, 0]
            a1 = x5[:, 1]
            be = jnp.where(lo4, a0, pltpu.roll(a1, 4, axis=2))
            bo = jnp.where(lo4, pltpu.roll(a0, 4, axis=2), a1)
            for sp in range(8):
                lhs_sp[sp].append(jnp.concatenate([be[:, sp], bo[:, sp]], axis=-1).reshape(32, 2 * LANES))
        acc = jnp.zeros((m_rows, 2 * CMP_HIDDEN), F32)
        for sp in range(8):
            lhs = jnp.concatenate(lhs_sp[sp], axis=0).astype(BF16)
            acc = acc + jnp.dot(lhs, wc_ref[k, sp * 256:(sp + 1) * 256, :], preferred_element_type=F32)

        pet = jnp.dot(pe_ref[k].astype(BF16), wc_ref[k], preferred_element_type=F32)
        peb0 = pet[0:1, :CMP_HIDDEN]
        peb1 = pet[1:2, CMP_HIDDEN:]
        h0 = acc[:, :CMP_HIDDEN] + peb0
        h1 = acc[:, CMP_HIDDEN:] + peb1

        xn = jnp.where(is_last, extra_ref[pl.ds(k, CMP_STRIDE * N_KV_HEADS, stride=2), :],
                       nxt_ref[pl.ds(k, CMP_STRIDE * N_KV_HEADS, stride=2), :])
        xn3 = xn.reshape(8, 8, LANES)
        ln = jnp.concatenate([jnp.where(lo3, xn3, 0.0), jnp.where(lo3, 0.0, xn3)], axis=-1).astype(BF16)
        nacc = jnp.zeros((8, CMP_HIDDEN), F32)
        for sp in range(8):
            nacc = nacc + jnp.dot(ln[sp], wc_ref[k, sp * 256:(sp + 1) * 256, CMP_HIDDEN:],
                                  preferred_element_type=F32)
        n8 = nacc + pltpu.roll(nacc, 4, axis=0) + peb1
        h1e = jnp.concatenate([h1, n8], axis=0)
        h1s = pltpu.roll(h1e, m_rows + 8 - N_KV_HEADS, axis=0)[:m_rows]
        pre = h0 + h1s + b1_ref[k]
        out_k = jnp.dot(_gelu_tanh(pre).astype(BF16), w2_ref[k], preferred_element_type=F32) + b2_ref[k]
        res_scr[...] = out_k
        for h in range(N_KV_HEADS):
            o_ref[0, 2 * h + k] = res_scr[pl.ds(h, m_rows // N_KV_HEADS, stride=N_KV_HEADS), :]


def compress(pool2d, page_table, extra2d, wc, pe8, b1, w2, b2):
    nseq, npages = page_table.shape
    pp = COMPRESS_PAGES
    n_steps = npages // pp
    assert npages % pp == 0
    rows_page = PAGE_SIZE * HK
    rows_chunk = CMP_STRIDE * HK
    blocks_step = pp * CHUNKS_PER_PAGE

    def page_spec(i):
        return pl.BlockSpec((rows_page, LANES), lambda b, s, pt: (pt[b, s * pp + i], 0))

    def nxt_map(b, s, pt):
        return (pt[b, jnp.minimum((s + 1) * pp, npages - 1)] * CHUNKS_PER_PAGE, 0)

    const3 = lambda shape: pl.BlockSpec(shape, lambda b, s, pt: (0, 0, 0))
    in_specs = [page_spec(i) for i in range(pp)] + [
        pl.BlockSpec((rows_chunk, LANES), nxt_map),
        pl.BlockSpec((rows_chunk, LANES), lambda b, s, pt: (b, 0)),
        const3((2, CMP_STRIDE * HEAD_DIM, 2 * CMP_HIDDEN)),
        const3((2, 8, CMP_STRIDE * HEAD_DIM)),
        const3((2, 1, CMP_HIDDEN)),
        const3((2, CMP_HIDDEN, HEAD_DIM)),
        const3((2, 1, HEAD_DIM)),
    ]
    grid_spec = pltpu.PrefetchScalarGridSpec(
        num_scalar_prefetch=1,
        grid=(nseq, n_steps),
        in_specs=in_specs,
        out_specs=pl.BlockSpec((1, HK, blocks_step, HEAD_DIM), lambda b, s, pt: (b, 0, s, 0)),
        scratch_shapes=[pltpu.VMEM((blocks_step * N_KV_HEADS, HEAD_DIM), F32)],
    )
    return pl.pallas_call(
        functools.partial(_compress_body, pp=pp, n_steps=n_steps),
        grid_spec=grid_spec,
        out_shape=jax.ShapeDtypeStruct((nseq, HK, npages * CHUNKS_PER_PAGE, HEAD_DIM), F32),
        compiler_params=_params(("arbitrary", "arbitrary"), 56),
        name="compress",
    )(page_table, *([pool2d] * pp), pool2d, extra2d, wc, pe8, b1, w2, b2)


def _overlap_matrix(n_cmp, n_cmp_pad, n_sel, n_sel_pad):
    cs = np.arange(n_cmp_pad)[:, None] * CMP_STRIDE
    ss = np.arange(n_sel_pad)[None, :] * SEL_BLK
    ov = np.minimum(cs + CMP_BLK, ss + SEL_BLK) - np.maximum(cs, ss)
    m = np.clip(ov, 0, CMP_BLK).astype(np.float32) / CMP_BLK
    m[n_cmp:, :] = 0.0
    m[:, n_sel:] = 0.0
    return m


QBLK = 128
SEL_KCHUNK = 512


def _attn_prompt_body(q_ref, ksel_ref, kwin_ref, kvc_ref, bg_ref, e_ref, mt_ref, o_ref,
                      ksel_bf, vsel_bf, kwin_bf, vwin_bf, *, n_cmp, n_sel, seq):
    qi = pl.program_id(2)

    @pl.when(qi == 0)
    def _():
        ksel_bf[...] = ksel_ref[0, :, :HEAD_DIM].astype(BF16)
        vsel_bf[...] = ksel_ref[0, :, HEAD_DIM:].astype(BF16)
        kwin_bf[...] = kwin_ref[0, :, :HEAD_DIM].astype(BF16)
        vwin_bf[...] = kwin_ref[0, :, HEAD_DIM:].astype(BF16)

    q0 = qi * QBLK
    q = q_ref[0]
    q4 = jnp.concatenate([q[:, g * HEAD_DIM:(g + 1) * HEAD_DIM] for g in range(GROUP)], axis=0)
    rows = GROUP * QBLK

    def rpos(ncols):
        return q0 + (lax.broadcasted_iota(I32, (rows, ncols), 0) & (QBLK - 1))

    wkeys = WINDOW + QBLK
    ks = pl.multiple_of(jnp.maximum(q0 - WINDOW, 0), QBLK)
    kw = kwin_bf[pl.ds(ks, wkeys), :]
    vw = vwin_bf[pl.ds(ks, wkeys), :]
    s_w = lax.dot_general(q4, kw, _NT, preferred_element_type=F32) * ATTN_SCALE
    dist = rpos(wkeys) - (ks + lax.broadcasted_iota(I32, (rows, wkeys), 1))
    p_w = _msoftmax(s_w, (dist >= 0) & (dist < WINDOW))
    o_w = jnp.dot(p_w.astype(BF16), vw, preferred_element_type=F32)

    nb = kvc_ref.shape[2]
    kc = kvc_ref[0, 0].astype(BF16)
    vc = kvc_ref[0, 1].astype(BF16)
    s_c = lax.dot_general(q4, kc, _NT, preferred_element_type=F32) * ATTN_SCALE
    ncol = lax.broadcasted_iota(I32, (rows, nb), 1)
    mask_c = (ncol * CMP_STRIDE + (CMP_BLK - 1) <= rpos(nb)) & (ncol < n_cmp)
    p_c = _msoftmax(s_c, mask_c)
    o_c = jnp.dot(p_c.astype(BF16), vc, preferred_element_type=F32)

    psum = p_c[0:QBLK] + p_c[QBLK:2 * QBLK] + p_c[2 * QBLK:3 * QBLK] + p_c[3 * QBLK:4 * QBLK]
    nsp = mt_ref.shape[0]
    imp_t = lax.dot_general(mt_ref[...], psum.astype(BF16), _NT, preferred_element_type=F32)
    jt = lax.broadcasted_iota(I32, (nsp, QBLK), 0)
    cur = lax.shift_right_logical(q0 + lax.broadcasted_iota(I32, (nsp, QBLK), 1), SEL_SHIFT)
    valid = (jt <= cur) & (jt < n_sel)
    forced = (jt == 0) | (jt >= cur - 1)
    score = jnp.where(forced, FORCED_SCORE, imp_t)
    score = jnp.where(valid, score, -1.0)
    rank = jnp.zeros((nsp, QBLK), I32)
    for k in range(n_sel):
        sk = score[k:k + 1, :]
        beats = (sk > score) | ((sk == score) & (k < jt))
        rank = rank + beats.astype(I32)
    sel_t = jnp.where((rank < N_SEL) & valid, 1.0, 0.0).astype(BF16)
    eye = jnp.where(lax.broadcasted_iota(I32, (QBLK, QBLK), 0) == lax.broadcasted_iota(I32, (QBLK, QBLK), 1),
                    1.0, 0.0).astype(BF16)
    sel = lax.dot_general(eye, sel_t, _NT, preferred_element_type=F32).astype(BF16)

    def sel_step(c, carry):
        m_old, l_old, acc = carry
        k0 = pl.multiple_of(c * SEL_KCHUNK, SEL_KCHUNK)
        kk = ksel_bf[pl.ds(k0, SEL_KCHUNK), :]
        vv = vsel_bf[pl.ds(k0, SEL_KCHUNK), :]
        s = lax.dot_general(q4, kk, _NT, preferred_element_type=F32) * ATTN_SCALE
        km = jnp.dot(sel, e_ref[:, pl.ds(k0, SEL_KCHUNK)], preferred_element_type=F32)
        km4 = jnp.concatenate([km] * GROUP, axis=0)
        kpos = k0 + lax.broadcasted_iota(I32, (rows, SEL_KCHUNK), 1)
        mask = (km4 > 0.5) & (kpos <= rpos(SEL_KCHUNK))
        sm = jnp.where(mask, s, MASK_VALUE)
        m_new = jnp.maximum(m_old, jnp.max(sm, axis=-1, keepdims=True))
        alpha = jnp.exp(m_old - m_new)
        p = jnp.where(mask, jnp.exp(sm - m_new), 0.0)
        l_new = alpha * l_old + jnp.sum(p, axis=-1, keepdims=True)
        acc = alpha * acc + jnp.dot(p.astype(BF16), vv, preferred_element_type=F32)
        return m_new, l_new, acc

    n_chunks = (q0 + QBLK + SEL_KCHUNK - 1) // SEL_KCHUNK
    init = (jnp.full((rows, 1), MASK_VALUE, F32), jnp.zeros((rows, 1), F32), jnp.zeros((rows, HEAD_DIM), F32))
    _, l_s, acc_s = lax.fori_loop(0, n_chunks, sel_step, init)
    o_s = acc_s * jnp.where(l_s > 0.0, 1.0 / l_s, 0.0)

    gates = jax.nn.sigmoid(bg_ref[0])
    for g in range(GROUP):
        r0 = g * QBLK
        og = (gates[:, g:g + 1] * o_c[r0:r0 + QBLK] + gates[:, GROUP + g:GROUP + g + 1] * o_s[r0:r0 + QBLK]
              + gates[:, 2 * GROUP + g:2 * GROUP + g + 1] * o_w[r0:r0 + QBLK])
        o_ref[0, :, g * HEAD_DIM:(g + 1) * HEAD_DIM] = og


def attn_prompt(q3, ukv3, kvc, ubg3, n_cmp):
    b, t, _ = q3.shape
    nb = kvc.shape[2]
    n_sel = -(-t // SEL_BLK)
    nsp = max(8, -(-n_sel // 8) * 8)
    e_mat = (np.arange(t)[None, :] // SEL_BLK == np.arange(nsp)[:, None]).astype(np.float32)
    m_t = _overlap_matrix(n_cmp, nb, n_sel, nsp).T
    kv_cols = 2 * HEAD_DIM
    sel_blk0 = KV_DIM * 2 // kv_cols
    win_blk0 = 2 * sel_blk0
    return pl.pallas_call(
        functools.partial(_attn_prompt_body, n_cmp=n_cmp, n_sel=n_sel, seq=t),
        grid=(b, N_KV_HEADS, t // QBLK),
        in_specs=[
            pl.BlockSpec((1, QBLK, GROUP * HEAD_DIM), lambda i, h, j: (i, j, h)),
            pl.BlockSpec((1, t, kv_cols), lambda i, h, j: (i, 0, sel_blk0 + h)),
            pl.BlockSpec((1, t, kv_cols), lambda i, h, j: (i, 0, win_blk0 + h)),
            pl.BlockSpec((1, 2, nb, HEAD_DIM), lambda i, h, j: (i, h, 0, 0)),
            pl.BlockSpec((1, QBLK, LANES), lambda i, h, j: (i, j, h)),
            pl.BlockSpec((nsp, t), lambda i, h, j: (0, 0)),
            pl.BlockSpec((nsp, nb), lambda i, h, j: (0, 0)),
        ],
        out_specs=pl.BlockSpec((1, QBLK, GROUP * HEAD_DIM), lambda i, h, j: (i, j, h)),
        out_shape=jax.ShapeDtypeStruct((b, t, N_HEADS * HEAD_DIM), F32),
        scratch_shapes=[pltpu.VMEM((t, HEAD_DIM), BF16)] * 4,
        compiler_params=_params(("arbitrary", "arbitrary", "arbitrary"), 56),
        name="attn_prompt",
    )(q3, ukv3, ukv3, kvc, ubg3, jnp.asarray(e_mat, BF16), jnp.asarray(m_t, BF16))


def _row_head(shape):
    return lax.shift_right_logical(lax.broadcasted_iota(I32, shape, 0), 2)


def _cmp_sample_body(q_ref, kvc_ref, m_ref, oc_ref, imp_ref, *, qpos, n_cmp):
    qb = q_ref[0].astype(BF16)
    nb = kvc_ref.shape[2]
    rg = _row_head((N_HEADS, nb))
    ncol = lax.broadcasted_iota(I32, (N_HEADS, nb), 1)
    mask = (ncol * CMP_STRIDE + (CMP_BLK - 1) <= qpos) & (ncol < n_cmp)
    row8 = lax.broadcasted_iota(I32, (8, nb), 0)
    o = jnp.zeros((N_HEADS, HEAD_DIM), F32)
    ps8 = jnp.zeros((8, nb), F32)
    for h in range(N_KV_HEADS):
        kc = kvc_ref[0, 2 * h].astype(BF16)
        vc = kvc_ref[0, 2 * h + 1].astype(BF16)
        s = lax.dot_general(qb, kc, _NT, preferred_element_type=F32) * ATTN_SCALE
        ph = jnp.where(rg == h, _msoftmax(s, mask), 0.0)
        o = o + jnp.dot(ph.astype(BF16), vc, preferred_element_type=F32)
        ps8 = jnp.where(row8 == h, jnp.sum(ph, axis=0, keepdims=True), ps8)
    oc_ref[0] = o
    imp_ref[0] = jnp.dot(ps8.astype(BF16), m_ref[...], preferred_element_type=F32)[0:N_KV_HEADS]


def cmp_sample(q16, kvc, m_mat, qpos, n_cmp):
    ns = q16.shape[0]
    nb = kvc.shape[2]
    nsp = m_mat.shape[1]
    return pl.pallas_call(
        functools.partial(_cmp_sample_body, qpos=qpos, n_cmp=n_cmp),
        grid=(ns,),
        in_specs=[pl.BlockSpec((1, N_HEADS, HEAD_DIM), lambda b: (b, 0, 0)),
                  pl.BlockSpec((1, HK, nb, HEAD_DIM), lambda b: (b, 0, 0, 0)),
                  pl.BlockSpec((nb, nsp), lambda b: (0, 0))],
        out_specs=[pl.BlockSpec((1, N_HEADS, HEAD_DIM), lambda b: (b, 0, 0)),
                   pl.BlockSpec((1, N_KV_HEADS, nsp), lambda b: (b, 0, 0))],
        out_shape=[jax.ShapeDtypeStruct((ns, N_HEADS, HEAD_DIM), F32),
                   jax.ShapeDtypeStruct((ns, N_KV_HEADS, nsp), F32)],
        compiler_params=_params(("arbitrary",), 40),
        name="cmp_sample",
    )(q16, kvc, m_mat)


def _rank_sample_body(imp_ref, idx_ref, sc_scr, *, cur, n_sel):
    npad, nr = sc_scr.shape
    st = imp_ref[...].T
    j = lax.broadcasted_iota(I32, (npad, nr), 0)
    real = j < n_sel
    valid = (j <= cur) & real
    forced = (j == 0) | (j >= cur - 1)
    score = jnp.where(forced, FORCED_SCORE, st)
    score = jnp.where(valid, score, -1.0)
    score = jnp.where(real, score, -2.0)
    sc_scr[...] = score

    def body(k, rank):
        sk = sc_scr[pl.ds(k, 1), :]
        beats = (sk > score) | ((sk == score) & (k < j))
        return rank + beats.astype(I32)

    rank = lax.fori_loop(0, n_sel, body, jnp.zeros((npad, nr), I32))
    sel = (rank < N_SEL) & valid
    for slot in range(N_SEL):
        hit = sel & (rank == slot)
        found = jnp.max(hit.astype(I32), axis=0, keepdims=True)
        val = jnp.sum(jnp.where(hit, j, 0), axis=0, keepdims=True)
        idx_ref[slot:slot + 1, :] = jnp.where(found > 0, val, -1)


def rank_sample(imp2, cur, n_sel):
    nr, npad = imp2.shape
    return pl.pallas_call(
        functools.partial(_rank_sample_body, cur=cur, n_sel=n_sel),
        out_shape=jax.ShapeDtypeStruct((N_SEL, nr), I32),
        scratch_shapes=[pltpu.VMEM((npad, nr), F32)],
        name="rank_sample",
    )(imp2)


def _sel_sample_body(idx_ref, pt_ref, q_ref, kn_ref, vn_ref, kv0, kv1, kv2, kv3, o_ref, m_scr, l_scr, acc_scr,
                     *, n_past_blocks):
    del pt_ref
    b = pl.program_id(0)
    s = pl.program_id(1)
    kvs = (kv0, kv1, kv2, kv3)
    q = q_ref[0]
    qb = q.astype(BF16)
    rg = _row_head((N_HEADS, SEL_BLK))

    @pl.when(s == 0)
    def _():
        s_new = jnp.sum(q * kn_ref[0], axis=-1, keepdims=True) * ATTN_SCALE
        m_scr[...] = jnp.broadcast_to(s_new, (N_HEADS, LANES))
        l_scr[...] = jnp.ones((N_HEADS, LANES), F32)
        acc_scr[...] = vn_ref[0]

    sc = jnp.zeros((N_HEADS, SEL_BLK), F32)
    okv = jnp.zeros((N_HEADS, SEL_BLK), I32)
    for h in range(N_KV_HEADS):
        blk = idx_ref[s, b * N_KV_HEADS + h]
        ok = jnp.where((blk >= 0) & (blk < n_past_blocks), 1, 0)
        kh = kvs[h][pl.ds(2 * h, SEL_BLK, stride=HK), :].astype(BF16)
        sh = lax.dot_general(qb, kh, _NT, preferred_element_type=F32)
        sc = jnp.where(rg == h, sh, sc)
        okv = jnp.where(rg == h, ok, okv)
    mask = okv > 0
    sm = jnp.where(mask, sc * ATTN_SCALE, MASK_VALUE)
    m_old = m_scr[:, 0:1]
    m_new = jnp.maximum(m_old, jnp.max(sm, axis=-1, keepdims=True))
    alpha = jnp.exp(m_old - m_new)
    p = jnp.where(mask, jnp.exp(sm - m_new), 0.0)
    pv = jnp.zeros((N_HEADS, HEAD_DIM), F32)
    for h in range(N_KV_HEADS):
        vh = kvs[h][pl.ds(2 * h + 1, SEL_BLK, stride=HK), :].astype(BF16)
        pv = pv + jnp.dot(jnp.where(rg == h, p, 0.0).astype(BF16), vh, preferred_element_type=F32)
    l_new = alpha * l_scr[:, 0:1] + jnp.sum(p, axis=-1, keepdims=True)
    acc = alpha * acc_scr[...] + pv
    m_scr[...] = jnp.broadcast_to(m_new, (N_HEADS, LANES))
    l_scr[...] = jnp.broadcast_to(l_new, (N_HEADS, LANES))
    acc_scr[...] = acc

    @pl.when(s == pl.num_programs(1) - 1)
    def _():
        o_ref[0] = acc / l_new


def sel_sample(idx, page_table, q16, kn16, vn16, pool2d, n_past_blocks):
    ns = q16.shape[0]
    rows_blk = SEL_BLK * HK
    blk_per_page = PAGE_SIZE // SEL_BLK

    def kv_spec(h):
        def imap(b, s, idx_r, pt_r):
            blk = jnp.clip(idx_r[s, b * N_KV_HEADS + h], 0, n_past_blocks - 1)
            return (pt_r[b, blk // blk_per_page] * blk_per_page + blk % blk_per_page, 0)
        return pl.BlockSpec((rows_blk, LANES), imap)

    head3 = lambda: pl.BlockSpec((1, N_HEADS, HEAD_DIM), lambda b, s, i, p: (b, 0, 0))
    grid_spec = pltpu.PrefetchScalarGridSpec(
        num_scalar_prefetch=2,
        grid=(ns, N_SEL),
        in_specs=[head3(), head3(), head3()] + [kv_spec(h) for h in range(N_KV_HEADS)],
        out_specs=head3(),
        scratch_shapes=[pltpu.VMEM((N_HEADS, LANES), F32)] * 3,
    )
    return pl.pallas_call(
        functools.partial(_sel_sample_body, n_past_blocks=n_past_blocks),
        grid_spec=grid_spec,
        out_shape=jax.ShapeDtypeStruct((ns, N_HEADS, HEAD_DIM), F32),
        compiler_params=_params(("arbitrary", "arbitrary"), 40),
        name="sel_sample",
    )(idx, page_table, q16, kn16, vn16, pool2d, pool2d, pool2d, pool2d)


def _win_sample_body(q_ref, win_ref, kn_ref, vn_ref, oc_ref, os_ref, bg_ref, o_ref, *, wbuf):
    q = q_ref[0]
    qb = q.astype(BF16)
    rg = _row_head((N_HEADS, wbuf))
    sc = jnp.zeros((N_HEADS, wbuf), F32)
    for h in range(N_KV_HEADS):
        kh = win_ref[pl.ds(2 * h, wbuf, stride=HK), :].astype(BF16)
        sc = jnp.where(rg == h, lax.dot_general(qb, kh, _NT, preferred_element_type=F32), sc)
    mask = lax.broadcasted_iota(I32, (N_HEADS, wbuf), 1) > wbuf - WINDOW
    sm = jnp.where(mask, sc * ATTN_SCALE, MASK_VALUE)
    s_new = jnp.sum(q * kn_ref[0], axis=-1, keepdims=True) * ATTN_SCALE
    m = jnp.maximum(jnp.max(sm, axis=-1, keepdims=True), s_new)
    p = jnp.where(mask, jnp.exp(sm - m), 0.0)
    p_new = jnp.exp(s_new - m)
    den = jnp.sum(p, axis=-1, keepdims=True) + p_new
    pv = jnp.zeros((N_HEADS, HEAD_DIM), F32)
    for h in range(N_KV_HEADS):
        vh = win_ref[pl.ds(2 * h + 1, wbuf, stride=HK), :].astype(BF16)
        pv = pv + jnp.dot(jnp.where(rg == h, p, 0.0).astype(BF16), vh, preferred_element_type=F32)
    o_w = (pv + p_new * vn_ref[0]) / den
    gates = jax.nn.sigmoid(bg_ref[0])
    o_ref[0] = gates[:, 0:1] * oc_ref[0] + gates[:, 1:2] * os_ref[0] + gates[:, 2:3] * o_w


def win_sample(q16, win2d, kn16, vn16, o_cmp, o_sel, bgs, wbuf):
    ns = q16.shape[0]
    head3 = lambda: pl.BlockSpec((1, N_HEADS, HEAD_DIM), lambda b: (b, 0, 0))
    return pl.pallas_call(
        functools.partial(_win_sample_body, wbuf=wbuf),
        grid=(ns,),
        in_specs=[head3(), pl.BlockSpec((wbuf * HK, LANES), lambda b: (b, 0)), head3(), head3(), head3(), head3(),
                  head3()],
        out_specs=head3(),
        out_shape=jax.ShapeDtypeStruct((ns, N_HEADS, HEAD_DIM), F32),
        compiler_params=_params(("arbitrary",), 40),
        name="win_sample",
    )(q16, win2d, kn16, vn16, o_cmp, o_sel, bgs)


ROUTE_LANE0 = N_GROUPS


def _route_body(x_ref, g_ref, wr_ref, br_ref, xn_ref, comb_ref):
    x = x_ref[...]
    var = jnp.mean(x * x, axis=-1, keepdims=True)
    xnb = ((x * lax.rsqrt(var + EPS)) * g_ref[...]).astype(BF16)
    xn_ref[...] = xnb
    logits = jnp.dot(xnb, wr_ref[...], preferred_element_type=F32) + br_ref[...]
    lane = lax.broadcasted_iota(I32, logits.shape, 1)
    big = 4 * LANES

    isg = lane < N_GROUPS
    lg = jnp.where(isg, logits, -jnp.inf)
    mg = jnp.max(lg, axis=-1, keepdims=True)
    gi = jnp.min(jnp.where(lg == mg, lane, big), axis=-1, keepdims=True)
    pg = 1.0 / jnp.sum(jnp.where(isg, jnp.exp(lg - mg), 0.0), axis=-1, keepdims=True)

    lo = ROUTE_LANE0 + gi * EXP_PER_GROUP
    ing = (lane >= lo) & (lane < lo + EXP_PER_GROUP)
    le = jnp.where(ing, logits, -jnp.inf)
    me = jnp.max(le, axis=-1, keepdims=True)
    ee = jnp.where(ing, jnp.exp(le - me), 0.0)
    pe = jnp.where(ing, ee / jnp.sum(ee, axis=-1, keepdims=True), -1.0)
    p1 = jnp.max(pe, axis=-1, keepdims=True)
    i1 = jnp.min(jnp.where(pe == p1, lane, big), axis=-1, keepdims=True)
    pe2 = jnp.where(lane == i1, -1.0, pe)
    p2 = jnp.max(pe2, axis=-1, keepdims=True)
    i2 = jnp.min(jnp.where(pe2 == p2, lane, big), axis=-1, keepdims=True)
    tot = p1 + p2
    comb_ref[...] = jnp.where(lane == i1, pg * (p1 / tot), 0.0) + jnp.where(lane == i2, pg * (p2 / tot), 0.0)


def route(x, g, wr, br, tm):
    m, d = x.shape
    return pl.pallas_call(
        _route_body,
        grid=(m // tm,),
        in_specs=[pl.BlockSpec((tm, d), lambda i: (i, 0)), pl.BlockSpec((1, d), lambda i: (0, 0)),
                  pl.BlockSpec((d, LANES), lambda i: (0, 0)), pl.BlockSpec((1, LANES), lambda i: (0, 0))],
        out_specs=[pl.BlockSpec((tm, d), lambda i: (i, 0)), pl.BlockSpec((tm, LANES), lambda i: (i, 0))],
        out_shape=[jax.ShapeDtypeStruct((m, d), BF16), jax.ShapeDtypeStruct((m, LANES), F32)],
        compiler_params=_params(("arbitrary",), 40),
        name="route",
    )(x, g.reshape(1, d), wr, br)


def _moe_dense_body(x_ref, comb_ref, wg_ref, wu_ref, wd_ref, res_ref, gf_ref, y_ref, acc):
    e = pl.program_id(1)

    @pl.when(e == 0)
    def _():
        acc[...] = jnp.zeros(acc.shape, F32)

    x = x_ref[...]
    hg = jnp.dot(x, wg_ref[0], preferred_element_type=F32)
    hu = jnp.dot(x, wu_ref[0], preferred_element_type=F32)
    hid = (hg * jax.nn.sigmoid(hg)) * hu
    comb = comb_ref[...]
    lane = lax.broadcasted_iota(I32, comb.shape, 1)
    c = jnp.sum(jnp.where(lane == e + ROUTE_LANE0, comb, 0.0), axis=-1, keepdims=True)
    acc[...] += c * jnp.dot(hid.astype(BF16), wd_ref[0], preferred_element_type=F32)

    @pl.when(e == pl.num_programs(1) - 1)
    def _():
        xo = res_ref[...] + acc[...]
        var = jnp.mean(xo * xo, axis=-1, keepdims=True)
        y_ref[...] = (xo * lax.rsqrt(var + EPS)) * gf_ref[...]


def moe_dense(xn, comb, wg, wu, wd, res, g_final, tm):
    m, d = xn.shape
    ne, _, de = wg.shape
    return pl.pallas_call(
        _moe_dense_body,
        grid=(m // tm, ne),
        in_specs=[pl.BlockSpec((tm, d), lambda i, e: (i, 0)), pl.BlockSpec((tm, LANES), lambda i, e: (i, 0)),
                  pl.BlockSpec((1, d, de), lambda i, e: (e, 0, 0)), pl.BlockSpec((1, d, de), lambda i, e: (e, 0, 0)),
                  pl.BlockSpec((1, de, d), lambda i, e: (e, 0, 0)), pl.BlockSpec((tm, d), lambda i, e: (i, 0)),
                  pl.BlockSpec((1, d), lambda i, e: (0, 0))],
        out_specs=pl.BlockSpec((tm, d), lambda i, e: (i, 0)),
        out_shape=jax.ShapeDtypeStruct((m, d), F32),
        scratch_shapes=[pltpu.VMEM((tm, d), F32)],
        compiler_params=_params(("arbitrary", "arbitrary"), 58),
        name="moe_dense",
    )(xn, comb, wg, wu, wd, res, g_final.reshape(1, d))


def _mixer_inputs(x2, g_mix, w_in, w_bg, w_mg, tm, q_dtype):
    tn = 512
    xn = rmsnorm(x2, g_mix, min(tm, 512), BF16)
    u_rnn = matmul_cols(xn, w_in, 0, D_RNN, tm, tn, F32, name="in_rnn")
    u_gate = matmul_cols(xn, w_in, D_RNN, D_RNN, tm, tn, F32, name="in_gate")
    q = matmul_cols(xn, w_in, 2 * D_RNN, N_HEADS * HEAD_DIM, tm, tn, q_dtype, name="in_q")
    u_kv = matmul_cols(xn, w_in, 2 * D_RNN + N_HEADS * HEAD_DIM, 6 * KV_DIM, tm, tn, F32, name="in_kv")
    u_bg = matmul_cols(xn, w_bg, 0, w_bg.shape[1], tm, tn, F32, name="in_bg")
    u_mg = matmul_cols(xn, w_mg, 0, 2 * D_MODEL, tm, tn, F32, name="in_mg")
    return u_rnn, u_gate, q, u_kv, u_bg, u_mg


def kernel(x_prompt, x_sample, cache_cmp_kv, cache_sel_kv, cache_win_kv, state_conv, state_rglru, page_table, g_mix, w_in, conv_w, conv_b, lru_w_a, lru_b_a, lru_w_x, lru_b_x, lru_lambda, cmp_pe, cmp_w1, cmp_b1, cmp_w2, cmp_b2, w_out, g_ffn, moe_w_group, moe_b_group, moe_w_expert, moe_b_expert, moe_w_gate, moe_w_up, moe_w_down, g_final):
    b, t, d = x_prompt.shape
    ns = x_sample.shape[0]
    npages = page_table.shape[1]
    past_len = npages * PAGE_SIZE
    wbuf = cache_win_kv.shape[1]
    kvshape = (N_KV_HEADS, 2, HEAD_DIM)

    bg0 = 2 * D_RNN + N_HEADS * HEAD_DIM + 6 * KV_DIM
    n_bg = 3 * N_HEADS
    w_bg = w_in[:, bg0:bg0 + n_bg].reshape(d, 3, N_KV_HEADS, GROUP).transpose(0, 2, 1, 3).reshape(d, N_KV_HEADS, 3 * GROUP)
    w_bg = jnp.pad(w_bg, ((0, 0), (0, 0), (0, LANES - 3 * GROUP))).reshape(d, N_KV_HEADS * LANES)
    w_mg = w_in[:, bg0 + n_bg:]
    wa = lru_w_a.astype(BF16)
    wx = lru_w_x.astype(BF16)
    half_rows = CMP_STRIDE * HEAD_DIM
    wc = jnp.concatenate([cmp_w1[:, :half_rows], cmp_w1[:, half_rows:]], axis=-1).astype(BF16)
    pe8 = jnp.pad(cmp_pe.reshape(2, 2, half_rows), ((0, 0), (0, 6), (0, 0)))
    b1 = cmp_b1.reshape(2, 1, CMP_HIDDEN)
    w2 = cmp_w2.astype(BF16)
    b2 = cmp_b2.reshape(2, 1, HEAD_DIM)
    n_route = N_GROUPS + N_EXPERTS
    wr = jnp.pad(jnp.concatenate([moe_w_group, moe_w_expert], axis=1), ((0, 0), (0, LANES - n_route))).astype(BF16)
    br = jnp.pad(jnp.concatenate([moe_b_group, moe_b_expert]), (0, LANES - n_route)).reshape(1, LANES)
    moe_w = (moe_w_gate.astype(BF16), moe_w_up.astype(BF16), moe_w_down.astype(BF16))
    lru_args = (conv_w, conv_b, wa, wx, lru_b_a, lru_b_x, lru_lambda)
    cmp_args = (wc, pe8, b1, w2, b2)

    xp2 = x_prompt.reshape(b * t, d)
    u_rnn, u_gate, q, u_kv, u_bg, u_mg = _mixer_inputs(xp2, g_mix, w_in, w_bg, w_mg, 1024, BF16)
    o_rnn, p_conv, p_h = lru_prompt(u_rnn.reshape(b, t, d), u_gate.reshape(b, t, d), *lru_args, 256)
    kv5 = u_kv.reshape((b, t, 3) + kvshape)
    p_cmp, p_sel = kv5[:, :, 0], kv5[:, :, 1]
    p_win = kv5[:, t - min(WINDOW, t):, 2]
    pt_prompt = jnp.arange(b * t // PAGE_SIZE, dtype=I32).reshape(b, t // PAGE_SIZE)
    kvc_p = compress(p_cmp.reshape(-1, HEAD_DIM), pt_prompt, jnp.zeros((b * CMP_STRIDE * HK, HEAD_DIM), F32), *cmp_args)
    o_nsa = attn_prompt(q.reshape(b, t, -1), u_kv.reshape(b, t, -1), kvc_p, u_bg.reshape(b, t, -1),
                        t // CMP_STRIDE - 1)
    merged = merge(u_mg, o_rnn.reshape(b * t, d), o_nsa.reshape(b * t, d), 512)
    x_mid = matmul_cols(merged, w_out, 0, d, 1024, 512, F32, residual=xp2, name="out_proj")
    xn2, comb = route(x_mid, g_ffn, wr, br, 512)
    y_prompt = moe_dense(xn2, comb, *moe_w, x_mid, g_final, 512).reshape(b, t, d)

    xs2 = x_sample.reshape(ns, d)
    su_rnn, su_gate, sq, su_kv, su_bg, su_mg = _mixer_inputs(xs2, g_mix, w_in, w_bg, w_mg, ns, F32)
    so_rnn, cn, s_h = lru_sample(su_rnn, su_gate, state_conv.transpose(1, 0, 2), state_rglru, *lru_args)
    s_conv = cn.transpose(1, 0, 2)
    kvs = su_kv.reshape(ns, 3, HK, HEAD_DIM)
    s_cmp = kvs[:, 0].reshape((ns, 1) + kvshape)
    s_sel = kvs[:, 1].reshape((ns, 1) + kvshape)
    s_win = jnp.concatenate([cache_win_kv[:, 1:], kvs[:, 2].reshape((ns, 1) + kvshape)], axis=1)

    extra = jnp.concatenate([kvs[:, 0], jnp.zeros((ns, (CMP_STRIDE - 1) * HK, HEAD_DIM), F32)], axis=1)
    kvc_s = compress(cache_cmp_kv.reshape(-1, HEAD_DIM), page_table, extra.reshape(-1, HEAD_DIM), *cmp_args)
    n_cmp_s = -(-(past_len + 1) // CMP_STRIDE) - 1
    n_sel_s = -(-(past_len + 1) // SEL_BLK)
    nsp = -(-n_sel_s // LANES) * LANES
    q16 = sq.reshape(ns, N_HEADS, HEAD_DIM)
    m_s = jnp.asarray(_overlap_matrix(n_cmp_s, kvc_s.shape[2], n_sel_s, nsp), BF16)
    so_cmp, imp = cmp_sample(q16, kvc_s, m_s, past_len, n_cmp_s)
    idx = rank_sample(imp.reshape(ns * N_KV_HEADS, nsp), past_len // SEL_BLK, n_sel_s)

    def per_head_rows(new_kv, kv):
        return jnp.repeat(new_kv[:, kv::2], GROUP, axis=1)

    so_sel = sel_sample(idx, page_table, q16, per_head_rows(kvs[:, 1], 0), per_head_rows(kvs[:, 1], 1),
                        cache_sel_kv.reshape(-1, HEAD_DIM), past_len // SEL_BLK)
    bgs = su_bg.reshape(ns, N_KV_HEADS, LANES)[:, :, :3 * GROUP].reshape(ns, N_KV_HEADS, 3, GROUP)
    bgs = jnp.pad(bgs.transpose(0, 1, 3, 2).reshape(ns, N_HEADS, 3), ((0, 0), (0, 0), (0, LANES - 3)))
    so_nsa = win_sample(q16, cache_win_kv.reshape(-1, HEAD_DIM), per_head_rows(kvs[:, 2], 0),
                        per_head_rows(kvs[:, 2], 1), so_cmp, so_sel, bgs, wbuf)
    s_merged = merge(su_mg, so_rnn, so_nsa.reshape(ns, d), ns)
    sx_mid = matmul_cols(s_merged, w_out, 0, d, ns, 512, F32, residual=xs2, name="out_proj_s")
    sxn2, scomb = route(sx_mid, g_ffn, wr, br, ns)
    y_sample = moe_dense(sxn2, scomb, *moe_w, sx_mid, g_final, ns).reshape(ns, 1, d)

    return (y_prompt, y_sample, p_cmp, s_cmp, p_sel, s_sel, p_win, s_win, p_conv, s_conv,
            p_h.reshape(b, d), s_h)
```

```python
import functools

import jax
import jax.numpy as jnp
import numpy as np
from jax import lax
from jax.experimental import pallas as pl
from jax.experimental.pallas import tpu as pltpu

F32 = jnp.float32
BF16 = jnp.bfloat16
I32 = jnp.int32

D_MODEL = 2048
D_RNN = 2048
RNN_BLOCKS = 16
RNN_BLOCK_DIM = 128
CONV_W = 4
LRU_C = 8.0
N_HEADS = 16
HEAD_DIM = 128
N_KV_HEADS = 4
GROUP = 4
KV_DIM = N_KV_HEADS * HEAD_DIM
HK = 2 * N_KV_HEADS
CMP_STRIDE = 16
CMP_BLK = 32
CMP_HIDDEN = 256
SEL_BLK = 64
SEL_SHIFT = 6
N_SEL = 16
WINDOW = 512
PAGE_SIZE = 128
ATTN_SCALE = HEAD_DIM ** -0.5
FORCED_SCORE = 1e4
N_GROUPS = 4
EXP_PER_GROUP = 8
N_EXPERTS = 32
D_EXPERT = 512
EPS = 1e-6
MASK_VALUE = -1e30
LANES = 128
MIB = 1024 * 1024

_NT = (((1,), (1,)), ((), ()))


def _params(sem, vmem_mib):
    return pltpu.CompilerParams(dimension_semantics=sem, vmem_limit_bytes=vmem_mib * MIB)


def _msoftmax(s, mask):
    sm = jnp.where(mask, s, MASK_VALUE)
    m = jnp.max(sm, axis=-1, keepdims=True)
    e = jnp.where(mask, jnp.exp(sm - m), 0.0)
    den = jnp.sum(e, axis=-1, keepdims=True)
    return e * jnp.where(den > 0.0, 1.0 / den, 0.0)


def _gelu_tanh(x):
    return 0.5 * x * (1.0 + jnp.tanh(0.7978845608028654 * (x + 0.044715 * (x * x * x))))


def _rmsnorm_body(x_ref, g_ref, o_ref):
    x = x_ref[...]
    var = jnp.mean(x * x, axis=-1, keepdims=True)
    o_ref[...] = ((x * lax.rsqrt(var + EPS)) * g_ref[...]).astype(o_ref.dtype)


def rmsnorm(x, g, tm, out_dtype):
    m, d = x.shape
    return pl.pallas_call(
        _rmsnorm_body,
        grid=(m // tm,),
        in_specs=[pl.BlockSpec((tm, d), lambda i: (i, 0)), pl.BlockSpec((1, d), lambda i: (0, 0))],
        out_specs=pl.BlockSpec((tm, d), lambda i: (i, 0)),
        out_shape=jax.ShapeDtypeStruct((m, d), out_dtype),
        compiler_params=_params(("arbitrary",), 40),
        name="rmsnorm",
    )(x, g.reshape(1, d))


def _mm_body(x_ref, w_ref, o_ref, wbf_ref):
    @pl.when(pl.program_id(1) == 0)
    def _():
        wbf_ref[...] = w_ref[...].astype(BF16)

    o_ref[...] = jnp.dot(x_ref[...], wbf_ref[...], preferred_element_type=F32).astype(o_ref.dtype)


def _mm_res_body(x_ref, w_ref, r_ref, o_ref, wbf_ref):
    @pl.when(pl.program_id(1) == 0)
    def _():
        wbf_ref[...] = w_ref[...].astype(BF16)

    o_ref[...] = r_ref[...] + jnp.dot(x_ref[...], wbf_ref[...], preferred_element_type=F32)


def matmul_cols(x, w, col0, ncols, tm, tn, out_dtype, residual=None, name="matmul"):
    m, k = x.shape
    cb0 = col0 // tn
    assert col0 % tn == 0 and ncols % tn == 0 and m % tm == 0
    in_specs = [pl.BlockSpec((tm, k), lambda j, i: (i, 0)), pl.BlockSpec((k, tn), lambda j, i: (0, cb0 + j))]
    args = [x, w]
    body = _mm_body
    if residual is not None:
        in_specs.append(pl.BlockSpec((tm, tn), lambda j, i: (i, j)))
        args.append(residual)
        body = _mm_res_body
    return pl.pallas_call(
        body,
        grid=(ncols // tn, m // tm),
        in_specs=in_specs,
        out_specs=pl.BlockSpec((tm, tn), lambda j, i: (i, j)),
        out_shape=jax.ShapeDtypeStruct((m, ncols), out_dtype),
        scratch_shapes=[pltpu.VMEM((k, tn), BF16)],
        compiler_params=_params(("arbitrary", "arbitrary"), 48),
        name=name,
    )(*args)


def _mm_kv_body(x_ref, w_ref, oi_ref, *rest):
    wbf_ref = rest[-1]

    @pl.when(pl.program_id(0) == 0)
    def _():
        wbf_ref[...] = w_ref[...].astype(BF16)

    res = jnp.dot(x_ref[...], wbf_ref[...], preferred_element_type=F32)
    tm = res.shape[0]
    for hk in range(HK):
        blk = res[:, hk * HEAD_DIM:(hk + 1) * HEAD_DIM]
        oi_ref[pl.ds(hk, tm, stride=HK), :] = blk
        if len(rest) == 2:
            rest[0][hk] = blk.astype(BF16)


def matmul_kv(x, w, col0, tm, planes, name):
    m, k = x.shape
    ncols = HK * HEAD_DIM
    assert col0 % ncols == 0 and m % tm == 0
    out_specs = [pl.BlockSpec((tm * HK, HEAD_DIM), lambda i: (i, 0))]
    out_shape = [jax.ShapeDtypeStruct((m * HK, HEAD_DIM), F32)]
    if planes:
        out_specs.append(pl.BlockSpec((HK, tm, HEAD_DIM), lambda i: (0, i, 0)))
        out_shape.append(jax.ShapeDtypeStruct((HK, m, HEAD_DIM), BF16))
    return pl.pallas_call(
        _mm_kv_body,
        grid=(m // tm,),
        in_specs=[pl.BlockSpec((tm, k), lambda i: (i, 0)), pl.BlockSpec((k, ncols), lambda i: (0, col0 // ncols))],
        out_specs=out_specs,
        out_shape=out_shape,
        scratch_shapes=[pltpu.VMEM((k, ncols), BF16)],
        compiler_params=_params(("arbitrary",), 48),
        name=name,
    )(x, w)


def _merge_body(mg_ref, rnn_ref, nsa_ref, o_ref):
    d = rnn_ref.shape[-1]
    ga = jax.nn.sigmoid(mg_ref[:, :d])
    gb = jax.nn.sigmoid(mg_ref[:, d:])
    o_ref[...] = (ga * rnn_ref[...] + gb * nsa_ref[...]).astype(o_ref.dtype)


def merge(u_mg, o_rnn, o_nsa, tm):
    m, d = o_rnn.shape
    return pl.pallas_call(
        _merge_body,
        grid=(m // tm,),
        in_specs=[pl.BlockSpec((tm, 2 * d), lambda i: (i, 0)), pl.BlockSpec((tm, d), lambda i: (i, 0)),
                  pl.BlockSpec((tm, d), lambda i: (i, 0))],
        out_specs=pl.BlockSpec((tm, d), lambda i: (i, 0)),
        out_shape=jax.ShapeDtypeStruct((m, d), BF16),
        compiler_params=_params(("arbitrary",), 48),
        name="merge",
    )(u_mg, o_rnn, o_nsa)


def _lru_gates(uc, wa_ref, wx_ref, ba_ref, bx_ref, lam_ref):
    ucb = uc.astype(BF16)
    ra, ri = [], []
    for n in range(RNN_BLOCKS):
        blk = ucb[:, n * RNN_BLOCK_DIM:(n + 1) * RNN_BLOCK_DIM]
        ra.append(jnp.dot(blk, wa_ref[n], preferred_element_type=F32))
        ri.append(jnp.dot(blk, wx_ref[n], preferred_element_type=F32))
    r = jax.nn.sigmoid(jnp.concatenate(ra, axis=-1) + ba_ref[...])
    i = jax.nn.sigmoid(jnp.concatenate(ri, axis=-1) + bx_ref[...])
    z = -lam_ref[...]
    softplus = jnp.maximum(z, 0.0) + jnp.log1p(jnp.exp(-jnp.abs(z)))
    log_a = (-LRU_C * r) * softplus
    a = jnp.exp(log_a)
    mult = jnp.sqrt(-jnp.tanh(log_a) * (a * a + 1.0))
    return a, mult * (i * uc)


def _lru_prompt_body(u_ref, gate_ref, cw_ref, cb_ref, wa_ref, wx_ref, ba_ref, bx_ref, lam_ref,
                     o_ref, conv_ref, hlast_ref, xbuf, hcar, acum, bcum):
    t = pl.program_id(1)
    nt = pl.num_programs(1)
    tt, d = u_ref.shape[1], u_ref.shape[2]

    @pl.when(t == 0)
    def _():
        xbuf[0:8, :] = jnp.zeros((8, d), F32)
        hcar[...] = jnp.zeros((8, d), F32)

    x = u_ref[0]
    xbuf[8:8 + tt, :] = x
    uc = (cw_ref[0:1, :] * xbuf[5:5 + tt, :] + cw_ref[1:2, :] * xbuf[6:6 + tt, :]
          + cw_ref[2:3, :] * xbuf[7:7 + tt, :] + cw_ref[3:4, :] * x) + cb_ref[...]
    xbuf[0:8, :] = xbuf[tt:tt + 8, :]

    a, b = _lru_gates(uc, wa_ref, wx_ref, ba_ref, bx_ref, lam_ref)
    g = tt // 8
    a3 = a.reshape(g, 8, d)
    b3 = b.reshape(g, 8, d)
    row = lax.broadcasted_iota(I32, (g, 8, d), 1)
    for s in (1, 2, 4):
        a_sh = pltpu.roll(a3, s, axis=1)
        b_sh = pltpu.roll(b3, s, axis=1)
        keep = row >= s
        b3 = jnp.where(keep, a3 * b_sh + b3, b3)
        a3 = jnp.where(keep, a3 * a_sh, a3)
    acum[...] = a3
    bcum[...] = b3

    def body(gi, h):
        hg = bcum[gi] + acum[gi] * h
        bcum[gi] = hg
        return jnp.broadcast_to(hg[7:8, :], (8, d))

    hfin = lax.fori_loop(0, g, body, hcar[...])
    hcar[...] = hfin
    o_ref[0] = bcum[...].reshape(tt, d) * _gelu_tanh(gate_ref[0])

    @pl.when(t == nt - 1)
    def _():
        conv_ref[0] = x[tt - (CONV_W - 1):tt, :]
        hlast_ref[0] = hfin[0:1, :]


def lru_prompt(u_rnn, u_gate, conv_w, conv_b, wa, wx, ba, bx, lam, tt):
    b, t, d = u_rnn.shape
    vec = lambda: pl.BlockSpec((1, d), lambda i, j: (0, 0))
    wspec = lambda: pl.BlockSpec((RNN_BLOCKS, RNN_BLOCK_DIM, RNN_BLOCK_DIM), lambda i, j: (0, 0, 0))
    return pl.pallas_call(
        _lru_prompt_body,
        grid=(b, t // tt),
        in_specs=[pl.BlockSpec((1, tt, d), lambda i, j: (i, j, 0)), pl.BlockSpec((1, tt, d), lambda i, j: (i, j, 0)),
                  pl.BlockSpec((CONV_W, d), lambda i, j: (0, 0)), vec(), wspec(), wspec(), vec(), vec(), vec()],
        out_specs=[pl.BlockSpec((1, tt, d), lambda i, j: (i, j, 0)),
                   pl.BlockSpec((1, CONV_W - 1, d), lambda i, j: (i, 0, 0)),
                   pl.BlockSpec((1, 1, d), lambda i, j: (i, 0, 0))],
        out_shape=[jax.ShapeDtypeStruct((b, t, d), F32), jax.ShapeDtypeStruct((b, CONV_W - 1, d), F32),
                   jax.ShapeDtypeStruct((b, 1, d), F32)],
        scratch_shapes=[pltpu.VMEM((tt + 8, d), F32), pltpu.VMEM((8, d), F32),
                        pltpu.VMEM((tt // 8, 8, d), F32), pltpu.VMEM((tt // 8, 8, d), F32)],
        compiler_params=_params(("arbitrary", "arbitrary"), 56),
        name="lru_prompt",
    )(u_rnn, u_gate, conv_w, conv_b.reshape(1, d), wa, wx, ba.reshape(1, d), bx.reshape(1, d), lam.reshape(1, d))


def _lru_sample_body(u_ref, gate_ref, cp_ref, h0_ref, cw_ref, cb_ref, wa_ref, wx_ref, ba_ref, bx_ref, lam_ref,
                     o_ref, cn_ref, h_ref):
    x = u_ref[...]
    uc = (cw_ref[0:1, :] * cp_ref[0] + cw_ref[1:2, :] * cp_ref[1] + cw_ref[2:3, :] * cp_ref[2]
          + cw_ref[3:4, :] * x) + cb_ref[...]
    a, b = _lru_gates(uc, wa_ref, wx_ref, ba_ref, bx_ref, lam_ref)
    h = a * h0_ref[...] + b
    o_ref[...] = h * _gelu_tanh(gate_ref[...])
    h_ref[...] = h
    cn_ref[0] = cp_ref[1]
    cn_ref[1] = cp_ref[2]
    cn_ref[2] = x


def lru_sample(u_rnn, u_gate, conv_prev_t, h0, conv_w, conv_b, wa, wx, ba, bx, lam):
    n, d = u_rnn.shape
    return pl.pallas_call(
        _lru_sample_body,
        out_shape=[jax.ShapeDtypeStruct((n, d), F32), jax.ShapeDtypeStruct((CONV_W - 1, n, d), F32),
                   jax.ShapeDtypeStruct((n, d), F32)],
        compiler_params=pltpu.CompilerParams(vmem_limit_bytes=40 * MIB),
        name="lru_sample",
    )(u_rnn, u_gate, conv_prev_t, h0, conv_w, conv_b.reshape(1, d), wa, wx, ba.reshape(1, d), bx.reshape(1, d),
      lam.reshape(1, d))


COMPRESS_PAGES = 16
CHUNKS_PER_PAGE = PAGE_SIZE // CMP_STRIDE


def _compress_body(pt_ref, *refs, pp, n_steps):
    page_refs = refs[:pp]
    nxt_ref, extra_ref, wc_ref, pe_ref, b1_ref, w2_ref, b2_ref, o_ref, res_scr = refs[pp:]
    del pt_ref
    is_last = pl.program_id(1) == n_steps - 1
    m_rows = pp * CHUNKS_PER_PAGE * N_KV_HEADS
    lo4 = lax.broadcasted_iota(I32, (4, 8, 8, LANES), 2) < 4
    lo3 = lax.broadcasted_iota(I32, (8, 8, LANES), 1) < 4

    for k in range(2):
        lhs_sp = [[] for _ in range(8)]
        for pr in page_refs:
            xk = pr[pl.ds(k, PAGE_SIZE * N_KV_HEADS, stride=2), :]
            x5 = xk.reshape(4, 2, 8, 8, LANES)
            a0 = x5[:, 0]
            a1 = x5[:, 1]
            be = jnp.where(lo4, a0, pltpu.roll(a1, 4, axis=2))
            bo = jnp.where(lo4, pltpu.roll(a0, 4, axis=2), a1)
            for sp in range(8):
                lhs_sp[sp].append(jnp.concatenate([be[:, sp], bo[:, sp]], axis=-1).reshape(32, 2 * LANES))
        acc = jnp.zeros((m_rows, 2 * CMP_HIDDEN), F32)
        for sp in range(8):
            lhs = jnp.concatenate(lhs_sp[sp], axis=0).astype(BF16)
            acc = acc + jnp.dot(lhs, wc_ref[k, sp * 256:(sp + 1) * 256, :], preferred_element_type=F32)

        pet = jnp.dot(pe_ref[k].astype(BF16), wc_ref[k], preferred_element_type=F32)
        peb0 = pet[0:1, :CMP_HIDDEN]
        peb1 = pet[1:2, CMP_HIDDEN:]
        h0 = acc[:, :CMP_HIDDEN] + peb0
        h1 = acc[:, CMP_HIDDEN:] + peb1

        xn = jnp.where(is_last, extra_ref[pl.ds(k, CMP_STRIDE * N_KV_HEADS, stride=2), :],
                       nxt_ref[pl.ds(k, CMP_STRIDE * N_KV_HEADS, stride=2), :])
        xn3 = xn.reshape(8, 8, LANES)
        ln = jnp.concatenate([jnp.where(lo3, xn3, 0.0), jnp.where(lo3, 0.0, xn3)], axis=-1).astype(BF16)
        nacc = jnp.zeros((8, CMP_HIDDEN), F32)
        for sp in range(8):
            nacc = nacc + jnp.dot(ln[sp], wc_ref[k, sp * 256:(sp + 1) * 256, CMP_HIDDEN:],
                                  preferred_element_type=F32)
        n8 = nacc + pltpu.roll(nacc, 4, axis=0) + peb1
        h1e = jnp.concatenate([h1, n8], axis=0)
        h1s = pltpu.roll(h1e, m_rows + 8 - N_KV_HEADS, axis=0)[:m_rows]
        pre = h0 + h1s + b1_ref[k]
        out_k = jnp.dot(_gelu_tanh(pre).astype(BF16), w2_ref[k], preferred_element_type=F32) + b2_ref[k]
        res_scr[...] = out_k
        for h in range(N_KV_HEADS):
            o_ref[0, 2 * h + k] = res_scr[pl.ds(h, m_rows // N_KV_HEADS, stride=N_KV_HEADS), :]


def compress(pool2d, page_table, extra2d, wc, pe8, b1, w2, b2):
    nseq, npages = page_table.shape
    pp = COMPRESS_PAGES
    n_steps = npages // pp
    assert npages % pp == 0
    rows_page = PAGE_SIZE * HK
    rows_chunk = CMP_STRIDE * HK
    blocks_step = pp * CHUNKS_PER_PAGE

    def page_spec(i):
        return pl.BlockSpec((rows_page, LANES), lambda b, s, pt: (pt[b, s * pp + i], 0))

    def nxt_map(b, s, pt):
        return (pt[b, jnp.minimum((s + 1) * pp, npages - 1)] * CHUNKS_PER_PAGE, 0)

    const3 = lambda shape: pl.BlockSpec(shape, lambda b, s, pt: (0, 0, 0))
    in_specs = [page_spec(i) for i in range(pp)] + [
        pl.BlockSpec((rows_chunk, LANES), nxt_map),
        pl.BlockSpec((rows_chunk, LANES), lambda b, s, pt: (b, 0)),
        const3((2, CMP_STRIDE * HEAD_DIM, 2 * CMP_HIDDEN)),
        const3((2, 8, CMP_STRIDE * HEAD_DIM)),
        const3((2, 1, CMP_HIDDEN)),
        const3((2, CMP_HIDDEN, HEAD_DIM)),
        const3((2, 1, HEAD_DIM)),
    ]
    grid_spec = pltpu.PrefetchScalarGridSpec(
        num_scalar_prefetch=1,
        grid=(nseq, n_steps),
        in_specs=in_specs,
        out_specs=pl.BlockSpec((1, HK, blocks_step, HEAD_DIM), lambda b, s, pt: (b, 0, s, 0)),
        scratch_shapes=[pltpu.VMEM((blocks_step * N_KV_HEADS, HEAD_DIM), F32)],
    )
    return pl.pallas_call(
        functools.partial(_compress_body, pp=pp, n_steps=n_steps),
        grid_spec=grid_spec,
        out_shape=jax.ShapeDtypeStruct((nseq, HK, npages * CHUNKS_PER_PAGE, HEAD_DIM), F32),
        compiler_params=_params(("arbitrary", "arbitrary"), 56),
        name="compress",
    )(page_table, *([pool2d] * pp), pool2d, extra2d, wc, pe8, b1, w2, b2)


def _overlap_matrix(n_cmp, n_cmp_pad, n_sel, n_sel_pad):
    cs = np.arange(n_cmp_pad)[:, None] * CMP_STRIDE
    ss = np.arange(n_sel_pad)[None, :] * SEL_BLK
    ov = np.minimum(cs + CMP_BLK, ss + SEL_BLK) - np.maximum(cs, ss)
    m = np.clip(ov, 0, CMP_BLK).astype(np.float32) / CMP_BLK
    m[n_cmp:, :] = 0.0
    m[:, n_sel:] = 0.0
    return m


QBLK = 128
SEL_KCHUNK = 512


def _attn_prompt_body(q_ref, ksel_ref, vsel_ref, kwin_ref, vwin_ref, kvc_ref, bg_ref, e_ref, mt_ref, o_ref,
                      *, n_cmp, n_sel):
    qi = pl.program_id(2)
    q0 = qi * QBLK
    q = q_ref[0]
    q4 = jnp.concatenate([q[:, g * HEAD_DIM:(g + 1) * HEAD_DIM] for g in range(GROUP)], axis=0)
    rows = GROUP * QBLK

    def rpos(ncols):
        return q0 + (lax.broadcasted_iota(I32, (rows, ncols), 0) & (QBLK - 1))

    wkeys = WINDOW + QBLK
    ks = pl.multiple_of(jnp.maximum(q0 - WINDOW, 0), QBLK)
    kw = kwin_ref[0, pl.ds(ks, wkeys), :]
    vw = vwin_ref[0, pl.ds(ks, wkeys), :]
    s_w = lax.dot_general(q4, kw, _NT, preferred_element_type=F32) * ATTN_SCALE
    dist = rpos(wkeys) - (ks + lax.broadcasted_iota(I32, (rows, wkeys), 1))
    p_w = _msoftmax(s_w, (dist >= 0) & (dist < WINDOW))
    o_w = jnp.dot(p_w.astype(BF16), vw, preferred_element_type=F32)

    nb = kvc_ref.shape[2]
    kc = kvc_ref[0, 0].astype(BF16)
    vc = kvc_ref[0, 1].astype(BF16)
    s_c = lax.dot_general(q4, kc, _NT, preferred_element_type=F32) * ATTN_SCALE
    ncol = lax.broadcasted_iota(I32, (rows, nb), 1)
    mask_c = (ncol * CMP_STRIDE + (CMP_BLK - 1) <= rpos(nb)) & (ncol < n_cmp)
    p_c = _msoftmax(s_c, mask_c)
    o_c = jnp.dot(p_c.astype(BF16), vc, preferred_element_type=F32)

    psum = p_c[0:QBLK] + p_c[QBLK:2 * QBLK] + p_c[2 * QBLK:3 * QBLK] + p_c[3 * QBLK:4 * QBLK]
    nsp = mt_ref.shape[0]
    imp_t = lax.dot_general(mt_ref[...], psum.astype(BF16), _NT, preferred_element_type=F32)
    jt = lax.broadcasted_iota(I32, (nsp, QBLK), 0)
    cur = lax.shift_right_logical(q0 + lax.broadcasted_iota(I32, (nsp, QBLK), 1), SEL_SHIFT)
    valid = (jt <= cur) & (jt < n_sel)
    forced = (jt == 0) | (jt >= cur - 1)
    score = jnp.where(forced, FORCED_SCORE, imp_t)
    score = jnp.where(valid, score, -1.0)
    rank = jnp.zeros((nsp, QBLK), I32)
    for k in range(n_sel):
        sk = score[k:k + 1, :]
        beats = (sk > score) | ((sk == score) & (k < jt))
        rank = rank + beats.astype(I32)
    sel_t = jnp.where((rank < N_SEL) & valid, 1.0, 0.0).astype(BF16)
    eye = jnp.where(lax.broadcasted_iota(I32, (QBLK, QBLK), 0) == lax.broadcasted_iota(I32, (QBLK, QBLK), 1),
                    1.0, 0.0).astype(BF16)
    sel = lax.dot_general(eye, sel_t, _NT, preferred_element_type=F32).astype(BF16)

    def sel_step(c, carry):
        m_old, l_old, acc = carry
        k0 = pl.multiple_of(c * SEL_KCHUNK, SEL_KCHUNK)
        kk = ksel_ref[0, pl.ds(k0, SEL_KCHUNK), :]
        vv = vsel_ref[0, pl.ds(k0, SEL_KCHUNK), :]
        s = lax.dot_general(q4, kk, _NT, preferred_element_type=F32) * ATTN_SCALE
        km = jnp.dot(sel, e_ref[:, pl.ds(k0, SEL_KCHUNK)], preferred_element_type=F32)
        km4 = jnp.concatenate([km] * GROUP, axis=0)
        kpos = k0 + lax.broadcasted_iota(I32, (rows, SEL_KCHUNK), 1)
        mask = (km4 > 0.5) & (kpos <= rpos(SEL_KCHUNK))
        sm = jnp.where(mask, s, MASK_VALUE)
        m_new = jnp.maximum(m_old, jnp.max(sm, axis=-1, keepdims=True))
        alpha = jnp.exp(m_old - m_new)
        p = jnp.where(mask, jnp.exp(sm - m_new), 0.0)
        l_new = alpha * l_old + jnp.sum(p, axis=-1, keepdims=True)
        acc = alpha * acc + jnp.dot(p.astype(BF16), vv, preferred_element_type=F32)
        return m_new, l_new, acc

    n_chunks = (q0 + QBLK + SEL_KCHUNK - 1) // SEL_KCHUNK
    init = (jnp.full((rows, 1), MASK_VALUE, F32), jnp.zeros((rows, 1), F32), jnp.zeros((rows, HEAD_DIM), F32))
    _, l_s, acc_s = lax.fori_loop(0, n_chunks, sel_step, init)
    o_s = acc_s * jnp.where(l_s > 0.0, 1.0 / l_s, 0.0)

    gates = jax.nn.sigmoid(bg_ref[0])
    for g in range(GROUP):
        r0 = g * QBLK
        og = (gates[:, g:g + 1] * o_c[r0:r0 + QBLK] + gates[:, GROUP + g:GROUP + g + 1] * o_s[r0:r0 + QBLK]
              + gates[:, 2 * GROUP + g:2 * GROUP + g + 1] * o_w[r0:r0 + QBLK])
        o_ref[0, :, g * HEAD_DIM:(g + 1) * HEAD_DIM] = og


def attn_prompt(q3, sel_planes, win_planes, kvc, ubg3, n_cmp):
    b, t, _ = q3.shape
    nb = kvc.shape[2]
    n_sel = -(-t // SEL_BLK)
    nsp = max(8, -(-n_sel // 8) * 8)
    e_mat = (np.arange(t)[None, :] // SEL_BLK == np.arange(nsp)[:, None]).astype(np.float32)
    m_t = _overlap_matrix(n_cmp, nb, n_sel, nsp).T
    kspec = lambda: pl.BlockSpec((1, t, HEAD_DIM), lambda i, h, j: (2 * h, i, 0))
    vspec = lambda: pl.BlockSpec((1, t, HEAD_DIM), lambda i, h, j: (2 * h + 1, i, 0))
    return pl.pallas_call(
        functools.partial(_attn_prompt_body, n_cmp=n_cmp, n_sel=n_sel),
        grid=(b, N_KV_HEADS, t // QBLK),
        in_specs=[
            pl.BlockSpec((1, QBLK, GROUP * HEAD_DIM), lambda i, h, j: (i, j, h)),
            kspec(), vspec(), kspec(), vspec(),
            pl.BlockSpec((1, 2, nb, HEAD_DIM), lambda i, h, j: (i, h, 0, 0)),
            pl.BlockSpec((1, QBLK, LANES), lambda i, h, j: (i, j, h)),
            pl.BlockSpec((nsp, t), lambda i, h, j: (0, 0)),
            pl.BlockSpec((nsp, nb), lambda i, h, j: (0, 0)),
        ],
        out_specs=pl.BlockSpec((1, QBLK, GROUP * HEAD_DIM), lambda i, h, j: (i, j, h)),
        out_shape=jax.ShapeDtypeStruct((b, t, N_HEADS * HEAD_DIM), F32),
        compiler_params=_params(("arbitrary", "arbitrary", "arbitrary"), 56),
        name="attn_prompt",
    )(q3, sel_planes, sel_planes, win_planes, win_planes, kvc, ubg3, jnp.asarray(e_mat, BF16),
      jnp.asarray(m_t, BF16))


def _row_head(shape):
    return lax.shift_right_logical(lax.broadcasted_iota(I32, shape, 0), 2)


def _cmp_sample_body(q_ref, kvc_ref, m_ref, oc_ref, imp_ref, *, qpos, n_cmp):
    qb = q_ref[0].astype(BF16)
    nb = kvc_ref.shape[2]
    rg = _row_head((N_HEADS, nb))
    ncol = lax.broadcasted_iota(I32, (N_HEADS, nb), 1)
    mask = (ncol * CMP_STRIDE + (CMP_BLK - 1) <= qpos) & (ncol < n_cmp)
    row8 = lax.broadcasted_iota(I32, (8, nb), 0)
    o = jnp.zeros((N_HEADS, HEAD_DIM), F32)
    ps8 = jnp.zeros((8, nb), F32)
    for h in range(N_KV_HEADS):
        kc = kvc_ref[0, 2 * h].astype(BF16)
        vc = kvc_ref[0, 2 * h + 1].astype(BF16)
        s = lax.dot_general(qb, kc, _NT, preferred_element_type=F32) * ATTN_SCALE
        ph = jnp.where(rg == h, _msoftmax(s, mask), 0.0)
        o = o + jnp.dot(ph.astype(BF16), vc, preferred_element_type=F32)
        ps8 = jnp.where(row8 == h, jnp.sum(ph, axis=0, keepdims=True), ps8)
    oc_ref[0] = o
    imp_ref[0] = jnp.dot(ps8.astype(BF16), m_ref[...], preferred_element_type=F32)[0:N_KV_HEADS]


def cmp_sample(q16, kvc, m_mat, qpos, n_cmp):
    ns = q16.shape[0]
    nb = kvc.shape[2]
    nsp = m_mat.shape[1]
    return pl.pallas_call(
        functools.partial(_cmp_sample_body, qpos=qpos, n_cmp=n_cmp),
        grid=(ns,),
        in_specs=[pl.BlockSpec((1, N_HEADS, HEAD_DIM), lambda b: (b, 0, 0)),
                  pl.BlockSpec((1, HK, nb, HEAD_DIM), lambda b: (b, 0, 0, 0)),
                  pl.BlockSpec((nb, nsp), lambda b: (0, 0))],
        out_specs=[pl.BlockSpec((1, N_HEADS, HEAD_DIM), lambda b: (b, 0, 0)),
                   pl.BlockSpec((1, N_KV_HEADS, nsp), lambda b: (b, 0, 0))],
        out_shape=[jax.ShapeDtypeStruct((ns, N_HEADS, HEAD_DIM), F32),
                   jax.ShapeDtypeStruct((ns, N_KV_HEADS, nsp), F32)],
        compiler_params=_params(("arbitrary",), 40),
        name="cmp_sample",
    )(q16, kvc, m_mat)


def _rank_sample_body(imp_ref, idx_ref, sc_scr, *, cur, n_sel):
    npad, nr = sc_scr.shape
    st = imp_ref[...].T
    j = lax.broadcasted_iota(I32, (npad, nr), 0)
    real = j < n_sel
    valid = (j <= cur) & real
    forced = (j == 0) | (j >= cur - 1)
    score = jnp.where(forced, FORCED_SCORE, st)
    score = jnp.where(valid, score, -1.0)
    score = jnp.where(real, score, -2.0)
    sc_scr[...] = score

    def body(k, rank):
        sk = sc_scr[pl.ds(k, 1), :]
        beats = (sk > score) | ((sk == score) & (k < j))
        return rank + beats.astype(I32)

    rank = lax.fori_loop(0, n_sel, body, jnp.zeros((npad, nr), I32))
    sel = (rank < N_SEL) & valid
    for slot in range(N_SEL):
        hit = sel & (rank == slot)
        found = jnp.max(hit.astype(I32), axis=0, keepdims=True)
        val = jnp.sum(jnp.where(hit, j, 0), axis=0, keepdims=True)
        idx_ref[slot:slot + 1, :] = jnp.where(found > 0, val, -1)


def rank_sample(imp2, cur, n_sel):
    nr, npad = imp2.shape
    return pl.pallas_call(
        functools.partial(_rank_sample_body, cur=cur, n_sel=n_sel),
        out_shape=jax.ShapeDtypeStruct((N_SEL, nr), I32),
        scratch_shapes=[pltpu.VMEM((npad, nr), F32)],
        name="rank_sample",
    )(imp2)


def _sel_sample_body(idx_ref, pt_ref, q_ref, kn_ref, vn_ref, *rest, n_past_blocks):
    del pt_ref
    kvs = rest[:N_SEL * N_KV_HEADS]
    o_ref = rest[N_SEL * N_KV_HEADS]
    b = pl.program_id(0)
    q = q_ref[0]
    qb = q.astype(BF16)
    nk = N_SEL * SEL_BLK
    rg = _row_head((N_HEADS, nk))
    col_slot = lax.shift_right_logical(lax.broadcasted_iota(I32, (N_HEADS, nk), 1), SEL_SHIFT)

    sc = jnp.zeros((N_HEADS, nk), F32)
    okv = jnp.zeros((N_HEADS, nk), I32)
    for h in range(N_KV_HEADS):
        kcat = jnp.concatenate([kvs[s * N_KV_HEADS + h][:, 0, 0, :] for s in range(N_SEL)], axis=0).astype(BF16)
        sh = lax.dot_general(qb, kcat, _NT, preferred_element_type=F32)
        ok = jnp.zeros((N_HEADS, nk), I32)
        for s in range(N_SEL):
            blk = idx_ref[s, b * N_KV_HEADS + h]
            ok = jnp.where(col_slot == s, jnp.where((blk >= 0) & (blk < n_past_blocks), 1, 0), ok)
        sc = jnp.where(rg == h, sh, sc)
        okv = jnp.where(rg == h, ok, okv)
    sm = jnp.where(okv > 0, sc * ATTN_SCALE, MASK_VALUE)
    s_new = jnp.sum(q * kn_ref[0], axis=-1, keepdims=True) * ATTN_SCALE
    m = jnp.maximum(jnp.max(sm, axis=-1, keepdims=True), s_new)
    p = jnp.where(okv > 0, jnp.exp(sm - m), 0.0)
    p_new = jnp.exp(s_new - m)
    den = jnp.sum(p, axis=-1, keepdims=True) + p_new
    pv = jnp.zeros((N_HEADS, HEAD_DIM), F32)
    for h in range(N_KV_HEADS):
        vcat = jnp.concatenate([kvs[s * N_KV_HEADS + h][:, 0, 1, :] for s in range(N_SEL)], axis=0).astype(BF16)
        pv = pv + jnp.dot(jnp.where(rg == h, p, 0.0).astype(BF16), vcat, preferred_element_type=F32)
    o_ref[0] = (pv + p_new * vn_ref[0]) / den


def sel_sample(idx, page_table, q16, kn16, vn16, pool4, n_past_blocks):
    ns = q16.shape[0]
    blk_per_page = PAGE_SIZE // SEL_BLK

    def kv_spec(s, h):
        def imap(b, idx_r, pt_r):
            blk = jnp.clip(idx_r[s, b * N_KV_HEADS + h], 0, n_past_blocks - 1)
            return (pt_r[b, blk // blk_per_page] * blk_per_page + blk % blk_per_page, h, 0, 0)
        return pl.BlockSpec((SEL_BLK, 1, 2, HEAD_DIM), imap)

    head3 = lambda: pl.BlockSpec((1, N_HEADS, HEAD_DIM), lambda b, i, p: (b, 0, 0))
    kv_specs = [kv_spec(s, h) for s in range(N_SEL) for h in range(N_KV_HEADS)]
    grid_spec = pltpu.PrefetchScalarGridSpec(
        num_scalar_prefetch=2,
        grid=(ns,),
        in_specs=[head3(), head3(), head3()] + kv_specs,
        out_specs=head3(),
    )
    return pl.pallas_call(
        functools.partial(_sel_sample_body, n_past_blocks=n_past_blocks),
        grid_spec=grid_spec,
        out_shape=jax.ShapeDtypeStruct((ns, N_HEADS, HEAD_DIM), F32),
        compiler_params=_params(("arbitrary",), 40),
        name="sel_sample",
    )(idx, page_table, q16, kn16, vn16, *([pool4] * len(kv_specs)))


def _win_sample_body(q_ref, win_ref, kn_ref, vn_ref, oc_ref, os_ref, bg_ref, o_ref, *, wbuf):
    q = q_ref[0]
    qb = q.astype(BF16)
    rg = _row_head((N_HEADS, wbuf))
    sc = jnp.zeros((N_HEADS, wbuf), F32)
    for h in range(N_KV_HEADS):
        kh = win_ref[pl.ds(2 * h, wbuf, stride=HK), :].astype(BF16)
        sc = jnp.where(rg == h, lax.dot_general(qb, kh, _NT, preferred_element_type=F32), sc)
    mask = lax.broadcasted_iota(I32, (N_HEADS, wbuf), 1) > wbuf - WINDOW
    sm = jnp.where(mask, sc * ATTN_SCALE, MASK_VALUE)
    s_new = jnp.sum(q * kn_ref[0], axis=-1, keepdims=True) * ATTN_SCALE
    m = jnp.maximum(jnp.max(sm, axis=-1, keepdims=True), s_new)
    p = jnp.where(mask, jnp.exp(sm - m), 0.0)
    p_new = jnp.exp(s_new - m)
    den = jnp.sum(p, axis=-1, keepdims=True) + p_new
    pv = jnp.zeros((N_HEADS, HEAD_DIM), F32)
    for h in range(N_KV_HEADS):
        vh = win_ref[pl.ds(2 * h + 1, wbuf, stride=HK), :].astype(BF16)
        pv = pv + jnp.dot(jnp.where(rg == h, p, 0.0).astype(BF16), vh, preferred_element_type=F32)
    o_w = (pv + p_new * vn_ref[0]) / den
    gates = jax.nn.sigmoid(bg_ref[0])
    o_ref[0] = gates[:, 0:1] * oc_ref[0] + gates[:, 1:2] * os_ref[0] + gates[:, 2:3] * o_w


def win_sample(q16, win2d, kn16, vn16, o_cmp, o_sel, bgs, wbuf):
    ns = q16.shape[0]
    head3 = lambda: pl.BlockSpec((1, N_HEADS, HEAD_DIM), lambda b: (b, 0, 0))
    return pl.pallas_call(
        functools.partial(_win_sample_body, wbuf=wbuf),
        grid=(ns,),
        in_specs=[head3(), pl.BlockSpec((wbuf * HK, LANES), lambda b: (b, 0)), head3(), head3(), head3(), head3(),
                  head3()],
        out_specs=head3(),
        out_shape=jax.ShapeDtypeStruct((ns, N_HEADS, HEAD_DIM), F32),
        compiler_params=_params(("arbitrary",), 40),
        name="win_sample",
    )(q16, win2d, kn16, vn16, o_cmp, o_sel, bgs)


ROUTE_LANE0 = N_GROUPS


INFO_E1, INFO_E2, INFO_R1, INFO_R2, INFO_C1, INFO_C2 = range(6)


def _route_body(x_ref, g_ref, wr_ref, br_ref, xn_ref, info_ref, cnt_ref, run_scr, *, n_real):
    step = pl.program_id(0)
    tm = x_ref.shape[0]

    @pl.when(step == 0)
    def _():
        run_scr[...] = jnp.zeros(run_scr.shape, F32)

    x = x_ref[...]
    var = jnp.mean(x * x, axis=-1, keepdims=True)
    xn = (x * lax.rsqrt(var + EPS)) * g_ref[...]
    xn_ref[...] = xn
    xnb = xn.astype(BF16)
    logits = jnp.dot(xnb, wr_ref[...], preferred_element_type=F32) + br_ref[...]
    lane = lax.broadcasted_iota(I32, logits.shape, 1)
    big = 4 * LANES

    isg = lane < N_GROUPS
    lg = jnp.where(isg, logits, -jnp.inf)
    mg = jnp.max(lg, axis=-1, keepdims=True)
    gi = jnp.min(jnp.where(lg == mg, lane, big), axis=-1, keepdims=True)
    pg = 1.0 / jnp.sum(jnp.where(isg, jnp.exp(lg - mg), 0.0), axis=-1, keepdims=True)

    lo = ROUTE_LANE0 + gi * EXP_PER_GROUP
    ing = (lane >= lo) & (lane < lo + EXP_PER_GROUP)
    le = jnp.where(ing, logits, -jnp.inf)
    me = jnp.max(le, axis=-1, keepdims=True)
    ee = jnp.where(ing, jnp.exp(le - me), 0.0)
    pe = jnp.where(ing, ee / jnp.sum(ee, axis=-1, keepdims=True), -1.0)
    p1 = jnp.max(pe, axis=-1, keepdims=True)
    i1 = jnp.min(jnp.where(pe == p1, lane, big), axis=-1, keepdims=True)
    pe2 = jnp.where(lane == i1, -1.0, pe)
    p2 = jnp.max(pe2, axis=-1, keepdims=True)
    i2 = jnp.min(jnp.where(pe2 == p2, lane, big), axis=-1, keepdims=True)
    tot = p1 + p2
    c1 = pg * (p1 / tot)
    c2 = pg * (p2 / tot)

    row = step * tm + lax.broadcasted_iota(I32, logits.shape, 0)
    hit = jnp.where(((lane == i1) | (lane == i2)) & (row < n_real), 1.0, 0.0)
    tri = jnp.where(lax.broadcasted_iota(I32, (tm, tm), 1) < lax.broadcasted_iota(I32, (tm, tm), 0), 1.0, 0.0)
    before = run_scr[...] + jnp.dot(tri.astype(BF16), hit.astype(BF16), preferred_element_type=F32)
    r1 = jnp.sum(jnp.where(lane == i1, before, 0.0), axis=-1, keepdims=True)
    r2 = jnp.sum(jnp.where(lane == i2, before, 0.0), axis=-1, keepdims=True)
    run_new = run_scr[...] + jnp.sum(hit, axis=0, keepdims=True)
    run_scr[...] = run_new
    cnt_ref[...] = run_new

    info = jnp.zeros(logits.shape, F32)
    for ln, val in ((INFO_E1, (i1 - ROUTE_LANE0).astype(F32)), (INFO_E2, (i2 - ROUTE_LANE0).astype(F32)),
                    (INFO_R1, r1), (INFO_R2, r2), (INFO_C1, c1), (INFO_C2, c2)):
        info = jnp.where(lane == ln, val, info)
    info_ref[...] = info


def route(x, g, wr, br, tm, n_real):
    m, d = x.shape
    return pl.pallas_call(
        functools.partial(_route_body, n_real=n_real),
        grid=(m // tm,),
        in_specs=[pl.BlockSpec((tm, d), lambda i: (i, 0)), pl.BlockSpec((1, d), lambda i: (0, 0)),
                  pl.BlockSpec((d, LANES), lambda i: (0, 0)), pl.BlockSpec((1, LANES), lambda i: (0, 0))],
        out_specs=[pl.BlockSpec((tm, d), lambda i: (i, 0)), pl.BlockSpec((tm, LANES), lambda i: (i, 0)),
                   pl.BlockSpec((1, LANES), lambda i: (0, 0))],
        out_shape=[jax.ShapeDtypeStruct((m, d), F32), jax.ShapeDtypeStruct((m, LANES), F32),
                   jax.ShapeDtypeStruct((1, LANES), F32)],
        scratch_shapes=[pltpu.VMEM((1, LANES), F32)],
        compiler_params=_params(("arbitrary",), 40),
        name="route",
    )(x, g.reshape(1, d), wr, br)


MOE_TILE = 256


def _moe_plan(info, counts_row, n_real, n_tiles):
    n_pad = info.shape[0]
    e = jnp.clip(info[:, INFO_E1:INFO_E2 + 1].astype(I32), 0, N_EXPERTS - 1)
    r = info[:, INFO_R1:INFO_R2 + 1].astype(I32)
    counts = counts_row[0, ROUTE_LANE0:ROUTE_LANE0 + N_EXPERTS].astype(I32)
    padded = ((counts + MOE_TILE - 1) // MOE_TILE) * MOE_TILE
    ends = jnp.cumsum(padded)
    offs = ends - padded
    tok = jnp.arange(n_pad, dtype=I32)
    valid = (tok < n_real)[:, None]
    dest = jnp.where(valid, offs[e] + r, 0)
    n_slots = n_tiles * MOE_TILE
    src = jnp.zeros((n_slots,), I32).at[jnp.where(valid, dest, n_slots).reshape(-1)].set(
        jnp.repeat(tok, 2), mode="drop")
    n_used = ends[-1] // MOE_TILE
    tiles = jnp.arange(n_tiles, dtype=I32)
    te = jnp.minimum(jnp.searchsorted(ends, tiles * MOE_TILE, side="right").astype(I32), N_EXPERTS - 1)
    te = jnp.where(tiles < n_used, te, jnp.take(te, jnp.maximum(n_used - 1, 0)))
    tile3 = lambda v: v.reshape(-1, 1, MOE_TILE)
    return tile3(src), tile3(dest[:, 0]), tile3(dest[:, 1]), te, n_used.reshape(1)


def _issue_rows(idx_ref, src_hbm, dst_buf, slot, sem):
    n = idx_ref.shape[2]

    def body(r, carry):
        pltpu.make_async_copy(src_hbm.at[pl.ds(idx_ref[0, 0, r], 1), :], dst_buf.at[slot, pl.ds(r, 1), :],
                              sem.at[slot]).start()
        return carry

    lax.fori_loop(0, n, body, 0, unroll=8)


def _wait_rows(src_hbm, dst_buf, slot, sem):
    n = dst_buf.shape[1]
    pltpu.make_async_copy(src_hbm.at[pl.ds(0, n), :], dst_buf.at[slot], sem.at[slot]).wait()


def _moe_expert_body(te_ref, nu_ref, src_cur, src_nxt, xn_hbm, wg_ref, wu_ref, wd_ref, y_ref,
                     xbuf, sem, wgb, wub, wdb):
    i = pl.program_id(0)
    n_used = nu_ref[0]
    slot = lax.rem(i, 2)

    @pl.when(i == 0)
    def _():
        _issue_rows(src_cur, xn_hbm, xbuf, 0, sem)

    @pl.when(i + 1 < n_used)
    def _():
        _issue_rows(src_nxt, xn_hbm, xbuf, 1 - slot, sem)

    @pl.when(i < n_used)
    def _():
        _wait_rows(xn_hbm, xbuf, slot, sem)

        @pl.when((i == 0) | (te_ref[i] != te_ref[jnp.maximum(i - 1, 0)]))
        def _():
            wgb[...] = wg_ref[0].astype(BF16)
            wub[...] = wu_ref[0].astype(BF16)
            wdb[...] = wd_ref[0].astype(BF16)

        x = xbuf[slot].astype(BF16)
        hg = jnp.dot(x, wgb[...], preferred_element_type=F32)
        hu = jnp.dot(x, wub[...], preferred_element_type=F32)
        hid = (hg * jax.nn.sigmoid(hg)) * hu
        y_ref[...] = jnp.dot(hid.astype(BF16), wdb[...], preferred_element_type=F32)

    @pl.when(i >= n_used)
    def _():
        y_ref[...] = jnp.zeros(y_ref.shape, F32)


def moe_experts(tile_expert, n_used, src3, xn, wg, wu, wd):
    n_tiles = src3.shape[0]
    d = xn.shape[1]
    ne, _, de = wg.shape
    smem_tile = lambda imap: pl.BlockSpec((1, 1, MOE_TILE), imap, memory_space=pltpu.SMEM)
    grid_spec = pltpu.PrefetchScalarGridSpec(
        num_scalar_prefetch=2,
        grid=(n_tiles,),
        in_specs=[smem_tile(lambda i, te, nu: (i, 0, 0)),
                  smem_tile(lambda i, te, nu: (jnp.minimum(i + 1, n_tiles - 1), 0, 0)),
                  pl.BlockSpec(memory_space=pl.ANY),
                  pl.BlockSpec((1, d, de), lambda i, te, nu: (te[i], 0, 0)),
                  pl.BlockSpec((1, d, de), lambda i, te, nu: (te[i], 0, 0)),
                  pl.BlockSpec((1, de, d), lambda i, te, nu: (te[i], 0, 0))],
        out_specs=pl.BlockSpec((MOE_TILE, d), lambda i, te, nu: (i, 0)),
        scratch_shapes=[pltpu.VMEM((2, MOE_TILE, d), F32), pltpu.SemaphoreType.DMA((2,)),
                        pltpu.VMEM((d, de), BF16), pltpu.VMEM((d, de), BF16), pltpu.VMEM((de, d), BF16)],
    )
    return pl.pallas_call(
        _moe_expert_body,
        grid_spec=grid_spec,
        out_shape=jax.ShapeDtypeStruct((n_tiles * MOE_TILE, d), F32),
        compiler_params=_params(("arbitrary",), 56),
        name="moe_experts",
    )(tile_expert, n_used, src3, src3, xn, wg, wu, wd)


def _moe_combine_body(d1_cur, d1_nxt, d2_cur, d2_nxt, ys_hbm, x_ref, info_ref, gf_ref, y_ref, buf1, buf2, sem1, sem2):
    i = pl.program_id(0)
    slot = lax.rem(i, 2)

    @pl.when(i == 0)
    def _():
        _issue_rows(d1_cur, ys_hbm, buf1, 0, sem1)
        _issue_rows(d2_cur, ys_hbm, buf2, 0, sem2)

    @pl.when(i + 1 < pl.num_programs(0))
    def _():
        _issue_rows(d1_nxt, ys_hbm, buf1, 1 - slot, sem1)
        _issue_rows(d2_nxt, ys_hbm, buf2, 1 - slot, sem2)

    _wait_rows(ys_hbm, buf1, slot, sem1)
    _wait_rows(ys_hbm, buf2, slot, sem2)
    info = info_ref[...]
    xo = x_ref[...] + (info[:, INFO_C1:INFO_C1 + 1] * buf1[slot] + info[:, INFO_C2:INFO_C2 + 1] * buf2[slot])
    var = jnp.mean(xo * xo, axis=-1, keepdims=True)
    y_ref[...] = (xo * lax.rsqrt(var + EPS)) * gf_ref[...]


def moe_combine(dest1, dest2, ys, x, info, g_final):
    m, d = x.shape
    n_tiles = m // MOE_TILE
    cur = lambda: pl.BlockSpec((1, 1, MOE_TILE), lambda i: (i, 0, 0), memory_space=pltpu.SMEM)
    nxt = lambda: pl.BlockSpec((1, 1, MOE_TILE), lambda i: (jnp.minimum(i + 1, n_tiles - 1), 0, 0),
                               memory_space=pltpu.SMEM)
    return pl.pallas_call(
        _moe_combine_body,
        grid=(n_tiles,),
        in_specs=[cur(), nxt(), cur(), nxt(), pl.BlockSpec(memory_space=pl.ANY),
                  pl.BlockSpec((MOE_TILE, d), lambda i: (i, 0)), pl.BlockSpec((MOE_TILE, LANES), lambda i: (i, 0)),
                  pl.BlockSpec((1, d), lambda i: (0, 0))],
        out_specs=pl.BlockSpec((MOE_TILE, d), lambda i: (i, 0)),
        out_shape=jax.ShapeDtypeStruct((m, d), F32),
        scratch_shapes=[pltpu.VMEM((2, MOE_TILE, d), F32), pltpu.VMEM((2, MOE_TILE, d), F32),
                        pltpu.SemaphoreType.DMA((2,)), pltpu.SemaphoreType.DMA((2,))],
        compiler_params=_params(("arbitrary",), 40),
        name="moe_combine",
    )(dest1, dest1, dest2, dest2, ys, x, info, g_final.reshape(1, d))


def _mixer_inputs(x2, g_mix, w_in, w_bg, w_mg, tm, q_dtype):
    tn = 1024
    xn = rmsnorm(x2, g_mix, tm, BF16)
    u_rnn = matmul_cols(xn, w_in, 0, D_RNN, tm, tn, F32, name="in_rnn")
    u_gate = matmul_cols(xn, w_in, D_RNN, D_RNN, tm, tn, F32, name="in_gate")
    q = matmul_cols(xn, w_in, 2 * D_RNN, N_HEADS * HEAD_DIM, tm, tn, q_dtype, name="in_q")
    kv0 = 2 * D_RNN + N_HEADS * HEAD_DIM
    kv_cmp, = matmul_kv(xn, w_in, kv0, tm, False, "in_kv_cmp")
    kv_sel, sel_planes = matmul_kv(xn, w_in, kv0 + 2 * KV_DIM, tm, True, "in_kv_sel")
    kv_win, win_planes = matmul_kv(xn, w_in, kv0 + 4 * KV_DIM, tm, True, "in_kv_win")
    u_bg = matmul_cols(xn, w_bg, 0, w_bg.shape[1], tm, w_bg.shape[1], F32, name="in_bg")
    u_mg = matmul_cols(xn, w_mg, 0, 2 * D_MODEL, tm, tn, F32, name="in_mg")
    return u_rnn, u_gate, q, (kv_cmp, kv_sel, kv_win), (sel_planes, win_planes), u_bg, u_mg


def kernel(x_prompt, x_sample, cache_cmp_kv, cache_sel_kv, cache_win_kv, state_conv, state_rglru, page_table, g_mix, w_in, conv_w, conv_b, lru_w_a, lru_b_a, lru_w_x, lru_b_x, lru_lambda, cmp_pe, cmp_w1, cmp_b1, cmp_w2, cmp_b2, w_out, g_ffn, moe_w_group, moe_b_group, moe_w_expert, moe_b_expert, moe_w_gate, moe_w_up, moe_w_down, g_final):
    b, t, d = x_prompt.shape
    ns = x_sample.shape[0]
    npages = page_table.shape[1]
    past_len = npages * PAGE_SIZE
    wbuf = cache_win_kv.shape[1]
    kvshape = (N_KV_HEADS, 2, HEAD_DIM)

    bg0 = 2 * D_RNN + N_HEADS * HEAD_DIM + 6 * KV_DIM
    n_bg = 3 * N_HEADS
    w_bg = w_in[:, bg0:bg0 + n_bg].reshape(d, 3, N_KV_HEADS, GROUP).transpose(0, 2, 1, 3).reshape(d, N_KV_HEADS, 3 * GROUP)
    w_bg = jnp.pad(w_bg, ((0, 0), (0, 0), (0, LANES - 3 * GROUP))).reshape(d, N_KV_HEADS * LANES)
    w_mg = w_in[:, bg0 + n_bg:]
    wa = lru_w_a.astype(BF16)
    wx = lru_w_x.astype(BF16)
    half_rows = CMP_STRIDE * HEAD_DIM
    wc = jnp.concatenate([cmp_w1[:, :half_rows], cmp_w1[:, half_rows:]], axis=-1).astype(BF16)
    pe8 = jnp.pad(cmp_pe.reshape(2, 2, half_rows), ((0, 0), (0, 6), (0, 0)))
    b1 = cmp_b1.reshape(2, 1, CMP_HIDDEN)
    w2 = cmp_w2.astype(BF16)
    b2 = cmp_b2.reshape(2, 1, HEAD_DIM)
    n_route = N_GROUPS + N_EXPERTS
    wr = jnp.pad(jnp.concatenate([moe_w_group, moe_w_expert], axis=1), ((0, 0), (0, LANES - n_route))).astype(BF16)
    br = jnp.pad(jnp.concatenate([moe_b_group, moe_b_expert]), (0, LANES - n_route)).reshape(1, LANES)
    lru_args = (conv_w, conv_b, wa, wx, lru_b_a, lru_b_x, lru_lambda)
    cmp_args = (wc, pe8, b1, w2, b2)

    xp2 = x_prompt.reshape(b * t, d)
    u_rnn, u_gate, q, kv2d, planes, u_bg, u_mg = _mixer_inputs(xp2, g_mix, w_in, w_bg, w_mg, 512, BF16)
    o_rnn, p_conv, p_h = lru_prompt(u_rnn.reshape(b, t, d), u_gate.reshape(b, t, d), *lru_args, 256)
    p_cmp, p_sel, p_win_full = (a.reshape((b, t) + kvshape) for a in kv2d)
    p_win = p_win_full[:, t - min(WINDOW, t):]
    pt_prompt = jnp.arange(b * t // PAGE_SIZE, dtype=I32).reshape(b, t // PAGE_SIZE)
    kvc_p = compress(kv2d[0], pt_prompt, jnp.zeros((b * CMP_STRIDE * HK, HEAD_DIM), F32), *cmp_args)
    o_nsa = attn_prompt(q.reshape(b, t, -1), planes[0], planes[1], kvc_p, u_bg.reshape(b, t, -1),
                        t // CMP_STRIDE - 1)
    merged = merge(u_mg, o_rnn.reshape(b * t, d), o_nsa.reshape(b * t, d), 512)
    x_mid = matmul_cols(merged, w_out, 0, d, 512, 1024, F32, residual=xp2, name="out_proj")

    xs2 = x_sample.reshape(ns, d)
    su_rnn, su_gate, sq, skv2d, _, su_bg, su_mg = _mixer_inputs(xs2, g_mix, w_in, w_bg, w_mg, ns, F32)
    so_rnn, cn, s_h = lru_sample(su_rnn, su_gate, state_conv.transpose(1, 0, 2), state_rglru, *lru_args)
    s_conv = cn.transpose(1, 0, 2)
    kvs = jnp.stack([a.reshape(ns, HK, HEAD_DIM) for a in skv2d], axis=1)
    s_cmp = kvs[:, 0].reshape((ns, 1) + kvshape)
    s_sel = kvs[:, 1].reshape((ns, 1) + kvshape)
    s_win = jnp.concatenate([cache_win_kv[:, 1:], kvs[:, 2].reshape((ns, 1) + kvshape)], axis=1)

    extra = jnp.concatenate([kvs[:, 0], jnp.zeros((ns, (CMP_STRIDE - 1) * HK, HEAD_DIM), F32)], axis=1)
    kvc_s = compress(cache_cmp_kv.reshape(-1, HEAD_DIM), page_table, extra.reshape(-1, HEAD_DIM), *cmp_args)
    n_cmp_s = -(-(past_len + 1) // CMP_STRIDE) - 1
    n_sel_s = -(-(past_len + 1) // SEL_BLK)
    nsp = -(-n_sel_s // LANES) * LANES
    q16 = sq.reshape(ns, N_HEADS, HEAD_DIM)
    m_s = jnp.asarray(_overlap_matrix(n_cmp_s, kvc_s.shape[2], n_sel_s, nsp), BF16)
    so_cmp, imp = cmp_sample(q16, kvc_s, m_s, past_len, n_cmp_s)
    idx = rank_sample(imp.reshape(ns * N_KV_HEADS, nsp), past_len // SEL_BLK, n_sel_s)

    def per_head_rows(new_kv, kv):
        return jnp.repeat(new_kv[:, kv::2], GROUP, axis=1)

    so_sel = sel_sample(idx, page_table, q16, per_head_rows(kvs[:, 1], 0), per_head_rows(kvs[:, 1], 1),
                        cache_sel_kv.reshape((-1,) + kvshape), past_len // SEL_BLK)
    bgs = su_bg.reshape(ns, N_KV_HEADS, LANES)[:, :, :3 * GROUP].reshape(ns, N_KV_HEADS, 3, GROUP)
    bgs = jnp.pad(bgs.transpose(0, 1, 3, 2).reshape(ns, N_HEADS, 3), ((0, 0), (0, 0), (0, LANES - 3)))
    so_nsa = win_sample(q16, cache_win_kv.reshape(-1, HEAD_DIM), per_head_rows(kvs[:, 2], 0),
                        per_head_rows(kvs[:, 2], 1), so_cmp, so_sel, bgs, wbuf)
    s_merged = merge(su_mg, so_rnn, so_nsa.reshape(ns, d), ns)
    sx_mid = matmul_cols(s_merged, w_out, 0, d, ns, 1024, F32, residual=xs2, name="out_proj_s")

    n_real = b * t + ns
    n_pad = -(-n_real // MOE_TILE) * MOE_TILE
    x_all = jnp.concatenate([x_mid, sx_mid, jnp.zeros((n_pad - n_real, d), F32)], axis=0)
    xn_all, info, counts = route(x_all, g_ffn, wr, br, MOE_TILE, n_real)
    n_tiles = -(-(2 * n_real + N_EXPERTS * (MOE_TILE - 1)) // MOE_TILE)
    src3, dest1, dest2, tile_expert, n_used = _moe_plan(info, counts, n_real, n_tiles)
    ys = moe_experts(tile_expert, n_used, src3, xn_all, moe_w_gate, moe_w_up, moe_w_down)
    y_all = moe_combine(dest1, dest2, ys, x_all, info, g_final)
    y_prompt = y_all[:b * t].reshape(b, t, d)
    y_sample = y_all[b * t:n_real].reshape(ns, 1, d)

    return (y_prompt, y_sample, p_cmp, s_cmp, p_sel, s_sel, p_win, s_win, p_conv, s_conv,
            p_h.reshape(b, d), s_h)
```

```python
import functools

import jax
import jax.numpy as jnp
import numpy as np
from jax import lax
from jax.experimental import pallas as pl
from jax.experimental.pallas import tpu as pltpu

F32 = jnp.float32
BF16 = jnp.bfloat16
I32 = jnp.int32

D_MODEL = 2048
D_RNN = 2048
RNN_BLOCKS = 16
RNN_BLOCK_DIM = 128
CONV_W = 4
LRU_C = 8.0
N_HEADS = 16
HEAD_DIM = 128
N_KV_HEADS = 4
GROUP = 4
KV_DIM = N_KV_HEADS * HEAD_DIM
HK = 2 * N_KV_HEADS
CMP_STRIDE = 16
CMP_BLK = 32
CMP_HIDDEN = 256
SEL_BLK = 64
SEL_SHIFT = 6
N_SEL = 16
WINDOW = 512
PAGE_SIZE = 128
ATTN_SCALE = HEAD_DIM ** -0.5
FORCED_SCORE = 1e4
N_GROUPS = 4
EXP_PER_GROUP = 8
N_EXPERTS = 32
D_EXPERT = 512
EPS = 1e-6
MASK_VALUE = -1e30
LANES = 128
MIB = 1024 * 1024

_NT = (((1,), (1,)), ((), ()))


def _params(sem, vmem_mib):
    return pltpu.CompilerParams(dimension_semantics=sem, vmem_limit_bytes=vmem_mib * MIB)


def _msoftmax(s, mask):
    sm = jnp.where(mask, s, MASK_VALUE)
    m = jnp.max(sm, axis=-1, keepdims=True)
    e = jnp.where(mask, jnp.exp(sm - m), 0.0)
    den = jnp.sum(e, axis=-1, keepdims=True)
    return e * jnp.where(den > 0.0, 1.0 / den, 0.0)


def _gelu_tanh(x):
    return 0.5 * x * (1.0 + jnp.tanh(0.7978845608028654 * (x + 0.044715 * (x * x * x))))


def _rmsnorm_body(x_ref, g_ref, o_ref):
    x = x_ref[...]
    var = jnp.mean(x * x, axis=-1, keepdims=True)
    o_ref[...] = ((x * lax.rsqrt(var + EPS)) * g_ref[...]).astype(o_ref.dtype)


def rmsnorm(x, g, tm, out_dtype):
    m, d = x.shape
    return pl.pallas_call(
        _rmsnorm_body,
        grid=(m // tm,),
        in_specs=[pl.BlockSpec((tm, d), lambda i: (i, 0)), pl.BlockSpec((1, d), lambda i: (0, 0))],
        out_specs=pl.BlockSpec((tm, d), lambda i: (i, 0)),
        out_shape=jax.ShapeDtypeStruct((m, d), out_dtype),
        compiler_params=_params(("arbitrary",), 40),
        name="rmsnorm",
    )(x, g.reshape(1, d))


def _xw(x, wbf, w_is_t):
    if w_is_t:
        return lax.dot_general(x, wbf, _NT, preferred_element_type=F32)
    return jnp.dot(x, wbf, preferred_element_type=F32)


def _mm_body(x_ref, w_ref, *rest, w_is_t):
    r_ref = rest[0] if len(rest) == 3 else None
    o_ref, wbf_ref = rest[-2:]

    @pl.when(pl.program_id(1) == 0)
    def _():
        wbf_ref[...] = w_ref[...].astype(BF16)

    res = _xw(x_ref[...], wbf_ref[...], w_is_t)
    if r_ref is not None:
        res = r_ref[...] + res
    o_ref[...] = res.astype(o_ref.dtype)


def matmul_cols(x, w, col0, ncols, tm, tn, out_dtype, residual=None, w_is_t=False, name="matmul"):
    m, k = x.shape
    cb0 = col0 // tn
    assert col0 % tn == 0 and ncols % tn == 0 and m % tm == 0
    if w_is_t:
        w_spec = pl.BlockSpec((tn, k), lambda j, i: (cb0 + j, 0))
        w_tile = (tn, k)
    else:
        w_spec = pl.BlockSpec((k, tn), lambda j, i: (0, cb0 + j))
        w_tile = (k, tn)
    in_specs = [pl.BlockSpec((tm, k), lambda j, i: (i, 0)), w_spec]
    args = [x, w]
    if residual is not None:
        in_specs.append(pl.BlockSpec((tm, tn), lambda j, i: (i, j)))
        args.append(residual)
    return pl.pallas_call(
        functools.partial(_mm_body, w_is_t=w_is_t),
        grid=(ncols // tn, m // tm),
        in_specs=in_specs,
        out_specs=pl.BlockSpec((tm, tn), lambda j, i: (i, j)),
        out_shape=jax.ShapeDtypeStruct((m, ncols), out_dtype),
        scratch_shapes=[pltpu.VMEM(w_tile, BF16)],
        compiler_params=_params(("arbitrary", "arbitrary"), 48),
        name=name,
    )(*args)


def _mm_kv_body(x_ref, w_ref, oi_ref, *rest):
    wbf_ref = rest[-1]

    @pl.when(pl.program_id(0) == 0)
    def _():
        wbf_ref[...] = w_ref[...].astype(BF16)

    res = _xw(x_ref[...], wbf_ref[...], True)
    tm = res.shape[0]
    for hk in range(HK):
        blk = res[:, hk * HEAD_DIM:(hk + 1) * HEAD_DIM]
        oi_ref[pl.ds(hk, tm, stride=HK), :] = blk
        if len(rest) == 2:
            rest[0][hk] = blk.astype(BF16)


def matmul_kv(x, w_t, col0, tm, planes, name):
    m, k = x.shape
    ncols = HK * HEAD_DIM
    assert col0 % ncols == 0 and m % tm == 0
    out_specs = [pl.BlockSpec((tm * HK, HEAD_DIM), lambda i: (i, 0))]
    out_shape = [jax.ShapeDtypeStruct((m * HK, HEAD_DIM), F32)]
    if planes:
        out_specs.append(pl.BlockSpec((HK, tm, HEAD_DIM), lambda i: (0, i, 0)))
        out_shape.append(jax.ShapeDtypeStruct((HK, m, HEAD_DIM), BF16))
    return pl.pallas_call(
        _mm_kv_body,
        grid=(m // tm,),
        in_specs=[pl.BlockSpec((tm, k), lambda i: (i, 0)), pl.BlockSpec((ncols, k), lambda i: (col0 // ncols, 0))],
        out_specs=out_specs,
        out_shape=out_shape,
        scratch_shapes=[pltpu.VMEM((ncols, k), BF16)],
        compiler_params=_params(("arbitrary",), 48),
        name=name,
    )(x, w_t)


def _merge_body(mg_ref, rnn_ref, nsa_ref, o_ref):
    d = rnn_ref.shape[-1]
    ga = jax.nn.sigmoid(mg_ref[:, :d])
    gb = jax.nn.sigmoid(mg_ref[:, d:])
    o_ref[...] = (ga * rnn_ref[...] + gb * nsa_ref[...]).astype(o_ref.dtype)


def merge(u_mg, o_rnn, o_nsa, tm):
    m, d = o_rnn.shape
    return pl.pallas_call(
        _merge_body,
        grid=(m // tm,),
        in_specs=[pl.BlockSpec((tm, 2 * d), lambda i: (i, 0)), pl.BlockSpec((tm, d), lambda i: (i, 0)),
                  pl.BlockSpec((tm, d), lambda i: (i, 0))],
        out_specs=pl.BlockSpec((tm, d), lambda i: (i, 0)),
        out_shape=jax.ShapeDtypeStruct((m, d), BF16),
        compiler_params=_params(("arbitrary",), 48),
        name="merge",
    )(u_mg, o_rnn, o_nsa)


def _lru_gates(uc, wa_ref, wx_ref, ba_ref, bx_ref, lam_ref):
    ucb = uc.astype(BF16)
    ra, ri = [], []
    for n in range(RNN_BLOCKS):
        blk = ucb[:, n * RNN_BLOCK_DIM:(n + 1) * RNN_BLOCK_DIM]
        ra.append(jnp.dot(blk, wa_ref[n], preferred_element_type=F32))
        ri.append(jnp.dot(blk, wx_ref[n], preferred_element_type=F32))
    r = jax.nn.sigmoid(jnp.concatenate(ra, axis=-1) + ba_ref[...])
    i = jax.nn.sigmoid(jnp.concatenate(ri, axis=-1) + bx_ref[...])
    z = -lam_ref[...]
    softplus = jnp.maximum(z, 0.0) + jnp.log1p(jnp.exp(-jnp.abs(z)))
    log_a = (-LRU_C * r) * softplus
    a = jnp.exp(log_a)
    mult = jnp.sqrt(-jnp.tanh(log_a) * (a * a + 1.0))
    return a, mult * (i * uc)


def _lru_prompt_body(u_ref, gate_ref, cw_ref, cb_ref, wa_ref, wx_ref, ba_ref, bx_ref, lam_ref,
                     o_ref, conv_ref, hlast_ref, xbuf, hcar, acum, bcum):
    t = pl.program_id(1)
    nt = pl.num_programs(1)
    tt, d = u_ref.shape[1], u_ref.shape[2]

    @pl.when(t == 0)
    def _():
        xbuf[0:8, :] = jnp.zeros((8, d), F32)
        hcar[...] = jnp.zeros((8, d), F32)

    x = u_ref[0]
    xbuf[8:8 + tt, :] = x
    uc = (cw_ref[0:1, :] * xbuf[5:5 + tt, :] + cw_ref[1:2, :] * xbuf[6:6 + tt, :]
          + cw_ref[2:3, :] * xbuf[7:7 + tt, :] + cw_ref[3:4, :] * x) + cb_ref[...]
    xbuf[0:8, :] = xbuf[tt:tt + 8, :]

    a, b = _lru_gates(uc, wa_ref, wx_ref, ba_ref, bx_ref, lam_ref)
    g = tt // 8
    a3 = a.reshape(g, 8, d)
    b3 = b.reshape(g, 8, d)
    row = lax.broadcasted_iota(I32, (g, 8, d), 1)
    for s in (1, 2, 4):
        a_sh = pltpu.roll(a3, s, axis=1)
        b_sh = pltpu.roll(b3, s, axis=1)
        keep = row >= s
        b3 = jnp.where(keep, a3 * b_sh + b3, b3)
        a3 = jnp.where(keep, a3 * a_sh, a3)
    acum[...] = a3
    bcum[...] = b3

    def body(gi, h):
        hg = bcum[gi] + acum[gi] * h
        bcum[gi] = hg
        return jnp.broadcast_to(hg[7:8, :], (8, d))

    hfin = lax.fori_loop(0, g, body, hcar[...])
    hcar[...] = hfin
    o_ref[0] = bcum[...].reshape(tt, d) * _gelu_tanh(gate_ref[0])

    @pl.when(t == nt - 1)
    def _():
        conv_ref[0] = x[tt - (CONV_W - 1):tt, :]
        hlast_ref[0] = hfin[0:1, :]


def lru_prompt(u_rnn, u_gate, conv_w, conv_b, wa, wx, ba, bx, lam, tt):
    b, t, d = u_rnn.shape
    vec = lambda: pl.BlockSpec((1, d), lambda i, j: (0, 0))
    wspec = lambda: pl.BlockSpec((RNN_BLOCKS, RNN_BLOCK_DIM, RNN_BLOCK_DIM), lambda i, j: (0, 0, 0))
    return pl.pallas_call(
        _lru_prompt_body,
        grid=(b, t // tt),
        in_specs=[pl.BlockSpec((1, tt, d), lambda i, j: (i, j, 0)), pl.BlockSpec((1, tt, d), lambda i, j: (i, j, 0)),
                  pl.BlockSpec((CONV_W, d), lambda i, j: (0, 0)), vec(), wspec(), wspec(), vec(), vec(), vec()],
        out_specs=[pl.BlockSpec((1, tt, d), lambda i, j: (i, j, 0)),
                   pl.BlockSpec((1, CONV_W - 1, d), lambda i, j: (i, 0, 0)),
                   pl.BlockSpec((1, 1, d), lambda i, j: (i, 0, 0))],
        out_shape=[jax.ShapeDtypeStruct((b, t, d), F32), jax.ShapeDtypeStruct((b, CONV_W - 1, d), F32),
                   jax.ShapeDtypeStruct((b, 1, d), F32)],
        scratch_shapes=[pltpu.VMEM((tt + 8, d), F32), pltpu.VMEM((8, d), F32),
                        pltpu.VMEM((tt // 8, 8, d), F32), pltpu.VMEM((tt // 8, 8, d), F32)],
        compiler_params=_params(("arbitrary", "arbitrary"), 56),
        name="lru_prompt",
    )(u_rnn, u_gate, conv_w, conv_b.reshape(1, d), wa, wx, ba.reshape(1, d), bx.reshape(1, d), lam.reshape(1, d))


def _lru_sample_body(u_ref, gate_ref, cp_ref, h0_ref, cw_ref, cb_ref, wa_ref, wx_ref, ba_ref, bx_ref, lam_ref,
                     o_ref, cn_ref, h_ref):
    x = u_ref[...]
    uc = (cw_ref[0:1, :] * cp_ref[0] + cw_ref[1:2, :] * cp_ref[1] + cw_ref[2:3, :] * cp_ref[2]
          + cw_ref[3:4, :] * x) + cb_ref[...]
    a, b = _lru_gates(uc, wa_ref, wx_ref, ba_ref, bx_ref, lam_ref)
    h = a * h0_ref[...] + b
    o_ref[...] = h * _gelu_tanh(gate_ref[...])
    h_ref[...] = h
    cn_ref[0] = cp_ref[1]
    cn_ref[1] = cp_ref[2]
    cn_ref[2] = x


def lru_sample(u_rnn, u_gate, conv_prev_t, h0, conv_w, conv_b, wa, wx, ba, bx, lam):
    n, d = u_rnn.shape
    return pl.pallas_call(
        _lru_sample_body,
        out_shape=[jax.ShapeDtypeStruct((n, d), F32), jax.ShapeDtypeStruct((CONV_W - 1, n, d), F32),
                   jax.ShapeDtypeStruct((n, d), F32)],
        compiler_params=pltpu.CompilerParams(vmem_limit_bytes=40 * MIB),
        name="lru_sample",
    )(u_rnn, u_gate, conv_prev_t, h0, conv_w, conv_b.reshape(1, d), wa, wx, ba.reshape(1, d), bx.reshape(1, d),
      lam.reshape(1, d))


COMPRESS_PAGES = 16
CHUNKS_PER_PAGE = PAGE_SIZE // CMP_STRIDE


def _compress_body(pt_ref, *refs, pp, n_steps):
    page_refs = refs[:pp]
    nxt_ref, extra_ref, wc_ref, pe_ref, b1_ref, w2_ref, b2_ref, o_ref, res_scr = refs[pp:]
    del pt_ref
    is_last = pl.program_id(1) == n_steps - 1
    m_rows = pp * CHUNKS_PER_PAGE * N_KV_HEADS
    lo4 = lax.broadcasted_iota(I32, (4, 8, 8, LANES), 2) < 4
    lo3 = lax.broadcasted_iota(I32, (8, 8, LANES), 1) < 4

    for k in range(2):
        lhs_sp = [[] for _ in range(8)]
        for pr in page_refs:
            xk = pr[pl.ds(k, PAGE_SIZE * N_KV_HEADS, stride=2), :]
            x5 = xk.reshape(4, 2, 8, 8, LANES)
            a0 = x5[:, 0]
            a1 = x5[:, 1]
            be = jnp.where(lo4, a0, pltpu.roll(a1, 4, axis=2))
            bo = jnp.where(lo4, pltpu.roll(a0, 4, axis=2), a1)
            for sp in range(8):
                lhs_sp[sp].append(jnp.concatenate([be[:, sp], bo[:, sp]], axis=-1).reshape(32, 2 * LANES))
        acc = jnp.zeros((m_rows, 2 * CMP_HIDDEN), F32)
        for sp in range(8):
            lhs = jnp.concatenate(lhs_sp[sp], axis=0).astype(BF16)
            acc = acc + jnp.dot(lhs, wc_ref[k, sp * 256:(sp + 1) * 256, :], preferred_element_type=F32)

        pet = jnp.dot(pe_ref[k].astype(BF16), wc_ref[k], preferred_element_type=F32)
        peb0 = pet[0:1, :CMP_HIDDEN]
        peb1 = pet[1:2, CMP_HIDDEN:]
        h0 = acc[:, :CMP_HIDDEN] + peb0
        h1 = acc[:, CMP_HIDDEN:] + peb1

        xn = jnp.where(is_last, extra_ref[pl.ds(k, CMP_STRIDE * N_KV_HEADS, stride=2), :],
                       nxt_ref[pl.ds(k, CMP_STRIDE * N_KV_HEADS, stride=2), :])
        xn3 = xn.reshape(8, 8, LANES)
        ln = jnp.concatenate([jnp.where(lo3, xn3, 0.0), jnp.where(lo3, 0.0, xn3)], axis=-1).astype(BF16)
        nacc = jnp.zeros((8, CMP_HIDDEN), F32)
        for sp in range(8):
            nacc = nacc + jnp.dot(ln[sp], wc_ref[k, sp * 256:(sp + 1) * 256, CMP_HIDDEN:],
                                  preferred_element_type=F32)
        n8 = nacc + pltpu.roll(nacc, 4, axis=0) + peb1
        h1e = jnp.concatenate([h1, n8], axis=0)
        h1s = pltpu.roll(h1e, m_rows + 8 - N_KV_HEADS, axis=0)[:m_rows]
        pre = h0 + h1s + b1_ref[k]
        out_k = jnp.dot(_gelu_tanh(pre).astype(BF16), w2_ref[k], preferred_element_type=F32) + b2_ref[k]
        res_scr[...] = out_k
        for h in range(N_KV_HEADS):
            o_ref[0, 2 * h + k] = res_scr[pl.ds(h, m_rows // N_KV_HEADS, stride=N_KV_HEADS), :]


def compress(pool2d, page_table, extra2d, wc, pe8, b1, w2, b2):
    nseq, npages = page_table.shape
    pp = COMPRESS_PAGES
    n_steps = npages // pp
    assert npages % pp == 0
    rows_page = PAGE_SIZE * HK
    rows_chunk = CMP_STRIDE * HK
    blocks_step = pp * CHUNKS_PER_PAGE

    def page_spec(i):
        return pl.BlockSpec((rows_page, LANES), lambda b, s, pt: (pt[b, s * pp + i], 0))

    def nxt_map(b, s, pt):
        return (pt[b, jnp.minimum((s + 1) * pp, npages - 1)] * CHUNKS_PER_PAGE, 0)

    const3 = lambda shape: pl.BlockSpec(shape, lambda b, s, pt: (0, 0, 0))
    in_specs = [page_spec(i) for i in range(pp)] + [
        pl.BlockSpec((rows_chunk, LANES), nxt_map),
        pl.BlockSpec((rows_chunk, LANES), lambda b, s, pt: (b, 0)),
        const3((2, CMP_STRIDE * HEAD_DIM, 2 * CMP_HIDDEN)),
        const3((2, 8, CMP_STRIDE * HEAD_DIM)),
        const3((2, 1, CMP_HIDDEN)),
        const3((2, CMP_HIDDEN, HEAD_DIM)),
        const3((2, 1, HEAD_DIM)),
    ]
    grid_spec = pltpu.PrefetchScalarGridSpec(
        num_scalar_prefetch=1,
        grid=(nseq, n_steps),
        in_specs=in_specs,
        out_specs=pl.BlockSpec((1, HK, blocks_step, HEAD_DIM), lambda b, s, pt: (b, 0, s, 0)),
        scratch_shapes=[pltpu.VMEM((blocks_step * N_KV_HEADS, HEAD_DIM), F32)],
    )
    return pl.pallas_call(
        functools.partial(_compress_body, pp=pp, n_steps=n_steps),
        grid_spec=grid_spec,
        out_shape=jax.ShapeDtypeStruct((nseq, HK, npages * CHUNKS_PER_PAGE, HEAD_DIM), F32),
        compiler_params=_params(("arbitrary", "arbitrary"), 56),
        name="compress",
    )(page_table, *([pool2d] * pp), pool2d, extra2d, wc, pe8, b1, w2, b2)


def _overlap_matrix(n_cmp, n_cmp_pad, n_sel, n_sel_pad):
    cs = np.arange(n_cmp_pad)[:, None] * CMP_STRIDE
    ss = np.arange(n_sel_pad)[None, :] * SEL_BLK
    ov = np.minimum(cs + CMP_BLK, ss + SEL_BLK) - np.maximum(cs, ss)
    m = np.clip(ov, 0, CMP_BLK).astype(np.float32) / CMP_BLK
    m[n_cmp:, :] = 0.0
    m[:, n_sel:] = 0.0
    return m


QBLK = 128


KCHUNK = 64
KPIECE = 512
LOG2E = 1.4426950408889634


def _softmax_passes(s_scr, p_scr, n_plain, n_masked, n_total, plain_mask, full_mask):
    cols = s_scr.shape[1]
    grp = KCHUNK // 8
    c2 = ATTN_SCALE * LOG2E

    def rows(c):
        return pl.ds(pl.multiple_of(c * KCHUNK, KCHUNK), KCHUNK)

    def masked_max(mask_fn):
        def body(c, m8):
            sm = jnp.where(mask_fn(c), s_scr[rows(c), :], MASK_VALUE)
            s_scr[rows(c), :] = sm
            return jnp.maximum(m8, jnp.max(sm.reshape(grp, 8, cols), axis=0))
        return body

    m8 = lax.fori_loop(0, n_plain, masked_max(plain_mask), jnp.full((8, cols), MASK_VALUE, F32))
    m8 = lax.fori_loop(n_plain, n_plain + n_masked, masked_max(full_mask), m8)
    mc = jnp.max(m8, axis=0, keepdims=True) * c2

    def pass_exp(c, l8):
        p = jnp.exp2(s_scr[rows(c), :] * c2 - mc)
        p_scr[rows(c), :] = p.astype(BF16)
        return l8 + jnp.sum(p.reshape(grp, 8, cols), axis=0)

    l8 = lax.fori_loop(0, n_plain + n_masked, pass_exp, jnp.zeros((8, cols), F32))

    def zero_fill(c, carry):
        p_scr[rows(c), :] = jnp.zeros((KCHUNK, cols), BF16)
        return carry

    lax.fori_loop(n_plain + n_masked, n_total, zero_fill, 0)
    return 1.0 / jnp.sum(l8, axis=0, keepdims=True)


def _attn_prompt_s_body(q_ref, ksel_ref, vsel_ref, kwin_ref, vwin_ref, kvc_ref, bg_ref, mt_ref, o_ref,
                        vsel_t, vwin_t, s_scr, p_scr, sel_scr, *, n_cmp, n_sel):
    qi = pl.program_id(2)
    q0 = qi * QBLK
    cols = GROUP * QBLK

    @pl.when(qi == 0)
    def _():
        vsel_t[...] = vsel_ref[0].astype(F32).T.astype(BF16)
        vwin_t[...] = vwin_ref[0].astype(F32).T.astype(BF16)

    q = q_ref[0]
    q4 = jnp.concatenate([q[:, g * HEAD_DIM:(g + 1) * HEAD_DIM] for g in range(GROUP)], axis=0)
    qpos_row = q0 + (lax.broadcasted_iota(I32, (1, cols), 1) & (QBLK - 1))
    row_in_chunk = lax.broadcasted_iota(I32, (KCHUNK, cols), 0)

    nb = kvc_ref.shape[2]
    kc = kvc_ref[0, 0].astype(BF16)
    vc_t = kvc_ref[0, 1].T.astype(BF16)
    n_io = lax.broadcasted_iota(I32, (nb, cols), 0)
    mask_c = (n_io * CMP_STRIDE + (CMP_BLK - 1) <= qpos_row) & (n_io < n_cmp)
    sc = jnp.where(mask_c, lax.dot_general(kc, q4, _NT, preferred_element_type=F32) * ATTN_SCALE, MASK_VALUE)
    ec = jnp.where(mask_c, jnp.exp(sc - jnp.max(sc, axis=0, keepdims=True)), 0.0)
    den_c = jnp.sum(ec, axis=0, keepdims=True)
    pc = ec * jnp.where(den_c > 0.0, 1.0 / den_c, 0.0)
    o_c = jnp.dot(vc_t, pc.astype(BF16), preferred_element_type=F32)
    psum = pc[:, 0:QBLK] + pc[:, QBLK:2 * QBLK] + pc[:, 2 * QBLK:3 * QBLK] + pc[:, 3 * QBLK:4 * QBLK]

    nsp = mt_ref.shape[0]
    imp_t = jnp.dot(mt_ref[...], psum.astype(BF16), preferred_element_type=F32)
    jt = lax.broadcasted_iota(I32, (nsp, QBLK), 0)
    cur = lax.shift_right_logical(q0 + lax.broadcasted_iota(I32, (nsp, QBLK), 1), SEL_SHIFT)
    valid = (jt <= cur) & (jt < n_sel)
    forced = (jt == 0) | (jt >= cur - 1)
    score = jnp.where(forced, FORCED_SCORE, imp_t)
    score = jnp.where(valid, score, -1.0)
    rank = jnp.zeros((nsp, QBLK), I32)
    for k in range(n_sel):
        sk = score[k:k + 1, :]
        beats = (sk > score) | ((sk == score) & (k < jt))
        rank = rank + beats.astype(I32)
    sel_t = jnp.where((rank < N_SEL) & valid, 1.0, 0.0)
    sel_scr[...] = jnp.concatenate([sel_t] * GROUP, axis=1)

    wkeys = WINDOW + QBLK
    ks = pl.multiple_of(jnp.maximum(q0 - WINDOW, 0), QBLK)
    s_scr[0:wkeys, :] = lax.dot_general(kwin_ref[0, pl.ds(ks, wkeys), :], q4, _NT, preferred_element_type=F32)

    def win_mask(c):
        dist = qpos_row - (ks + c * KCHUNK + row_in_chunk)
        return (dist >= 0) & (dist < WINDOW)

    n_wchunks = wkeys // KCHUNK
    inv_w = _softmax_passes(s_scr, p_scr, 0, n_wchunks, n_wchunks, win_mask, win_mask)
    o_w = jnp.dot(vwin_t[:, pl.ds(ks, wkeys)], p_scr[0:wkeys, :], preferred_element_type=F32) * inv_w

    n_pieces = qi // (KPIECE // QBLK) + 1

    def score_piece(c, carry):
        r0 = pl.multiple_of(c * KPIECE, KPIECE)
        s_scr[pl.ds(r0, KPIECE), :] = lax.dot_general(ksel_ref[0, pl.ds(r0, KPIECE), :], q4, _NT,
                                                      preferred_element_type=F32)
        return carry

    lax.fori_loop(0, n_pieces, score_piece, 0)

    def picked(c):
        return sel_scr[pl.ds(c, 1), :] > 0.5

    def picked_causal(c):
        return picked(c) & (c * KCHUNK + row_in_chunk <= qpos_row)

    chunks_q = QBLK // KCHUNK
    inv_s = _softmax_passes(s_scr, p_scr, qi * chunks_q, chunks_q, n_pieces * (KPIECE // KCHUNK),
                            picked, picked_causal)

    def pv_piece(c, acc):
        r0 = pl.multiple_of(c * KPIECE, KPIECE)
        return acc + jnp.dot(vsel_t[:, pl.ds(r0, KPIECE)], p_scr[pl.ds(r0, KPIECE), :], preferred_element_type=F32)

    o_s = lax.fori_loop(0, n_pieces, pv_piece, jnp.zeros((HEAD_DIM, cols), F32)) * inv_s

    gates_t = jax.nn.sigmoid(bg_ref[0]).T
    grow = lambda br: jnp.concatenate([gates_t[br * GROUP + g:br * GROUP + g + 1, :] for g in range(GROUP)], axis=1)
    o_t = grow(0) * o_c + grow(1) * o_s + grow(2) * o_w
    for g in range(GROUP):
        o_ref[0, :, g * HEAD_DIM:(g + 1) * HEAD_DIM] = o_t[:, g * QBLK:(g + 1) * QBLK].T


def attn_prompt_s(q3, sel_planes, win_planes, kvc, ubg3, n_cmp):
    b, t, _ = q3.shape
    nb = kvc.shape[2]
    assert KCHUNK == SEL_BLK and t % KPIECE == 0 and t >= WINDOW + QBLK
    n_sel = -(-t // SEL_BLK)
    nsp = max(8, -(-n_sel // 8) * 8)
    m_t = _overlap_matrix(n_cmp, nb, n_sel, nsp).T
    kspec = lambda: pl.BlockSpec((1, t, HEAD_DIM), lambda i, h, j: (2 * h, i, 0))
    vspec = lambda: pl.BlockSpec((1, t, HEAD_DIM), lambda i, h, j: (2 * h + 1, i, 0))
    cols = GROUP * QBLK
    return pl.pallas_call(
        functools.partial(_attn_prompt_s_body, n_cmp=n_cmp, n_sel=n_sel),
        grid=(b, N_KV_HEADS, t // QBLK),
        in_specs=[
            pl.BlockSpec((1, QBLK, GROUP * HEAD_DIM), lambda i, h, j: (i, j, h)),
            kspec(), vspec(), kspec(), vspec(),
            pl.BlockSpec((1, 2, nb, HEAD_DIM), lambda i, h, j: (i, h, 0, 0)),
            pl.BlockSpec((1, QBLK, LANES), lambda i, h, j: (i, j, h)),
            pl.BlockSpec((nsp, nb), lambda i, h, j: (0, 0)),
        ],
        out_specs=pl.BlockSpec((1, QBLK, GROUP * HEAD_DIM), lambda i, h, j: (i, j, h)),
        out_shape=jax.ShapeDtypeStruct((b, t, N_HEADS * HEAD_DIM), F32),
        scratch_shapes=[pltpu.VMEM((HEAD_DIM, t), BF16), pltpu.VMEM((HEAD_DIM, t), BF16),
                        pltpu.VMEM((t, cols), F32), pltpu.VMEM((t, cols), BF16), pltpu.VMEM((nsp, cols), F32)],
        compiler_params=_params(("arbitrary", "arbitrary", "arbitrary"), 48),
        name="attn_prompt",
    )(q3, sel_planes, sel_planes, win_planes, win_planes, kvc, ubg3, jnp.asarray(m_t, BF16))


def _row_head(shape):
    return lax.shift_right_logical(lax.broadcasted_iota(I32, shape, 0), 2)


def _cmp_sample_body(q_ref, kvc_ref, m_ref, oc_ref, imp_ref, *, qpos, n_cmp):
    qb = q_ref[0].astype(BF16)
    nb = kvc_ref.shape[2]
    rg = _row_head((N_HEADS, nb))
    ncol = lax.broadcasted_iota(I32, (N_HEADS, nb), 1)
    mask = (ncol * CMP_STRIDE + (CMP_BLK - 1) <= qpos) & (ncol < n_cmp)
    row8 = lax.broadcasted_iota(I32, (8, nb), 0)
    o = jnp.zeros((N_HEADS, HEAD_DIM), F32)
    ps8 = jnp.zeros((8, nb), F32)
    for h in range(N_KV_HEADS):
        kc = kvc_ref[0, 2 * h].astype(BF16)
        vc = kvc_ref[0, 2 * h + 1].astype(BF16)
        s = lax.dot_general(qb, kc, _NT, preferred_element_type=F32) * ATTN_SCALE
        ph = jnp.where(rg == h, _msoftmax(s, mask), 0.0)
        o = o + jnp.dot(ph.astype(BF16), vc, preferred_element_type=F32)
        ps8 = jnp.where(row8 == h, jnp.sum(ph, axis=0, keepdims=True), ps8)
    oc_ref[0] = o
    imp_ref[0] = jnp.dot(ps8.astype(BF16), m_ref[...], preferred_element_type=F32)[0:N_KV_HEADS]


def cmp_sample(q16, kvc, m_mat, qpos, n_cmp):
    ns = q16.shape[0]
    nb = kvc.shape[2]
    nsp = m_mat.shape[1]
    return pl.pallas_call(
        functools.partial(_cmp_sample_body, qpos=qpos, n_cmp=n_cmp),
        grid=(ns,),
        in_specs=[pl.BlockSpec((1, N_HEADS, HEAD_DIM), lambda b: (b, 0, 0)),
                  pl.BlockSpec((1, HK, nb, HEAD_DIM), lambda b: (b, 0, 0, 0)),
                  pl.BlockSpec((nb, nsp), lambda b: (0, 0))],
        out_specs=[pl.BlockSpec((1, N_HEADS, HEAD_DIM), lambda b: (b, 0, 0)),
                   pl.BlockSpec((1, N_KV_HEADS, nsp), lambda b: (b, 0, 0))],
        out_shape=[jax.ShapeDtypeStruct((ns, N_HEADS, HEAD_DIM), F32),
                   jax.ShapeDtypeStruct((ns, N_KV_HEADS, nsp), F32)],
        compiler_params=_params(("arbitrary",), 40),
        name="cmp_sample",
    )(q16, kvc, m_mat)


def _rank_sample_body(imp_ref, idx_ref, sc_scr, *, cur, n_sel):
    npad, nr = sc_scr.shape
    st = imp_ref[...].T
    j = lax.broadcasted_iota(I32, (npad, nr), 0)
    real = j < n_sel
    valid = (j <= cur) & real
    forced = (j == 0) | (j >= cur - 1)
    score = jnp.where(forced, FORCED_SCORE, st)
    score = jnp.where(valid, score, -1.0)
    score = jnp.where(real, score, -2.0)
    sc_scr[...] = score

    def body(k, rank):
        sk = sc_scr[pl.ds(k, 1), :]
        beats = (sk > score) | ((sk == score) & (k < j))
        return rank + beats.astype(I32)

    rank = lax.fori_loop(0, n_sel, body, jnp.zeros((npad, nr), I32))
    sel = (rank < N_SEL) & valid
    for slot in range(N_SEL):
        hit = sel & (rank == slot)
        found = jnp.max(hit.astype(I32), axis=0, keepdims=True)
        val = jnp.sum(jnp.where(hit, j, 0), axis=0, keepdims=True)
        idx_ref[slot:slot + 1, :] = jnp.where(found > 0, val, -1)


def rank_sample(imp2, cur, n_sel):
    nr, npad = imp2.shape
    return pl.pallas_call(
        functools.partial(_rank_sample_body, cur=cur, n_sel=n_sel),
        out_shape=jax.ShapeDtypeStruct((N_SEL, nr), I32),
        scratch_shapes=[pltpu.VMEM((npad, nr), F32)],
        name="rank_sample",
    )(imp2)


def _sel_sample_body(idx_ref, pt_ref, q_ref, kn_ref, vn_ref, *rest, n_past_blocks):
    del pt_ref
    kvs = rest[:N_SEL * N_KV_HEADS]
    o_ref = rest[N_SEL * N_KV_HEADS]
    b = pl.program_id(0)
    q = q_ref[0]
    qb = q.astype(BF16)
    nk = N_SEL * SEL_BLK
    rg = _row_head((N_HEADS, nk))
    col_slot = lax.shift_right_logical(lax.broadcasted_iota(I32, (N_HEADS, nk), 1), SEL_SHIFT)

    sc = jnp.zeros((N_HEADS, nk), F32)
    okv = jnp.zeros((N_HEADS, nk), I32)
    for h in range(N_KV_HEADS):
        kcat = jnp.concatenate([kvs[s * N_KV_HEADS + h][:, 0, 0, :] for s in range(N_SEL)], axis=0).astype(BF16)
        sh = lax.dot_general(qb, kcat, _NT, preferred_element_type=F32)
        ok = jnp.zeros((N_HEADS, nk), I32)
        for s in range(N_SEL):
            blk = idx_ref[s, b * N_KV_HEADS + h]
            ok = jnp.where(col_slot == s, jnp.where((blk >= 0) & (blk < n_past_blocks), 1, 0), ok)
        sc = jnp.where(rg == h, sh, sc)
        okv = jnp.where(rg == h, ok, okv)
    sm = jnp.where(okv > 0, sc * ATTN_SCALE, MASK_VALUE)
    s_new = jnp.sum(q * kn_ref[0], axis=-1, keepdims=True) * ATTN_SCALE
    m = jnp.maximum(jnp.max(sm, axis=-1, keepdims=True), s_new)
    p = jnp.where(okv > 0, jnp.exp(sm - m), 0.0)
    p_new = jnp.exp(s_new - m)
    den = jnp.sum(p, axis=-1, keepdims=True) + p_new
    pv = jnp.zeros((N_HEADS, HEAD_DIM), F32)
    for h in range(N_KV_HEADS):
        vcat = jnp.concatenate([kvs[s * N_KV_HEADS + h][:, 0, 1, :] for s in range(N_SEL)], axis=0).astype(BF16)
        pv = pv + jnp.dot(jnp.where(rg == h, p, 0.0).astype(BF16), vcat, preferred_element_type=F32)
    o_ref[0] = (pv + p_new * vn_ref[0]) / den


def sel_sample(idx, page_table, q16, kn16, vn16, pool4, n_past_blocks):
    ns = q16.shape[0]
    blk_per_page = PAGE_SIZE // SEL_BLK

    def kv_spec(s, h):
        def imap(b, idx_r, pt_r):
            blk = jnp.clip(idx_r[s, b * N_KV_HEADS + h], 0, n_past_blocks - 1)
            return (pt_r[b, blk // blk_per_page] * blk_per_page + blk % blk_per_page, h, 0, 0)
        return pl.BlockSpec((SEL_BLK, 1, 2, HEAD_DIM), imap)

    head3 = lambda: pl.BlockSpec((1, N_HEADS, HEAD_DIM), lambda b, i, p: (b, 0, 0))
    kv_specs = [kv_spec(s, h) for s in range(N_SEL) for h in range(N_KV_HEADS)]
    grid_spec = pltpu.PrefetchScalarGridSpec(
        num_scalar_prefetch=2,
        grid=(ns,),
        in_specs=[head3(), head3(), head3()] + kv_specs,
        out_specs=head3(),
    )
    return pl.pallas_call(
        functools.partial(_sel_sample_body, n_past_blocks=n_past_blocks),
        grid_spec=grid_spec,
        out_shape=jax.ShapeDtypeStruct((ns, N_HEADS, HEAD_DIM), F32),
        compiler_params=_params(("arbitrary",), 40),
        name="sel_sample",
    )(idx, page_table, q16, kn16, vn16, *([pool4] * len(kv_specs)))


def _win_sample_body(q_ref, win_ref, new_ref, kn_ref, vn_ref, oc_ref, os_ref, bg_ref, o_ref, wout_ref, *, wbuf):
    wout_ref[0:(wbuf - 1) * HK, :] = win_ref[HK:wbuf * HK, :]
    wout_ref[(wbuf - 1) * HK:wbuf * HK, :] = new_ref[...]
    q = q_ref[0]
    qb = q.astype(BF16)
    rg = _row_head((N_HEADS, wbuf))
    sc = jnp.zeros((N_HEADS, wbuf), F32)
    for h in range(N_KV_HEADS):
        kh = win_ref[pl.ds(2 * h, wbuf, stride=HK), :].astype(BF16)
        sc = jnp.where(rg == h, lax.dot_general(qb, kh, _NT, preferred_element_type=F32), sc)
    mask = lax.broadcasted_iota(I32, (N_HEADS, wbuf), 1) > wbuf - WINDOW
    sm = jnp.where(mask, sc * ATTN_SCALE, MASK_VALUE)
    s_new = jnp.sum(q * kn_ref[0], axis=-1, keepdims=True) * ATTN_SCALE
    m = jnp.maximum(jnp.max(sm, axis=-1, keepdims=True), s_new)
    p = jnp.where(mask, jnp.exp(sm - m), 0.0)
    p_new = jnp.exp(s_new - m)
    den = jnp.sum(p, axis=-1, keepdims=True) + p_new
    pv = jnp.zeros((N_HEADS, HEAD_DIM), F32)
    for h in range(N_KV_HEADS):
        vh = win_ref[pl.ds(2 * h + 1, wbuf, stride=HK), :].astype(BF16)
        pv = pv + jnp.dot(jnp.where(rg == h, p, 0.0).astype(BF16), vh, preferred_element_type=F32)
    o_w = (pv + p_new * vn_ref[0]) / den
    gates = jax.nn.sigmoid(bg_ref[0])
    o_ref[0] = gates[:, 0:1] * oc_ref[0] + gates[:, 1:2] * os_ref[0] + gates[:, 2:3] * o_w


def win_sample(q16, win2d, new2d, kn16, vn16, o_cmp, o_sel, bgs, wbuf):
    ns = q16.shape[0]
    head3 = lambda: pl.BlockSpec((1, N_HEADS, HEAD_DIM), lambda b: (b, 0, 0))
    wspec = lambda: pl.BlockSpec((wbuf * HK, LANES), lambda b: (b, 0))
    return pl.pallas_call(
        functools.partial(_win_sample_body, wbuf=wbuf),
        grid=(ns,),
        in_specs=[head3(), wspec(), pl.BlockSpec((HK, LANES), lambda b: (b, 0)), head3(), head3(), head3(), head3(),
                  head3()],
        out_specs=[head3(), wspec()],
        out_shape=[jax.ShapeDtypeStruct((ns, N_HEADS, HEAD_DIM), F32),
                   jax.ShapeDtypeStruct((ns * wbuf * HK, LANES), F32)],
        compiler_params=_params(("arbitrary",), 40),
        name="win_sample",
    )(q16, win2d, new2d, kn16, vn16, o_cmp, o_sel, bgs)


ROUTE_LANE0 = N_GROUPS


INFO_E1, INFO_E2, INFO_R1, INFO_R2, INFO_C1, INFO_C2 = range(6)


def _route_body(xp_ref, xs_ref, g_ref, wr_ref, br_ref, xn_ref, info_ref, cnt_ref, run_scr, *, n_real, n_ptiles):
    step = pl.program_id(0)

    @pl.when(step == 0)
    def _():
        run_scr[...] = jnp.zeros(run_scr.shape, F32)

    refs = (g_ref, wr_ref, br_ref, xn_ref, info_ref, cnt_ref, run_scr)

    @pl.when(step < n_ptiles)
    def _():
        _route_tile(xp_ref[...], step, n_real, *refs)

    @pl.when(step >= n_ptiles)
    def _():
        _route_tile(xs_ref[...], step, n_real, *refs)


def _route_tile(x, step, n_real, g_ref, wr_ref, br_ref, xn_ref, info_ref, cnt_ref, run_scr):
    tm = x.shape[0]
    var = jnp.mean(x * x, axis=-1, keepdims=True)
    xn = (x * lax.rsqrt(var + EPS)) * g_ref[...]
    xn_ref[...] = xn
    xnb = xn.astype(BF16)
    logits = jnp.dot(xnb, wr_ref[...], preferred_element_type=F32) + br_ref[...]
    lane = lax.broadcasted_iota(I32, logits.shape, 1)
    big = 4 * LANES

    isg = lane < N_GROUPS
    lg = jnp.where(isg, logits, -jnp.inf)
    mg = jnp.max(lg, axis=-1, keepdims=True)
    gi = jnp.min(jnp.where(lg == mg, lane, big), axis=-1, keepdims=True)
    pg = 1.0 / jnp.sum(jnp.where(isg, jnp.exp(lg - mg), 0.0), axis=-1, keepdims=True)

    lo = ROUTE_LANE0 + gi * EXP_PER_GROUP
    ing = (lane >= lo) & (lane < lo + EXP_PER_GROUP)
    le = jnp.where(ing, logits, -jnp.inf)
    me = jnp.max(le, axis=-1, keepdims=True)
    ee = jnp.where(ing, jnp.exp(le - me), 0.0)
    pe = jnp.where(ing, ee / jnp.sum(ee, axis=-1, keepdims=True), -1.0)
    p1 = jnp.max(pe, axis=-1, keepdims=True)
    i1 = jnp.min(jnp.where(pe == p1, lane, big), axis=-1, keepdims=True)
    pe2 = jnp.where(lane == i1, -1.0, pe)
    p2 = jnp.max(pe2, axis=-1, keepdims=True)
    i2 = jnp.min(jnp.where(pe2 == p2, lane, big), axis=-1, keepdims=True)
    tot = p1 + p2
    c1 = pg * (p1 / tot)
    c2 = pg * (p2 / tot)

    row = step * tm + lax.broadcasted_iota(I32, logits.shape, 0)
    hit = jnp.where(((lane == i1) | (lane == i2)) & (row < n_real), 1.0, 0.0)
    tri = jnp.where(lax.broadcasted_iota(I32, (tm, tm), 1) < lax.broadcasted_iota(I32, (tm, tm), 0), 1.0, 0.0)
    before = run_scr[...] + jnp.dot(tri.astype(BF16), hit.astype(BF16), preferred_element_type=F32)
    r1 = jnp.sum(jnp.where(lane == i1, before, 0.0), axis=-1, keepdims=True)
    r2 = jnp.sum(jnp.where(lane == i2, before, 0.0), axis=-1, keepdims=True)
    run_new = run_scr[...] + jnp.sum(hit, axis=0, keepdims=True)
    run_scr[...] = run_new
    cnt_ref[...] = run_new

    info = jnp.zeros(logits.shape, F32)
    for ln, val in ((INFO_E1, (i1 - ROUTE_LANE0).astype(F32)), (INFO_E2, (i2 - ROUTE_LANE0).astype(F32)),
                    (INFO_R1, r1), (INFO_R2, r2), (INFO_C1, c1), (INFO_C2, c2)):
        info = jnp.where(lane == ln, val, info)
    info_ref[...] = info


MOE_TILE = 256


def route(x_prompt, x_sample, g, wr, br, n_real):
    mp, d = x_prompt.shape
    n_ptiles = mp // MOE_TILE
    assert mp % MOE_TILE == 0 and x_sample.shape[0] == MOE_TILE
    m = mp + MOE_TILE
    return pl.pallas_call(
        functools.partial(_route_body, n_real=n_real, n_ptiles=n_ptiles),
        grid=(n_ptiles + 1,),
        in_specs=[pl.BlockSpec((MOE_TILE, d), lambda i: (jnp.minimum(i, n_ptiles - 1), 0)),
                  pl.BlockSpec((MOE_TILE, d), lambda i: (0, 0)), pl.BlockSpec((1, d), lambda i: (0, 0)),
                  pl.BlockSpec((d, LANES), lambda i: (0, 0)), pl.BlockSpec((1, LANES), lambda i: (0, 0))],
        out_specs=[pl.BlockSpec((MOE_TILE, d), lambda i: (i, 0)), pl.BlockSpec((MOE_TILE, LANES), lambda i: (i, 0)),
                   pl.BlockSpec((1, LANES), lambda i: (0, 0))],
        out_shape=[jax.ShapeDtypeStruct((m, d), F32), jax.ShapeDtypeStruct((m, LANES), F32),
                   jax.ShapeDtypeStruct((1, LANES), F32)],
        scratch_shapes=[pltpu.VMEM((1, LANES), F32)],
        compiler_params=_params(("arbitrary",), 40),
        name="route",
    )(x_prompt, x_sample, g.reshape(1, d), wr, br)


def _moe_plan(info, counts_row, n_real, n_tiles):
    n_pad = info.shape[0]
    e = jnp.clip(info[:, INFO_E1:INFO_E2 + 1].astype(I32), 0, N_EXPERTS - 1)
    r = info[:, INFO_R1:INFO_R2 + 1].astype(I32)
    counts = counts_row[0, ROUTE_LANE0:ROUTE_LANE0 + N_EXPERTS].astype(I32)
    padded = ((counts + MOE_TILE - 1) // MOE_TILE) * MOE_TILE
    ex = jnp.arange(N_EXPERTS, dtype=I32)
    ends = jnp.sum(jnp.where(ex[None, :] <= ex[:, None], padded[None, :], 0), axis=1)
    offs = ends - padded
    tok = jnp.arange(n_pad, dtype=I32)
    valid = (tok < n_real)[:, None]
    dest = jnp.where(valid, offs[e] + r, 0)
    n_slots = n_tiles * MOE_TILE
    src = jnp.zeros((n_slots,), I32).at[jnp.where(valid, dest, n_slots).reshape(-1)].set(
        jnp.repeat(tok, 2), mode="drop")
    n_used = ends[N_EXPERTS - 1] // MOE_TILE
    tiles = jnp.arange(n_tiles, dtype=I32)
    used = tiles < n_used
    te = jnp.minimum(jnp.sum((ends[None, :] <= (tiles * MOE_TILE)[:, None]).astype(I32), axis=1), N_EXPERTS - 1)
    te_prev = jnp.concatenate([te[:1] - 1, te[:-1]])
    first = (te != te_prev) & used
    later = tiles[None, :] > tiles[:, None]
    run = jnp.sum((first[None, :] & ~later).astype(I32), axis=1) - 1
    nxt_tile = jnp.min(jnp.where(first[None, :] & later, tiles[None, :], n_tiles), axis=1)
    nxt = jnp.where(nxt_tile < n_tiles, te[jnp.minimum(nxt_tile, n_tiles - 1)], -1)
    tile3 = lambda v: v.reshape(-1, 1, MOE_TILE)
    return (tile3(src), tile3(dest[:, 0]), tile3(dest[:, 1]),
            (te, first.astype(I32), nxt, jnp.bitwise_and(run, 1), n_used.reshape(1)))


def _issue_rows(idx_ref, src_hbm, dst_buf, slot, sem):
    n = idx_ref.shape[2]

    def body(r, carry):
        pltpu.make_async_copy(src_hbm.at[pl.ds(idx_ref[0, 0, r], 1), :], dst_buf.at[slot, pl.ds(r, 1), :],
                              sem.at[slot]).start()
        return carry

    lax.fori_loop(0, n, body, 0, unroll=8)


def _wait_rows(src_hbm, dst_buf, slot, sem):
    n = dst_buf.shape[1]
    pltpu.make_async_copy(src_hbm.at[pl.ds(0, n), :], dst_buf.at[slot], sem.at[slot]).wait()


def _moe_expert_body(te_ref, first_ref, nxt_ref, par_ref, nu_ref, src_cur, src_nxt, xn_hbm, wg_hbm, wu_hbm, wd_hbm,
                     y_ref, xbuf, xsem, wg_buf, wu_buf, wd_buf, wsem, wgb, wub, wdb):
    i = pl.program_id(0)
    n_used = nu_ref[0]
    slot = lax.rem(i, 2)

    def weight_copies(e, ws):
        return (pltpu.make_async_copy(wg_hbm.at[e], wg_buf.at[ws], wsem.at[0, ws]),
                pltpu.make_async_copy(wu_hbm.at[e], wu_buf.at[ws], wsem.at[1, ws]),
                pltpu.make_async_copy(wd_hbm.at[e], wd_buf.at[ws], wsem.at[2, ws]))

    @pl.when(i == 0)
    def _():
        for cp in weight_copies(te_ref[0], par_ref[0]):
            cp.start()
        _issue_rows(src_cur, xn_hbm, xbuf, 0, xsem)

    @pl.when(i + 1 < n_used)
    def _():
        _issue_rows(src_nxt, xn_hbm, xbuf, 1 - slot, xsem)

    @pl.when(i < n_used)
    def _():
        @pl.when(first_ref[i] == 1)
        def _():
            ws = par_ref[i]
            for cp in weight_copies(te_ref[i], ws):
                cp.wait()

            @pl.when(nxt_ref[i] >= 0)
            def _():
                for cp in weight_copies(nxt_ref[i], 1 - ws):
                    cp.start()

            wgb[...] = wg_buf[ws].astype(BF16)
            wub[...] = wu_buf[ws].astype(BF16)
            wdb[...] = wd_buf[ws].astype(BF16)

        _wait_rows(xn_hbm, xbuf, slot, xsem)
        x = xbuf[slot].astype(BF16)
        hg = jnp.dot(x, wgb[...], preferred_element_type=F32)
        hu = jnp.dot(x, wub[...], preferred_element_type=F32)
        hid = (hg * jax.nn.sigmoid(hg)) * hu
        y_ref[...] = jnp.dot(hid.astype(BF16), wdb[...], preferred_element_type=F32)

    @pl.when(i >= n_used)
    def _():
        y_ref[...] = jnp.zeros(y_ref.shape, F32)


def moe_experts(tile_plan, src3, xn, wg, wu, wd):
    n_tiles = src3.shape[0]
    d = xn.shape[1]
    _, _, de = wg.shape
    smem_tile = lambda imap: pl.BlockSpec((1, 1, MOE_TILE), imap, memory_space=pltpu.SMEM)
    hbm = lambda: pl.BlockSpec(memory_space=pl.ANY)
    grid_spec = pltpu.PrefetchScalarGridSpec(
        num_scalar_prefetch=5,
        grid=(n_tiles,),
        in_specs=[smem_tile(lambda i, *_: (i, 0, 0)),
                  smem_tile(lambda i, *_: (jnp.minimum(i + 1, n_tiles - 1), 0, 0)),
                  hbm(), hbm(), hbm(), hbm()],
        out_specs=pl.BlockSpec((MOE_TILE, d), lambda i, *_: (i, 0)),
        scratch_shapes=[pltpu.VMEM((2, MOE_TILE, d), F32), pltpu.SemaphoreType.DMA((2,)),
                        pltpu.VMEM((2, d, de), F32), pltpu.VMEM((2, d, de), F32), pltpu.VMEM((2, de, d), F32),
                        pltpu.SemaphoreType.DMA((3, 2)),
                        pltpu.VMEM((d, de), BF16), pltpu.VMEM((d, de), BF16), pltpu.VMEM((de, d), BF16)],
    )
    return pl.pallas_call(
        _moe_expert_body,
        grid_spec=grid_spec,
        out_shape=jax.ShapeDtypeStruct((n_tiles * MOE_TILE, d), F32),
        compiler_params=_params(("arbitrary",), 56),
        name="moe_experts",
    )(*tile_plan, src3, src3, xn, wg, wu, wd)


def _moe_combine_body(d1_cur, d1_nxt, d2_cur, d2_nxt, ys_hbm, xp_ref, xs_ref, info_ref, gf_ref, yp_ref, ysm_ref,
                      buf1, buf2, sem1, sem2, *, n_ptiles):
    i = pl.program_id(0)
    slot = lax.rem(i, 2)

    @pl.when(i == 0)
    def _():
        _issue_rows(d1_cur, ys_hbm, buf1, 0, sem1)
        _issue_rows(d2_cur, ys_hbm, buf2, 0, sem2)

    @pl.when(i + 1 < pl.num_programs(0))
    def _():
        _issue_rows(d1_nxt, ys_hbm, buf1, 1 - slot, sem1)
        _issue_rows(d2_nxt, ys_hbm, buf2, 1 - slot, sem2)

    _wait_rows(ys_hbm, buf1, slot, sem1)
    _wait_rows(ys_hbm, buf2, slot, sem2)

    def finish(x_ref, y_ref):
        info = info_ref[...]
        xo = x_ref[...] + (info[:, INFO_C1:INFO_C1 + 1] * buf1[slot] + info[:, INFO_C2:INFO_C2 + 1] * buf2[slot])
        var = jnp.mean(xo * xo, axis=-1, keepdims=True)
        y_ref[...] = (xo * lax.rsqrt(var + EPS)) * gf_ref[...]

    @pl.when(i < n_ptiles)
    def _():
        finish(xp_ref, yp_ref)

    @pl.when(i >= n_ptiles)
    def _():
        finish(xs_ref, ysm_ref)


def moe_combine(dest1, dest2, ys, x_prompt, x_sample, info, g_final):
    mp, d = x_prompt.shape
    n_ptiles = mp // MOE_TILE
    n_tiles = n_ptiles + 1
    cur = lambda: pl.BlockSpec((1, 1, MOE_TILE), lambda i: (i, 0, 0), memory_space=pltpu.SMEM)
    nxt = lambda: pl.BlockSpec((1, 1, MOE_TILE), lambda i: (jnp.minimum(i + 1, n_tiles - 1), 0, 0),
                               memory_space=pltpu.SMEM)
    ptile = lambda: pl.BlockSpec((MOE_TILE, d), lambda i: (jnp.minimum(i, n_ptiles - 1), 0))
    stile = lambda: pl.BlockSpec((MOE_TILE, d), lambda i: (0, 0))
    return pl.pallas_call(
        functools.partial(_moe_combine_body, n_ptiles=n_ptiles),
        grid=(n_tiles,),
        in_specs=[cur(), nxt(), cur(), nxt(), pl.BlockSpec(memory_space=pl.ANY), ptile(), stile(),
                  pl.BlockSpec((MOE_TILE, LANES), lambda i: (i, 0)), pl.BlockSpec((1, d), lambda i: (0, 0))],
        out_specs=[ptile(), stile()],
        out_shape=[jax.ShapeDtypeStruct((mp, d), F32), jax.ShapeDtypeStruct((MOE_TILE, d), F32)],
        scratch_shapes=[pltpu.VMEM((2, MOE_TILE, d), F32), pltpu.VMEM((2, MOE_TILE, d), F32),
                        pltpu.SemaphoreType.DMA((2,)), pltpu.SemaphoreType.DMA((2,))],
        compiler_params=_params(("arbitrary",), 40),
        name="moe_combine",
    )(dest1, dest1, dest2, dest2, ys, x_prompt, x_sample, info, g_final.reshape(1, d))


def _mixer_inputs(x2, g_mix, w_in_t, w_bg_t, w_mg_t, tm, q_dtype):
    tn = 1024
    xn = rmsnorm(x2, g_mix, tm, BF16)
    mm = functools.partial(matmul_cols, xn, tm=tm, w_is_t=True)
    u_rnn = mm(w_in_t, 0, D_RNN, tn=tn, out_dtype=F32, name="in_rnn")
    u_gate = mm(w_in_t, D_RNN, D_RNN, tn=tn, out_dtype=F32, name="in_gate")
    q = mm(w_in_t, 2 * D_RNN, N_HEADS * HEAD_DIM, tn=tn, out_dtype=q_dtype, name="in_q")
    kv0 = 2 * D_RNN + N_HEADS * HEAD_DIM
    kv_cmp, = matmul_kv(xn, w_in_t, kv0, tm, False, "in_kv_cmp")
    kv_sel, sel_planes = matmul_kv(xn, w_in_t, kv0 + 2 * KV_DIM, tm, True, "in_kv_sel")
    kv_win, win_planes = matmul_kv(xn, w_in_t, kv0 + 4 * KV_DIM, tm, True, "in_kv_win")
    u_bg = mm(w_bg_t, 0, w_bg_t.shape[0], tn=w_bg_t.shape[0], out_dtype=F32, name="in_bg")
    u_mg = mm(w_mg_t, 0, 2 * D_MODEL, tn=tn, out_dtype=F32, name="in_mg")
    return u_rnn, u_gate, q, (kv_cmp, kv_sel, kv_win), (sel_planes, win_planes), u_bg, u_mg


def kernel(x_prompt, x_sample, cache_cmp_kv, cache_sel_kv, cache_win_kv, state_conv, state_rglru, page_table, g_mix, w_in, conv_w, conv_b, lru_w_a, lru_b_a, lru_w_x, lru_b_x, lru_lambda, cmp_pe, cmp_w1, cmp_b1, cmp_w2, cmp_b2, w_out, g_ffn, moe_w_group, moe_b_group, moe_w_expert, moe_b_expert, moe_w_gate, moe_w_up, moe_w_down, g_final):
    b, t, d = x_prompt.shape
    ns = x_sample.shape[0]
    npages = page_table.shape[1]
    past_len = npages * PAGE_SIZE
    wbuf = cache_win_kv.shape[1]
    kvshape = (N_KV_HEADS, 2, HEAD_DIM)

    w_in_t = w_in.T
    bg0 = 2 * D_RNN + N_HEADS * HEAD_DIM + 6 * KV_DIM
    n_bg = 3 * N_HEADS
    w_bg = w_in_t[bg0:bg0 + n_bg].reshape(3, N_KV_HEADS, GROUP, d).transpose(1, 0, 2, 3).reshape(N_KV_HEADS, 3 * GROUP, d)
    w_bg = jnp.pad(w_bg, ((0, 0), (0, LANES - 3 * GROUP), (0, 0))).reshape(N_KV_HEADS * LANES, d)
    w_mg = w_in_t[bg0 + n_bg:]
    wa = lru_w_a.astype(BF16)
    wx = lru_w_x.astype(BF16)
    half_rows = CMP_STRIDE * HEAD_DIM
    wc = jnp.concatenate([cmp_w1[:, :half_rows], cmp_w1[:, half_rows:]], axis=-1).astype(BF16)
    pe8 = jnp.pad(cmp_pe.reshape(2, 2, half_rows), ((0, 0), (0, 6), (0, 0)))
    b1 = cmp_b1.reshape(2, 1, CMP_HIDDEN)
    w2 = cmp_w2.astype(BF16)
    b2 = cmp_b2.reshape(2, 1, HEAD_DIM)
    n_route = N_GROUPS + N_EXPERTS
    wr = jnp.pad(jnp.concatenate([moe_w_group, moe_w_expert], axis=1), ((0, 0), (0, LANES - n_route))).astype(BF16)
    br = jnp.pad(jnp.concatenate([moe_b_group, moe_b_expert]), (0, LANES - n_route)).reshape(1, LANES)
    lru_args = (conv_w, conv_b, wa, wx, lru_b_a, lru_b_x, lru_lambda)
    cmp_args = (wc, pe8, b1, w2, b2)

    xp2 = x_prompt.reshape(b * t, d)
    u_rnn, u_gate, q, kv2d, planes, u_bg, u_mg = _mixer_inputs(xp2, g_mix, w_in_t, w_bg, w_mg, 512, BF16)
    o_rnn, p_conv, p_h = lru_prompt(u_rnn.reshape(b, t, d), u_gate.reshape(b, t, d), *lru_args, 256)
    p_cmp, p_sel, p_win_full = (a.reshape((b, t) + kvshape) for a in kv2d)
    p_win = p_win_full[:, t - min(WINDOW, t):]
    pt_prompt = jnp.arange(b * t // PAGE_SIZE, dtype=I32).reshape(b, t // PAGE_SIZE)
    kvc_p = compress(kv2d[0], pt_prompt, jnp.zeros((b * CMP_STRIDE * HK, HEAD_DIM), F32), *cmp_args)
    o_nsa = attn_prompt_s(q.reshape(b, t, -1), planes[0], planes[1], kvc_p, u_bg.reshape(b, t, -1),
                        t // CMP_STRIDE - 1)
    merged = merge(u_mg, o_rnn.reshape(b * t, d), o_nsa.reshape(b * t, d), 512)
    x_mid = matmul_cols(merged, w_out, 0, d, 512, 1024, F32, residual=xp2, name="out_proj")

    xs2 = x_sample.reshape(ns, d)
    su_rnn, su_gate, sq, skv2d, _, su_bg, su_mg = _mixer_inputs(xs2, g_mix, w_in_t, w_bg, w_mg, ns, F32)
    so_rnn, cn, s_h = lru_sample(su_rnn, su_gate, state_conv.transpose(1, 0, 2), state_rglru, *lru_args)
    s_conv = cn.transpose(1, 0, 2)
    kvs = jnp.stack([a.reshape(ns, HK, HEAD_DIM) for a in skv2d], axis=1)
    s_cmp = kvs[:, 0].reshape((ns, 1) + kvshape)
    s_sel = kvs[:, 1].reshape((ns, 1) + kvshape)

    extra = jnp.concatenate([kvs[:, 0], jnp.zeros((ns, (CMP_STRIDE - 1) * HK, HEAD_DIM), F32)], axis=1)
    kvc_s = compress(cache_cmp_kv.reshape(-1, HEAD_DIM), page_table, extra.reshape(-1, HEAD_DIM), *cmp_args)
    n_cmp_s = -(-(past_len + 1) // CMP_STRIDE) - 1
    n_sel_s = -(-(past_len + 1) // SEL_BLK)
    nsp = -(-n_sel_s // LANES) * LANES
    q16 = sq.reshape(ns, N_HEADS, HEAD_DIM)
    m_s = jnp.asarray(_overlap_matrix(n_cmp_s, kvc_s.shape[2], n_sel_s, nsp), BF16)
    so_cmp, imp = cmp_sample(q16, kvc_s, m_s, past_len, n_cmp_s)
    idx = rank_sample(imp.reshape(ns * N_KV_HEADS, nsp), past_len // SEL_BLK, n_sel_s)

    def per_head_rows(new_kv, kv):
        return jnp.repeat(new_kv[:, kv::2], GROUP, axis=1)

    so_sel = sel_sample(idx, page_table, q16, per_head_rows(kvs[:, 1], 0), per_head_rows(kvs[:, 1], 1),
                        cache_sel_kv.reshape((-1,) + kvshape), past_len // SEL_BLK)
    bgs = su_bg.reshape(ns, N_KV_HEADS, LANES)[:, :, :3 * GROUP].reshape(ns, N_KV_HEADS, 3, GROUP)
    bgs = jnp.pad(bgs.transpose(0, 1, 3, 2).reshape(ns, N_HEADS, 3), ((0, 0), (0, 0), (0, LANES - 3)))
    so_nsa, s_win2d = win_sample(q16, cache_win_kv.reshape(-1, HEAD_DIM), skv2d[2], per_head_rows(kvs[:, 2], 0),
                                 per_head_rows(kvs[:, 2], 1), so_cmp, so_sel, bgs, wbuf)
    s_win = s_win2d.reshape((ns, wbuf) + kvshape)
    s_merged = merge(su_mg, so_rnn, so_nsa.reshape(ns, d), ns)
    sx_mid = matmul_cols(s_merged, w_out, 0, d, ns, 1024, F32, residual=xs2, name="out_proj_s")

    n_real = b * t + ns
    assert ns <= MOE_TILE
    sx_pad = jnp.pad(sx_mid, ((0, MOE_TILE - ns), (0, 0)))
    xn_all, info, counts = route(x_mid, sx_pad, g_ffn, wr, br, n_real)
    n_tiles = -(-(2 * n_real + N_EXPERTS * (MOE_TILE - 1)) // MOE_TILE)
    src3, dest1, dest2, tile_plan = _moe_plan(info, counts, n_real, n_tiles)
    ys = moe_experts(tile_plan, src3, xn_all, moe_w_gate, moe_w_up, moe_w_down)
    y_p2, y_s2 = moe_combine(dest1, dest2, ys, x_mid, sx_pad, info, g_final)
    y_prompt = y_p2.reshape(b, t, d)
    y_sample = y_s2[:ns].reshape(ns, 1, d)

    return (y_prompt, y_sample, p_cmp, s_cmp, p_sel, s_sel, p_win, s_win, p_conv, s_conv,
            p_h.reshape(b, d), s_h)
```

```python
import functools

import jax
import jax.numpy as jnp
import numpy as np
from jax import lax
from jax.experimental import pallas as pl
from jax.experimental.pallas import tpu as pltpu

F32 = jnp.float32
BF16 = jnp.bfloat16
I32 = jnp.int32

D_MODEL = 2048
D_RNN = 2048
RNN_BLOCKS = 16
RNN_BLOCK_DIM = 128
CONV_W = 4
LRU_C = 8.0
N_HEADS = 16
HEAD_DIM = 128
N_KV_HEADS = 4
GROUP = 4
KV_DIM = N_KV_HEADS * HEAD_DIM
HK = 2 * N_KV_HEADS
CMP_STRIDE = 16
CMP_BLK = 32
CMP_HIDDEN = 256
SEL_BLK = 64
SEL_SHIFT = 6
N_SEL = 16
WINDOW = 512
PAGE_SIZE = 128
ATTN_SCALE = HEAD_DIM ** -0.5
FORCED_SCORE = 1e4
N_GROUPS = 4
EXP_PER_GROUP = 8
N_EXPERTS = 32
D_EXPERT = 512
EPS = 1e-6
MASK_VALUE = -1e30
LANES = 128
MIB = 1024 * 1024

_NT = (((1,), (1,)), ((), ()))


def _params(sem, vmem_mib):
    return pltpu.CompilerParams(dimension_semantics=sem, vmem_limit_bytes=vmem_mib * MIB)


def _msoftmax(s, mask):
    sm = jnp.where(mask, s, MASK_VALUE)
    m = jnp.max(sm, axis=-1, keepdims=True)
    e = jnp.where(mask, jnp.exp(sm - m), 0.0)
    den = jnp.sum(e, axis=-1, keepdims=True)
    return e * jnp.where(den > 0.0, 1.0 / den, 0.0)


def _gelu_tanh(x):
    return 0.5 * x * (1.0 + jnp.tanh(0.7978845608028654 * (x + 0.044715 * (x * x * x))))


def _rmsnorm_body(x_ref, g_ref, o_ref):
    x = x_ref[...]
    var = jnp.mean(x * x, axis=-1, keepdims=True)
    o_ref[...] = ((x * lax.rsqrt(var + EPS)) * g_ref[...]).astype(o_ref.dtype)


def rmsnorm(x, g, tm, out_dtype):
    m, d = x.shape
    return pl.pallas_call(
        _rmsnorm_body,
        grid=(m // tm,),
        in_specs=[pl.BlockSpec((tm, d), lambda i: (i, 0)), pl.BlockSpec((1, d), lambda i: (0, 0))],
        out_specs=pl.BlockSpec((tm, d), lambda i: (i, 0)),
        out_shape=jax.ShapeDtypeStruct((m, d), out_dtype),
        compiler_params=_params(("arbitrary",), 56),
        name="rmsnorm",
    )(x, g.reshape(1, d))


def _xw(x, wbf, w_is_t):
    if w_is_t:
        return lax.dot_general(x, wbf, _NT, preferred_element_type=F32)
    return jnp.dot(x, wbf, preferred_element_type=F32)


def _mm_body(x_ref, w_ref, *rest, w_is_t):
    r_ref = rest[0] if len(rest) == 3 else None
    o_ref, wbf_ref = rest[-2:]

    @pl.when(pl.program_id(1) == 0)
    def _():
        wbf_ref[...] = w_ref[...].astype(BF16)

    res = _xw(x_ref[...], wbf_ref[...], w_is_t)
    if r_ref is not None:
        res = r_ref[...] + res
    o_ref[...] = res.astype(o_ref.dtype)


def matmul_cols(x, w, col0, ncols, tm, tn, out_dtype, residual=None, w_is_t=False, name="matmul"):
    m, k = x.shape
    cb0 = col0 // tn
    assert col0 % tn == 0 and ncols % tn == 0 and m % tm == 0
    if w_is_t:
        w_spec = pl.BlockSpec((tn, k), lambda j, i: (cb0 + j, 0))
        w_tile = (tn, k)
    else:
        w_spec = pl.BlockSpec((k, tn), lambda j, i: (0, cb0 + j))
        w_tile = (k, tn)
    in_specs = [pl.BlockSpec((tm, k), lambda j, i: (i, 0)), w_spec]
    args = [x, w]
    if residual is not None:
        in_specs.append(pl.BlockSpec((tm, tn), lambda j, i: (i, j)))
        args.append(residual)
    return pl.pallas_call(
        functools.partial(_mm_body, w_is_t=w_is_t),
        grid=(ncols // tn, m // tm),
        in_specs=in_specs,
        out_specs=pl.BlockSpec((tm, tn), lambda j, i: (i, j)),
        out_shape=jax.ShapeDtypeStruct((m, ncols), out_dtype),
        scratch_shapes=[pltpu.VMEM(w_tile, BF16)],
        compiler_params=_params(("arbitrary", "arbitrary"), 48),
        name=name,
    )(*args)


def _mm_kv_body(x_ref, w_ref, oi_ref, *rest):
    wbf_ref = rest[-1]

    @pl.when(pl.program_id(0) == 0)
    def _():
        wbf_ref[...] = w_ref[...].astype(BF16)

    res = _xw(x_ref[...], wbf_ref[...], True)
    tm = res.shape[0]
    for hk in range(HK):
        blk = res[:, hk * HEAD_DIM:(hk + 1) * HEAD_DIM]
        oi_ref[pl.ds(hk, tm, stride=HK), :] = blk
        if len(rest) == 2:
            rest[0][hk] = blk.astype(BF16)


def matmul_kv(x, w_t, col0, tm, planes, name):
    m, k = x.shape
    ncols = HK * HEAD_DIM
    assert col0 % ncols == 0 and m % tm == 0
    out_specs = [pl.BlockSpec((tm * HK, HEAD_DIM), lambda i: (i, 0))]
    out_shape = [jax.ShapeDtypeStruct((m * HK, HEAD_DIM), F32)]
    if planes:
        out_specs.append(pl.BlockSpec((HK, tm, HEAD_DIM), lambda i: (0, i, 0)))
        out_shape.append(jax.ShapeDtypeStruct((HK, m, HEAD_DIM), BF16))
    return pl.pallas_call(
        _mm_kv_body,
        grid=(m // tm,),
        in_specs=[pl.BlockSpec((tm, k), lambda i: (i, 0)), pl.BlockSpec((ncols, k), lambda i: (col0 // ncols, 0))],
        out_specs=out_specs,
        out_shape=out_shape,
        scratch_shapes=[pltpu.VMEM((ncols, k), BF16)],
        compiler_params=_params(("arbitrary",), 56),
        name=name,
    )(x, w_t)


def _merge_body(mg_ref, rnn_ref, nsa_ref, o_ref):
    d = rnn_ref.shape[-1]
    ga = jax.nn.sigmoid(mg_ref[:, :d])
    gb = jax.nn.sigmoid(mg_ref[:, d:])
    o_ref[...] = (ga * rnn_ref[...] + gb * nsa_ref[...]).astype(o_ref.dtype)


def merge(u_mg, o_rnn, o_nsa, tm):
    m, d = o_rnn.shape
    return pl.pallas_call(
        _merge_body,
        grid=(m // tm,),
        in_specs=[pl.BlockSpec((tm, 2 * d), lambda i: (i, 0)), pl.BlockSpec((tm, d), lambda i: (i, 0)),
                  pl.BlockSpec((tm, d), lambda i: (i, 0))],
        out_specs=pl.BlockSpec((tm, d), lambda i: (i, 0)),
        out_shape=jax.ShapeDtypeStruct((m, d), BF16),
        compiler_params=_params(("arbitrary",), 48),
        name="merge",
    )(u_mg, o_rnn, o_nsa)


def _lru_gates(uc, wa_ref, wx_ref, ba_ref, bx_ref, lam_ref):
    ucb = uc.astype(BF16)
    ra, ri = [], []
    for n in range(RNN_BLOCKS):
        blk = ucb[:, n * RNN_BLOCK_DIM:(n + 1) * RNN_BLOCK_DIM]
        ra.append(jnp.dot(blk, wa_ref[n], preferred_element_type=F32))
        ri.append(jnp.dot(blk, wx_ref[n], preferred_element_type=F32))
    r = jax.nn.sigmoid(jnp.concatenate(ra, axis=-1) + ba_ref[...])
    i = jax.nn.sigmoid(jnp.concatenate(ri, axis=-1) + bx_ref[...])
    z = -lam_ref[...]
    softplus = jnp.maximum(z, 0.0) + jnp.log1p(jnp.exp(-jnp.abs(z)))
    log_a = (-LRU_C * r) * softplus
    a = jnp.exp(log_a)
    mult = jnp.sqrt(-jnp.tanh(log_a) * (a * a + 1.0))
    return a, mult * (i * uc)


def _lru_prompt_body(u_ref, gate_ref, cw_ref, cb_ref, wa_ref, wx_ref, ba_ref, bx_ref, lam_ref,
                     o_ref, conv_ref, hlast_ref, xbuf, hcar, acum, bcum):
    t = pl.program_id(1)
    nt = pl.num_programs(1)
    tt, d = u_ref.shape[1], u_ref.shape[2]

    @pl.when(t == 0)
    def _():
        xbuf[0:8, :] = jnp.zeros((8, d), F32)
        hcar[...] = jnp.zeros((8, d), F32)

    x = u_ref[0]
    xbuf[8:8 + tt, :] = x
    uc = (cw_ref[0:1, :] * xbuf[5:5 + tt, :] + cw_ref[1:2, :] * xbuf[6:6 + tt, :]
          + cw_ref[2:3, :] * xbuf[7:7 + tt, :] + cw_ref[3:4, :] * x) + cb_ref[...]
    xbuf[0:8, :] = xbuf[tt:tt + 8, :]

    a, b = _lru_gates(uc, wa_ref, wx_ref, ba_ref, bx_ref, lam_ref)
    g = tt // 8
    a3 = a.reshape(g, 8, d)
    b3 = b.reshape(g, 8, d)
    row = lax.broadcasted_iota(I32, (g, 8, d), 1)
    for s in (1, 2, 4):
        a_sh = pltpu.roll(a3, s, axis=1)
        b_sh = pltpu.roll(b3, s, axis=1)
        keep = row >= s
        b3 = jnp.where(keep, a3 * b_sh + b3, b3)
        a3 = jnp.where(keep, a3 * a_sh, a3)
    acum[...] = a3
    bcum[...] = b3

    def body(gi, h):
        hg = bcum[gi] + acum[gi] * h
        bcum[gi] = hg
        return jnp.broadcast_to(hg[7:8, :], (8, d))

    hfin = lax.fori_loop(0, g, body, hcar[...])
    hcar[...] = hfin
    o_ref[0] = bcum[...].reshape(tt, d) * _gelu_tanh(gate_ref[0])

    @pl.when(t == nt - 1)
    def _():
        conv_ref[0] = x[tt - (CONV_W - 1):tt, :]
        hlast_ref[0] = hfin[0:1, :]


def lru_prompt(u_rnn, u_gate, conv_w, conv_b, wa, wx, ba, bx, lam, tt):
    b, t, d = u_rnn.shape
    vec = lambda: pl.BlockSpec((1, d), lambda i, j: (0, 0))
    wspec = lambda: pl.BlockSpec((RNN_BLOCKS, RNN_BLOCK_DIM, RNN_BLOCK_DIM), lambda i, j: (0, 0, 0))
    return pl.pallas_call(
        _lru_prompt_body,
        grid=(b, t // tt),
        in_specs=[pl.BlockSpec((1, tt, d), lambda i, j: (i, j, 0)), pl.BlockSpec((1, tt, d), lambda i, j: (i, j, 0)),
                  pl.BlockSpec((CONV_W, d), lambda i, j: (0, 0)), vec(), wspec(), wspec(), vec(), vec(), vec()],
        out_specs=[pl.BlockSpec((1, tt, d), lambda i, j: (i, j, 0)),
                   pl.BlockSpec((1, CONV_W - 1, d), lambda i, j: (i, 0, 0)),
                   pl.BlockSpec((1, 1, d), lambda i, j: (i, 0, 0))],
        out_shape=[jax.ShapeDtypeStruct((b, t, d), F32), jax.ShapeDtypeStruct((b, CONV_W - 1, d), F32),
                   jax.ShapeDtypeStruct((b, 1, d), F32)],
        scratch_shapes=[pltpu.VMEM((tt + 8, d), F32), pltpu.VMEM((8, d), F32),
                        pltpu.VMEM((tt // 8, 8, d), F32), pltpu.VMEM((tt // 8, 8, d), F32)],
        compiler_params=_params(("arbitrary", "arbitrary"), 56),
        name="lru_prompt",
    )(u_rnn, u_gate, conv_w, conv_b.reshape(1, d), wa, wx, ba.reshape(1, d), bx.reshape(1, d), lam.reshape(1, d))


def _lru_sample_body(u_ref, gate_ref, cp_ref, h0_ref, cw_ref, cb_ref, wa_ref, wx_ref, ba_ref, bx_ref, lam_ref,
                     o_ref, cn_ref, h_ref):
    x = u_ref[...]
    uc = (cw_ref[0:1, :] * cp_ref[0] + cw_ref[1:2, :] * cp_ref[1] + cw_ref[2:3, :] * cp_ref[2]
          + cw_ref[3:4, :] * x) + cb_ref[...]
    a, b = _lru_gates(uc, wa_ref, wx_ref, ba_ref, bx_ref, lam_ref)
    h = a * h0_ref[...] + b
    o_ref[...] = h * _gelu_tanh(gate_ref[...])
    h_ref[...] = h
    cn_ref[0] = cp_ref[1]
    cn_ref[1] = cp_ref[2]
    cn_ref[2] = x


def lru_sample(u_rnn, u_gate, conv_prev_t, h0, conv_w, conv_b, wa, wx, ba, bx, lam):
    n, d = u_rnn.shape
    return pl.pallas_call(
        _lru_sample_body,
        out_shape=[jax.ShapeDtypeStruct((n, d), F32), jax.ShapeDtypeStruct((CONV_W - 1, n, d), F32),
                   jax.ShapeDtypeStruct((n, d), F32)],
        compiler_params=pltpu.CompilerParams(vmem_limit_bytes=40 * MIB),
        name="lru_sample",
    )(u_rnn, u_gate, conv_prev_t, h0, conv_w, conv_b.reshape(1, d), wa, wx, ba.reshape(1, d), bx.reshape(1, d),
      lam.reshape(1, d))


COMPRESS_PAGES = 32
CHUNKS_PER_PAGE = PAGE_SIZE // CMP_STRIDE


def _compress_body(pt_ref, *refs, pp, n_steps):
    page_refs = refs[:pp]
    nxt_ref, extra_ref, wc_ref, pe_ref, b1_ref, w2_ref, b2_ref, o_ref, res_scr, pet_scr = refs[pp:]
    del pt_ref
    is_last = pl.program_id(1) == n_steps - 1
    m_rows = pp * CHUNKS_PER_PAGE * N_KV_HEADS
    lo4 = lax.broadcasted_iota(I32, (4, 8, 8, LANES), 2) < 4
    lo3 = lax.broadcasted_iota(I32, (8, 8, LANES), 1) < 4

    for k in range(2):
        lhs_sp = [[] for _ in range(8)]
        for pr in page_refs:
            xk = pr[pl.ds(k, PAGE_SIZE * N_KV_HEADS, stride=2), :]
            x5 = xk.reshape(4, 2, 8, 8, LANES)
            a0 = x5[:, 0]
            a1 = x5[:, 1]
            be = jnp.where(lo4, a0, pltpu.roll(a1, 4, axis=2))
            bo = jnp.where(lo4, pltpu.roll(a0, 4, axis=2), a1)
            for sp in range(8):
                lhs_sp[sp].append(jnp.concatenate([be[:, sp], bo[:, sp]], axis=-1).reshape(32, 2 * LANES))
        acc = jnp.zeros((m_rows, 2 * CMP_HIDDEN), F32)
        for sp in range(8):
            lhs = jnp.concatenate(lhs_sp[sp], axis=0).astype(BF16)
            acc = acc + jnp.dot(lhs, wc_ref[k, sp * 256:(sp + 1) * 256, :], preferred_element_type=F32)

        @pl.when((pl.program_id(0) == 0) & (pl.program_id(1) == 0))
        def _():
            pet_scr[k] = jnp.dot(pe_ref[k].astype(BF16), wc_ref[k], preferred_element_type=F32)

        peb0 = pet_scr[k, 0:1, :CMP_HIDDEN]
        peb1 = pet_scr[k, 1:2, CMP_HIDDEN:]
        h0 = acc[:, :CMP_HIDDEN] + peb0
        h1 = acc[:, CMP_HIDDEN:] + peb1

        xn = jnp.where(is_last, extra_ref[pl.ds(k, CMP_STRIDE * N_KV_HEADS, stride=2), :],
                       nxt_ref[pl.ds(k, CMP_STRIDE * N_KV_HEADS, stride=2), :])
        xn3 = xn.reshape(8, 8, LANES)
        ln = jnp.concatenate([jnp.where(lo3, xn3, 0.0), jnp.where(lo3, 0.0, xn3)], axis=-1).astype(BF16)
        nacc = jnp.zeros((8, CMP_HIDDEN), F32)
        for sp in range(8):
            nacc = nacc + jnp.dot(ln[sp], wc_ref[k, sp * 256:(sp + 1) * 256, CMP_HIDDEN:],
                                  preferred_element_type=F32)
        n8 = nacc + pltpu.roll(nacc, 4, axis=0) + peb1
        h1e = jnp.concatenate([h1, n8], axis=0)
        h1s = pltpu.roll(h1e, m_rows + 8 - N_KV_HEADS, axis=0)[:m_rows]
        pre = h0 + h1s + b1_ref[k]
        out_k = jnp.dot(_gelu_tanh(pre).astype(BF16), w2_ref[k], preferred_element_type=F32) + b2_ref[k]
        res_scr[...] = out_k
        for h in range(N_KV_HEADS):
            o_ref[0, 2 * h + k] = res_scr[pl.ds(h, m_rows // N_KV_HEADS, stride=N_KV_HEADS), :]


def compress(pool2d, page_table, extra2d, wc, pe8, b1, w2, b2):
    nseq, npages = page_table.shape
    pp = min(COMPRESS_PAGES, npages)
    n_steps = npages // pp
    assert npages % pp == 0
    rows_page = PAGE_SIZE * HK
    rows_chunk = CMP_STRIDE * HK
    blocks_step = pp * CHUNKS_PER_PAGE

    def page_spec(i):
        return pl.BlockSpec((rows_page, LANES), lambda b, s, pt: (pt[b, s * pp + i], 0))

    def nxt_map(b, s, pt):
        return (pt[b, jnp.minimum((s + 1) * pp, npages - 1)] * CHUNKS_PER_PAGE, 0)

    const3 = lambda shape: pl.BlockSpec(shape, lambda b, s, pt: (0, 0, 0))
    in_specs = [page_spec(i) for i in range(pp)] + [
        pl.BlockSpec((rows_chunk, LANES), nxt_map),
        pl.BlockSpec((rows_chunk, LANES), lambda b, s, pt: (b, 0)),
        const3((2, CMP_STRIDE * HEAD_DIM, 2 * CMP_HIDDEN)),
        const3((2, 8, CMP_STRIDE * HEAD_DIM)),
        const3((2, 1, CMP_HIDDEN)),
        const3((2, CMP_HIDDEN, HEAD_DIM)),
        const3((2, 1, HEAD_DIM)),
    ]
    grid_spec = pltpu.PrefetchScalarGridSpec(
        num_scalar_prefetch=1,
        grid=(nseq, n_steps),
        in_specs=in_specs,
        out_specs=pl.BlockSpec((1, HK, blocks_step, HEAD_DIM), lambda b, s, pt: (b, 0, s, 0)),
        scratch_shapes=[pltpu.VMEM((blocks_step * N_KV_HEADS, HEAD_DIM), F32),
                        pltpu.VMEM((2, 8, 2 * CMP_HIDDEN), F32)],
    )
    return pl.pallas_call(
        functools.partial(_compress_body, pp=pp, n_steps=n_steps),
        grid_spec=grid_spec,
        out_shape=jax.ShapeDtypeStruct((nseq, HK, npages * CHUNKS_PER_PAGE, HEAD_DIM), F32),
        compiler_params=_params(("arbitrary", "arbitrary"), 56),
        name="compress",
    )(page_table, *([pool2d] * pp), pool2d, extra2d, wc, pe8, b1, w2, b2)


def _overlap_matrix(n_cmp, n_cmp_pad, n_sel, n_sel_pad):
    cs = np.arange(n_cmp_pad)[:, None] * CMP_STRIDE
    ss = np.arange(n_sel_pad)[None, :] * SEL_BLK
    ov = np.minimum(cs + CMP_BLK, ss + SEL_BLK) - np.maximum(cs, ss)
    m = np.clip(ov, 0, CMP_BLK).astype(np.float32) / CMP_BLK
    m[n_cmp:, :] = 0.0
    m[:, n_sel:] = 0.0
    return m


QBLK = 128


KCHUNK = 64
KPIECE = 512
LOG2E = 1.4426950408889634


def _softmax_passes(s_scr, p_scr, n_plain, n_masked, n_total, plain_mask, full_mask):
    cols = s_scr.shape[1]
    grp = KCHUNK // 8
    c2 = ATTN_SCALE * LOG2E

    def rows(c):
        return pl.ds(pl.multiple_of(c * KCHUNK, KCHUNK), KCHUNK)

    def masked_max(mask_fn):
        def body(c, m8):
            sm = jnp.where(mask_fn(c), s_scr[rows(c), :], MASK_VALUE)
            s_scr[rows(c), :] = sm
            return jnp.maximum(m8, jnp.max(sm.reshape(grp, 8, cols), axis=0))
        return body

    m8 = lax.fori_loop(0, n_plain, masked_max(plain_mask), jnp.full((8, cols), MASK_VALUE, F32))
    m8 = lax.fori_loop(n_plain, n_plain + n_masked, masked_max(full_mask), m8)
    mc = jnp.max(m8, axis=0, keepdims=True) * c2

    def pass_exp(c, l8):
        p = jnp.exp2(s_scr[rows(c), :] * c2 - mc)
        p_scr[rows(c), :] = p.astype(BF16)
        return l8 + jnp.sum(p.reshape(grp, 8, cols), axis=0)

    l8 = lax.fori_loop(0, n_plain + n_masked, pass_exp, jnp.zeros((8, cols), F32))

    def zero_fill(c, carry):
        p_scr[rows(c), :] = jnp.zeros((KCHUNK, cols), BF16)
        return carry

    lax.fori_loop(n_plain + n_masked, n_total, zero_fill, 0)
    return 1.0 / jnp.sum(l8, axis=0, keepdims=True)


def _attn_prompt_s_body(q_ref, ksel_ref, vsel_ref, kwin_ref, vwin_ref, kvc_ref, bg_ref, mt_ref, o_ref,
                        vsel_t, vwin_t, s_scr, p_scr, sel_scr, *, n_cmp, n_sel):
    qi = pl.program_id(2)
    q0 = qi * QBLK
    cols = GROUP * QBLK

    @pl.when(qi == 0)
    def _():
        vsel_t[...] = vsel_ref[0].astype(F32).T.astype(BF16)
        vwin_t[...] = vwin_ref[0].astype(F32).T.astype(BF16)

    q = q_ref[0]
    q4 = jnp.concatenate([q[:, g * HEAD_DIM:(g + 1) * HEAD_DIM] for g in range(GROUP)], axis=0)
    qpos_row = q0 + (lax.broadcasted_iota(I32, (1, cols), 1) & (QBLK - 1))
    row_in_chunk = lax.broadcasted_iota(I32, (KCHUNK, cols), 0)

    nb = kvc_ref.shape[2]
    kc = kvc_ref[0, 0].astype(BF16)
    vc_t = kvc_ref[0, 1].T.astype(BF16)
    n_io = lax.broadcasted_iota(I32, (nb, cols), 0)
    mask_c = (n_io * CMP_STRIDE + (CMP_BLK - 1) <= qpos_row) & (n_io < n_cmp)
    sc = jnp.where(mask_c, lax.dot_general(kc, q4, _NT, preferred_element_type=F32) * ATTN_SCALE, MASK_VALUE)
    ec = jnp.where(mask_c, jnp.exp(sc - jnp.max(sc, axis=0, keepdims=True)), 0.0)
    den_c = jnp.sum(ec, axis=0, keepdims=True)
    pc = ec * jnp.where(den_c > 0.0, 1.0 / den_c, 0.0)
    o_c = jnp.dot(vc_t, pc.astype(BF16), preferred_element_type=F32)
    psum = pc[:, 0:QBLK] + pc[:, QBLK:2 * QBLK] + pc[:, 2 * QBLK:3 * QBLK] + pc[:, 3 * QBLK:4 * QBLK]

    nsp = mt_ref.shape[0]
    imp_t = jnp.dot(mt_ref[...], psum.astype(BF16), preferred_element_type=F32)
    jt = lax.broadcasted_iota(I32, (nsp, QBLK), 0)
    cur = lax.shift_right_logical(q0 + lax.broadcasted_iota(I32, (nsp, QBLK), 1), SEL_SHIFT)
    valid = (jt <= cur) & (jt < n_sel)
    forced = (jt == 0) | (jt >= cur - 1)
    score = jnp.where(forced, FORCED_SCORE, imp_t)
    score = jnp.where(valid, score, -1.0)
    rank = jnp.zeros((nsp, QBLK), I32)
    for k in range(n_sel):
        sk = score[k:k + 1, :]
        beats = (sk > score) | ((sk == score) & (k < jt))
        rank = rank + beats.astype(I32)
    sel_t = jnp.where((rank < N_SEL) & valid, 1.0, 0.0)
    sel_scr[...] = jnp.concatenate([sel_t] * GROUP, axis=1)

    wkeys = WINDOW + QBLK
    ks = pl.multiple_of(jnp.maximum(q0 - WINDOW, 0), QBLK)
    s_scr[0:wkeys, :] = lax.dot_general(kwin_ref[0, pl.ds(ks, wkeys), :], q4, _NT, preferred_element_type=F32)

    def win_mask(c):
        dist = qpos_row - (ks + c * KCHUNK + row_in_chunk)
        return (dist >= 0) & (dist < WINDOW)

    n_wchunks = wkeys // KCHUNK
    inv_w = _softmax_passes(s_scr, p_scr, 0, n_wchunks, n_wchunks, win_mask, win_mask)
    o_w = jnp.dot(vwin_t[:, pl.ds(ks, wkeys)], p_scr[0:wkeys, :], preferred_element_type=F32) * inv_w

    n_pieces = qi // (KPIECE // QBLK) + 1

    def score_piece(c, carry):
        r0 = pl.multiple_of(c * KPIECE, KPIECE)
        s_scr[pl.ds(r0, KPIECE), :] = lax.dot_general(ksel_ref[0, pl.ds(r0, KPIECE), :], q4, _NT,
                                                      preferred_element_type=F32)
        return carry

    lax.fori_loop(0, n_pieces, score_piece, 0)

    def picked(c):
        return sel_scr[pl.ds(c, 1), :] > 0.5

    def picked_causal(c):
        return picked(c) & (c * KCHUNK + row_in_chunk <= qpos_row)

    chunks_q = QBLK // KCHUNK
    inv_s = _softmax_passes(s_scr, p_scr, qi * chunks_q, chunks_q, n_pieces * (KPIECE // KCHUNK),
                            picked, picked_causal)

    def pv_piece(c, acc):
        r0 = pl.multiple_of(c * KPIECE, KPIECE)
        return acc + jnp.dot(vsel_t[:, pl.ds(r0, KPIECE)], p_scr[pl.ds(r0, KPIECE), :], preferred_element_type=F32)

    o_s = lax.fori_loop(0, n_pieces, pv_piece, jnp.zeros((HEAD_DIM, cols), F32)) * inv_s

    gates_t = jax.nn.sigmoid(bg_ref[0]).T
    grow = lambda br: jnp.concatenate([gates_t[br * GROUP + g:br * GROUP + g + 1, :] for g in range(GROUP)], axis=1)
    o_t = grow(0) * o_c + grow(1) * o_s + grow(2) * o_w
    for g in range(GROUP):
        o_ref[0, :, g * HEAD_DIM:(g + 1) * HEAD_DIM] = o_t[:, g * QBLK:(g + 1) * QBLK].T


def attn_prompt_s(q3, sel_planes, win_planes, kvc, ubg3, n_cmp):
    b, t, _ = q3.shape
    nb = kvc.shape[2]
    assert KCHUNK == SEL_BLK and t % KPIECE == 0 and t >= WINDOW + QBLK
    n_sel = -(-t // SEL_BLK)
    nsp = max(8, -(-n_sel // 8) * 8)
    m_t = _overlap_matrix(n_cmp, nb, n_sel, nsp).T
    kspec = lambda: pl.BlockSpec((1, t, HEAD_DIM), lambda i, h, j: (2 * h, i, 0))
    vspec = lambda: pl.BlockSpec((1, t, HEAD_DIM), lambda i, h, j: (2 * h + 1, i, 0))
    cols = GROUP * QBLK
    return pl.pallas_call(
        functools.partial(_attn_prompt_s_body, n_cmp=n_cmp, n_sel=n_sel),
        grid=(b, N_KV_HEADS, t // QBLK),
        in_specs=[
            pl.BlockSpec((1, QBLK, GROUP * HEAD_DIM), lambda i, h, j: (i, j, h)),
            kspec(), vspec(), kspec(), vspec(),
            pl.BlockSpec((1, 2, nb, HEAD_DIM), lambda i, h, j: (i, h, 0, 0)),
            pl.BlockSpec((1, QBLK, LANES), lambda i, h, j: (i, j, h)),
            pl.BlockSpec((nsp, nb), lambda i, h, j: (0, 0)),
        ],
        out_specs=pl.BlockSpec((1, QBLK, GROUP * HEAD_DIM), lambda i, h, j: (i, j, h)),
        out_shape=jax.ShapeDtypeStruct((b, t, N_HEADS * HEAD_DIM), F32),
        scratch_shapes=[pltpu.VMEM((HEAD_DIM, t), BF16), pltpu.VMEM((HEAD_DIM, t), BF16),
                        pltpu.VMEM((t, cols), F32), pltpu.VMEM((t, cols), BF16), pltpu.VMEM((nsp, cols), F32)],
        compiler_params=_params(("arbitrary", "arbitrary", "arbitrary"), 48),
        name="attn_prompt",
    )(q3, sel_planes, sel_planes, win_planes, win_planes, kvc, ubg3, jnp.asarray(m_t, BF16))


def _row_head(shape):
    return lax.shift_right_logical(lax.broadcasted_iota(I32, shape, 0), 2)


def _cmp_sample_body(q_ref, kvc_ref, m_ref, oc_ref, imp_ref, *, qpos, n_cmp):
    qb = q_ref[0].astype(BF16)
    nb = kvc_ref.shape[2]
    rg = _row_head((N_HEADS, nb))
    ncol = lax.broadcasted_iota(I32, (N_HEADS, nb), 1)
    mask = (ncol * CMP_STRIDE + (CMP_BLK - 1) <= qpos) & (ncol < n_cmp)
    row8 = lax.broadcasted_iota(I32, (8, nb), 0)
    o = jnp.zeros((N_HEADS, HEAD_DIM), F32)
    ps8 = jnp.zeros((8, nb), F32)
    for h in range(N_KV_HEADS):
        kc = kvc_ref[0, 2 * h].astype(BF16)
        vc = kvc_ref[0, 2 * h + 1].astype(BF16)
        s = lax.dot_general(qb, kc, _NT, preferred_element_type=F32) * ATTN_SCALE
        ph = jnp.where(rg == h, _msoftmax(s, mask), 0.0)
        o = o + jnp.dot(ph.astype(BF16), vc, preferred_element_type=F32)
        ps8 = jnp.where(row8 == h, jnp.sum(ph, axis=0, keepdims=True), ps8)
    oc_ref[0] = o
    imp_ref[0] = jnp.dot(ps8.astype(BF16), m_ref[...], preferred_element_type=F32)[0:N_KV_HEADS]


def cmp_sample(q16, kvc, m_mat, qpos, n_cmp):
    ns = q16.shape[0]
    nb = kvc.shape[2]
    nsp = m_mat.shape[1]
    return pl.pallas_call(
        functools.partial(_cmp_sample_body, qpos=qpos, n_cmp=n_cmp),
        grid=(ns,),
        in_specs=[pl.BlockSpec((1, N_HEADS, HEAD_DIM), lambda b: (b, 0, 0)),
                  pl.BlockSpec((1, HK, nb, HEAD_DIM), lambda b: (b, 0, 0, 0)),
                  pl.BlockSpec((nb, nsp), lambda b: (0, 0))],
        out_specs=[pl.BlockSpec((1, N_HEADS, HEAD_DIM), lambda b: (b, 0, 0)),
                   pl.BlockSpec((1, N_KV_HEADS, nsp), lambda b: (b, 0, 0))],
        out_shape=[jax.ShapeDtypeStruct((ns, N_HEADS, HEAD_DIM), F32),
                   jax.ShapeDtypeStruct((ns, N_KV_HEADS, nsp), F32)],
        compiler_params=_params(("arbitrary",), 40),
        name="cmp_sample",
    )(q16, kvc, m_mat)


def _rank_sample_body(imp_ref, idx_ref, sc_scr, *, cur, n_sel):
    npad, nr = sc_scr.shape
    st = imp_ref[...].T
    j = lax.broadcasted_iota(I32, (npad, nr), 0)
    real = j < n_sel
    valid = (j <= cur) & real
    forced = (j == 0) | (j >= cur - 1)
    score = jnp.where(forced, FORCED_SCORE, st)
    score = jnp.where(valid, score, -1.0)
    score = jnp.where(real, score, -2.0)
    sc_scr[...] = score

    def body(k, rank):
        sk = sc_scr[pl.ds(k, 1), :]
        beats = (sk > score) | ((sk == score) & (k < j))
        return rank + beats.astype(I32)

    rank = lax.fori_loop(0, n_sel, body, jnp.zeros((npad, nr), I32))
    sel = (rank < N_SEL) & valid
    for slot in range(N_SEL):
        hit = sel & (rank == slot)
        found = jnp.max(hit.astype(I32), axis=0, keepdims=True)
        val = jnp.sum(jnp.where(hit, j, 0), axis=0, keepdims=True)
        idx_ref[slot:slot + 1, :] = jnp.where(found > 0, val, -1)


def rank_sample(imp2, cur, n_sel):
    nr, npad = imp2.shape
    return pl.pallas_call(
        functools.partial(_rank_sample_body, cur=cur, n_sel=n_sel),
        out_shape=jax.ShapeDtypeStruct((N_SEL, nr), I32),
        scratch_shapes=[pltpu.VMEM((npad, nr), F32)],
        name="rank_sample",
    )(imp2)


def _sel_sample_body(idx_ref, pt_ref, q_ref, kn_ref, vn_ref, *rest, n_past_blocks):
    del pt_ref
    kvs = rest[:N_SEL * N_KV_HEADS]
    o_ref = rest[N_SEL * N_KV_HEADS]
    b = pl.program_id(0)
    q = q_ref[0]
    qb = q.astype(BF16)
    nk = N_SEL * SEL_BLK
    rg = _row_head((N_HEADS, nk))
    col_slot = lax.shift_right_logical(lax.broadcasted_iota(I32, (N_HEADS, nk), 1), SEL_SHIFT)

    sc = jnp.zeros((N_HEADS, nk), F32)
    okv = jnp.zeros((N_HEADS, nk), I32)
    for h in range(N_KV_HEADS):
        kcat = jnp.concatenate([kvs[s * N_KV_HEADS + h][:, 0, 0, :] for s in range(N_SEL)], axis=0).astype(BF16)
        sh = lax.dot_general(qb, kcat, _NT, preferred_element_type=F32)
        ok = jnp.zeros((N_HEADS, nk), I32)
        for s in range(N_SEL):
            blk = idx_ref[s, b * N_KV_HEADS + h]
            ok = jnp.where(col_slot == s, jnp.where((blk >= 0) & (blk < n_past_blocks), 1, 0), ok)
        sc = jnp.where(rg == h, sh, sc)
        okv = jnp.where(rg == h, ok, okv)
    sm = jnp.where(okv > 0, sc * ATTN_SCALE, MASK_VALUE)
    s_new = jnp.sum(q * kn_ref[0], axis=-1, keepdims=True) * ATTN_SCALE
    m = jnp.maximum(jnp.max(sm, axis=-1, keepdims=True), s_new)
    p = jnp.where(okv > 0, jnp.exp(sm - m), 0.0)
    p_new = jnp.exp(s_new - m)
    den = jnp.sum(p, axis=-1, keepdims=True) + p_new
    pv = jnp.zeros((N_HEADS, HEAD_DIM), F32)
    for h in range(N_KV_HEADS):
        vcat = jnp.concatenate([kvs[s * N_KV_HEADS + h][:, 0, 1, :] for s in range(N_SEL)], axis=0).astype(BF16)
        pv = pv + jnp.dot(jnp.where(rg == h, p, 0.0).astype(BF16), vcat, preferred_element_type=F32)
    o_ref[0] = (pv + p_new * vn_ref[0]) / den


def sel_sample(idx, page_table, q16, kn16, vn16, pool4, n_past_blocks):
    ns = q16.shape[0]
    blk_per_page = PAGE_SIZE // SEL_BLK

    def kv_spec(s, h):
        def imap(b, idx_r, pt_r):
            blk = jnp.clip(idx_r[s, b * N_KV_HEADS + h], 0, n_past_blocks - 1)
            return (pt_r[b, blk // blk_per_page] * blk_per_page + blk % blk_per_page, h, 0, 0)
        return pl.BlockSpec((SEL_BLK, 1, 2, HEAD_DIM), imap)

    head3 = lambda: pl.BlockSpec((1, N_HEADS, HEAD_DIM), lambda b, i, p: (b, 0, 0))
    kv_specs = [kv_spec(s, h) for s in range(N_SEL) for h in range(N_KV_HEADS)]
    grid_spec = pltpu.PrefetchScalarGridSpec(
        num_scalar_prefetch=2,
        grid=(ns,),
        in_specs=[head3(), head3(), head3()] + kv_specs,
        out_specs=head3(),
    )
    return pl.pallas_call(
        functools.partial(_sel_sample_body, n_past_blocks=n_past_blocks),
        grid_spec=grid_spec,
        out_shape=jax.ShapeDtypeStruct((ns, N_HEADS, HEAD_DIM), F32),
        compiler_params=_params(("arbitrary",), 40),
        name="sel_sample",
    )(idx, page_table, q16, kn16, vn16, *([pool4] * len(kv_specs)))


def _win_sample_body(q_ref, win_ref, new_ref, kn_ref, vn_ref, oc_ref, os_ref, bg_ref, o_ref, wout_ref, *, wbuf):
    wout_ref[0:(wbuf - 1) * HK, :] = win_ref[HK:wbuf * HK, :]
    wout_ref[(wbuf - 1) * HK:wbuf * HK, :] = new_ref[...]
    q = q_ref[0]
    qb = q.astype(BF16)
    rg = _row_head((N_HEADS, wbuf))
    sc = jnp.zeros((N_HEADS, wbuf), F32)
    for h in range(N_KV_HEADS):
        kh = win_ref[pl.ds(2 * h, wbuf, stride=HK), :].astype(BF16)
        sc = jnp.where(rg == h, lax.dot_general(qb, kh, _NT, preferred_element_type=F32), sc)
    mask = lax.broadcasted_iota(I32, (N_HEADS, wbuf), 1) > wbuf - WINDOW
    sm = jnp.where(mask, sc * ATTN_SCALE, MASK_VALUE)
    s_new = jnp.sum(q * kn_ref[0], axis=-1, keepdims=True) * ATTN_SCALE
    m = jnp.maximum(jnp.max(sm, axis=-1, keepdims=True), s_new)
    p = jnp.where(mask, jnp.exp(sm - m), 0.0)
    p_new = jnp.exp(s_new - m)
    den = jnp.sum(p, axis=-1, keepdims=True) + p_new
    pv = jnp.zeros((N_HEADS, HEAD_DIM), F32)
    for h in range(N_KV_HEADS):
        vh = win_ref[pl.ds(2 * h + 1, wbuf, stride=HK), :].astype(BF16)
        pv = pv + jnp.dot(jnp.where(rg == h, p, 0.0).astype(BF16), vh, preferred_element_type=F32)
    o_w = (pv + p_new * vn_ref[0]) / den
    gates = jax.nn.sigmoid(bg_ref[0])
    o_ref[0] = gates[:, 0:1] * oc_ref[0] + gates[:, 1:2] * os_ref[0] + gates[:, 2:3] * o_w


def win_sample(q16, win2d, new2d, kn16, vn16, o_cmp, o_sel, bgs, wbuf):
    ns = q16.shape[0]
    head3 = lambda: pl.BlockSpec((1, N_HEADS, HEAD_DIM), lambda b: (b, 0, 0))
    wspec = lambda: pl.BlockSpec((wbuf * HK, LANES), lambda b: (b, 0))
    return pl.pallas_call(
        functools.partial(_win_sample_body, wbuf=wbuf),
        grid=(ns,),
        in_specs=[head3(), wspec(), pl.BlockSpec((HK, LANES), lambda b: (b, 0)), head3(), head3(), head3(), head3(),
                  head3()],
        out_specs=[head3(), wspec()],
        out_shape=[jax.ShapeDtypeStruct((ns, N_HEADS, HEAD_DIM), F32),
                   jax.ShapeDtypeStruct((ns * wbuf * HK, LANES), F32)],
        compiler_params=_params(("arbitrary",), 40),
        name="win_sample",
    )(q16, win2d, new2d, kn16, vn16, o_cmp, o_sel, bgs)


ROUTE_LANE0 = N_GROUPS


INFO_E1, INFO_E2, INFO_R1, INFO_R2, INFO_C1, INFO_C2 = range(6)


def _route_body(xp_ref, xs_ref, g_ref, wr_ref, br_ref, xn_ref, info_ref, cnt_ref, run_scr, *, n_real, n_ptiles):
    step = pl.program_id(0)

    @pl.when(step == 0)
    def _():
        run_scr[...] = jnp.zeros(run_scr.shape, F32)

    refs = (g_ref, wr_ref, br_ref, xn_ref, info_ref, cnt_ref, run_scr)

    @pl.when(step < n_ptiles)
    def _():
        _route_tile(xp_ref[...], step, n_real, *refs)

    @pl.when(step >= n_ptiles)
    def _():
        _route_tile(xs_ref[...], step, n_real, *refs)


def _route_tile(x, step, n_real, g_ref, wr_ref, br_ref, xn_ref, info_ref, cnt_ref, run_scr):
    tm = x.shape[0]
    var = jnp.mean(x * x, axis=-1, keepdims=True)
    xn = (x * lax.rsqrt(var + EPS)) * g_ref[...]
    xn_ref[...] = xn
    xnb = xn.astype(BF16)
    logits = jnp.dot(xnb, wr_ref[...], preferred_element_type=F32) + br_ref[...]
    lane = lax.broadcasted_iota(I32, logits.shape, 1)
    big = 4 * LANES

    isg = lane < N_GROUPS
    lg = jnp.where(isg, logits, -jnp.inf)
    mg = jnp.max(lg, axis=-1, keepdims=True)
    gi = jnp.min(jnp.where(lg == mg, lane, big), axis=-1, keepdims=True)
    pg = 1.0 / jnp.sum(jnp.where(isg, jnp.exp(lg - mg), 0.0), axis=-1, keepdims=True)

    lo = ROUTE_LANE0 + gi * EXP_PER_GROUP
    ing = (lane >= lo) & (lane < lo + EXP_PER_GROUP)
    le = jnp.where(ing, logits, -jnp.inf)
    me = jnp.max(le, axis=-1, keepdims=True)
    ee = jnp.where(ing, jnp.exp(le - me), 0.0)
    pe = jnp.where(ing, ee / jnp.sum(ee, axis=-1, keepdims=True), -1.0)
    p1 = jnp.max(pe, axis=-1, keepdims=True)
    i1 = jnp.min(jnp.where(pe == p1, lane, big), axis=-1, keepdims=True)
    pe2 = jnp.where(lane == i1, -1.0, pe)
    p2 = jnp.max(pe2, axis=-1, keepdims=True)
    i2 = jnp.min(jnp.where(pe2 == p2, lane, big), axis=-1, keepdims=True)
    tot = p1 + p2
    c1 = pg * (p1 / tot)
    c2 = pg * (p2 / tot)

    row = step * tm + lax.broadcasted_iota(I32, logits.shape, 0)
    hit = jnp.where(((lane == i1) | (lane == i2)) & (row < n_real), 1.0, 0.0)
    tri = jnp.where(lax.broadcasted_iota(I32, (tm, tm), 1) < lax.broadcasted_iota(I32, (tm, tm), 0), 1.0, 0.0)
    before = run_scr[...] + jnp.dot(tri.astype(BF16), hit.astype(BF16), preferred_element_type=F32)
    r1 = jnp.sum(jnp.where(lane == i1, before, 0.0), axis=-1, keepdims=True)
    r2 = jnp.sum(jnp.where(lane == i2, before, 0.0), axis=-1, keepdims=True)
    run_new = run_scr[...] + jnp.sum(hit, axis=0, keepdims=True)
    run_scr[...] = run_new
    cnt_ref[...] = run_new

    info = jnp.zeros(logits.shape, F32)
    for ln, val in ((INFO_E1, (i1 - ROUTE_LANE0).astype(F32)), (INFO_E2, (i2 - ROUTE_LANE0).astype(F32)),
                    (INFO_R1, r1), (INFO_R2, r2), (INFO_C1, c1), (INFO_C2, c2)):
        info = jnp.where(lane == ln, val, info)
    info_ref[...] = info


MOE_TILE = 256


def route(x_prompt, x_sample, g, wr, br, n_real):
    mp, d = x_prompt.shape
    n_ptiles = mp // MOE_TILE
    assert mp % MOE_TILE == 0 and x_sample.shape[0] == MOE_TILE
    m = mp + MOE_TILE
    return pl.pallas_call(
        functools.partial(_route_body, n_real=n_real, n_ptiles=n_ptiles),
        grid=(n_ptiles + 1,),
        in_specs=[pl.BlockSpec((MOE_TILE, d), lambda i: (jnp.minimum(i, n_ptiles - 1), 0)),
                  pl.BlockSpec((MOE_TILE, d), lambda i: (0, 0)), pl.BlockSpec((1, d), lambda i: (0, 0)),
                  pl.BlockSpec((d, LANES), lambda i: (0, 0)), pl.BlockSpec((1, LANES), lambda i: (0, 0))],
        out_specs=[pl.BlockSpec((MOE_TILE, d), lambda i: (i, 0)), pl.BlockSpec((MOE_TILE, LANES), lambda i: (i, 0)),
                   pl.BlockSpec((1, LANES), lambda i: (0, 0))],
        out_shape=[jax.ShapeDtypeStruct((m, d), F32), jax.ShapeDtypeStruct((m, LANES), F32),
                   jax.ShapeDtypeStruct((1, LANES), F32)],
        scratch_shapes=[pltpu.VMEM((1, LANES), F32)],
        compiler_params=_params(("arbitrary",), 40),
        name="route",
    )(x_prompt, x_sample, g.reshape(1, d), wr, br)


def _moe_plan(info, counts_row, n_real, n_tiles):
    n_pad = info.shape[0]
    e = jnp.clip(info[:, INFO_E1:INFO_E2 + 1].astype(I32), 0, N_EXPERTS - 1)
    r = info[:, INFO_R1:INFO_R2 + 1].astype(I32)
    counts = counts_row[0, ROUTE_LANE0:ROUTE_LANE0 + N_EXPERTS].astype(I32)
    padded = ((counts + MOE_TILE - 1) // MOE_TILE) * MOE_TILE
    ex = jnp.arange(N_EXPERTS, dtype=I32)
    ends = jnp.sum(jnp.where(ex[None, :] <= ex[:, None], padded[None, :], 0), axis=1)
    offs = ends - padded
    tok = jnp.arange(n_pad, dtype=I32)
    valid = (tok < n_real)[:, None]
    dest = jnp.where(valid, offs[e] + r, 0)
    n_slots = n_tiles * MOE_TILE
    src = jnp.zeros((n_slots,), I32).at[jnp.where(valid, dest, n_slots).reshape(-1)].set(
        jnp.repeat(tok, 2), mode="drop")
    n_used = ends[N_EXPERTS - 1] // MOE_TILE
    tiles = jnp.arange(n_tiles, dtype=I32)
    used = tiles < n_used
    te = jnp.minimum(jnp.sum((ends[None, :] <= (tiles * MOE_TILE)[:, None]).astype(I32), axis=1), N_EXPERTS - 1)
    te_prev = jnp.concatenate([te[:1] - 1, te[:-1]])
    first = (te != te_prev) & used
    later = tiles[None, :] > tiles[:, None]
    run = jnp.sum((first[None, :] & ~later).astype(I32), axis=1) - 1
    nxt_tile = jnp.min(jnp.where(first[None, :] & later, tiles[None, :], n_tiles), axis=1)
    nxt = jnp.where(nxt_tile < n_tiles, te[jnp.minimum(nxt_tile, n_tiles - 1)], -1)
    tile3 = lambda v: v.reshape(-1, 1, MOE_TILE)
    return (tile3(src), tile3(dest[:, 0]), tile3(dest[:, 1]),
            (te, first.astype(I32), nxt, jnp.bitwise_and(run, 1), n_used.reshape(1)))


def _issue_rows(idx_ref, src_hbm, dst_buf, slot, sem):
    n = idx_ref.shape[2]
    group = 8

    def body(g, carry):
        for j in range(group):
            r = g * group + j
            pltpu.make_async_copy(src_hbm.at[pl.ds(idx_ref[0, 0, r], 1), :], dst_buf.at[slot, pl.ds(r, 1), :],
                                  sem.at[slot]).start(priority=j % 2)
        return carry

    lax.fori_loop(0, n // group, body, 0)


def _wait_rows(src_hbm, dst_buf, slot, sem):
    n = dst_buf.shape[1]
    pltpu.make_async_copy(src_hbm.at[pl.ds(0, n), :], dst_buf.at[slot], sem.at[slot]).wait()


def _moe_expert_body(te_ref, first_ref, nxt_ref, par_ref, nu_ref, src_cur, src_nxt, xn_hbm, wg_hbm, wu_hbm, wd_hbm,
                     y_ref, xbuf, xsem, wg_buf, wu_buf, wd_buf, wsem, wgb, wub, wdb):
    i = pl.program_id(0)
    n_used = nu_ref[0]
    slot = lax.rem(i, 2)

    def weight_copies(e, ws):
        return (pltpu.make_async_copy(wg_hbm.at[e], wg_buf.at[ws], wsem.at[0, ws]),
                pltpu.make_async_copy(wu_hbm.at[e], wu_buf.at[ws], wsem.at[1, ws]),
                pltpu.make_async_copy(wd_hbm.at[e], wd_buf.at[ws], wsem.at[2, ws]))

    @pl.when(i == 0)
    def _():
        for cp in weight_copies(te_ref[0], par_ref[0]):
            cp.start()
        _issue_rows(src_cur, xn_hbm, xbuf, 0, xsem)

    @pl.when(i + 1 < n_used)
    def _():
        _issue_rows(src_nxt, xn_hbm, xbuf, 1 - slot, xsem)

    @pl.when(i < n_used)
    def _():
        @pl.when(first_ref[i] == 1)
        def _():
            ws = par_ref[i]
            for cp in weight_copies(te_ref[i], ws):
                cp.wait()

            @pl.when(nxt_ref[i] >= 0)
            def _():
                for cp in weight_copies(nxt_ref[i], 1 - ws):
                    cp.start()

            wgb[...] = wg_buf[ws].astype(BF16)
            wub[...] = wu_buf[ws].astype(BF16)
            wdb[...] = wd_buf[ws].astype(BF16)

        _wait_rows(xn_hbm, xbuf, slot, xsem)
        x = xbuf[slot].astype(BF16)
        hg = jnp.dot(x, wgb[...], preferred_element_type=F32)
        hu = jnp.dot(x, wub[...], preferred_element_type=F32)
        hid = (hg * jax.nn.sigmoid(hg)) * hu
        y_ref[...] = jnp.dot(hid.astype(BF16), wdb[...], preferred_element_type=F32)

    @pl.when(i >= n_used)
    def _():
        y_ref[...] = jnp.zeros(y_ref.shape, F32)


def moe_experts(tile_plan, src3, xn, wg, wu, wd):
    n_tiles = src3.shape[0]
    d = xn.shape[1]
    _, _, de = wg.shape
    smem_tile = lambda imap: pl.BlockSpec((1, 1, MOE_TILE), imap, memory_space=pltpu.SMEM)
    hbm = lambda: pl.BlockSpec(memory_space=pl.ANY)
    grid_spec = pltpu.PrefetchScalarGridSpec(
        num_scalar_prefetch=5,
        grid=(n_tiles,),
        in_specs=[smem_tile(lambda i, *_: (i, 0, 0)),
                  smem_tile(lambda i, *_: (jnp.minimum(i + 1, n_tiles - 1), 0, 0)),
                  hbm(), hbm(), hbm(), hbm()],
        out_specs=pl.BlockSpec((MOE_TILE, d), lambda i, *_: (i, 0)),
        scratch_shapes=[pltpu.VMEM((2, MOE_TILE, d), F32), pltpu.SemaphoreType.DMA((2,)),
                        pltpu.VMEM((2, d, de), F32), pltpu.VMEM((2, d, de), F32), pltpu.VMEM((2, de, d), F32),
                        pltpu.SemaphoreType.DMA((3, 2)),
                        pltpu.VMEM((d, de), BF16), pltpu.VMEM((d, de), BF16), pltpu.VMEM((de, d), BF16)],
    )
    return pl.pallas_call(
        _moe_expert_body,
        grid_spec=grid_spec,
        out_shape=jax.ShapeDtypeStruct((n_tiles * MOE_TILE, d), F32),
        compiler_params=_params(("arbitrary",), 56),
        name="moe_experts",
    )(*tile_plan, src3, src3, xn, wg, wu, wd)


def _moe_combine_body(d1_cur, d1_nxt, d2_cur, d2_nxt, ys_hbm, xp_ref, xs_ref, info_ref, gf_ref, yp_ref, ysm_ref,
                      buf1, buf2, sem1, sem2, *, n_ptiles):
    i = pl.program_id(0)
    slot = lax.rem(i, 2)

    @pl.when(i == 0)
    def _():
        _issue_rows(d1_cur, ys_hbm, buf1, 0, sem1)
        _issue_rows(d2_cur, ys_hbm, buf2, 0, sem2)

    @pl.when(i + 1 < pl.num_programs(0))
    def _():
        _issue_rows(d1_nxt, ys_hbm, buf1, 1 - slot, sem1)
        _issue_rows(d2_nxt, ys_hbm, buf2, 1 - slot, sem2)

    _wait_rows(ys_hbm, buf1, slot, sem1)
    _wait_rows(ys_hbm, buf2, slot, sem2)

    def finish(x_ref, y_ref):
        info = info_ref[...]
        xo = x_ref[...] + (info[:, INFO_C1:INFO_C1 + 1] * buf1[slot] + info[:, INFO_C2:INFO_C2 + 1] * buf2[slot])
        var = jnp.mean(xo * xo, axis=-1, keepdims=True)
        y_ref[...] = (xo * lax.rsqrt(var + EPS)) * gf_ref[...]

    @pl.when(i < n_ptiles)
    def _():
        finish(xp_ref, yp_ref)

    @pl.when(i >= n_ptiles)
    def _():
        finish(xs_ref, ysm_ref)


def moe_combine(dest1, dest2, ys, x_prompt, x_sample, info, g_final):
    mp, d = x_prompt.shape
    n_ptiles = mp // MOE_TILE
    n_tiles = n_ptiles + 1
    cur = lambda: pl.BlockSpec((1, 1, MOE_TILE), lambda i: (i, 0, 0), memory_space=pltpu.SMEM)
    nxt = lambda: pl.BlockSpec((1, 1, MOE_TILE), lambda i: (jnp.minimum(i + 1, n_tiles - 1), 0, 0),
                               memory_space=pltpu.SMEM)
    ptile = lambda: pl.BlockSpec((MOE_TILE, d), lambda i: (jnp.minimum(i, n_ptiles - 1), 0))
    stile = lambda: pl.BlockSpec((MOE_TILE, d), lambda i: (0, 0))
    return pl.pallas_call(
        functools.partial(_moe_combine_body, n_ptiles=n_ptiles),
        grid=(n_tiles,),
        in_specs=[cur(), nxt(), cur(), nxt(), pl.BlockSpec(memory_space=pl.ANY), ptile(), stile(),
                  pl.BlockSpec((MOE_TILE, LANES), lambda i: (i, 0)), pl.BlockSpec((1, d), lambda i: (0, 0))],
        out_specs=[ptile(), stile()],
        out_shape=[jax.ShapeDtypeStruct((mp, d), F32), jax.ShapeDtypeStruct((MOE_TILE, d), F32)],
        scratch_shapes=[pltpu.VMEM((2, MOE_TILE, d), F32), pltpu.VMEM((2, MOE_TILE, d), F32),
                        pltpu.SemaphoreType.DMA((2,)), pltpu.SemaphoreType.DMA((2,))],
        compiler_params=_params(("arbitrary",), 40),
        name="moe_combine",
    )(dest1, dest1, dest2, dest2, ys, x_prompt, x_sample, info, g_final.reshape(1, d))


def _mixer_inputs(x2, g_mix, w_in_t, w_bg_t, w_mg_t, tm, q_dtype):
    tn = 1024
    xn = rmsnorm(x2, g_mix, tm, BF16)
    mm = functools.partial(matmul_cols, xn, tm=tm, w_is_t=True)
    u_rnn = mm(w_in_t, 0, D_RNN, tn=tn, out_dtype=F32, name="in_rnn")
    u_gate = mm(w_in_t, D_RNN, D_RNN, tn=tn, out_dtype=F32, name="in_gate")
    q = mm(w_in_t, 2 * D_RNN, N_HEADS * HEAD_DIM, tn=tn, out_dtype=q_dtype, name="in_q")
    kv0 = 2 * D_RNN + N_HEADS * HEAD_DIM
    kv_cmp, = matmul_kv(xn, w_in_t, kv0, tm, False, "in_kv_cmp")
    kv_sel, sel_planes = matmul_kv(xn, w_in_t, kv0 + 2 * KV_DIM, tm, True, "in_kv_sel")
    kv_win, win_planes = matmul_kv(xn, w_in_t, kv0 + 4 * KV_DIM, tm, True, "in_kv_win")
    u_bg = mm(w_bg_t, 0, w_bg_t.shape[0], tn=w_bg_t.shape[0], out_dtype=F32, name="in_bg")
    u_mg = mm(w_mg_t, 0, 2 * D_MODEL, tn=tn, out_dtype=F32, name="in_mg")
    return u_rnn, u_gate, q, (kv_cmp, kv_sel, kv_win), (sel_planes, win_planes), u_bg, u_mg


def kernel(x_prompt, x_sample, cache_cmp_kv, cache_sel_kv, cache_win_kv, state_conv, state_rglru, page_table, g_mix, w_in, conv_w, conv_b, lru_w_a, lru_b_a, lru_w_x, lru_b_x, lru_lambda, cmp_pe, cmp_w1, cmp_b1, cmp_w2, cmp_b2, w_out, g_ffn, moe_w_group, moe_b_group, moe_w_expert, moe_b_expert, moe_w_gate, moe_w_up, moe_w_down, g_final):
    b, t, d = x_prompt.shape
    ns = x_sample.shape[0]
    npages = page_table.shape[1]
    past_len = npages * PAGE_SIZE
    wbuf = cache_win_kv.shape[1]
    kvshape = (N_KV_HEADS, 2, HEAD_DIM)

    w_in_t = w_in.T
    bg0 = 2 * D_RNN + N_HEADS * HEAD_DIM + 6 * KV_DIM
    n_bg = 3 * N_HEADS
    w_bg = w_in_t[bg0:bg0 + n_bg].reshape(3, N_KV_HEADS, GROUP, d).transpose(1, 0, 2, 3).reshape(N_KV_HEADS, 3 * GROUP, d)
    w_bg = jnp.pad(w_bg, ((0, 0), (0, LANES - 3 * GROUP), (0, 0))).reshape(N_KV_HEADS * LANES, d)
    w_mg = w_in_t[bg0 + n_bg:]
    wa = lru_w_a.astype(BF16)
    wx = lru_w_x.astype(BF16)
    half_rows = CMP_STRIDE * HEAD_DIM
    wc = jnp.concatenate([cmp_w1[:, :half_rows], cmp_w1[:, half_rows:]], axis=-1).astype(BF16)
    pe8 = jnp.pad(cmp_pe.reshape(2, 2, half_rows), ((0, 0), (0, 6), (0, 0)))
    b1 = cmp_b1.reshape(2, 1, CMP_HIDDEN)
    w2 = cmp_w2.astype(BF16)
    b2 = cmp_b2.reshape(2, 1, HEAD_DIM)
    n_route = N_GROUPS + N_EXPERTS
    wr = jnp.pad(jnp.concatenate([moe_w_group, moe_w_expert], axis=1), ((0, 0), (0, LANES - n_route))).astype(BF16)
    br = jnp.pad(jnp.concatenate([moe_b_group, moe_b_expert]), (0, LANES - n_route)).reshape(1, LANES)
    lru_args = (conv_w, conv_b, wa, wx, lru_b_a, lru_b_x, lru_lambda)
    cmp_args = (wc, pe8, b1, w2, b2)

    xp2 = x_prompt.reshape(b * t, d)
    u_rnn, u_gate, q, kv2d, planes, u_bg, u_mg = _mixer_inputs(xp2, g_mix, w_in_t, w_bg, w_mg, 1024, BF16)
    o_rnn, p_conv, p_h = lru_prompt(u_rnn.reshape(b, t, d), u_gate.reshape(b, t, d), *lru_args, 256)
    p_cmp, p_sel, p_win_full = (a.reshape((b, t) + kvshape) for a in kv2d)
    p_win = p_win_full[:, t - min(WINDOW, t):]
    pt_prompt = jnp.arange(b * t // PAGE_SIZE, dtype=I32).reshape(b, t // PAGE_SIZE)
    kvc_p = compress(kv2d[0], pt_prompt, jnp.zeros((b * CMP_STRIDE * HK, HEAD_DIM), F32), *cmp_args)
    o_nsa = attn_prompt_s(q.reshape(b, t, -1), planes[0], planes[1], kvc_p, u_bg.reshape(b, t, -1),
                        t // CMP_STRIDE - 1)
    merged = merge(u_mg, o_rnn.reshape(b * t, d), o_nsa.reshape(b * t, d), 512)
    x_mid = matmul_cols(merged, w_out, 0, d, 512, 1024, F32, residual=xp2, name="out_proj")

    xs2 = x_sample.reshape(ns, d)
    su_rnn, su_gate, sq, skv2d, _, su_bg, su_mg = _mixer_inputs(xs2, g_mix, w_in_t, w_bg, w_mg, ns, F32)
    so_rnn, cn, s_h = lru_sample(su_rnn, su_gate, state_conv.transpose(1, 0, 2), state_rglru, *lru_args)
    s_conv = cn.transpose(1, 0, 2)
    kvs = jnp.stack([a.reshape(ns, HK, HEAD_DIM) for a in skv2d], axis=1)
    s_cmp = kvs[:, 0].reshape((ns, 1) + kvshape)
    s_sel = kvs[:, 1].reshape((ns, 1) + kvshape)

    extra = jnp.concatenate([kvs[:, 0], jnp.zeros((ns, (CMP_STRIDE - 1) * HK, HEAD_DIM), F32)], axis=1)
    kvc_s = compress(cache_cmp_kv.reshape(-1, HEAD_DIM), page_table, extra.reshape(-1, HEAD_DIM), *cmp_args)
    n_cmp_s = -(-(past_len + 1) // CMP_STRIDE) - 1
    n_sel_s = -(-(past_len + 1) // SEL_BLK)
    nsp = -(-n_sel_s // LANES) * LANES
    q16 = sq.reshape(ns, N_HEADS, HEAD_DIM)
    m_s = jnp.asarray(_overlap_matrix(n_cmp_s, kvc_s.shape[2], n_sel_s, nsp), BF16)
    so_cmp, imp = cmp_sample(q16, kvc_s, m_s, past_len, n_cmp_s)
    idx = rank_sample(imp.reshape(ns * N_KV_HEADS, nsp), past_len // SEL_BLK, n_sel_s)

    def per_head_rows(new_kv, kv):
        return jnp.repeat(new_kv[:, kv::2], GROUP, axis=1)

    so_sel = sel_sample(idx, page_table, q16, per_head_rows(kvs[:, 1], 0), per_head_rows(kvs[:, 1], 1),
                        cache_sel_kv.reshape((-1,) + kvshape), past_len // SEL_BLK)
    bgs = su_bg.reshape(ns, N_KV_HEADS, LANES)[:, :, :3 * GROUP].reshape(ns, N_KV_HEADS, 3, GROUP)
    bgs = jnp.pad(bgs.transpose(0, 1, 3, 2).reshape(ns, N_HEADS, 3), ((0, 0), (0, 0), (0, LANES - 3)))
    so_nsa, s_win2d = win_sample(q16, cache_win_kv.reshape(-1, HEAD_DIM), skv2d[2], per_head_rows(kvs[:, 2], 0),
                                 per_head_rows(kvs[:, 2], 1), so_cmp, so_sel, bgs, wbuf)
    s_win = s_win2d.reshape((ns, wbuf) + kvshape)
    s_merged = merge(su_mg, so_rnn, so_nsa.reshape(ns, d), ns)
    sx_mid = matmul_cols(s_merged, w_out, 0, d, ns, 1024, F32, residual=xs2, name="out_proj_s")

    n_real = b * t + ns
    assert ns <= MOE_TILE
    sx_pad = jnp.pad(sx_mid, ((0, MOE_TILE - ns), (0, 0)))
    xn_all, info, counts = route(x_mid, sx_pad, g_ffn, wr, br, n_real)
    n_tiles = -(-(2 * n_real + N_EXPERTS * (MOE_TILE - 1)) // MOE_TILE)
    src3, dest1, dest2, tile_plan = _moe_plan(info, counts, n_real, n_tiles)
    ys = moe_experts(tile_plan, src3, xn_all, moe_w_gate, moe_w_up, moe_w_down)
    y_p2, y_s2 = moe_combine(dest1, dest2, ys, x_mid, sx_pad, info, g_final)
    y_prompt = y_p2.reshape(b, t, d)
    y_sample = y_s2[:ns].reshape(ns, 1, d)

    return (y_prompt, y_sample, p_cmp, s_cmp, p_sel, s_sel, p_win, s_win, p_conv, s_conv,
            p_h.reshape(b, d), s_h)
```

```python
import functools

import jax
import jax.numpy as jnp
import numpy as np
from jax import lax
from jax.experimental import pallas as pl
from jax.experimental.pallas import tpu as pltpu

F32 = jnp.float32
BF16 = jnp.bfloat16
I32 = jnp.int32

D_MODEL = 2048
D_RNN = 2048
RNN_BLOCKS = 16
RNN_BLOCK_DIM = 128
CONV_W = 4
LRU_C = 8.0
N_HEADS = 16
HEAD_DIM = 128
N_KV_HEADS = 4
GROUP = 4
KV_DIM = N_KV_HEADS * HEAD_DIM
HK = 2 * N_KV_HEADS
CMP_STRIDE = 16
CMP_BLK = 32
CMP_HIDDEN = 256
SEL_BLK = 64
SEL_SHIFT = 6
N_SEL = 16
WINDOW = 512
PAGE_SIZE = 128
ATTN_SCALE = HEAD_DIM ** -0.5
FORCED_SCORE = 1e4
N_GROUPS = 4
EXP_PER_GROUP = 8
N_EXPERTS = 32
D_EXPERT = 512
EPS = 1e-6
MASK_VALUE = -1e30
LANES = 128
MIB = 1024 * 1024

PROJ_TILE_M = 1024
PROJ_TILE_N = 1024
LRU_TILE_T = 256

_NT = (((1,), (1,)), ((), ()))


def _params(sem, vmem_mib):
    return pltpu.CompilerParams(dimension_semantics=sem, vmem_limit_bytes=vmem_mib * MIB)


def _msoftmax(s, mask):
    sm = jnp.where(mask, s, MASK_VALUE)
    m = jnp.max(sm, axis=-1, keepdims=True)
    e = jnp.where(mask, jnp.exp(sm - m), 0.0)
    den = jnp.sum(e, axis=-1, keepdims=True)
    return e * jnp.where(den > 0.0, 1.0 / den, 0.0)


def _gelu_tanh(x):
    return 0.5 * x * (1.0 + jnp.tanh(0.7978845608028654 * (x + 0.044715 * (x * x * x))))


def _rmsnorm_body(x_ref, g_ref, o_ref):
    x = x_ref[...]
    var = jnp.mean(x * x, axis=-1, keepdims=True)
    o_ref[...] = ((x * lax.rsqrt(var + EPS)) * g_ref[...]).astype(o_ref.dtype)


def rmsnorm(x, g, tm, out_dtype):
    m, d = x.shape
    return pl.pallas_call(
        _rmsnorm_body,
        grid=(m // tm,),
        in_specs=[pl.BlockSpec((tm, d), lambda i: (i, 0)), pl.BlockSpec((1, d), lambda i: (0, 0))],
        out_specs=pl.BlockSpec((tm, d), lambda i: (i, 0)),
        out_shape=jax.ShapeDtypeStruct((m, d), out_dtype),
        compiler_params=_params(("arbitrary",), 56),
        name="rmsnorm",
    )(x, g.reshape(1, d))


def _xw(x, wbf, w_is_t):
    if w_is_t:
        return lax.dot_general(x, wbf, _NT, preferred_element_type=F32)
    return jnp.dot(x, wbf, preferred_element_type=F32)


def _mm_body(x_ref, w_ref, *rest, w_is_t):
    r_ref = rest[0] if len(rest) == 3 else None
    o_ref, wbf_ref = rest[-2:]

    @pl.when(pl.program_id(1) == 0)
    def _():
        wbf_ref[...] = w_ref[...].astype(BF16)

    res = _xw(x_ref[...], wbf_ref[...], w_is_t)
    if r_ref is not None:
        res = r_ref[...] + res
    o_ref[...] = res.astype(o_ref.dtype)


def matmul_cols(x, w, col0, ncols, tm, tn, out_dtype, residual=None, w_is_t=False, name="matmul"):
    m, k = x.shape
    cb0 = col0 // tn
    assert col0 % tn == 0 and ncols % tn == 0 and m % tm == 0
    if w_is_t:
        w_spec = pl.BlockSpec((tn, k), lambda j, i: (cb0 + j, 0))
        w_tile = (tn, k)
    else:
        w_spec = pl.BlockSpec((k, tn), lambda j, i: (0, cb0 + j))
        w_tile = (k, tn)
    in_specs = [pl.BlockSpec((tm, k), lambda j, i: (i, 0)), w_spec]
    args = [x, w]
    if residual is not None:
        in_specs.append(pl.BlockSpec((tm, tn), lambda j, i: (i, j)))
        args.append(residual)
    return pl.pallas_call(
        functools.partial(_mm_body, w_is_t=w_is_t),
        grid=(ncols // tn, m // tm),
        in_specs=in_specs,
        out_specs=pl.BlockSpec((tm, tn), lambda j, i: (i, j)),
        out_shape=jax.ShapeDtypeStruct((m, ncols), out_dtype),
        scratch_shapes=[pltpu.VMEM(w_tile, BF16)],
        compiler_params=_params(("arbitrary", "arbitrary"), 56),
        name=name,
    )(*args)


def _mm_kv_body(x_ref, w_ref, oi_ref, *rest):
    wbf_ref = rest[-1]

    @pl.when(pl.program_id(0) == 0)
    def _():
        wbf_ref[...] = w_ref[...].astype(BF16)

    res = _xw(x_ref[...], wbf_ref[...], True)
    tm = res.shape[0]
    for hk in range(HK):
        blk = res[:, hk * HEAD_DIM:(hk + 1) * HEAD_DIM]
        oi_ref[pl.ds(hk, tm, stride=HK), :] = blk
        if len(rest) == 2:
            rest[0][hk] = blk.astype(BF16)


def matmul_kv(x, w_t, col0, tm, planes, name):
    m, k = x.shape
    ncols = HK * HEAD_DIM
    assert col0 % ncols == 0 and m % tm == 0
    out_specs = [pl.BlockSpec((tm * HK, HEAD_DIM), lambda i: (i, 0))]
    out_shape = [jax.ShapeDtypeStruct((m * HK, HEAD_DIM), F32)]
    if planes:
        out_specs.append(pl.BlockSpec((HK, tm, HEAD_DIM), lambda i: (0, i, 0)))
        out_shape.append(jax.ShapeDtypeStruct((HK, m, HEAD_DIM), BF16))
    return pl.pallas_call(
        _mm_kv_body,
        grid=(m // tm,),
        in_specs=[pl.BlockSpec((tm, k), lambda i: (i, 0)), pl.BlockSpec((ncols, k), lambda i: (col0 // ncols, 0))],
        out_specs=out_specs,
        out_shape=out_shape,
        scratch_shapes=[pltpu.VMEM((ncols, k), BF16)],
        compiler_params=_params(("arbitrary",), 56),
        name=name,
    )(x, w_t)


def _merge_body(mg_ref, rnn_ref, nsa_ref, o_ref):
    d = rnn_ref.shape[-1]
    ga = jax.nn.sigmoid(mg_ref[:, :d])
    gb = jax.nn.sigmoid(mg_ref[:, d:])
    o_ref[...] = (ga * rnn_ref[...] + gb * nsa_ref[...]).astype(o_ref.dtype)


def merge(u_mg, o_rnn, o_nsa, tm):
    m, d = o_rnn.shape
    return pl.pallas_call(
        _merge_body,
        grid=(m // tm,),
        in_specs=[pl.BlockSpec((tm, 2 * d), lambda i: (i, 0)), pl.BlockSpec((tm, d), lambda i: (i, 0)),
                  pl.BlockSpec((tm, d), lambda i: (i, 0))],
        out_specs=pl.BlockSpec((tm, d), lambda i: (i, 0)),
        out_shape=jax.ShapeDtypeStruct((m, d), BF16),
        compiler_params=_params(("arbitrary",), 48),
        name="merge",
    )(u_mg, o_rnn, o_nsa)


def _lru_gates(uc, wa_ref, wx_ref, ba_ref, bx_ref, lam_ref):
    ucb = uc.astype(BF16)
    ra, ri = [], []
    for n in range(RNN_BLOCKS):
        blk = ucb[:, n * RNN_BLOCK_DIM:(n + 1) * RNN_BLOCK_DIM]
        ra.append(jnp.dot(blk, wa_ref[n], preferred_element_type=F32))
        ri.append(jnp.dot(blk, wx_ref[n], preferred_element_type=F32))
    r = jax.nn.sigmoid(jnp.concatenate(ra, axis=-1) + ba_ref[...])
    i = jax.nn.sigmoid(jnp.concatenate(ri, axis=-1) + bx_ref[...])
    z = -lam_ref[...]
    softplus = jnp.maximum(z, 0.0) + jnp.log1p(jnp.exp(-jnp.abs(z)))
    log_a = (-LRU_C * r) * softplus
    a = jnp.exp(log_a)
    mult = jnp.sqrt(-jnp.tanh(log_a) * (a * a + 1.0))
    return a, mult * (i * uc)


def _lru_prompt_body(u_ref, gate_ref, cw_ref, cb_ref, wa_ref, wx_ref, ba_ref, bx_ref, lam_ref,
                     o_ref, conv_ref, hlast_ref, xbuf, hcar, acum, bcum):
    t = pl.program_id(1)
    nt = pl.num_programs(1)
    tt, d = u_ref.shape[1], u_ref.shape[2]

    @pl.when(t == 0)
    def _():
        xbuf[0:8, :] = jnp.zeros((8, d), F32)
        hcar[...] = jnp.zeros((8, d), F32)

    x = u_ref[0]
    xbuf[8:8 + tt, :] = x
    uc = (cw_ref[0:1, :] * xbuf[5:5 + tt, :] + cw_ref[1:2, :] * xbuf[6:6 + tt, :]
          + cw_ref[2:3, :] * xbuf[7:7 + tt, :] + cw_ref[3:4, :] * x) + cb_ref[...]
    xbuf[0:8, :] = xbuf[tt:tt + 8, :]

    a, b = _lru_gates(uc, wa_ref, wx_ref, ba_ref, bx_ref, lam_ref)
    g = tt // 8
    a3 = a.reshape(g, 8, d)
    b3 = b.reshape(g, 8, d)
    row = lax.broadcasted_iota(I32, (g, 8, d), 1)
    for s in (1, 2, 4):
        a_sh = pltpu.roll(a3, s, axis=1)
        b_sh = pltpu.roll(b3, s, axis=1)
        keep = row >= s
        b3 = jnp.where(keep, a3 * b_sh + b3, b3)
        a3 = jnp.where(keep, a3 * a_sh, a3)
    acum[...] = a3
    bcum[...] = b3

    def body(gi, h):
        hg = bcum[gi] + acum[gi] * h
        bcum[gi] = hg
        return jnp.broadcast_to(hg[7:8, :], (8, d))

    hfin = lax.fori_loop(0, g, body, hcar[...])
    hcar[...] = hfin
    o_ref[0] = bcum[...].reshape(tt, d) * _gelu_tanh(gate_ref[0])

    @pl.when(t == nt - 1)
    def _():
        conv_ref[0] = x[tt - (CONV_W - 1):tt, :]
        hlast_ref[0] = hfin[0:1, :]


def lru_prompt(u_rnn, u_gate, conv_w, conv_b, wa, wx, ba, bx, lam, tt):
    b, t, d = u_rnn.shape
    vec = lambda: pl.BlockSpec((1, d), lambda i, j: (0, 0))
    wspec = lambda: pl.BlockSpec((RNN_BLOCKS, RNN_BLOCK_DIM, RNN_BLOCK_DIM), lambda i, j: (0, 0, 0))
    return pl.pallas_call(
        _lru_prompt_body,
        grid=(b, t // tt),
        in_specs=[pl.BlockSpec((1, tt, d), lambda i, j: (i, j, 0)), pl.BlockSpec((1, tt, d), lambda i, j: (i, j, 0)),
                  pl.BlockSpec((CONV_W, d), lambda i, j: (0, 0)), vec(), wspec(), wspec(), vec(), vec(), vec()],
        out_specs=[pl.BlockSpec((1, tt, d), lambda i, j: (i, j, 0)),
                   pl.BlockSpec((1, CONV_W - 1, d), lambda i, j: (i, 0, 0)),
                   pl.BlockSpec((1, 1, d), lambda i, j: (i, 0, 0))],
        out_shape=[jax.ShapeDtypeStruct((b, t, d), F32), jax.ShapeDtypeStruct((b, CONV_W - 1, d), F32),
                   jax.ShapeDtypeStruct((b, 1, d), F32)],
        scratch_shapes=[pltpu.VMEM((tt + 8, d), F32), pltpu.VMEM((8, d), F32),
                        pltpu.VMEM((tt // 8, 8, d), F32), pltpu.VMEM((tt // 8, 8, d), F32)],
        compiler_params=_params(("arbitrary", "arbitrary"), 56),
        name="lru_prompt",
    )(u_rnn, u_gate, conv_w, conv_b.reshape(1, d), wa, wx, ba.reshape(1, d), bx.reshape(1, d), lam.reshape(1, d))


def _lru_sample_body(u_ref, gate_ref, cp_ref, h0_ref, cw_ref, cb_ref, wa_ref, wx_ref, ba_ref, bx_ref, lam_ref,
                     o_ref, cn_ref, h_ref):
    x = u_ref[...]
    uc = (cw_ref[0:1, :] * cp_ref[0] + cw_ref[1:2, :] * cp_ref[1] + cw_ref[2:3, :] * cp_ref[2]
          + cw_ref[3:4, :] * x) + cb_ref[...]
    a, b = _lru_gates(uc, wa_ref, wx_ref, ba_ref, bx_ref, lam_ref)
    h = a * h0_ref[...] + b
    o_ref[...] = h * _gelu_tanh(gate_ref[...])
    h_ref[...] = h
    cn_ref[0] = cp_ref[1]
    cn_ref[1] = cp_ref[2]
    cn_ref[2] = x


def lru_sample(u_rnn, u_gate, conv_prev_t, h0, conv_w, conv_b, wa, wx, ba, bx, lam):
    n, d = u_rnn.shape
    return pl.pallas_call(
        _lru_sample_body,
        out_shape=[jax.ShapeDtypeStruct((n, d), F32), jax.ShapeDtypeStruct((CONV_W - 1, n, d), F32),
                   jax.ShapeDtypeStruct((n, d), F32)],
        compiler_params=pltpu.CompilerParams(vmem_limit_bytes=40 * MIB),
        name="lru_sample",
    )(u_rnn, u_gate, conv_prev_t, h0, conv_w, conv_b.reshape(1, d), wa, wx, ba.reshape(1, d), bx.reshape(1, d),
      lam.reshape(1, d))


COMPRESS_PAGES = 32
CHUNKS_PER_PAGE = PAGE_SIZE // CMP_STRIDE


def _compress_body(pt_ref, *refs, pp, n_steps):
    page_refs = refs[:pp]
    nxt_ref, extra_ref, wc_ref, pe_ref, b1_ref, w2_ref, b2_ref, o_ref, res_scr, pet_scr = refs[pp:]
    del pt_ref
    is_last = pl.program_id(1) == n_steps - 1
    m_rows = pp * CHUNKS_PER_PAGE * N_KV_HEADS
    lo4 = lax.broadcasted_iota(I32, (4, 8, 8, LANES), 2) < 4
    lo3 = lax.broadcasted_iota(I32, (8, 8, LANES), 1) < 4

    for k in range(2):
        lhs_sp = [[] for _ in range(8)]
        for pr in page_refs:
            xk = pr[pl.ds(k, PAGE_SIZE * N_KV_HEADS, stride=2), :]
            x5 = xk.reshape(4, 2, 8, 8, LANES)
            a0 = x5[:, 0]
            a1 = x5[:, 1]
            be = jnp.where(lo4, a0, pltpu.roll(a1, 4, axis=2))
            bo = jnp.where(lo4, pltpu.roll(a0, 4, axis=2), a1)
            for sp in range(8):
                lhs_sp[sp].append(jnp.concatenate([be[:, sp], bo[:, sp]], axis=-1).reshape(32, 2 * LANES))
        acc = jnp.zeros((m_rows, 2 * CMP_HIDDEN), F32)
        for sp in range(8):
            lhs = jnp.concatenate(lhs_sp[sp], axis=0).astype(BF16)
            acc = acc + jnp.dot(lhs, wc_ref[k, sp * 256:(sp + 1) * 256, :], preferred_element_type=F32)

        @pl.when((pl.program_id(0) == 0) & (pl.program_id(1) == 0))
        def _():
            pet_scr[k] = jnp.dot(pe_ref[k].astype(BF16), wc_ref[k], preferred_element_type=F32)

        peb0 = pet_scr[k, 0:1, :CMP_HIDDEN]
        peb1 = pet_scr[k, 1:2, CMP_HIDDEN:]
        h0 = acc[:, :CMP_HIDDEN] + peb0
        h1 = acc[:, CMP_HIDDEN:] + peb1

        xn = jnp.where(is_last, extra_ref[pl.ds(k, CMP_STRIDE * N_KV_HEADS, stride=2), :],
                       nxt_ref[pl.ds(k, CMP_STRIDE * N_KV_HEADS, stride=2), :])
        xn3 = xn.reshape(8, 8, LANES)
        ln = jnp.concatenate([jnp.where(lo3, xn3, 0.0), jnp.where(lo3, 0.0, xn3)], axis=-1).astype(BF16)
        nacc = jnp.zeros((8, CMP_HIDDEN), F32)
        for sp in range(8):
            nacc = nacc + jnp.dot(ln[sp], wc_ref[k, sp * 256:(sp + 1) * 256, CMP_HIDDEN:],
                                  preferred_element_type=F32)
        n8 = nacc + pltpu.roll(nacc, 4, axis=0) + peb1
        h1e = jnp.concatenate([h1, n8], axis=0)
        h1s = pltpu.roll(h1e, m_rows + 8 - N_KV_HEADS, axis=0)[:m_rows]
        pre = h0 + h1s + b1_ref[k]
        out_k = jnp.dot(_gelu_tanh(pre).astype(BF16), w2_ref[k], preferred_element_type=F32) + b2_ref[k]
        res_scr[...] = out_k
        for h in range(N_KV_HEADS):
            o_ref[0, 2 * h + k] = res_scr[pl.ds(h, m_rows // N_KV_HEADS, stride=N_KV_HEADS), :]


def compress(pool2d, page_table, extra2d, wc, pe8, b1, w2, b2):
    nseq, npages = page_table.shape
    pp = min(COMPRESS_PAGES, npages)
    n_steps = npages // pp
    assert npages % pp == 0
    rows_page = PAGE_SIZE * HK
    rows_chunk = CMP_STRIDE * HK
    blocks_step = pp * CHUNKS_PER_PAGE

    def page_spec(i):
        return pl.BlockSpec((rows_page, LANES), lambda b, s, pt: (pt[b, s * pp + i], 0))

    def nxt_map(b, s, pt):
        return (pt[b, jnp.minimum((s + 1) * pp, npages - 1)] * CHUNKS_PER_PAGE, 0)

    const3 = lambda shape: pl.BlockSpec(shape, lambda b, s, pt: (0, 0, 0))
    in_specs = [page_spec(i) for i in range(pp)] + [
        pl.BlockSpec((rows_chunk, LANES), nxt_map),
        pl.BlockSpec((rows_chunk, LANES), lambda b, s, pt: (b, 0)),
        const3((2, CMP_STRIDE * HEAD_DIM, 2 * CMP_HIDDEN)),
        const3((2, 8, CMP_STRIDE * HEAD_DIM)),
        const3((2, 1, CMP_HIDDEN)),
        const3((2, CMP_HIDDEN, HEAD_DIM)),
        const3((2, 1, HEAD_DIM)),
    ]
    grid_spec = pltpu.PrefetchScalarGridSpec(
        num_scalar_prefetch=1,
        grid=(nseq, n_steps),
        in_specs=in_specs,
        out_specs=pl.BlockSpec((1, HK, blocks_step, HEAD_DIM), lambda b, s, pt: (b, 0, s, 0)),
        scratch_shapes=[pltpu.VMEM((blocks_step * N_KV_HEADS, HEAD_DIM), F32),
                        pltpu.VMEM((2, 8, 2 * CMP_HIDDEN), F32)],
    )
    return pl.pallas_call(
        functools.partial(_compress_body, pp=pp, n_steps=n_steps),
        grid_spec=grid_spec,
        out_shape=jax.ShapeDtypeStruct((nseq, HK, npages * CHUNKS_PER_PAGE, HEAD_DIM), F32),
        compiler_params=_params(("arbitrary", "arbitrary"), 56),
        name="compress",
    )(page_table, *([pool2d] * pp), pool2d, extra2d, wc, pe8, b1, w2, b2)


def _overlap_matrix(n_cmp, n_cmp_pad, n_sel, n_sel_pad):
    cs = np.arange(n_cmp_pad)[:, None] * CMP_STRIDE
    ss = np.arange(n_sel_pad)[None, :] * SEL_BLK
    ov = np.minimum(cs + CMP_BLK, ss + SEL_BLK) - np.maximum(cs, ss)
    m = np.clip(ov, 0, CMP_BLK).astype(np.float32) / CMP_BLK
    m[n_cmp:, :] = 0.0
    m[:, n_sel:] = 0.0
    return m


QBLK = 128


KCHUNK = 64
KPIECE = 512
LOG2E = 1.4426950408889634


def _softmax_passes(s_scr, p_scr, n_plain, n_masked, n_total, plain_mask, full_mask):
    cols = s_scr.shape[1]
    grp = KCHUNK // 8
    c2 = ATTN_SCALE * LOG2E

    def rows(c):
        return pl.ds(pl.multiple_of(c * KCHUNK, KCHUNK), KCHUNK)

    def masked_max(mask_fn):
        def body(c, m8):
            sm = jnp.where(mask_fn(c), s_scr[rows(c), :], MASK_VALUE)
            s_scr[rows(c), :] = sm
            return jnp.maximum(m8, jnp.max(sm.reshape(grp, 8, cols), axis=0))
        return body

    m8 = lax.fori_loop(0, n_plain, masked_max(plain_mask), jnp.full((8, cols), MASK_VALUE, F32))
    m8 = lax.fori_loop(n_plain, n_plain + n_masked, masked_max(full_mask), m8)
    mc = jnp.max(m8, axis=0, keepdims=True) * c2

    def pass_exp(c, l8):
        p = jnp.exp2(s_scr[rows(c), :] * c2 - mc)
        p_scr[rows(c), :] = p.astype(BF16)
        return l8 + jnp.sum(p.reshape(grp, 8, cols), axis=0)

    l8 = lax.fori_loop(0, n_plain + n_masked, pass_exp, jnp.zeros((8, cols), F32))

    def zero_fill(c, carry):
        p_scr[rows(c), :] = jnp.zeros((KCHUNK, cols), BF16)
        return carry

    lax.fori_loop(n_plain + n_masked, n_total, zero_fill, 0)
    return 1.0 / jnp.sum(l8, axis=0, keepdims=True)


def _attn_prompt_s_body(q_ref, ksel_ref, vsel_ref, kwin_ref, vwin_ref, kvc_ref, bg_ref, mga_ref, mgb_ref, rnn_ref,
                        mt_ref, o_ref, vsel_t, vwin_t, kc_bf, vc_t_bf, s_scr, p_scr, sel_scr, *, n_cmp, n_sel):
    qi = pl.program_id(2)
    q0 = qi * QBLK
    cols = GROUP * QBLK

    @pl.when(qi == 0)
    def _():
        vsel_t[...] = vsel_ref[0].astype(F32).T.astype(BF16)
        vwin_t[...] = vwin_ref[0].astype(F32).T.astype(BF16)
        kc_bf[...] = kvc_ref[0, 0].astype(BF16)
        vc_t_bf[...] = kvc_ref[0, 1].T.astype(BF16)

    q = q_ref[0]
    q4 = jnp.concatenate([q[:, g * HEAD_DIM:(g + 1) * HEAD_DIM] for g in range(GROUP)], axis=0)
    qpos_row = q0 + (lax.broadcasted_iota(I32, (1, cols), 1) & (QBLK - 1))
    row_in_chunk = lax.broadcasted_iota(I32, (KCHUNK, cols), 0)

    nb = kvc_ref.shape[2]
    kc = kc_bf[...]
    vc_t = vc_t_bf[...]
    n_io = lax.broadcasted_iota(I32, (nb, cols), 0)
    mask_c = (n_io * CMP_STRIDE + (CMP_BLK - 1) <= qpos_row) & (n_io < n_cmp)
    sc = jnp.where(mask_c, lax.dot_general(kc, q4, _NT, preferred_element_type=F32) * ATTN_SCALE, MASK_VALUE)
    ec = jnp.where(mask_c, jnp.exp(sc - jnp.max(sc, axis=0, keepdims=True)), 0.0)
    den_c = jnp.sum(ec, axis=0, keepdims=True)
    pc = ec * jnp.where(den_c > 0.0, 1.0 / den_c, 0.0)
    o_c = jnp.dot(vc_t, pc.astype(BF16), preferred_element_type=F32)
    psum = pc[:, 0:QBLK] + pc[:, QBLK:2 * QBLK] + pc[:, 2 * QBLK:3 * QBLK] + pc[:, 3 * QBLK:4 * QBLK]

    nsp = mt_ref.shape[0]
    imp_t = jnp.dot(mt_ref[...], psum.astype(BF16), preferred_element_type=F32)
    jt = lax.broadcasted_iota(I32, (nsp, QBLK), 0)
    cur = lax.shift_right_logical(q0 + lax.broadcasted_iota(I32, (nsp, QBLK), 1), SEL_SHIFT)
    valid = (jt <= cur) & (jt < n_sel)
    forced = (jt == 0) | (jt >= cur - 1)
    score = jnp.where(forced, FORCED_SCORE, imp_t)
    score = jnp.where(valid, score, -1.0)
    rank = jnp.zeros((nsp, QBLK), I32)
    for k in range(n_sel):
        sk = score[k:k + 1, :]
        beats = (sk > score) | ((sk == score) & (k < jt))
        rank = rank + beats.astype(I32)
    sel_t = jnp.where((rank < N_SEL) & valid, 1.0, 0.0)
    sel_scr[...] = jnp.concatenate([sel_t] * GROUP, axis=1)

    wkeys = WINDOW + QBLK
    ks = pl.multiple_of(jnp.maximum(q0 - WINDOW, 0), QBLK)
    s_scr[0:wkeys, :] = lax.dot_general(kwin_ref[0, pl.ds(ks, wkeys), :], q4, _NT, preferred_element_type=F32)

    def win_mask(c):
        dist = qpos_row - (ks + c * KCHUNK + row_in_chunk)
        return (dist >= 0) & (dist < WINDOW)

    n_wchunks = wkeys // KCHUNK
    inv_w = _softmax_passes(s_scr, p_scr, 0, n_wchunks, n_wchunks, win_mask, win_mask)
    o_w = jnp.dot(vwin_t[:, pl.ds(ks, wkeys)], p_scr[0:wkeys, :], preferred_element_type=F32) * inv_w

    n_pieces = qi // (KPIECE // QBLK) + 1

    def score_piece(c, carry):
        r0 = pl.multiple_of(c * KPIECE, KPIECE)
        s_scr[pl.ds(r0, KPIECE), :] = lax.dot_general(ksel_ref[0, pl.ds(r0, KPIECE), :], q4, _NT,
                                                      preferred_element_type=F32)
        return carry

    lax.fori_loop(0, n_pieces, score_piece, 0)

    def picked(c):
        return sel_scr[pl.ds(c, 1), :] > 0.5

    def picked_causal(c):
        return picked(c) & (c * KCHUNK + row_in_chunk <= qpos_row)

    chunks_q = QBLK // KCHUNK
    inv_s = _softmax_passes(s_scr, p_scr, qi * chunks_q, chunks_q, n_pieces * (KPIECE // KCHUNK),
                            picked, picked_causal)

    def pv_piece(c, acc):
        r0 = pl.multiple_of(c * KPIECE, KPIECE)
        return acc + jnp.dot(vsel_t[:, pl.ds(r0, KPIECE)], p_scr[pl.ds(r0, KPIECE), :], preferred_element_type=F32)

    o_s = lax.fori_loop(0, n_pieces, pv_piece, jnp.zeros((HEAD_DIM, cols), F32)) * inv_s

    gates_t = jax.nn.sigmoid(bg_ref[0]).T
    grow = lambda br: jnp.concatenate([gates_t[br * GROUP + g:br * GROUP + g + 1, :] for g in range(GROUP)], axis=1)
    o_t = grow(0) * o_c + grow(1) * o_s + grow(2) * o_w
    for g in range(GROUP):
        cs = slice(g * HEAD_DIM, (g + 1) * HEAD_DIM)
        o_nsa = o_t[:, g * QBLK:(g + 1) * QBLK].T
        merged = jax.nn.sigmoid(mga_ref[0, :, cs]) * rnn_ref[0, :, cs] + jax.nn.sigmoid(mgb_ref[0, :, cs]) * o_nsa
        o_ref[0, :, cs] = merged.astype(o_ref.dtype)


def attn_prompt_s(q3, sel_planes, win_planes, kvc, ubg3, umg3, o_rnn3, n_cmp):
    b, t, _ = q3.shape
    nb = kvc.shape[2]
    assert KCHUNK == SEL_BLK and t % KPIECE == 0 and t >= WINDOW + QBLK
    n_sel = -(-t // SEL_BLK)
    nsp = max(8, -(-n_sel // 8) * 8)
    m_t = _overlap_matrix(n_cmp, nb, n_sel, nsp).T
    kspec = lambda: pl.BlockSpec((1, t, HEAD_DIM), lambda i, h, j: (2 * h, i, 0))
    vspec = lambda: pl.BlockSpec((1, t, HEAD_DIM), lambda i, h, j: (2 * h + 1, i, 0))
    cols = GROUP * QBLK
    head_cols = GROUP * HEAD_DIM
    tile = lambda off: pl.BlockSpec((1, QBLK, head_cols), lambda i, h, j: (i, j, off + h))
    return pl.pallas_call(
        functools.partial(_attn_prompt_s_body, n_cmp=n_cmp, n_sel=n_sel),
        grid=(b, N_KV_HEADS, t // QBLK),
        in_specs=[
            tile(0),
            kspec(), vspec(), kspec(), vspec(),
            pl.BlockSpec((1, 2, nb, HEAD_DIM), lambda i, h, j: (i, h, 0, 0)),
            pl.BlockSpec((1, QBLK, LANES), lambda i, h, j: (i, j, h)),
            tile(0), tile(N_KV_HEADS), tile(0),
            pl.BlockSpec((nsp, nb), lambda i, h, j: (0, 0)),
        ],
        out_specs=tile(0),
        out_shape=jax.ShapeDtypeStruct((b, t, N_HEADS * HEAD_DIM), BF16),
        scratch_shapes=[pltpu.VMEM((HEAD_DIM, t), BF16), pltpu.VMEM((HEAD_DIM, t), BF16),
                        pltpu.VMEM((nb, HEAD_DIM), BF16), pltpu.VMEM((HEAD_DIM, nb), BF16),
                        pltpu.VMEM((t, cols), F32), pltpu.VMEM((t, cols), BF16), pltpu.VMEM((nsp, cols), F32)],
        compiler_params=_params(("arbitrary", "arbitrary", "arbitrary"), 48),
        name="attn_prompt",
    )(q3, sel_planes, sel_planes, win_planes, win_planes, kvc, ubg3, umg3, umg3, o_rnn3, jnp.asarray(m_t, BF16))


def _row_head(shape):
    return lax.shift_right_logical(lax.broadcasted_iota(I32, shape, 0), 2)


def _cmp_sample_body(q_ref, kvc_ref, m_ref, oc_ref, imp_ref, *, qpos, n_cmp):
    qb = q_ref[0].astype(BF16)
    nb = kvc_ref.shape[2]
    rg = _row_head((N_HEADS, nb))
    ncol = lax.broadcasted_iota(I32, (N_HEADS, nb), 1)
    mask = (ncol * CMP_STRIDE + (CMP_BLK - 1) <= qpos) & (ncol < n_cmp)
    row8 = lax.broadcasted_iota(I32, (8, nb), 0)
    o = jnp.zeros((N_HEADS, HEAD_DIM), F32)
    ps8 = jnp.zeros((8, nb), F32)
    for h in range(N_KV_HEADS):
        kc = kvc_ref[0, 2 * h].astype(BF16)
        vc = kvc_ref[0, 2 * h + 1].astype(BF16)
        s = lax.dot_general(qb, kc, _NT, preferred_element_type=F32) * ATTN_SCALE
        ph = jnp.where(rg == h, _msoftmax(s, mask), 0.0)
        o = o + jnp.dot(ph.astype(BF16), vc, preferred_element_type=F32)
        ps8 = jnp.where(row8 == h, jnp.sum(ph, axis=0, keepdims=True), ps8)
    oc_ref[0] = o
    imp_ref[0] = jnp.dot(ps8.astype(BF16), m_ref[...], preferred_element_type=F32)[0:N_KV_HEADS]


def cmp_sample(q16, kvc, m_mat, qpos, n_cmp):
    ns = q16.shape[0]
    nb = kvc.shape[2]
    nsp = m_mat.shape[1]
    return pl.pallas_call(
        functools.partial(_cmp_sample_body, qpos=qpos, n_cmp=n_cmp),
        grid=(ns,),
        in_specs=[pl.BlockSpec((1, N_HEADS, HEAD_DIM), lambda b: (b, 0, 0)),
                  pl.BlockSpec((1, HK, nb, HEAD_DIM), lambda b: (b, 0, 0, 0)),
                  pl.BlockSpec((nb, nsp), lambda b: (0, 0))],
        out_specs=[pl.BlockSpec((1, N_HEADS, HEAD_DIM), lambda b: (b, 0, 0)),
                   pl.BlockSpec((1, N_KV_HEADS, nsp), lambda b: (b, 0, 0))],
        out_shape=[jax.ShapeDtypeStruct((ns, N_HEADS, HEAD_DIM), F32),
                   jax.ShapeDtypeStruct((ns, N_KV_HEADS, nsp), F32)],
        compiler_params=_params(("arbitrary",), 40),
        name="cmp_sample",
    )(q16, kvc, m_mat)


def _rank_sample_body(imp_ref, idx_ref, sc_scr, *, cur, n_sel):
    npad, nr = sc_scr.shape
    st = imp_ref[...].T
    j = lax.broadcasted_iota(I32, (npad, nr), 0)
    real = j < n_sel
    valid = (j <= cur) & real
    forced = (j == 0) | (j >= cur - 1)
    score = jnp.where(forced, FORCED_SCORE, st)
    score = jnp.where(valid, score, -1.0)
    score = jnp.where(real, score, -2.0)
    sc_scr[...] = score

    def body(k, rank):
        sk = sc_scr[pl.ds(k, 1), :]
        beats = (sk > score) | ((sk == score) & (k < j))
        return rank + beats.astype(I32)

    rank = lax.fori_loop(0, n_sel, body, jnp.zeros((npad, nr), I32))
    sel = (rank < N_SEL) & valid
    for slot in range(N_SEL):
        hit = sel & (rank == slot)
        found = jnp.max(hit.astype(I32), axis=0, keepdims=True)
        val = jnp.sum(jnp.where(hit, j, 0), axis=0, keepdims=True)
        idx_ref[slot:slot + 1, :] = jnp.where(found > 0, val, -1)


def rank_sample(imp2, cur, n_sel):
    nr, npad = imp2.shape
    return pl.pallas_call(
        functools.partial(_rank_sample_body, cur=cur, n_sel=n_sel),
        out_shape=jax.ShapeDtypeStruct((N_SEL, nr), I32),
        scratch_shapes=[pltpu.VMEM((npad, nr), F32)],
        name="rank_sample",
    )(imp2)


def _sel_sample_body(idx_ref, pt_ref, q_ref, kn_ref, vn_ref, *rest, n_past_blocks):
    del pt_ref
    kvs = rest[:N_SEL * N_KV_HEADS]
    o_ref = rest[N_SEL * N_KV_HEADS]
    b = pl.program_id(0)
    q = q_ref[0]
    qb = q.astype(BF16)
    nk = N_SEL * SEL_BLK
    rg = _row_head((N_HEADS, nk))
    col_slot = lax.shift_right_logical(lax.broadcasted_iota(I32, (N_HEADS, nk), 1), SEL_SHIFT)

    sc = jnp.zeros((N_HEADS, nk), F32)
    okv = jnp.zeros((N_HEADS, nk), I32)
    for h in range(N_KV_HEADS):
        kcat = jnp.concatenate([kvs[s * N_KV_HEADS + h][:, 0, 0, :] for s in range(N_SEL)], axis=0).astype(BF16)
        sh = lax.dot_general(qb, kcat, _NT, preferred_element_type=F32)
        ok = jnp.zeros((N_HEADS, nk), I32)
        for s in range(N_SEL):
            blk = idx_ref[s, b * N_KV_HEADS + h]
            ok = jnp.where(col_slot == s, jnp.where((blk >= 0) & (blk < n_past_blocks), 1, 0), ok)
        sc = jnp.where(rg == h, sh, sc)
        okv = jnp.where(rg == h, ok, okv)
    sm = jnp.where(okv > 0, sc * ATTN_SCALE, MASK_VALUE)
    s_new = jnp.sum(q * kn_ref[0], axis=-1, keepdims=True) * ATTN_SCALE
    m = jnp.maximum(jnp.max(sm, axis=-1, keepdims=True), s_new)
    p = jnp.where(okv > 0, jnp.exp(sm - m), 0.0)
    p_new = jnp.exp(s_new - m)
    den = jnp.sum(p, axis=-1, keepdims=True) + p_new
    pv = jnp.zeros((N_HEADS, HEAD_DIM), F32)
    for h in range(N_KV_HEADS):
        vcat = jnp.concatenate([kvs[s * N_KV_HEADS + h][:, 0, 1, :] for s in range(N_SEL)], axis=0).astype(BF16)
        pv = pv + jnp.dot(jnp.where(rg == h, p, 0.0).astype(BF16), vcat, preferred_element_type=F32)
    o_ref[0] = (pv + p_new * vn_ref[0]) / den


def sel_sample(idx, page_table, q16, kn16, vn16, pool4, n_past_blocks):
    ns = q16.shape[0]
    blk_per_page = PAGE_SIZE // SEL_BLK

    def kv_spec(s, h):
        def imap(b, idx_r, pt_r):
            blk = jnp.clip(idx_r[s, b * N_KV_HEADS + h], 0, n_past_blocks - 1)
            return (pt_r[b, blk // blk_per_page] * blk_per_page + blk % blk_per_page, h, 0, 0)
        return pl.BlockSpec((SEL_BLK, 1, 2, HEAD_DIM), imap)

    head3 = lambda: pl.BlockSpec((1, N_HEADS, HEAD_DIM), lambda b, i, p: (b, 0, 0))
    kv_specs = [kv_spec(s, h) for s in range(N_SEL) for h in range(N_KV_HEADS)]
    grid_spec = pltpu.PrefetchScalarGridSpec(
        num_scalar_prefetch=2,
        grid=(ns,),
        in_specs=[head3(), head3(), head3()] + kv_specs,
        out_specs=head3(),
    )
    return pl.pallas_call(
        functools.partial(_sel_sample_body, n_past_blocks=n_past_blocks),
        grid_spec=grid_spec,
        out_shape=jax.ShapeDtypeStruct((ns, N_HEADS, HEAD_DIM), F32),
        compiler_params=_params(("arbitrary",), 40),
        name="sel_sample",
    )(idx, page_table, q16, kn16, vn16, *([pool4] * len(kv_specs)))


def _win_sample_body(q_ref, win_ref, new_ref, kn_ref, vn_ref, oc_ref, os_ref, bg_ref, o_ref, wout_ref, *, wbuf):
    wout_ref[0:(wbuf - 1) * HK, :] = win_ref[HK:wbuf * HK, :]
    wout_ref[(wbuf - 1) * HK:wbuf * HK, :] = new_ref[...]
    q = q_ref[0]
    qb = q.astype(BF16)
    rg = _row_head((N_HEADS, wbuf))
    sc = jnp.zeros((N_HEADS, wbuf), F32)
    for h in range(N_KV_HEADS):
        kh = win_ref[pl.ds(2 * h, wbuf, stride=HK), :].astype(BF16)
        sc = jnp.where(rg == h, lax.dot_general(qb, kh, _NT, preferred_element_type=F32), sc)
    mask = lax.broadcasted_iota(I32, (N_HEADS, wbuf), 1) > wbuf - WINDOW
    sm = jnp.where(mask, sc * ATTN_SCALE, MASK_VALUE)
    s_new = jnp.sum(q * kn_ref[0], axis=-1, keepdims=True) * ATTN_SCALE
    m = jnp.maximum(jnp.max(sm, axis=-1, keepdims=True), s_new)
    p = jnp.where(mask, jnp.exp(sm - m), 0.0)
    p_new = jnp.exp(s_new - m)
    den = jnp.sum(p, axis=-1, keepdims=True) + p_new
    pv = jnp.zeros((N_HEADS, HEAD_DIM), F32)
    for h in range(N_KV_HEADS):
        vh = win_ref[pl.ds(2 * h + 1, wbuf, stride=HK), :].astype(BF16)
        pv = pv + jnp.dot(jnp.where(rg == h, p, 0.0).astype(BF16), vh, preferred_element_type=F32)
    o_w = (pv + p_new * vn_ref[0]) / den
    gates = jax.nn.sigmoid(bg_ref[0])
    o_ref[0] = gates[:, 0:1] * oc_ref[0] + gates[:, 1:2] * os_ref[0] + gates[:, 2:3] * o_w


def win_sample(q16, win2d, new2d, kn16, vn16, o_cmp, o_sel, bgs, wbuf):
    ns = q16.shape[0]
    head3 = lambda: pl.BlockSpec((1, N_HEADS, HEAD_DIM), lambda b: (b, 0, 0))
    wspec = lambda: pl.BlockSpec((wbuf * HK, LANES), lambda b: (b, 0))
    return pl.pallas_call(
        functools.partial(_win_sample_body, wbuf=wbuf),
        grid=(ns,),
        in_specs=[head3(), wspec(), pl.BlockSpec((HK, LANES), lambda b: (b, 0)), head3(), head3(), head3(), head3(),
                  head3()],
        out_specs=[head3(), wspec()],
        out_shape=[jax.ShapeDtypeStruct((ns, N_HEADS, HEAD_DIM), F32),
                   jax.ShapeDtypeStruct((ns * wbuf * HK, LANES), F32)],
        compiler_params=_params(("arbitrary",), 40),
        name="win_sample",
    )(q16, win2d, new2d, kn16, vn16, o_cmp, o_sel, bgs)


ROUTE_LANE0 = N_GROUPS


INFO_E1, INFO_E2, INFO_R1, INFO_R2, INFO_C1, INFO_C2 = range(6)


def _route_body(xp_ref, xs_ref, g_ref, wr_ref, br_ref, xn_ref, info_ref, cnt_ref, run_scr, *, n_real, n_ptiles):
    step = pl.program_id(0)

    @pl.when(step == 0)
    def _():
        run_scr[...] = jnp.zeros(run_scr.shape, F32)

    refs = (g_ref, wr_ref, br_ref, xn_ref, info_ref, cnt_ref, run_scr)

    @pl.when(step < n_ptiles)
    def _():
        _route_tile(xp_ref[...], step, n_real, *refs)

    @pl.when(step >= n_ptiles)
    def _():
        _route_tile(xs_ref[...], step, n_real, *refs)


def _route_tile(x, step, n_real, g_ref, wr_ref, br_ref, xn_ref, info_ref, cnt_ref, run_scr):
    tm = x.shape[0]
    var = jnp.mean(x * x, axis=-1, keepdims=True)
    xn = (x * lax.rsqrt(var + EPS)) * g_ref[...]
    xn_ref[...] = xn
    xnb = xn.astype(BF16)
    logits = jnp.dot(xnb, wr_ref[...], preferred_element_type=F32) + br_ref[...]
    lane = lax.broadcasted_iota(I32, logits.shape, 1)
    big = 4 * LANES

    isg = lane < N_GROUPS
    lg = jnp.where(isg, logits, -jnp.inf)
    mg = jnp.max(lg, axis=-1, keepdims=True)
    gi = jnp.min(jnp.where(lg == mg, lane, big), axis=-1, keepdims=True)
    pg = 1.0 / jnp.sum(jnp.where(isg, jnp.exp(lg - mg), 0.0), axis=-1, keepdims=True)

    lo = ROUTE_LANE0 + gi * EXP_PER_GROUP
    ing = (lane >= lo) & (lane < lo + EXP_PER_GROUP)
    le = jnp.where(ing, logits, -jnp.inf)
    me = jnp.max(le, axis=-1, keepdims=True)
    ee = jnp.where(ing, jnp.exp(le - me), 0.0)
    pe = jnp.where(ing, ee / jnp.sum(ee, axis=-1, keepdims=True), -1.0)
    p1 = jnp.max(pe, axis=-1, keepdims=True)
    i1 = jnp.min(jnp.where(pe == p1, lane, big), axis=-1, keepdims=True)
    pe2 = jnp.where(lane == i1, -1.0, pe)
    p2 = jnp.max(pe2, axis=-1, keepdims=True)
    i2 = jnp.min(jnp.where(pe2 == p2, lane, big), axis=-1, keepdims=True)
    tot = p1 + p2
    c1 = pg * (p1 / tot)
    c2 = pg * (p2 / tot)

    row = step * tm + lax.broadcasted_iota(I32, logits.shape, 0)
    hit = jnp.where(((lane == i1) | (lane == i2)) & (row < n_real), 1.0, 0.0)
    tri = jnp.where(lax.broadcasted_iota(I32, (tm, tm), 1) < lax.broadcasted_iota(I32, (tm, tm), 0), 1.0, 0.0)
    before = run_scr[...] + jnp.dot(tri.astype(BF16), hit.astype(BF16), preferred_element_type=F32)
    r1 = jnp.sum(jnp.where(lane == i1, before, 0.0), axis=-1, keepdims=True)
    r2 = jnp.sum(jnp.where(lane == i2, before, 0.0), axis=-1, keepdims=True)
    run_new = run_scr[...] + jnp.sum(hit, axis=0, keepdims=True)
    run_scr[...] = run_new
    cnt_ref[...] = run_new

    info = jnp.zeros(logits.shape, F32)
    for ln, val in ((INFO_E1, (i1 - ROUTE_LANE0).astype(F32)), (INFO_E2, (i2 - ROUTE_LANE0).astype(F32)),
                    (INFO_R1, r1), (INFO_R2, r2), (INFO_C1, c1), (INFO_C2, c2)):
        info = jnp.where(lane == ln, val, info)
    info_ref[...] = info


MOE_TILE = 256


def route(x_prompt, x_sample, g, wr, br, n_real):
    mp, d = x_prompt.shape
    n_ptiles = mp // MOE_TILE
    assert mp % MOE_TILE == 0 and x_sample.shape[0] == MOE_TILE
    m = mp + MOE_TILE
    return pl.pallas_call(
        functools.partial(_route_body, n_real=n_real, n_ptiles=n_ptiles),
        grid=(n_ptiles + 1,),
        in_specs=[pl.BlockSpec((MOE_TILE, d), lambda i: (jnp.minimum(i, n_ptiles - 1), 0)),
                  pl.BlockSpec((MOE_TILE, d), lambda i: (0, 0)), pl.BlockSpec((1, d), lambda i: (0, 0)),
                  pl.BlockSpec((d, LANES), lambda i: (0, 0)), pl.BlockSpec((1, LANES), lambda i: (0, 0))],
        out_specs=[pl.BlockSpec((MOE_TILE, d), lambda i: (i, 0)), pl.BlockSpec((MOE_TILE, LANES), lambda i: (i, 0)),
                   pl.BlockSpec((1, LANES), lambda i: (0, 0))],
        out_shape=[jax.ShapeDtypeStruct((m, d), F32), jax.ShapeDtypeStruct((m, LANES), F32),
                   jax.ShapeDtypeStruct((1, LANES), F32)],
        scratch_shapes=[pltpu.VMEM((1, LANES), F32)],
        compiler_params=_params(("arbitrary",), 40),
        name="route",
    )(x_prompt, x_sample, g.reshape(1, d), wr, br)


def _moe_plan(info, counts_row, n_real, n_tiles):
    n_pad = info.shape[0]
    e = jnp.clip(info[:, INFO_E1:INFO_E2 + 1].astype(I32), 0, N_EXPERTS - 1)
    r = info[:, INFO_R1:INFO_R2 + 1].astype(I32)
    counts = counts_row[0, ROUTE_LANE0:ROUTE_LANE0 + N_EXPERTS].astype(I32)
    padded = ((counts + MOE_TILE - 1) // MOE_TILE) * MOE_TILE
    ex = jnp.arange(N_EXPERTS, dtype=I32)
    ends = jnp.sum(jnp.where(ex[None, :] <= ex[:, None], padded[None, :], 0), axis=1)
    offs = ends - padded
    tok = jnp.arange(n_pad, dtype=I32)
    valid = (tok < n_real)[:, None]
    dest = jnp.where(valid, offs[e] + r, 0)
    n_slots = n_tiles * MOE_TILE
    src = jnp.zeros((n_slots,), I32).at[jnp.where(valid, dest, n_slots).reshape(-1)].set(
        jnp.repeat(tok, 2), mode="drop")
    n_used = ends[N_EXPERTS - 1] // MOE_TILE
    tiles = jnp.arange(n_tiles, dtype=I32)
    used = tiles < n_used
    te = jnp.minimum(jnp.sum((ends[None, :] <= (tiles * MOE_TILE)[:, None]).astype(I32), axis=1), N_EXPERTS - 1)
    te_prev = jnp.concatenate([te[:1] - 1, te[:-1]])
    first = (te != te_prev) & used
    later = tiles[None, :] > tiles[:, None]
    run = jnp.sum((first[None, :] & ~later).astype(I32), axis=1) - 1
    nxt_tile = jnp.min(jnp.where(first[None, :] & later, tiles[None, :], n_tiles), axis=1)
    nxt = jnp.where(nxt_tile < n_tiles, te[jnp.minimum(nxt_tile, n_tiles - 1)], -1)
    tile3 = lambda v: v.reshape(-1, 1, MOE_TILE)
    return (tile3(src), tile3(dest[:, 0]), tile3(dest[:, 1]),
            (te, first.astype(I32), nxt, jnp.bitwise_and(run, 1), n_used.reshape(1)))


def _issue_rows(idx_ref, src_hbm, dst_buf, slot, sem):
    n = idx_ref.shape[2]
    group = 8

    def body(g, carry):
        for j in range(group):
            r = g * group + j
            pltpu.make_async_copy(src_hbm.at[pl.ds(idx_ref[0, 0, r], 1), :], dst_buf.at[slot, pl.ds(r, 1), :],
                                  sem.at[slot]).start(priority=j % 2)
        return carry

    lax.fori_loop(0, n // group, body, 0)


def _wait_rows(src_hbm, dst_buf, slot, sem):
    n = dst_buf.shape[1]
    pltpu.make_async_copy(src_hbm.at[pl.ds(0, n), :], dst_buf.at[slot], sem.at[slot]).wait()


def _moe_expert_body(te_ref, first_ref, nxt_ref, par_ref, nu_ref, src_cur, src_nxt, xn_hbm, wg_hbm, wu_hbm, wd_hbm,
                     y_ref, xbuf, xsem, wg_buf, wu_buf, wd_buf, wsem, wgb, wub, wdb):
    i = pl.program_id(0)
    n_used = nu_ref[0]
    slot = lax.rem(i, 2)

    def weight_copies(e, ws):
        return (pltpu.make_async_copy(wg_hbm.at[e], wg_buf.at[ws], wsem.at[0, ws]),
                pltpu.make_async_copy(wu_hbm.at[e], wu_buf.at[ws], wsem.at[1, ws]),
                pltpu.make_async_copy(wd_hbm.at[e], wd_buf.at[ws], wsem.at[2, ws]))

    @pl.when(i == 0)
    def _():
        for cp in weight_copies(te_ref[0], par_ref[0]):
            cp.start()
        _issue_rows(src_cur, xn_hbm, xbuf, 0, xsem)

    @pl.when(i + 1 < n_used)
    def _():
        _issue_rows(src_nxt, xn_hbm, xbuf, 1 - slot, xsem)

    @pl.when(i < n_used)
    def _():
        @pl.when(first_ref[i] == 1)
        def _():
            ws = par_ref[i]
            for cp in weight_copies(te_ref[i], ws):
                cp.wait()

            @pl.when(nxt_ref[i] >= 0)
            def _():
                for cp in weight_copies(nxt_ref[i], 1 - ws):
                    cp.start()

            wgb[...] = wg_buf[ws].astype(BF16)
            wub[...] = wu_buf[ws].astype(BF16)
            wdb[...] = wd_buf[ws].astype(BF16)

        _wait_rows(xn_hbm, xbuf, slot, xsem)
        x = xbuf[slot].astype(BF16)
        hg = jnp.dot(x, wgb[...], preferred_element_type=F32)
        hu = jnp.dot(x, wub[...], preferred_element_type=F32)
        hid = (hg * jax.nn.sigmoid(hg)) * hu
        y_ref[...] = jnp.dot(hid.astype(BF16), wdb[...], preferred_element_type=F32)

    @pl.when(i >= n_used)
    def _():
        y_ref[...] = jnp.zeros(y_ref.shape, F32)


def moe_experts(tile_plan, src3, xn, wg, wu, wd):
    n_tiles = src3.shape[0]
    d = xn.shape[1]
    _, _, de = wg.shape
    smem_tile = lambda imap: pl.BlockSpec((1, 1, MOE_TILE), imap, memory_space=pltpu.SMEM)
    hbm = lambda: pl.BlockSpec(memory_space=pl.ANY)
    grid_spec = pltpu.PrefetchScalarGridSpec(
        num_scalar_prefetch=5,
        grid=(n_tiles,),
        in_specs=[smem_tile(lambda i, *_: (i, 0, 0)),
                  smem_tile(lambda i, *_: (jnp.minimum(i + 1, n_tiles - 1), 0, 0)),
                  hbm(), hbm(), hbm(), hbm()],
        out_specs=pl.BlockSpec((MOE_TILE, d), lambda i, *_: (i, 0)),
        scratch_shapes=[pltpu.VMEM((2, MOE_TILE, d), F32), pltpu.SemaphoreType.DMA((2,)),
                        pltpu.VMEM((2, d, de), F32), pltpu.VMEM((2, d, de), F32), pltpu.VMEM((2, de, d), F32),
                        pltpu.SemaphoreType.DMA((3, 2)),
                        pltpu.VMEM((d, de), BF16), pltpu.VMEM((d, de), BF16), pltpu.VMEM((de, d), BF16)],
    )
    return pl.pallas_call(
        _moe_expert_body,
        grid_spec=grid_spec,
        out_shape=jax.ShapeDtypeStruct((n_tiles * MOE_TILE, d), F32),
        compiler_params=_params(("arbitrary",), 56),
        name="moe_experts",
    )(*tile_plan, src3, src3, xn, wg, wu, wd)


def _moe_combine_body(d1_cur, d1_nxt, d2_cur, d2_nxt, ys_hbm, xp_ref, xs_ref, info_ref, gf_ref, yp_ref, ysm_ref,
                      buf1, buf2, sem1, sem2, *, n_ptiles):
    i = pl.program_id(0)
    slot = lax.rem(i, 2)

    @pl.when(i == 0)
    def _():
        _issue_rows(d1_cur, ys_hbm, buf1, 0, sem1)
        _issue_rows(d2_cur, ys_hbm, buf2, 0, sem2)

    @pl.when(i + 1 < pl.num_programs(0))
    def _():
        _issue_rows(d1_nxt, ys_hbm, buf1, 1 - slot, sem1)
        _issue_rows(d2_nxt, ys_hbm, buf2, 1 - slot, sem2)

    _wait_rows(ys_hbm, buf1, slot, sem1)
    _wait_rows(ys_hbm, buf2, slot, sem2)

    def finish(x_ref, y_ref):
        info = info_ref[...]
        xo = x_ref[...] + (info[:, INFO_C1:INFO_C1 + 1] * buf1[slot] + info[:, INFO_C2:INFO_C2 + 1] * buf2[slot])
        var = jnp.mean(xo * xo, axis=-1, keepdims=True)
        y_ref[...] = (xo * lax.rsqrt(var + EPS)) * gf_ref[...]

    @pl.when(i < n_ptiles)
    def _():
        finish(xp_ref, yp_ref)

    @pl.when(i >= n_ptiles)
    def _():
        finish(xs_ref, ysm_ref)


def moe_combine(dest1, dest2, ys, x_prompt, x_sample, info, g_final):
    mp, d = x_prompt.shape
    n_ptiles = mp // MOE_TILE
    n_tiles = n_ptiles + 1
    cur = lambda: pl.BlockSpec((1, 1, MOE_TILE), lambda i: (i, 0, 0), memory_space=pltpu.SMEM)
    nxt = lambda: pl.BlockSpec((1, 1, MOE_TILE), lambda i: (jnp.minimum(i + 1, n_tiles - 1), 0, 0),
                               memory_space=pltpu.SMEM)
    ptile = lambda: pl.BlockSpec((MOE_TILE, d), lambda i: (jnp.minimum(i, n_ptiles - 1), 0))
    stile = lambda: pl.BlockSpec((MOE_TILE, d), lambda i: (0, 0))
    return pl.pallas_call(
        functools.partial(_moe_combine_body, n_ptiles=n_ptiles),
        grid=(n_tiles,),
        in_specs=[cur(), nxt(), cur(), nxt(), pl.BlockSpec(memory_space=pl.ANY), ptile(), stile(),
                  pl.BlockSpec((MOE_TILE, LANES), lambda i: (i, 0)), pl.BlockSpec((1, d), lambda i: (0, 0))],
        out_specs=[ptile(), stile()],
        out_shape=[jax.ShapeDtypeStruct((mp, d), F32), jax.ShapeDtypeStruct((MOE_TILE, d), F32)],
        scratch_shapes=[pltpu.VMEM((2, MOE_TILE, d), F32), pltpu.VMEM((2, MOE_TILE, d), F32),
                        pltpu.SemaphoreType.DMA((2,)), pltpu.SemaphoreType.DMA((2,))],
        compiler_params=_params(("arbitrary",), 40),
        name="moe_combine",
    )(dest1, dest1, dest2, dest2, ys, x_prompt, x_sample, info, g_final.reshape(1, d))


def _mixer_inputs(x2, g_mix, w_in_t, w_bg_t, w_mg_t, tm, q_dtype):
    tn = PROJ_TILE_N
    xn = rmsnorm(x2, g_mix, tm, BF16)
    mm = functools.partial(matmul_cols, xn, tm=tm, w_is_t=True)
    u_rnn = mm(w_in_t, 0, D_RNN, tn=tn, out_dtype=F32, name="in_rnn")
    u_gate = mm(w_in_t, D_RNN, D_RNN, tn=tn, out_dtype=F32, name="in_gate")
    q = mm(w_in_t, 2 * D_RNN, N_HEADS * HEAD_DIM, tn=tn, out_dtype=q_dtype, name="in_q")
    kv0 = 2 * D_RNN + N_HEADS * HEAD_DIM
    kv_cmp, = matmul_kv(xn, w_in_t, kv0, tm, False, "in_kv_cmp")
    kv_sel, sel_planes = matmul_kv(xn, w_in_t, kv0 + 2 * KV_DIM, tm, True, "in_kv_sel")
    kv_win, win_planes = matmul_kv(xn, w_in_t, kv0 + 4 * KV_DIM, tm, True, "in_kv_win")
    u_bg = mm(w_bg_t, 0, w_bg_t.shape[0], tn=w_bg_t.shape[0], out_dtype=F32, name="in_bg")
    u_mg = mm(w_mg_t, 0, 2 * D_MODEL, tn=tn, out_dtype=F32, name="in_mg")
    return u_rnn, u_gate, q, (kv_cmp, kv_sel, kv_win), (sel_planes, win_planes), u_bg, u_mg


def kernel(x_prompt, x_sample, cache_cmp_kv, cache_sel_kv, cache_win_kv, state_conv, state_rglru, page_table, g_mix, w_in, conv_w, conv_b, lru_w_a, lru_b_a, lru_w_x, lru_b_x, lru_lambda, cmp_pe, cmp_w1, cmp_b1, cmp_w2, cmp_b2, w_out, g_ffn, moe_w_group, moe_b_group, moe_w_expert, moe_b_expert, moe_w_gate, moe_w_up, moe_w_down, g_final):
    b, t, d = x_prompt.shape
    ns = x_sample.shape[0]
    npages = page_table.shape[1]
    past_len = npages * PAGE_SIZE
    wbuf = cache_win_kv.shape[1]
    kvshape = (N_KV_HEADS, 2, HEAD_DIM)

    w_in_t = w_in.T
    bg0 = 2 * D_RNN + N_HEADS * HEAD_DIM + 6 * KV_DIM
    n_bg = 3 * N_HEADS
    w_bg = w_in_t[bg0:bg0 + n_bg].reshape(3, N_KV_HEADS, GROUP, d).transpose(1, 0, 2, 3).reshape(N_KV_HEADS, 3 * GROUP, d)
    w_bg = jnp.pad(w_bg, ((0, 0), (0, LANES - 3 * GROUP), (0, 0))).reshape(N_KV_HEADS * LANES, d)
    w_mg = w_in_t[bg0 + n_bg:]
    wa = lru_w_a.astype(BF16)
    wx = lru_w_x.astype(BF16)
    half_rows = CMP_STRIDE * HEAD_DIM
    wc = jnp.concatenate([cmp_w1[:, :half_rows], cmp_w1[:, half_rows:]], axis=-1).astype(BF16)
    pe8 = jnp.pad(cmp_pe.reshape(2, 2, half_rows), ((0, 0), (0, 6), (0, 0)))
    b1 = cmp_b1.reshape(2, 1, CMP_HIDDEN)
    w2 = cmp_w2.astype(BF16)
    b2 = cmp_b2.reshape(2, 1, HEAD_DIM)
    n_route = N_GROUPS + N_EXPERTS
    wr = jnp.pad(jnp.concatenate([moe_w_group, moe_w_expert], axis=1), ((0, 0), (0, LANES - n_route))).astype(BF16)
    br = jnp.pad(jnp.concatenate([moe_b_group, moe_b_expert]), (0, LANES - n_route)).reshape(1, LANES)
    lru_args = (conv_w, conv_b, wa, wx, lru_b_a, lru_b_x, lru_lambda)
    cmp_args = (wc, pe8, b1, w2, b2)

    xp2 = x_prompt.reshape(b * t, d)
    u_rnn, u_gate, q, kv2d, planes, u_bg, u_mg = _mixer_inputs(xp2, g_mix, w_in_t, w_bg, w_mg, PROJ_TILE_M, BF16)
    o_rnn, p_conv, p_h = lru_prompt(u_rnn.reshape(b, t, d), u_gate.reshape(b, t, d), *lru_args, LRU_TILE_T)
    p_cmp, p_sel, p_win_full = (a.reshape((b, t) + kvshape) for a in kv2d)
    p_win = p_win_full[:, t - min(WINDOW, t):]
    pt_prompt = jnp.arange(b * t // PAGE_SIZE, dtype=I32).reshape(b, t // PAGE_SIZE)
    kvc_p = compress(kv2d[0], pt_prompt, jnp.zeros((b * CMP_STRIDE * HK, HEAD_DIM), F32), *cmp_args)
    merged = attn_prompt_s(q.reshape(b, t, -1), planes[0], planes[1], kvc_p, u_bg.reshape(b, t, -1),
                           u_mg.reshape(b, t, -1), o_rnn, t // CMP_STRIDE - 1)
    x_mid = matmul_cols(merged.reshape(b * t, d), w_out, 0, d, PROJ_TILE_M, PROJ_TILE_N, F32, residual=xp2,
                        name="out_proj")

    xs2 = x_sample.reshape(ns, d)
    su_rnn, su_gate, sq, skv2d, _, su_bg, su_mg = _mixer_inputs(xs2, g_mix, w_in_t, w_bg, w_mg, ns, F32)
    so_rnn, cn, s_h = lru_sample(su_rnn, su_gate, state_conv.transpose(1, 0, 2), state_rglru, *lru_args)
    s_conv = cn.transpose(1, 0, 2)
    kvs = jnp.stack([a.reshape(ns, HK, HEAD_DIM) for a in skv2d], axis=1)
    s_cmp = kvs[:, 0].reshape((ns, 1) + kvshape)
    s_sel = kvs[:, 1].reshape((ns, 1) + kvshape)

    extra = jnp.concatenate([kvs[:, 0], jnp.zeros((ns, (CMP_STRIDE - 1) * HK, HEAD_DIM), F32)], axis=1)
    kvc_s = compress(cache_cmp_kv.reshape(-1, HEAD_DIM), page_table, extra.reshape(-1, HEAD_DIM), *cmp_args)
    n_cmp_s = -(-(past_len + 1) // CMP_STRIDE) - 1
    n_sel_s = -(-(past_len + 1) // SEL_BLK)
    nsp = -(-n_sel_s // LANES) * LANES
    q16 = sq.reshape(ns, N_HEADS, HEAD_DIM)
    m_s = jnp.asarray(_overlap_matrix(n_cmp_s, kvc_s.shape[2], n_sel_s, nsp), BF16)
    so_cmp, imp = cmp_sample(q16, kvc_s, m_s, past_len, n_cmp_s)
    idx = rank_sample(imp.reshape(ns * N_KV_HEADS, nsp), past_len // SEL_BLK, n_sel_s)

    def per_head_rows(new_kv, kv):
        return jnp.repeat(new_kv[:, kv::2], GROUP, axis=1)

    so_sel = sel_sample(idx, page_table, q16, per_head_rows(kvs[:, 1], 0), per_head_rows(kvs[:, 1], 1),
                        cache_sel_kv.reshape((-1,) + kvshape), past_len // SEL_BLK)
    bgs = su_bg.reshape(ns, N_KV_HEADS, LANES)[:, :, :3 * GROUP].reshape(ns, N_KV_HEADS, 3, GROUP)
    bgs = jnp.pad(bgs.transpose(0, 1, 3, 2).reshape(ns, N_HEADS, 3), ((0, 0), (0, 0), (0, LANES - 3)))
    so_nsa, s_win2d = win_sample(q16, cache_win_kv.reshape(-1, HEAD_DIM), skv2d[2], per_head_rows(kvs[:, 2], 0),
                                 per_head_rows(kvs[:, 2], 1), so_cmp, so_sel, bgs, wbuf)
    s_win = s_win2d.reshape((ns, wbuf) + kvshape)
    s_merged = merge(su_mg, so_rnn, so_nsa.reshape(ns, d), ns)
    sx_mid = matmul_cols(s_merged, w_out, 0, d, ns, PROJ_TILE_N, F32, residual=xs2, name="out_proj_s")

    n_real = b * t + ns
    assert ns <= MOE_TILE
    sx_pad = jnp.pad(sx_mid, ((0, MOE_TILE - ns), (0, 0)))
    xn_all, info, counts = route(x_mid, sx_pad, g_ffn, wr, br, n_real)
    n_tiles = -(-(2 * n_real + N_EXPERTS * (MOE_TILE - 1)) // MOE_TILE)
    src3, dest1, dest2, tile_plan = _moe_plan(info, counts, n_real, n_tiles)
    ys = moe_experts(tile_plan, src3, xn_all, moe_w_gate, moe_w_up, moe_w_down)
    y_p2, y_s2 = moe_combine(dest1, dest2, ys, x_mid, sx_pad, info, g_final)
    y_prompt = y_p2.reshape(b, t, d)
    y_sample = y_s2[:ns].reshape(ns, 1, d)

    return (y_prompt, y_sample, p_cmp, s_cmp, p_sel, s_sel, p_win, s_win, p_conv, s_conv,
            p_h.reshape(b, d), s_h)
```

```python
import functools

import jax
import jax.numpy as jnp
import numpy as np
from jax import lax
from jax.experimental import pallas as pl
from jax.experimental.pallas import tpu as pltpu

F32 = jnp.float32
BF16 = jnp.bfloat16
I32 = jnp.int32

D_MODEL = 2048
D_RNN = 2048
RNN_BLOCKS = 16
RNN_BLOCK_DIM = 128
CONV_W = 4
LRU_C = 8.0
N_HEADS = 16
HEAD_DIM = 128
N_KV_HEADS = 4
GROUP = 4
KV_DIM = N_KV_HEADS * HEAD_DIM
HK = 2 * N_KV_HEADS
CMP_STRIDE = 16
CMP_BLK = 32
CMP_HIDDEN = 256
SEL_BLK = 64
SEL_SHIFT = 6
N_SEL = 16
WINDOW = 512
PAGE_SIZE = 128
ATTN_SCALE = HEAD_DIM ** -0.5
FORCED_SCORE = 1e4
N_GROUPS = 4
EXP_PER_GROUP = 8
N_EXPERTS = 32
D_EXPERT = 512
EPS = 1e-6
MASK_VALUE = -1e30
LANES = 128
MIB = 1024 * 1024

PROJ_TILE_M = 1024
PROJ_TILE_N = 1024
LRU_TILE_T = 256

_NT = (((1,), (1,)), ((), ()))


def _params(sem, vmem_mib):
    return pltpu.CompilerParams(dimension_semantics=sem, vmem_limit_bytes=vmem_mib * MIB)


def _msoftmax(s, mask):
    sm = jnp.where(mask, s, MASK_VALUE)
    m = jnp.max(sm, axis=-1, keepdims=True)
    e = jnp.where(mask, jnp.exp(sm - m), 0.0)
    den = jnp.sum(e, axis=-1, keepdims=True)
    return e * jnp.where(den > 0.0, 1.0 / den, 0.0)


def _gelu_tanh(x):
    return 0.5 * x * (1.0 + jnp.tanh(0.7978845608028654 * (x + 0.044715 * (x * x * x))))


def _rmsnorm_body(x_ref, g_ref, o_ref):
    x = x_ref[...]
    var = jnp.mean(x * x, axis=-1, keepdims=True)
    o_ref[...] = ((x * lax.rsqrt(var + EPS)) * g_ref[...]).astype(o_ref.dtype)


def rmsnorm(x, g, tm, out_dtype):
    m, d = x.shape
    return pl.pallas_call(
        _rmsnorm_body,
        grid=(m // tm,),
        in_specs=[pl.BlockSpec((tm, d), lambda i: (i, 0)), pl.BlockSpec((1, d), lambda i: (0, 0))],
        out_specs=pl.BlockSpec((tm, d), lambda i: (i, 0)),
        out_shape=jax.ShapeDtypeStruct((m, d), out_dtype),
        compiler_params=_params(("arbitrary",), 56),
        name="rmsnorm",
    )(x, g.reshape(1, d))


def _xw(x, wbf, w_is_t):
    if w_is_t:
        return lax.dot_general(x, wbf, _NT, preferred_element_type=F32)
    return jnp.dot(x, wbf, preferred_element_type=F32)


def _mm_body(x_ref, w_ref, *rest, w_is_t):
    r_ref = rest[0] if len(rest) == 3 else None
    o_ref, wbf_ref = rest[-2:]

    @pl.when(pl.program_id(1) == 0)
    def _():
        wbf_ref[...] = w_ref[...].astype(BF16)

    res = _xw(x_ref[...], wbf_ref[...], w_is_t)
    if r_ref is not None:
        res = r_ref[...] + res
    o_ref[...] = res.astype(o_ref.dtype)


def matmul_cols(x, w, col0, ncols, tm, tn, out_dtype, residual=None, w_is_t=False, name="matmul"):
    m, k = x.shape
    cb0 = col0 // tn
    assert col0 % tn == 0 and ncols % tn == 0 and m % tm == 0
    if w_is_t:
        w_spec = pl.BlockSpec((tn, k), lambda j, i: (cb0 + j, 0))
        w_tile = (tn, k)
    else:
        w_spec = pl.BlockSpec((k, tn), lambda j, i: (0, cb0 + j))
        w_tile = (k, tn)
    in_specs = [pl.BlockSpec((tm, k), lambda j, i: (i, 0)), w_spec]
    args = [x, w]
    if residual is not None:
        in_specs.append(pl.BlockSpec((tm, tn), lambda j, i: (i, j)))
        args.append(residual)
    return pl.pallas_call(
        functools.partial(_mm_body, w_is_t=w_is_t),
        grid=(ncols // tn, m // tm),
        in_specs=in_specs,
        out_specs=pl.BlockSpec((tm, tn), lambda j, i: (i, j)),
        out_shape=jax.ShapeDtypeStruct((m, ncols), out_dtype),
        scratch_shapes=[pltpu.VMEM(w_tile, BF16)],
        compiler_params=_params(("arbitrary", "arbitrary"), 56),
        name=name,
    )(*args)


def _mm_kv_body(x_ref, w_ref, oi_ref, *rest):
    wbf_ref = rest[-1]

    @pl.when(pl.program_id(0) == 0)
    def _():
        wbf_ref[...] = w_ref[...].astype(BF16)

    res = _xw(x_ref[...], wbf_ref[...], True)
    tm = res.shape[0]
    for hk in range(HK):
        blk = res[:, hk * HEAD_DIM:(hk + 1) * HEAD_DIM]
        oi_ref[pl.ds(hk, tm, stride=HK), :] = blk
        if len(rest) == 2:
            rest[0][hk] = blk.astype(BF16)


def matmul_kv(x, w_t, col0, tm, planes, name):
    m, k = x.shape
    ncols = HK * HEAD_DIM
    assert col0 % ncols == 0 and m % tm == 0
    out_specs = [pl.BlockSpec((tm * HK, HEAD_DIM), lambda i: (i, 0))]
    out_shape = [jax.ShapeDtypeStruct((m * HK, HEAD_DIM), F32)]
    if planes:
        out_specs.append(pl.BlockSpec((HK, tm, HEAD_DIM), lambda i: (0, i, 0)))
        out_shape.append(jax.ShapeDtypeStruct((HK, m, HEAD_DIM), BF16))
    return pl.pallas_call(
        _mm_kv_body,
        grid=(m // tm,),
        in_specs=[pl.BlockSpec((tm, k), lambda i: (i, 0)), pl.BlockSpec((ncols, k), lambda i: (col0 // ncols, 0))],
        out_specs=out_specs,
        out_shape=out_shape,
        scratch_shapes=[pltpu.VMEM((ncols, k), BF16)],
        compiler_params=_params(("arbitrary",), 56),
        name=name,
    )(x, w_t)


def _merge_body(mg_ref, rnn_ref, nsa_ref, o_ref):
    d = rnn_ref.shape[-1]
    ga = jax.nn.sigmoid(mg_ref[:, :d])
    gb = jax.nn.sigmoid(mg_ref[:, d:])
    o_ref[...] = (ga * rnn_ref[...] + gb * nsa_ref[...]).astype(o_ref.dtype)


def merge(u_mg, o_rnn, o_nsa, tm):
    m, d = o_rnn.shape
    return pl.pallas_call(
        _merge_body,
        grid=(m // tm,),
        in_specs=[pl.BlockSpec((tm, 2 * d), lambda i: (i, 0)), pl.BlockSpec((tm, d), lambda i: (i, 0)),
                  pl.BlockSpec((tm, d), lambda i: (i, 0))],
        out_specs=pl.BlockSpec((tm, d), lambda i: (i, 0)),
        out_shape=jax.ShapeDtypeStruct((m, d), BF16),
        compiler_params=_params(("arbitrary",), 48),
        name="merge",
    )(u_mg, o_rnn, o_nsa)


def _lru_gates(uc, wa_ref, wx_ref, ba_ref, bx_ref, lam_ref):
    ucb = uc.astype(BF16)
    ra, ri = [], []
    for n in range(RNN_BLOCKS):
        blk = ucb[:, n * RNN_BLOCK_DIM:(n + 1) * RNN_BLOCK_DIM]
        ra.append(jnp.dot(blk, wa_ref[n], preferred_element_type=F32))
        ri.append(jnp.dot(blk, wx_ref[n], preferred_element_type=F32))
    r = jax.nn.sigmoid(jnp.concatenate(ra, axis=-1) + ba_ref[...])
    i = jax.nn.sigmoid(jnp.concatenate(ri, axis=-1) + bx_ref[...])
    z = -lam_ref[...]
    softplus = jnp.maximum(z, 0.0) + jnp.log1p(jnp.exp(-jnp.abs(z)))
    log_a = (-LRU_C * r) * softplus
    a = jnp.exp(log_a)
    mult = jnp.sqrt(-jnp.tanh(log_a) * (a * a + 1.0))
    return a, mult * (i * uc)


def _lru_prompt_body(u_ref, gate_ref, cw_ref, cb_ref, wa_ref, wx_ref, ba_ref, bx_ref, lam_ref,
                     o_ref, conv_ref, hlast_ref, xbuf, hcar, acum, bcum):
    t = pl.program_id(1)
    nt = pl.num_programs(1)
    tt, d = u_ref.shape[1], u_ref.shape[2]

    @pl.when(t == 0)
    def _():
        xbuf[0:8, :] = jnp.zeros((8, d), F32)
        hcar[...] = jnp.zeros((8, d), F32)

    x = u_ref[0]
    xbuf[8:8 + tt, :] = x
    uc = (cw_ref[0:1, :] * xbuf[5:5 + tt, :] + cw_ref[1:2, :] * xbuf[6:6 + tt, :]
          + cw_ref[2:3, :] * xbuf[7:7 + tt, :] + cw_ref[3:4, :] * x) + cb_ref[...]
    xbuf[0:8, :] = xbuf[tt:tt + 8, :]

    a, b = _lru_gates(uc, wa_ref, wx_ref, ba_ref, bx_ref, lam_ref)
    g = tt // 8
    a3 = a.reshape(g, 8, d)
    b3 = b.reshape(g, 8, d)
    row = lax.broadcasted_iota(I32, (g, 8, d), 1)
    for s in (1, 2, 4):
        a_sh = pltpu.roll(a3, s, axis=1)
        b_sh = pltpu.roll(b3, s, axis=1)
        keep = row >= s
        b3 = jnp.where(keep, a3 * b_sh + b3, b3)
        a3 = jnp.where(keep, a3 * a_sh, a3)
    acum[...] = a3
    bcum[...] = b3

    def body(gi, h):
        hg = bcum[gi] + acum[gi] * h
        bcum[gi] = hg
        return jnp.broadcast_to(hg[7:8, :], (8, d))

    hfin = lax.fori_loop(0, g, body, hcar[...])
    hcar[...] = hfin
    o_ref[0] = bcum[...].reshape(tt, d) * _gelu_tanh(gate_ref[0])

    @pl.when(t == nt - 1)
    def _():
        conv_ref[0] = x[tt - (CONV_W - 1):tt, :]
        hlast_ref[0] = hfin[0:1, :]


def lru_prompt(u_rnn, u_gate, conv_w, conv_b, wa, wx, ba, bx, lam, tt):
    b, t, d = u_rnn.shape
    vec = lambda: pl.BlockSpec((1, d), lambda i, j: (0, 0))
    wspec = lambda: pl.BlockSpec((RNN_BLOCKS, RNN_BLOCK_DIM, RNN_BLOCK_DIM), lambda i, j: (0, 0, 0))
    return pl.pallas_call(
        _lru_prompt_body,
        grid=(b, t // tt),
        in_specs=[pl.BlockSpec((1, tt, d), lambda i, j: (i, j, 0)), pl.BlockSpec((1, tt, d), lambda i, j: (i, j, 0)),
                  pl.BlockSpec((CONV_W, d), lambda i, j: (0, 0)), vec(), wspec(), wspec(), vec(), vec(), vec()],
        out_specs=[pl.BlockSpec((1, tt, d), lambda i, j: (i, j, 0)),
                   pl.BlockSpec((1, CONV_W - 1, d), lambda i, j: (i, 0, 0)),
                   pl.BlockSpec((1, 1, d), lambda i, j: (i, 0, 0))],
        out_shape=[jax.ShapeDtypeStruct((b, t, d), F32), jax.ShapeDtypeStruct((b, CONV_W - 1, d), F32),
                   jax.ShapeDtypeStruct((b, 1, d), F32)],
        scratch_shapes=[pltpu.VMEM((tt + 8, d), F32), pltpu.VMEM((8, d), F32),
                        pltpu.VMEM((tt // 8, 8, d), F32), pltpu.VMEM((tt // 8, 8, d), F32)],
        compiler_params=_params(("arbitrary", "arbitrary"), 56),
        name="lru_prompt",
    )(u_rnn, u_gate, conv_w, conv_b.reshape(1, d), wa, wx, ba.reshape(1, d), bx.reshape(1, d), lam.reshape(1, d))


def _lru_sample_body(u_ref, gate_ref, cp_ref, h0_ref, cw_ref, cb_ref, wa_ref, wx_ref, ba_ref, bx_ref, lam_ref,
                     o_ref, cn_ref, h_ref):
    x = u_ref[...]
    uc = (cw_ref[0:1, :] * cp_ref[0] + cw_ref[1:2, :] * cp_ref[1] + cw_ref[2:3, :] * cp_ref[2]
          + cw_ref[3:4, :] * x) + cb_ref[...]
    a, b = _lru_gates(uc, wa_ref, wx_ref, ba_ref, bx_ref, lam_ref)
    h = a * h0_ref[...] + b
    o_ref[...] = h * _gelu_tanh(gate_ref[...])
    h_ref[...] = h
    cn_ref[0] = cp_ref[1]
    cn_ref[1] = cp_ref[2]
    cn_ref[2] = x


def lru_sample(u_rnn, u_gate, conv_prev_t, h0, conv_w, conv_b, wa, wx, ba, bx, lam):
    n, d = u_rnn.shape
    return pl.pallas_call(
        _lru_sample_body,
        out_shape=[jax.ShapeDtypeStruct((n, d), F32), jax.ShapeDtypeStruct((CONV_W - 1, n, d), F32),
                   jax.ShapeDtypeStruct((n, d), F32)],
        compiler_params=pltpu.CompilerParams(vmem_limit_bytes=40 * MIB),
        name="lru_sample",
    )(u_rnn, u_gate, conv_prev_t, h0, conv_w, conv_b.reshape(1, d), wa, wx, ba.reshape(1, d), bx.reshape(1, d),
      lam.reshape(1, d))


COMPRESS_PAGES = 32
CHUNKS_PER_PAGE = PAGE_SIZE // CMP_STRIDE


def _compress_body(pt_ref, *refs, pp, n_steps):
    page_refs = refs[:pp]
    nxt_ref, extra_ref, wc_ref, pe_ref, b1_ref, w2_ref, b2_ref, o_ref, res_scr, pet_scr = refs[pp:]
    del pt_ref
    is_last = pl.program_id(1) == n_steps - 1
    m_rows = pp * CHUNKS_PER_PAGE * N_KV_HEADS
    lo4 = lax.broadcasted_iota(I32, (4, 8, 8, LANES), 2) < 4
    lo3 = lax.broadcasted_iota(I32, (8, 8, LANES), 1) < 4

    for k in range(2):
        lhs_sp = [[] for _ in range(8)]
        for pr in page_refs:
            xk = pr[pl.ds(k, PAGE_SIZE * N_KV_HEADS, stride=2), :]
            x5 = xk.reshape(4, 2, 8, 8, LANES)
            a0 = x5[:, 0]
            a1 = x5[:, 1]
            be = jnp.where(lo4, a0, pltpu.roll(a1, 4, axis=2))
            bo = jnp.where(lo4, pltpu.roll(a0, 4, axis=2), a1)
            for sp in range(8):
                lhs_sp[sp].append(jnp.concatenate([be[:, sp], bo[:, sp]], axis=-1).reshape(32, 2 * LANES))
        acc = jnp.zeros((m_rows, 2 * CMP_HIDDEN), F32)
        for sp in range(8):
            lhs = jnp.concatenate(lhs_sp[sp], axis=0).astype(BF16)
            acc = acc + jnp.dot(lhs, wc_ref[k, sp * 256:(sp + 1) * 256, :], preferred_element_type=F32)

        @pl.when((pl.program_id(0) == 0) & (pl.program_id(1) == 0))
        def _():
            pet_scr[k] = jnp.dot(pe_ref[k].astype(BF16), wc_ref[k], preferred_element_type=F32)

        peb0 = pet_scr[k, 0:1, :CMP_HIDDEN]
        peb1 = pet_scr[k, 1:2, CMP_HIDDEN:]
        h0 = acc[:, :CMP_HIDDEN] + peb0
        h1 = acc[:, CMP_HIDDEN:] + peb1

        xn = jnp.where(is_last, extra_ref[pl.ds(k, CMP_STRIDE * N_KV_HEADS, stride=2), :],
                       nxt_ref[pl.ds(k, CMP_STRIDE * N_KV_HEADS, stride=2), :])
        xn3 = xn.reshape(8, 8, LANES)
        ln = jnp.concatenate([jnp.where(lo3, xn3, 0.0), jnp.where(lo3, 0.0, xn3)], axis=-1).astype(BF16)
        nacc = jnp.zeros((8, CMP_HIDDEN), F32)
        for sp in range(8):
            nacc = nacc + jnp.dot(ln[sp], wc_ref[k, sp * 256:(sp + 1) * 256, CMP_HIDDEN:],
                                  preferred_element_type=F32)
        n8 = nacc + pltpu.roll(nacc, 4, axis=0) + peb1
        h1e = jnp.concatenate([h1, n8], axis=0)
        h1s = pltpu.roll(h1e, m_rows + 8 - N_KV_HEADS, axis=0)[:m_rows]
        pre = h0 + h1s + b1_ref[k]
        out_k = jnp.dot(_gelu_tanh(pre).astype(BF16), w2_ref[k], preferred_element_type=F32) + b2_ref[k]
        res_scr[...] = out_k
        for h in range(N_KV_HEADS):
            o_ref[0, 2 * h + k] = res_scr[pl.ds(h, m_rows // N_KV_HEADS, stride=N_KV_HEADS), :]


def compress(pool2d, page_table, extra2d, wc, pe8, b1, w2, b2):
    nseq, npages = page_table.shape
    pp = min(COMPRESS_PAGES, npages)
    n_steps = npages // pp
    assert npages % pp == 0
    rows_page = PAGE_SIZE * HK
    rows_chunk = CMP_STRIDE * HK
    blocks_step = pp * CHUNKS_PER_PAGE

    def page_spec(i):
        return pl.BlockSpec((rows_page, LANES), lambda b, s, pt: (pt[b, s * pp + i], 0))

    def nxt_map(b, s, pt):
        return (pt[b, jnp.minimum((s + 1) * pp, npages - 1)] * CHUNKS_PER_PAGE, 0)

    const3 = lambda shape: pl.BlockSpec(shape, lambda b, s, pt: (0, 0, 0))
    in_specs = [page_spec(i) for i in range(pp)] + [
        pl.BlockSpec((rows_chunk, LANES), nxt_map),
        pl.BlockSpec((rows_chunk, LANES), lambda b, s, pt: (b, 0)),
        const3((2, CMP_STRIDE * HEAD_DIM, 2 * CMP_HIDDEN)),
        const3((2, 8, CMP_STRIDE * HEAD_DIM)),
        const3((2, 1, CMP_HIDDEN)),
        const3((2, CMP_HIDDEN, HEAD_DIM)),
        const3((2, 1, HEAD_DIM)),
    ]
    grid_spec = pltpu.PrefetchScalarGridSpec(
        num_scalar_prefetch=1,
        grid=(nseq, n_steps),
        in_specs=in_specs,
        out_specs=pl.BlockSpec((1, HK, blocks_step, HEAD_DIM), lambda b, s, pt: (b, 0, s, 0)),
        scratch_shapes=[pltpu.VMEM((blocks_step * N_KV_HEADS, HEAD_DIM), F32),
                        pltpu.VMEM((2, 8, 2 * CMP_HIDDEN), F32)],
    )
    return pl.pallas_call(
        functools.partial(_compress_body, pp=pp, n_steps=n_steps),
        grid_spec=grid_spec,
        out_shape=jax.ShapeDtypeStruct((nseq, HK, npages * CHUNKS_PER_PAGE, HEAD_DIM), F32),
        compiler_params=_params(("arbitrary", "arbitrary"), 56),
        name="compress",
    )(page_table, *([pool2d] * pp), pool2d, extra2d, wc, pe8, b1, w2, b2)


def _overlap_matrix(n_cmp, n_cmp_pad, n_sel, n_sel_pad):
    cs = np.arange(n_cmp_pad)[:, None] * CMP_STRIDE
    ss = np.arange(n_sel_pad)[None, :] * SEL_BLK
    ov = np.minimum(cs + CMP_BLK, ss + SEL_BLK) - np.maximum(cs, ss)
    m = np.clip(ov, 0, CMP_BLK).astype(np.float32) / CMP_BLK
    m[n_cmp:, :] = 0.0
    m[:, n_sel:] = 0.0
    return m


QBLK = 128


KCHUNK = 64
KPIECE = 512
LOG2E = 1.4426950408889634


def _softmax_passes(s_scr, p_scr, n_plain, n_masked, n_total, plain_mask, full_mask):
    cols = s_scr.shape[1]
    grp = KCHUNK // 8
    c2 = ATTN_SCALE * LOG2E

    def rows(c):
        return pl.ds(pl.multiple_of(c * KCHUNK, KCHUNK), KCHUNK)

    def masked_max(mask_fn):
        def body(c, m8):
            sm = jnp.where(mask_fn(c), s_scr[rows(c), :], MASK_VALUE)
            s_scr[rows(c), :] = sm
            return jnp.maximum(m8, jnp.max(sm.reshape(grp, 8, cols), axis=0))
        return body

    m8 = lax.fori_loop(0, n_plain, masked_max(plain_mask), jnp.full((8, cols), MASK_VALUE, F32))
    m8 = lax.fori_loop(n_plain, n_plain + n_masked, masked_max(full_mask), m8)
    mc = jnp.max(m8, axis=0, keepdims=True) * c2

    def pass_exp(c, l8):
        p = jnp.exp2(s_scr[rows(c), :] * c2 - mc)
        p_scr[rows(c), :] = p.astype(BF16)
        return l8 + jnp.sum(p.reshape(grp, 8, cols), axis=0)

    l8 = lax.fori_loop(0, n_plain + n_masked, pass_exp, jnp.zeros((8, cols), F32))

    def zero_fill(c, carry):
        p_scr[rows(c), :] = jnp.zeros((KCHUNK, cols), BF16)
        return carry

    lax.fori_loop(n_plain + n_masked, n_total, zero_fill, 0)
    return 1.0 / jnp.sum(l8, axis=0, keepdims=True)


def _attn_prompt_s_body(q_ref, ksel_ref, vsel_ref, kwin_ref, vwin_ref, kvc_ref, bg_ref, mga_ref, mgb_ref, rnn_ref,
                        mt_ref, o_ref, vsel_t, vwin_t, kc_bf, vc_t_bf, s_scr, p_scr, sel_scr, *, n_cmp, n_sel):
    qi = pl.program_id(2)
    q0 = qi * QBLK
    cols = GROUP * QBLK

    @pl.when(qi == 0)
    def _():
        vsel_t[...] = vsel_ref[0].astype(F32).T.astype(BF16)
        vwin_t[...] = vwin_ref[0].astype(F32).T.astype(BF16)
        kc_bf[...] = kvc_ref[0, 0].astype(BF16)
        vc_t_bf[...] = kvc_ref[0, 1].T.astype(BF16)

    q = q_ref[0]
    q4 = jnp.concatenate([q[:, g * HEAD_DIM:(g + 1) * HEAD_DIM] for g in range(GROUP)], axis=0)
    qpos_row = q0 + (lax.broadcasted_iota(I32, (1, cols), 1) & (QBLK - 1))
    row_in_chunk = lax.broadcasted_iota(I32, (KCHUNK, cols), 0)

    nb = kvc_ref.shape[2]
    kc = kc_bf[...]
    vc_t = vc_t_bf[...]
    n_io = lax.broadcasted_iota(I32, (nb, cols), 0)
    mask_c = (n_io * CMP_STRIDE + (CMP_BLK - 1) <= qpos_row) & (n_io < n_cmp)
    sc = jnp.where(mask_c, lax.dot_general(kc, q4, _NT, preferred_element_type=F32) * ATTN_SCALE, MASK_VALUE)
    ec = jnp.where(mask_c, jnp.exp(sc - jnp.max(sc, axis=0, keepdims=True)), 0.0)
    den_c = jnp.sum(ec, axis=0, keepdims=True)
    pc = ec * jnp.where(den_c > 0.0, 1.0 / den_c, 0.0)
    o_c = jnp.dot(vc_t, pc.astype(BF16), preferred_element_type=F32)
    psum = pc[:, 0:QBLK] + pc[:, QBLK:2 * QBLK] + pc[:, 2 * QBLK:3 * QBLK] + pc[:, 3 * QBLK:4 * QBLK]

    nsp = mt_ref.shape[0]
    imp_t = jnp.dot(mt_ref[...], psum.astype(BF16), preferred_element_type=F32)
    jt = lax.broadcasted_iota(I32, (nsp, QBLK), 0)
    cur = lax.shift_right_logical(q0 + lax.broadcasted_iota(I32, (nsp, QBLK), 1), SEL_SHIFT)
    valid = (jt <= cur) & (jt < n_sel)
    forced = (jt == 0) | (jt >= cur - 1)
    score = jnp.where(forced, FORCED_SCORE, imp_t)
    score = jnp.where(valid, score, -1.0)
    rank = jnp.zeros((nsp, QBLK), I32)
    for k in range(n_sel):
        sk = score[k:k + 1, :]
        beats = (sk > score) | ((sk == score) & (k < jt))
        rank = rank + beats.astype(I32)
    sel_t = jnp.where((rank < N_SEL) & valid, 1.0, 0.0)
    sel_scr[...] = jnp.concatenate([sel_t] * GROUP, axis=1)

    wkeys = WINDOW + QBLK
    ks = pl.multiple_of(jnp.maximum(q0 - WINDOW, 0), QBLK)
    s_scr[0:wkeys, :] = lax.dot_general(kwin_ref[0, pl.ds(ks, wkeys), :], q4, _NT, preferred_element_type=F32)

    def win_mask(c):
        dist = qpos_row - (ks + c * KCHUNK + row_in_chunk)
        return (dist >= 0) & (dist < WINDOW)

    n_wchunks = wkeys // KCHUNK
    inv_w = _softmax_passes(s_scr, p_scr, 0, n_wchunks, n_wchunks, win_mask, win_mask)
    o_w = jnp.dot(vwin_t[:, pl.ds(ks, wkeys)], p_scr[0:wkeys, :], preferred_element_type=F32) * inv_w

    n_pieces = qi // (KPIECE // QBLK) + 1

    def score_piece(c, carry):
        r0 = pl.multiple_of(c * KPIECE, KPIECE)
        s_scr[pl.ds(r0, KPIECE), :] = lax.dot_general(ksel_ref[0, pl.ds(r0, KPIECE), :], q4, _NT,
                                                      preferred_element_type=F32)
        return carry

    lax.fori_loop(0, n_pieces, score_piece, 0)

    def picked(c):
        return sel_scr[pl.ds(c, 1), :] > 0.5

    def picked_causal(c):
        return picked(c) & (c * KCHUNK + row_in_chunk <= qpos_row)

    chunks_q = QBLK // KCHUNK
    inv_s = _softmax_passes(s_scr, p_scr, qi * chunks_q, chunks_q, n_pieces * (KPIECE // KCHUNK),
                            picked, picked_causal)

    def pv_piece(c, acc):
        r0 = pl.multiple_of(c * KPIECE, KPIECE)
        return acc + jnp.dot(vsel_t[:, pl.ds(r0, KPIECE)], p_scr[pl.ds(r0, KPIECE), :], preferred_element_type=F32)

    o_s = lax.fori_loop(0, n_pieces, pv_piece, jnp.zeros((HEAD_DIM, cols), F32)) * inv_s

    gates_t = jax.nn.sigmoid(bg_ref[0]).T
    grow = lambda br: jnp.concatenate([gates_t[br * GROUP + g:br * GROUP + g + 1, :] for g in range(GROUP)], axis=1)
    o_t = grow(0) * o_c + grow(1) * o_s + grow(2) * o_w
    for g in range(GROUP):
        cs = slice(g * HEAD_DIM, (g + 1) * HEAD_DIM)
        o_nsa = o_t[:, g * QBLK:(g + 1) * QBLK].T
        merged = jax.nn.sigmoid(mga_ref[0, :, cs]) * rnn_ref[0, :, cs] + jax.nn.sigmoid(mgb_ref[0, :, cs]) * o_nsa
        o_ref[0, :, cs] = merged.astype(o_ref.dtype)


def attn_prompt_s(q3, sel_planes, win_planes, kvc, ubg3, umg3, o_rnn3, n_cmp):
    b, t, _ = q3.shape
    nb = kvc.shape[2]
    assert KCHUNK == SEL_BLK and t % KPIECE == 0 and t >= WINDOW + QBLK
    n_sel = -(-t // SEL_BLK)
    nsp = max(8, -(-n_sel // 8) * 8)
    m_t = _overlap_matrix(n_cmp, nb, n_sel, nsp).T
    kspec = lambda: pl.BlockSpec((1, t, HEAD_DIM), lambda i, h, j: (2 * h, i, 0))
    vspec = lambda: pl.BlockSpec((1, t, HEAD_DIM), lambda i, h, j: (2 * h + 1, i, 0))
    cols = GROUP * QBLK
    head_cols = GROUP * HEAD_DIM
    tile = lambda off: pl.BlockSpec((1, QBLK, head_cols), lambda i, h, j: (i, j, off + h))
    return pl.pallas_call(
        functools.partial(_attn_prompt_s_body, n_cmp=n_cmp, n_sel=n_sel),
        grid=(b, N_KV_HEADS, t // QBLK),
        in_specs=[
            tile(0),
            kspec(), vspec(), kspec(), vspec(),
            pl.BlockSpec((1, 2, nb, HEAD_DIM), lambda i, h, j: (i, h, 0, 0)),
            pl.BlockSpec((1, QBLK, LANES), lambda i, h, j: (i, j, h)),
            tile(0), tile(N_KV_HEADS), tile(0),
            pl.BlockSpec((nsp, nb), lambda i, h, j: (0, 0)),
        ],
        out_specs=tile(0),
        out_shape=jax.ShapeDtypeStruct((b, t, N_HEADS * HEAD_DIM), BF16),
        scratch_shapes=[pltpu.VMEM((HEAD_DIM, t), BF16), pltpu.VMEM((HEAD_DIM, t), BF16),
                        pltpu.VMEM((nb, HEAD_DIM), BF16), pltpu.VMEM((HEAD_DIM, nb), BF16),
                        pltpu.VMEM((t, cols), F32), pltpu.VMEM((t, cols), BF16), pltpu.VMEM((nsp, cols), F32)],
        compiler_params=_params(("arbitrary", "arbitrary", "arbitrary"), 48),
        name="attn_prompt",
    )(q3, sel_planes, sel_planes, win_planes, win_planes, kvc, ubg3, umg3, umg3, o_rnn3, jnp.asarray(m_t, BF16))


def _row_head(shape):
    return lax.shift_right_logical(lax.broadcasted_iota(I32, shape, 0), 2)


def _cmp_sample_body(q_ref, kvc_ref, m_ref, oc_ref, imp_ref, *, qpos, n_cmp):
    qb = q_ref[0].astype(BF16)
    nb = kvc_ref.shape[2]
    rg = _row_head((N_HEADS, nb))
    ncol = lax.broadcasted_iota(I32, (N_HEADS, nb), 1)
    mask = (ncol * CMP_STRIDE + (CMP_BLK - 1) <= qpos) & (ncol < n_cmp)
    row8 = lax.broadcasted_iota(I32, (8, nb), 0)
    o = jnp.zeros((N_HEADS, HEAD_DIM), F32)
    ps8 = jnp.zeros((8, nb), F32)
    for h in range(N_KV_HEADS):
        kc = kvc_ref[0, 2 * h].astype(BF16)
        vc = kvc_ref[0, 2 * h + 1].astype(BF16)
        s = lax.dot_general(qb, kc, _NT, preferred_element_type=F32) * ATTN_SCALE
        ph = jnp.where(rg == h, _msoftmax(s, mask), 0.0)
        o = o + jnp.dot(ph.astype(BF16), vc, preferred_element_type=F32)
        ps8 = jnp.where(row8 == h, jnp.sum(ph, axis=0, keepdims=True), ps8)
    oc_ref[0] = o
    imp_ref[0] = jnp.dot(ps8.astype(BF16), m_ref[...], preferred_element_type=F32)[0:N_KV_HEADS]


def cmp_sample(q16, kvc, m_mat, qpos, n_cmp):
    ns = q16.shape[0]
    nb = kvc.shape[2]
    nsp = m_mat.shape[1]
    return pl.pallas_call(
        functools.partial(_cmp_sample_body, qpos=qpos, n_cmp=n_cmp),
        grid=(ns,),
        in_specs=[pl.BlockSpec((1, N_HEADS, HEAD_DIM), lambda b: (b, 0, 0)),
                  pl.BlockSpec((1, HK, nb, HEAD_DIM), lambda b: (b, 0, 0, 0)),
                  pl.BlockSpec((nb, nsp), lambda b: (0, 0))],
        out_specs=[pl.BlockSpec((1, N_HEADS, HEAD_DIM), lambda b: (b, 0, 0)),
                   pl.BlockSpec((1, N_KV_HEADS, nsp), lambda b: (b, 0, 0))],
        out_shape=[jax.ShapeDtypeStruct((ns, N_HEADS, HEAD_DIM), F32),
                   jax.ShapeDtypeStruct((ns, N_KV_HEADS, nsp), F32)],
        compiler_params=_params(("arbitrary",), 40),
        name="cmp_sample",
    )(q16, kvc, m_mat)


def _rank_sample_body(imp_ref, idx_ref, sc_scr, *, cur, n_sel):
    npad, nr = sc_scr.shape
    st = imp_ref[...].T
    j = lax.broadcasted_iota(I32, (npad, nr), 0)
    real = j < n_sel
    valid = (j <= cur) & real
    forced = (j == 0) | (j >= cur - 1)
    score = jnp.where(forced, FORCED_SCORE, st)
    score = jnp.where(valid, score, -1.0)
    score = jnp.where(real, score, -2.0)
    sc_scr[...] = score

    def body(k, rank):
        sk = sc_scr[pl.ds(k, 1), :]
        beats = (sk > score) | ((sk == score) & (k < j))
        return rank + beats.astype(I32)

    rank = lax.fori_loop(0, n_sel, body, jnp.zeros((npad, nr), I32))
    sel = (rank < N_SEL) & valid
    for slot in range(N_SEL):
        hit = sel & (rank == slot)
        found = jnp.max(hit.astype(I32), axis=0, keepdims=True)
        val = jnp.sum(jnp.where(hit, j, 0), axis=0, keepdims=True)
        idx_ref[slot:slot + 1, :] = jnp.where(found > 0, val, -1)


def rank_sample(imp2, cur, n_sel):
    nr, npad = imp2.shape
    return pl.pallas_call(
        functools.partial(_rank_sample_body, cur=cur, n_sel=n_sel),
        out_shape=jax.ShapeDtypeStruct((N_SEL, nr), I32),
        scratch_shapes=[pltpu.VMEM((npad, nr), F32)],
        name="rank_sample",
    )(imp2)


def _sel_sample_body(idx_ref, pt_ref, q_ref, kn_ref, vn_ref, *rest, n_past_blocks):
    del pt_ref
    kvs = rest[:N_SEL * N_KV_HEADS]
    o_ref = rest[N_SEL * N_KV_HEADS]
    b = pl.program_id(0)
    q = q_ref[0]
    qb = q.astype(BF16)
    nk = N_SEL * SEL_BLK
    rg = _row_head((N_HEADS, nk))
    col_slot = lax.shift_right_logical(lax.broadcasted_iota(I32, (N_HEADS, nk), 1), SEL_SHIFT)

    sc = jnp.zeros((N_HEADS, nk), F32)
    okv = jnp.zeros((N_HEADS, nk), I32)
    for h in range(N_KV_HEADS):
        kcat = jnp.concatenate([kvs[s * N_KV_HEADS + h][:, 0, 0, :] for s in range(N_SEL)], axis=0).astype(BF16)
        sh = lax.dot_general(qb, kcat, _NT, preferred_element_type=F32)
        ok = jnp.zeros((N_HEADS, nk), I32)
        for s in range(N_SEL):
            blk = idx_ref[s, b * N_KV_HEADS + h]
            ok = jnp.where(col_slot == s, jnp.where((blk >= 0) & (blk < n_past_blocks), 1, 0), ok)
        sc = jnp.where(rg == h, sh, sc)
        okv = jnp.where(rg == h, ok, okv)
    sm = jnp.where(okv > 0, sc * ATTN_SCALE, MASK_VALUE)
    s_new = jnp.sum(q * kn_ref[0], axis=-1, keepdims=True) * ATTN_SCALE
    m = jnp.maximum(jnp.max(sm, axis=-1, keepdims=True), s_new)
    p = jnp.where(okv > 0, jnp.exp(sm - m), 0.0)
    p_new = jnp.exp(s_new - m)
    den = jnp.sum(p, axis=-1, keepdims=True) + p_new
    pv = jnp.zeros((N_HEADS, HEAD_DIM), F32)
    for h in range(N_KV_HEADS):
        vcat = jnp.concatenate([kvs[s * N_KV_HEADS + h][:, 0, 1, :] for s in range(N_SEL)], axis=0).astype(BF16)
        pv = pv + jnp.dot(jnp.where(rg == h, p, 0.0).astype(BF16), vcat, preferred_element_type=F32)
    o_ref[0] = (pv + p_new * vn_ref[0]) / den


def sel_sample(idx, page_table, q16, kn16, vn16, pool4, n_past_blocks):
    ns = q16.shape[0]
    blk_per_page = PAGE_SIZE // SEL_BLK

    def kv_spec(s, h):
        def imap(b, idx_r, pt_r):
            blk = jnp.clip(idx_r[s, b * N_KV_HEADS + h], 0, n_past_blocks - 1)
            return (pt_r[b, blk // blk_per_page] * blk_per_page + blk % blk_per_page, h, 0, 0)
        return pl.BlockSpec((SEL_BLK, 1, 2, HEAD_DIM), imap)

    head3 = lambda: pl.BlockSpec((1, N_HEADS, HEAD_DIM), lambda b, i, p: (b, 0, 0))
    kv_specs = [kv_spec(s, h) for s in range(N_SEL) for h in range(N_KV_HEADS)]
    grid_spec = pltpu.PrefetchScalarGridSpec(
        num_scalar_prefetch=2,
        grid=(ns,),
        in_specs=[head3(), head3(), head3()] + kv_specs,
        out_specs=head3(),
    )
    return pl.pallas_call(
        functools.partial(_sel_sample_body, n_past_blocks=n_past_blocks),
        grid_spec=grid_spec,
        out_shape=jax.ShapeDtypeStruct((ns, N_HEADS, HEAD_DIM), F32),
        compiler_params=_params(("arbitrary",), 40),
        name="sel_sample",
    )(idx, page_table, q16, kn16, vn16, *([pool4] * len(kv_specs)))


def _win_sample_body(q_ref, win_ref, new_ref, kn_ref, vn_ref, oc_ref, os_ref, bg_ref, o_ref, wout_ref, *, wbuf):
    wout_ref[0:(wbuf - 1) * HK, :] = win_ref[HK:wbuf * HK, :]
    wout_ref[(wbuf - 1) * HK:wbuf * HK, :] = new_ref[...]
    q = q_ref[0]
    qb = q.astype(BF16)
    rg = _row_head((N_HEADS, wbuf))
    sc = jnp.zeros((N_HEADS, wbuf), F32)
    for h in range(N_KV_HEADS):
        kh = win_ref[pl.ds(2 * h, wbuf, stride=HK), :].astype(BF16)
        sc = jnp.where(rg == h, lax.dot_general(qb, kh, _NT, preferred_element_type=F32), sc)
    mask = lax.broadcasted_iota(I32, (N_HEADS, wbuf), 1) > wbuf - WINDOW
    sm = jnp.where(mask, sc * ATTN_SCALE, MASK_VALUE)
    s_new = jnp.sum(q * kn_ref[0], axis=-1, keepdims=True) * ATTN_SCALE
    m = jnp.maximum(jnp.max(sm, axis=-1, keepdims=True), s_new)
    p = jnp.where(mask, jnp.exp(sm - m), 0.0)
    p_new = jnp.exp(s_new - m)
    den = jnp.sum(p, axis=-1, keepdims=True) + p_new
    pv = jnp.zeros((N_HEADS, HEAD_DIM), F32)
    for h in range(N_KV_HEADS):
        vh = win_ref[pl.ds(2 * h + 1, wbuf, stride=HK), :].astype(BF16)
        pv = pv + jnp.dot(jnp.where(rg == h, p, 0.0).astype(BF16), vh, preferred_element_type=F32)
    o_w = (pv + p_new * vn_ref[0]) / den
    gates = jax.nn.sigmoid(bg_ref[0])
    o_ref[0] = gates[:, 0:1] * oc_ref[0] + gates[:, 1:2] * os_ref[0] + gates[:, 2:3] * o_w


def win_sample(q16, win2d, new2d, kn16, vn16, o_cmp, o_sel, bgs, wbuf):
    ns = q16.shape[0]
    head3 = lambda: pl.BlockSpec((1, N_HEADS, HEAD_DIM), lambda b: (b, 0, 0))
    wspec = lambda: pl.BlockSpec((wbuf * HK, LANES), lambda b: (b, 0))
    return pl.pallas_call(
        functools.partial(_win_sample_body, wbuf=wbuf),
        grid=(ns,),
        in_specs=[head3(), wspec(), pl.BlockSpec((HK, LANES), lambda b: (b, 0)), head3(), head3(), head3(), head3(),
                  head3()],
        out_specs=[head3(), wspec()],
        out_shape=[jax.ShapeDtypeStruct((ns, N_HEADS, HEAD_DIM), F32),
                   jax.ShapeDtypeStruct((ns * wbuf * HK, LANES), F32)],
        compiler_params=_params(("arbitrary",), 40),
        name="win_sample",
    )(q16, win2d, new2d, kn16, vn16, o_cmp, o_sel, bgs)


ROUTE_LANE0 = N_GROUPS


INFO_E1, INFO_E2, INFO_R1, INFO_R2, INFO_C1, INFO_C2 = range(6)


def _route_body(xp_ref, xs_ref, g_ref, wr_ref, br_ref, xn_ref, info_ref, cnt_ref, run_scr, *, n_real, n_ptiles):
    step = pl.program_id(0)

    @pl.when(step == 0)
    def _():
        run_scr[...] = jnp.zeros(run_scr.shape, F32)

    refs = (g_ref, wr_ref, br_ref, xn_ref, info_ref, cnt_ref, run_scr)

    @pl.when(step < n_ptiles)
    def _():
        _route_tile(xp_ref[...], step, n_real, *refs)

    @pl.when(step >= n_ptiles)
    def _():
        _route_tile(xs_ref[...], step, n_real, *refs)


def _route_tile(x, step, n_real, g_ref, wr_ref, br_ref, xn_ref, info_ref, cnt_ref, run_scr):
    tm = x.shape[0]
    var = jnp.mean(x * x, axis=-1, keepdims=True)
    xn = (x * lax.rsqrt(var + EPS)) * g_ref[...]
    xn_ref[...] = xn
    xnb = xn.astype(BF16)
    logits = jnp.dot(xnb, wr_ref[...], preferred_element_type=F32) + br_ref[...]
    lane = lax.broadcasted_iota(I32, logits.shape, 1)
    big = 4 * LANES

    isg = lane < N_GROUPS
    lg = jnp.where(isg, logits, -jnp.inf)
    mg = jnp.max(lg, axis=-1, keepdims=True)
    gi = jnp.min(jnp.where(lg == mg, lane, big), axis=-1, keepdims=True)
    pg = 1.0 / jnp.sum(jnp.where(isg, jnp.exp(lg - mg), 0.0), axis=-1, keepdims=True)

    lo = ROUTE_LANE0 + gi * EXP_PER_GROUP
    ing = (lane >= lo) & (lane < lo + EXP_PER_GROUP)
    le = jnp.where(ing, logits, -jnp.inf)
    me = jnp.max(le, axis=-1, keepdims=True)
    ee = jnp.where(ing, jnp.exp(le - me), 0.0)
    pe = jnp.where(ing, ee / jnp.sum(ee, axis=-1, keepdims=True), -1.0)
    p1 = jnp.max(pe, axis=-1, keepdims=True)
    i1 = jnp.min(jnp.where(pe == p1, lane, big), axis=-1, keepdims=True)
    pe2 = jnp.where(lane == i1, -1.0, pe)
    p2 = jnp.max(pe2, axis=-1, keepdims=True)
    i2 = jnp.min(jnp.where(pe2 == p2, lane, big), axis=-1, keepdims=True)
    tot = p1 + p2
    c1 = pg * (p1 / tot)
    c2 = pg * (p2 / tot)

    row = step * tm + lax.broadcasted_iota(I32, logits.shape, 0)
    hit = jnp.where(((lane == i1) | (lane == i2)) & (row < n_real), 1.0, 0.0)
    tri = jnp.where(lax.broadcasted_iota(I32, (tm, tm), 1) < lax.broadcasted_iota(I32, (tm, tm), 0), 1.0, 0.0)
    before = run_scr[...] + jnp.dot(tri.astype(BF16), hit.astype(BF16), preferred_element_type=F32)
    r1 = jnp.sum(jnp.where(lane == i1, before, 0.0), axis=-1, keepdims=True)
    r2 = jnp.sum(jnp.where(lane == i2, before, 0.0), axis=-1, keepdims=True)
    run_new = run_scr[...] + jnp.sum(hit, axis=0, keepdims=True)
    run_scr[...] = run_new
    cnt_ref[...] = run_new

    info = jnp.zeros(logits.shape, F32)
    for ln, val in ((INFO_E1, (i1 - ROUTE_LANE0).astype(F32)), (INFO_E2, (i2 - ROUTE_LANE0).astype(F32)),
                    (INFO_R1, r1), (INFO_R2, r2), (INFO_C1, c1), (INFO_C2, c2)):
        info = jnp.where(lane == ln, val, info)
    info_ref[...] = info


MOE_TILE = 256


def route(x_prompt, x_sample, g, wr, br, n_real):
    mp, d = x_prompt.shape
    n_ptiles = mp // MOE_TILE
    assert mp % MOE_TILE == 0 and x_sample.shape[0] == MOE_TILE
    m = mp + MOE_TILE
    return pl.pallas_call(
        functools.partial(_route_body, n_real=n_real, n_ptiles=n_ptiles),
        grid=(n_ptiles + 1,),
        in_specs=[pl.BlockSpec((MOE_TILE, d), lambda i: (jnp.minimum(i, n_ptiles - 1), 0)),
                  pl.BlockSpec((MOE_TILE, d), lambda i: (0, 0)), pl.BlockSpec((1, d), lambda i: (0, 0)),
                  pl.BlockSpec((d, LANES), lambda i: (0, 0)), pl.BlockSpec((1, LANES), lambda i: (0, 0))],
        out_specs=[pl.BlockSpec((MOE_TILE, d), lambda i: (i, 0)), pl.BlockSpec((MOE_TILE, LANES), lambda i: (i, 0)),
                   pl.BlockSpec((1, LANES), lambda i: (0, 0))],
        out_shape=[jax.ShapeDtypeStruct((m, d), F32), jax.ShapeDtypeStruct((m, LANES), F32),
                   jax.ShapeDtypeStruct((1, LANES), F32)],
        scratch_shapes=[pltpu.VMEM((1, LANES), F32)],
        compiler_params=_params(("arbitrary",), 40),
        name="route",
    )(x_prompt, x_sample, g.reshape(1, d), wr, br)


def _moe_plan(info, counts_row, n_real, n_tiles):
    n_pad = info.shape[0]
    e = jnp.clip(info[:, INFO_E1:INFO_E2 + 1].astype(I32), 0, N_EXPERTS - 1)
    r = info[:, INFO_R1:INFO_R2 + 1].astype(I32)
    counts = counts_row[0, ROUTE_LANE0:ROUTE_LANE0 + N_EXPERTS].astype(I32)
    padded = ((counts + MOE_TILE - 1) // MOE_TILE) * MOE_TILE
    ex = jnp.arange(N_EXPERTS, dtype=I32)
    ends = jnp.sum(jnp.where(ex[None, :] <= ex[:, None], padded[None, :], 0), axis=1)
    offs = ends - padded
    tok = jnp.arange(n_pad, dtype=I32)
    valid = (tok < n_real)[:, None]
    dest = jnp.where(valid, offs[e] + r, 0)
    n_slots = n_tiles * MOE_TILE
    src = jnp.zeros((n_slots,), I32).at[jnp.where(valid, dest, n_slots).reshape(-1)].set(
        jnp.repeat(tok, 2), mode="drop")
    n_used = ends[N_EXPERTS - 1] // MOE_TILE
    tiles = jnp.arange(n_tiles, dtype=I32)
    used = tiles < n_used
    te = jnp.minimum(jnp.sum((ends[None, :] <= (tiles * MOE_TILE)[:, None]).astype(I32), axis=1), N_EXPERTS - 1)
    te_prev = jnp.concatenate([te[:1] - 1, te[:-1]])
    first = (te != te_prev) & used
    later = tiles[None, :] > tiles[:, None]
    run = jnp.sum((first[None, :] & ~later).astype(I32), axis=1) - 1
    nxt_tile = jnp.min(jnp.where(first[None, :] & later, tiles[None, :], n_tiles), axis=1)
    nxt = jnp.where(nxt_tile < n_tiles, te[jnp.minimum(nxt_tile, n_tiles - 1)], -1)
    tile3 = lambda v: v.reshape(-1, 1, MOE_TILE)
    return (tile3(src), tile3(dest[:, 0]), tile3(dest[:, 1]),
            (te, first.astype(I32), nxt, jnp.bitwise_and(run, 1), n_used.reshape(1)))


def _issue_rows(idx_ref, src_hbm, dst_buf, slot, sem):
    n = idx_ref.shape[2]
    group = 8

    def body(g, carry):
        for j in range(group):
            r = g * group + j
            pltpu.make_async_copy(src_hbm.at[pl.ds(idx_ref[0, 0, r], 1), :], dst_buf.at[slot, pl.ds(r, 1), :],
                                  sem.at[slot]).start(priority=j % 2)
        return carry

    lax.fori_loop(0, n // group, body, 0)


def _wait_rows(src_hbm, dst_buf, slot, sem):
    n = dst_buf.shape[1]
    pltpu.make_async_copy(src_hbm.at[pl.ds(0, n), :], dst_buf.at[slot], sem.at[slot]).wait()


def _moe_dispatch_body(nu_ref, src_ref, xn_hbm, xs_hbm, sem):
    i = pl.program_id(0)
    n_used = nu_ref[0]
    t = src_ref.shape[2]
    group = 8

    def wait_tile(tile, slot):
        pltpu.make_async_copy(xn_hbm.at[pl.ds(0, t), :], xs_hbm.at[pl.ds(tile * t, t), :], sem.at[slot]).wait()

    slot = lax.rem(i, 2)

    @pl.when(i < n_used)
    def _():
        def body(g, carry):
            for j in range(group):
                r = g * group + j
                pltpu.make_async_copy(xn_hbm.at[pl.ds(src_ref[0, 0, r], 1), :], xs_hbm.at[pl.ds(i * t + r, 1), :],
                                      sem.at[slot]).start(priority=j % 2)
            return carry

        lax.fori_loop(0, t // group, body, 0)

    @pl.when(i >= n_used)
    def _():
        pltpu.make_async_copy(xn_hbm.at[pl.ds(0, t), :], xs_hbm.at[pl.ds(i * t, t), :], sem.at[slot]).start()

    @pl.when(i > 0)
    def _():
        wait_tile(i - 1, 1 - slot)

    @pl.when(i == pl.num_programs(0) - 1)
    def _():
        wait_tile(i, slot)


def moe_dispatch(n_used, src3, xn):
    n_tiles = src3.shape[0]
    d = xn.shape[1]
    grid_spec = pltpu.PrefetchScalarGridSpec(
        num_scalar_prefetch=1,
        grid=(n_tiles,),
        in_specs=[pl.BlockSpec((1, 1, MOE_TILE), lambda i, nu: (i, 0, 0), memory_space=pltpu.SMEM),
                  pl.BlockSpec(memory_space=pl.ANY)],
        out_specs=pl.BlockSpec(memory_space=pl.ANY),
        scratch_shapes=[pltpu.SemaphoreType.DMA((2,))],
    )
    return pl.pallas_call(
        _moe_dispatch_body,
        grid_spec=grid_spec,
        out_shape=jax.ShapeDtypeStruct((n_tiles * MOE_TILE, d), F32),
        compiler_params=_params(("arbitrary",), 16),
        name="moe_dispatch",
    )(n_used, src3, xn)


def _moe_expert_body(te_ref, first_ref, nxt_ref, par_ref, nu_ref, x_ref, wg_hbm, wu_hbm, wd_hbm,
                     y_ref, wg_buf, wu_buf, wd_buf, wsem, wgb, wub, wdb):
    i = pl.program_id(0)
    n_used = nu_ref[0]

    def weight_copies(e, ws):
        return (pltpu.make_async_copy(wg_hbm.at[e], wg_buf.at[ws], wsem.at[0, ws]),
                pltpu.make_async_copy(wu_hbm.at[e], wu_buf.at[ws], wsem.at[1, ws]),
                pltpu.make_async_copy(wd_hbm.at[e], wd_buf.at[ws], wsem.at[2, ws]))

    @pl.when(i == 0)
    def _():
        for cp in weight_copies(te_ref[0], par_ref[0]):
            cp.start()

    @pl.when(i < n_used)
    def _():
        @pl.when(first_ref[i] == 1)
        def _():
            ws = par_ref[i]
            for cp in weight_copies(te_ref[i], ws):
                cp.wait()

            @pl.when(nxt_ref[i] >= 0)
            def _():
                for cp in weight_copies(nxt_ref[i], 1 - ws):
                    cp.start()

            wgb[...] = wg_buf[ws].astype(BF16)
            wub[...] = wu_buf[ws].astype(BF16)
            wdb[...] = wd_buf[ws].astype(BF16)

        x = x_ref[...].astype(BF16)
        hg = jnp.dot(x, wgb[...], preferred_element_type=F32)
        hu = jnp.dot(x, wub[...], preferred_element_type=F32)
        hid = (hg * jax.nn.sigmoid(hg)) * hu
        y_ref[...] = jnp.dot(hid.astype(BF16), wdb[...], preferred_element_type=F32)

    @pl.when(i >= n_used)
    def _():
        y_ref[...] = jnp.zeros(y_ref.shape, F32)


def moe_experts(tile_plan, xs, wg, wu, wd):
    rows, d = xs.shape
    n_tiles = rows // MOE_TILE
    _, _, de = wg.shape
    hbm = lambda: pl.BlockSpec(memory_space=pl.ANY)
    grid_spec = pltpu.PrefetchScalarGridSpec(
        num_scalar_prefetch=5,
        grid=(n_tiles,),
        in_specs=[pl.BlockSpec((MOE_TILE, d), lambda i, te, fi, nx, pa, nu: (jnp.minimum(i, nu[0] - 1), 0)),
                  hbm(), hbm(), hbm()],
        out_specs=pl.BlockSpec((MOE_TILE, d), lambda i, *_: (i, 0)),
        scratch_shapes=[pltpu.VMEM((2, d, de), F32), pltpu.VMEM((2, d, de), F32), pltpu.VMEM((2, de, d), F32),
                        pltpu.SemaphoreType.DMA((3, 2)),
                        pltpu.VMEM((d, de), BF16), pltpu.VMEM((d, de), BF16), pltpu.VMEM((de, d), BF16)],
    )
    return pl.pallas_call(
        _moe_expert_body,
        grid_spec=grid_spec,
        out_shape=jax.ShapeDtypeStruct((rows, d), F32),
        compiler_params=_params(("arbitrary",), 56),
        name="moe_experts",
    )(*tile_plan, xs, wg, wu, wd)


def _moe_combine_body(d1_cur, d1_nxt, d2_cur, d2_nxt, ys_hbm, xp_ref, xs_ref, info_ref, gf_ref, yp_ref, ysm_ref,
                      buf1, buf2, sem1, sem2, *, n_ptiles):
    i = pl.program_id(0)
    slot = lax.rem(i, 2)

    @pl.when(i == 0)
    def _():
        _issue_rows(d1_cur, ys_hbm, buf1, 0, sem1)
        _issue_rows(d2_cur, ys_hbm, buf2, 0, sem2)

    @pl.when(i + 1 < pl.num_programs(0))
    def _():
        _issue_rows(d1_nxt, ys_hbm, buf1, 1 - slot, sem1)
        _issue_rows(d2_nxt, ys_hbm, buf2, 1 - slot, sem2)

    _wait_rows(ys_hbm, buf1, slot, sem1)
    _wait_rows(ys_hbm, buf2, slot, sem2)

    def finish(x_ref, y_ref):
        info = info_ref[...]
        xo = x_ref[...] + (info[:, INFO_C1:INFO_C1 + 1] * buf1[slot] + info[:, INFO_C2:INFO_C2 + 1] * buf2[slot])
        var = jnp.mean(xo * xo, axis=-1, keepdims=True)
        y_ref[...] = (xo * lax.rsqrt(var + EPS)) * gf_ref[...]

    @pl.when(i < n_ptiles)
    def _():
        finish(xp_ref, yp_ref)

    @pl.when(i >= n_ptiles)
    def _():
        finish(xs_ref, ysm_ref)


def moe_combine(dest1, dest2, ys, x_prompt, x_sample, info, g_final):
    mp, d = x_prompt.shape
    n_ptiles = mp // MOE_TILE
    n_tiles = n_ptiles + 1
    cur = lambda: pl.BlockSpec((1, 1, MOE_TILE), lambda i: (i, 0, 0), memory_space=pltpu.SMEM)
    nxt = lambda: pl.BlockSpec((1, 1, MOE_TILE), lambda i: (jnp.minimum(i + 1, n_tiles - 1), 0, 0),
                               memory_space=pltpu.SMEM)
    ptile = lambda: pl.BlockSpec((MOE_TILE, d), lambda i: (jnp.minimum(i, n_ptiles - 1), 0))
    stile = lambda: pl.BlockSpec((MOE_TILE, d), lambda i: (0, 0))
    return pl.pallas_call(
        functools.partial(_moe_combine_body, n_ptiles=n_ptiles),
        grid=(n_tiles,),
        in_specs=[cur(), nxt(), cur(), nxt(), pl.BlockSpec(memory_space=pl.ANY), ptile(), stile(),
                  pl.BlockSpec((MOE_TILE, LANES), lambda i: (i, 0)), pl.BlockSpec((1, d), lambda i: (0, 0))],
        out_specs=[ptile(), stile()],
        out_shape=[jax.ShapeDtypeStruct((mp, d), F32), jax.ShapeDtypeStruct((MOE_TILE, d), F32)],
        scratch_shapes=[pltpu.VMEM((2, MOE_TILE, d), F32), pltpu.VMEM((2, MOE_TILE, d), F32),
                        pltpu.SemaphoreType.DMA((2,)), pltpu.SemaphoreType.DMA((2,))],
        compiler_params=_params(("arbitrary",), 40),
        name="moe_combine",
    )(dest1, dest1, dest2, dest2, ys, x_prompt, x_sample, info, g_final.reshape(1, d))


def _mixer_inputs(x2, g_mix, w_in_t, w_bg_t, w_mg_t, tm, q_dtype):
    tn = PROJ_TILE_N
    xn = rmsnorm(x2, g_mix, tm, BF16)
    mm = functools.partial(matmul_cols, xn, tm=tm, w_is_t=True)
    u_rnn = mm(w_in_t, 0, D_RNN, tn=tn, out_dtype=F32, name="in_rnn")
    u_gate = mm(w_in_t, D_RNN, D_RNN, tn=tn, out_dtype=F32, name="in_gate")
    q = mm(w_in_t, 2 * D_RNN, N_HEADS * HEAD_DIM, tn=tn, out_dtype=q_dtype, name="in_q")
    kv0 = 2 * D_RNN + N_HEADS * HEAD_DIM
    kv_cmp, = matmul_kv(xn, w_in_t, kv0, tm, False, "in_kv_cmp")
    kv_sel, sel_planes = matmul_kv(xn, w_in_t, kv0 + 2 * KV_DIM, tm, True, "in_kv_sel")
    kv_win, win_planes = matmul_kv(xn, w_in_t, kv0 + 4 * KV_DIM, tm, True, "in_kv_win")
    u_bg = mm(w_bg_t, 0, w_bg_t.shape[0], tn=w_bg_t.shape[0], out_dtype=F32, name="in_bg")
    u_mg = mm(w_mg_t, 0, 2 * D_MODEL, tn=tn, out_dtype=F32, name="in_mg")
    return u_rnn, u_gate, q, (kv_cmp, kv_sel, kv_win), (sel_planes, win_planes), u_bg, u_mg


def kernel(x_prompt, x_sample, cache_cmp_kv, cache_sel_kv, cache_win_kv, state_conv, state_rglru, page_table, g_mix, w_in, conv_w, conv_b, lru_w_a, lru_b_a, lru_w_x, lru_b_x, lru_lambda, cmp_pe, cmp_w1, cmp_b1, cmp_w2, cmp_b2, w_out, g_ffn, moe_w_group, moe_b_group, moe_w_expert, moe_b_expert, moe_w_gate, moe_w_up, moe_w_down, g_final):
    b, t, d = x_prompt.shape
    ns = x_sample.shape[0]
    npages = page_table.shape[1]
    past_len = npages * PAGE_SIZE
    wbuf = cache_win_kv.shape[1]
    kvshape = (N_KV_HEADS, 2, HEAD_DIM)

    w_in_t = w_in.T
    bg0 = 2 * D_RNN + N_HEADS * HEAD_DIM + 6 * KV_DIM
    n_bg = 3 * N_HEADS
    w_bg = w_in_t[bg0:bg0 + n_bg].reshape(3, N_KV_HEADS, GROUP, d).transpose(1, 0, 2, 3).reshape(N_KV_HEADS, 3 * GROUP, d)
    w_bg = jnp.pad(w_bg, ((0, 0), (0, LANES - 3 * GROUP), (0, 0))).reshape(N_KV_HEADS * LANES, d)
    w_mg = w_in_t[bg0 + n_bg:]
    wa = lru_w_a.astype(BF16)
    wx = lru_w_x.astype(BF16)
    half_rows = CMP_STRIDE * HEAD_DIM
    wc = jnp.concatenate([cmp_w1[:, :half_rows], cmp_w1[:, half_rows:]], axis=-1).astype(BF16)
    pe8 = jnp.pad(cmp_pe.reshape(2, 2, half_rows), ((0, 0), (0, 6), (0, 0)))
    b1 = cmp_b1.reshape(2, 1, CMP_HIDDEN)
    w2 = cmp_w2.astype(BF16)
    b2 = cmp_b2.reshape(2, 1, HEAD_DIM)
    n_route = N_GROUPS + N_EXPERTS
    wr = jnp.pad(jnp.concatenate([moe_w_group, moe_w_expert], axis=1), ((0, 0), (0, LANES - n_route))).astype(BF16)
    br = jnp.pad(jnp.concatenate([moe_b_group, moe_b_expert]), (0, LANES - n_route)).reshape(1, LANES)
    lru_args = (conv_w, conv_b, wa, wx, lru_b_a, lru_b_x, lru_lambda)
    cmp_args = (wc, pe8, b1, w2, b2)

    xp2 = x_prompt.reshape(b * t, d)
    u_rnn, u_gate, q, kv2d, planes, u_bg, u_mg = _mixer_inputs(xp2, g_mix, w_in_t, w_bg, w_mg, PROJ_TILE_M, BF16)
    o_rnn, p_conv, p_h = lru_prompt(u_rnn.reshape(b, t, d), u_gate.reshape(b, t, d), *lru_args, LRU_TILE_T)
    p_cmp, p_sel, p_win_full = (a.reshape((b, t) + kvshape) for a in kv2d)
    p_win = p_win_full[:, t - min(WINDOW, t):]
    pt_prompt = jnp.arange(b * t // PAGE_SIZE, dtype=I32).reshape(b, t // PAGE_SIZE)
    kvc_p = compress(kv2d[0], pt_prompt, jnp.zeros((b * CMP_STRIDE * HK, HEAD_DIM), F32), *cmp_args)
    merged = attn_prompt_s(q.reshape(b, t, -1), planes[0], planes[1], kvc_p, u_bg.reshape(b, t, -1),
                           u_mg.reshape(b, t, -1), o_rnn, t // CMP_STRIDE - 1)
    x_mid = matmul_cols(merged.reshape(b * t, d), w_out, 0, d, PROJ_TILE_M, PROJ_TILE_N, F32, residual=xp2,
                        name="out_proj")

    xs2 = x_sample.reshape(ns, d)
    su_rnn, su_gate, sq, skv2d, _, su_bg, su_mg = _mixer_inputs(xs2, g_mix, w_in_t, w_bg, w_mg, ns, F32)
    so_rnn, cn, s_h = lru_sample(su_rnn, su_gate, state_conv.transpose(1, 0, 2), state_rglru, *lru_args)
    s_conv = cn.transpose(1, 0, 2)
    kvs = jnp.stack([a.reshape(ns, HK, HEAD_DIM) for a in skv2d], axis=1)
    s_cmp = kvs[:, 0].reshape((ns, 1) + kvshape)
    s_sel = kvs[:, 1].reshape((ns, 1) + kvshape)

    extra = jnp.concatenate([kvs[:, 0], jnp.zeros((ns, (CMP_STRIDE - 1) * HK, HEAD_DIM), F32)], axis=1)
    kvc_s = compress(cache_cmp_kv.reshape(-1, HEAD_DIM), page_table, extra.reshape(-1, HEAD_DIM), *cmp_args)
    n_cmp_s = -(-(past_len + 1) // CMP_STRIDE) - 1
    n_sel_s = -(-(past_len + 1) // SEL_BLK)
    nsp = -(-n_sel_s // LANES) * LANES
    q16 = sq.reshape(ns, N_HEADS, HEAD_DIM)
    m_s = jnp.asarray(_overlap_matrix(n_cmp_s, kvc_s.shape[2], n_sel_s, nsp), BF16)
    so_cmp, imp = cmp_sample(q16, kvc_s, m_s, past_len, n_cmp_s)
    idx = rank_sample(imp.reshape(ns * N_KV_HEADS, nsp), past_len // SEL_BLK, n_sel_s)

    def per_head_rows(new_kv, kv):
        return jnp.repeat(new_kv[:, kv::2], GROUP, axis=1)

    so_sel = sel_sample(idx, page_table, q16, per_head_rows(kvs[:, 1], 0), per_head_rows(kvs[:, 1], 1),
                        cache_sel_kv.reshape((-1,) + kvshape), past_len // SEL_BLK)
    bgs = su_bg.reshape(ns, N_KV_HEADS, LANES)[:, :, :3 * GROUP].reshape(ns, N_KV_HEADS, 3, GROUP)
    bgs = jnp.pad(bgs.transpose(0, 1, 3, 2).reshape(ns, N_HEADS, 3), ((0, 0), (0, 0), (0, LANES - 3)))
    so_nsa, s_win2d = win_sample(q16, cache_win_kv.reshape(-1, HEAD_DIM), skv2d[2], per_head_rows(kvs[:, 2], 0),
                                 per_head_rows(kvs[:, 2], 1), so_cmp, so_sel, bgs, wbuf)
    s_win = s_win2d.reshape((ns, wbuf) + kvshape)
    s_merged = merge(su_mg, so_rnn, so_nsa.reshape(ns, d), ns)
    sx_mid = matmul_cols(s_merged, w_out, 0, d, ns, PROJ_TILE_N, F32, residual=xs2, name="out_proj_s")

    n_real = b * t + ns
    assert ns <= MOE_TILE
    sx_pad = jnp.pad(sx_mid, ((0, MOE_TILE - ns), (0, 0)))
    xn_all, info, counts = route(x_mid, sx_pad, g_ffn, wr, br, n_real)
    n_tiles = -(-(2 * n_real + N_EXPERTS * (MOE_TILE - 1)) // MOE_TILE)
    src3, dest1, dest2, tile_plan = _moe_plan(info, counts, n_real, n_tiles)
    xs = moe_dispatch(tile_plan[-1], src3, xn_all)
    ys = moe_experts(tile_plan, xs, moe_w_gate, moe_w_up, moe_w_down)
    y_p2, y_s2 = moe_combine(dest1, dest2, ys, x_mid, sx_pad, info, g_final)
    y_prompt = y_p2.reshape(b, t, d)
    y_sample = y_s2[:ns].reshape(ns, 1, d)

    return (y_prompt, y_sample, p_cmp, s_cmp, p_sel, s_sel, p_win, s_win, p_conv, s_conv,
            p_h.reshape(b, d), s_h)
```

```python
import functools

import jax
import jax.numpy as jnp
import numpy as np
from jax import lax
from jax.experimental import pallas as pl
from jax.experimental.pallas import tpu as pltpu

F32 = jnp.float32
BF16 = jnp.bfloat16
I32 = jnp.int32

D_MODEL = 2048
D_RNN = 2048
RNN_BLOCKS = 16
RNN_BLOCK_DIM = 128
CONV_W = 4
LRU_C = 8.0
N_HEADS = 16
HEAD_DIM = 128
N_KV_HEADS = 4
GROUP = 4
KV_DIM = N_KV_HEADS * HEAD_DIM
HK = 2 * N_KV_HEADS
CMP_STRIDE = 16
CMP_BLK = 32
CMP_HIDDEN = 256
SEL_BLK = 64
SEL_SHIFT = 6
N_SEL = 16
WINDOW = 512
PAGE_SIZE = 128
ATTN_SCALE = HEAD_DIM ** -0.5
FORCED_SCORE = 1e4
N_GROUPS = 4
EXP_PER_GROUP = 8
N_EXPERTS = 32
D_EXPERT = 512
EPS = 1e-6
MASK_VALUE = -1e30
LANES = 128
MIB = 1024 * 1024

PROJ_TILE_M = 1024
PROJ_TILE_N = 1024
LRU_TILE_T = 256

_NT = (((1,), (1,)), ((), ()))


def _params(sem, vmem_mib):
    return pltpu.CompilerParams(dimension_semantics=sem, vmem_limit_bytes=vmem_mib * MIB)


def _msoftmax(s, mask):
    sm = jnp.where(mask, s, MASK_VALUE)
    m = jnp.max(sm, axis=-1, keepdims=True)
    e = jnp.where(mask, jnp.exp(sm - m), 0.0)
    den = jnp.sum(e, axis=-1, keepdims=True)
    return e * jnp.where(den > 0.0, 1.0 / den, 0.0)


def _gelu_tanh(x):
    return 0.5 * x * (1.0 + jnp.tanh(0.7978845608028654 * (x + 0.044715 * (x * x * x))))


def _rmsnorm_body(x_ref, g_ref, o_ref):
    x = x_ref[...]
    var = jnp.mean(x * x, axis=-1, keepdims=True)
    o_ref[...] = ((x * lax.rsqrt(var + EPS)) * g_ref[...]).astype(o_ref.dtype)


def rmsnorm(x, g, tm, out_dtype):
    m, d = x.shape
    return pl.pallas_call(
        _rmsnorm_body,
        grid=(m // tm,),
        in_specs=[pl.BlockSpec((tm, d), lambda i: (i, 0)), pl.BlockSpec((1, d), lambda i: (0, 0))],
        out_specs=pl.BlockSpec((tm, d), lambda i: (i, 0)),
        out_shape=jax.ShapeDtypeStruct((m, d), out_dtype),
        compiler_params=_params(("arbitrary",), 56),
        name="rmsnorm",
    )(x, g.reshape(1, d))


def _xw(x, wbf, w_is_t):
    if w_is_t:
        return lax.dot_general(x, wbf, _NT, preferred_element_type=F32)
    return jnp.dot(x, wbf, preferred_element_type=F32)


def _mm_body(x_ref, w_ref, *rest, w_is_t):
    r_ref = rest[0] if len(rest) == 3 else None
    o_ref, wbf_ref = rest[-2:]

    @pl.when(pl.program_id(1) == 0)
    def _():
        wbf_ref[...] = w_ref[...].astype(BF16)

    res = _xw(x_ref[...], wbf_ref[...], w_is_t)
    if r_ref is not None:
        res = r_ref[...] + res
    o_ref[...] = res.astype(o_ref.dtype)


def matmul_cols(x, w, col0, ncols, tm, tn, out_dtype, residual=None, w_is_t=False, name="matmul"):
    m, k = x.shape
    cb0 = col0 // tn
    assert col0 % tn == 0 and ncols % tn == 0 and m % tm == 0
    if w_is_t:
        w_spec = pl.BlockSpec((tn, k), lambda j, i: (cb0 + j, 0))
        w_tile = (tn, k)
    else:
        w_spec = pl.BlockSpec((k, tn), lambda j, i: (0, cb0 + j))
        w_tile = (k, tn)
    in_specs = [pl.BlockSpec((tm, k), lambda j, i: (i, 0)), w_spec]
    args = [x, w]
    if residual is not None:
        in_specs.append(pl.BlockSpec((tm, tn), lambda j, i: (i, j)))
        args.append(residual)
    return pl.pallas_call(
        functools.partial(_mm_body, w_is_t=w_is_t),
        grid=(ncols // tn, m // tm),
        in_specs=in_specs,
        out_specs=pl.BlockSpec((tm, tn), lambda j, i: (i, j)),
        out_shape=jax.ShapeDtypeStruct((m, ncols), out_dtype),
        scratch_shapes=[pltpu.VMEM(w_tile, BF16)],
        compiler_params=_params(("arbitrary", "arbitrary"), 56),
        name=name,
    )(*args)


def _mm_kv_body(x_ref, w_ref, oi_ref, *rest):
    wbf_ref = rest[-1]

    @pl.when(pl.program_id(0) == 0)
    def _():
        wbf_ref[...] = w_ref[...].astype(BF16)

    res = _xw(x_ref[...], wbf_ref[...], True)
    tm = res.shape[0]
    for hk in range(HK):
        blk = res[:, hk * HEAD_DIM:(hk + 1) * HEAD_DIM]
        oi_ref[pl.ds(hk, tm, stride=HK), :] = blk
        if len(rest) == 2:
            rest[0][hk] = blk.astype(BF16)


def matmul_kv(x, w_t, col0, tm, planes, name):
    m, k = x.shape
    ncols = HK * HEAD_DIM
    assert col0 % ncols == 0 and m % tm == 0
    out_specs = [pl.BlockSpec((tm * HK, HEAD_DIM), lambda i: (i, 0))]
    out_shape = [jax.ShapeDtypeStruct((m * HK, HEAD_DIM), F32)]
    if planes:
        out_specs.append(pl.BlockSpec((HK, tm, HEAD_DIM), lambda i: (0, i, 0)))
        out_shape.append(jax.ShapeDtypeStruct((HK, m, HEAD_DIM), BF16))
    return pl.pallas_call(
        _mm_kv_body,
        grid=(m // tm,),
        in_specs=[pl.BlockSpec((tm, k), lambda i: (i, 0)), pl.BlockSpec((ncols, k), lambda i: (col0 // ncols, 0))],
        out_specs=out_specs,
        out_shape=out_shape,
        scratch_shapes=[pltpu.VMEM((ncols, k), BF16)],
        compiler_params=_params(("arbitrary",), 56),
        name=name,
    )(x, w_t)


def _merge_body(mg_ref, rnn_ref, nsa_ref, o_ref):
    d = rnn_ref.shape[-1]
    ga = jax.nn.sigmoid(mg_ref[:, :d])
    gb = jax.nn.sigmoid(mg_ref[:, d:])
    o_ref[...] = (ga * rnn_ref[...] + gb * nsa_ref[...]).astype(o_ref.dtype)


def merge(u_mg, o_rnn, o_nsa, tm):
    m, d = o_rnn.shape
    return pl.pallas_call(
        _merge_body,
        grid=(m // tm,),
        in_specs=[pl.BlockSpec((tm, 2 * d), lambda i: (i, 0)), pl.BlockSpec((tm, d), lambda i: (i, 0)),
                  pl.BlockSpec((tm, d), lambda i: (i, 0))],
        out_specs=pl.BlockSpec((tm, d), lambda i: (i, 0)),
        out_shape=jax.ShapeDtypeStruct((m, d), BF16),
        compiler_params=_params(("arbitrary",), 48),
        name="merge",
    )(u_mg, o_rnn, o_nsa)


def _lru_gates(uc, wa_ref, wx_ref, ba_ref, bx_ref, lam_ref):
    ucb = uc.astype(BF16)
    ra, ri = [], []
    for n in range(RNN_BLOCKS):
        blk = ucb[:, n * RNN_BLOCK_DIM:(n + 1) * RNN_BLOCK_DIM]
        ra.append(jnp.dot(blk, wa_ref[n], preferred_element_type=F32))
        ri.append(jnp.dot(blk, wx_ref[n], preferred_element_type=F32))
    r = jax.nn.sigmoid(jnp.concatenate(ra, axis=-1) + ba_ref[...])
    i = jax.nn.sigmoid(jnp.concatenate(ri, axis=-1) + bx_ref[...])
    z = -lam_ref[...]
    softplus = jnp.maximum(z, 0.0) + jnp.log1p(jnp.exp(-jnp.abs(z)))
    log_a = (-LRU_C * r) * softplus
    a = jnp.exp(log_a)
    mult = jnp.sqrt(-jnp.tanh(log_a) * (a * a + 1.0))
    return a, mult * (i * uc)


def _lru_prompt_body(u_ref, gate_ref, cw_ref, cb_ref, wa_ref, wx_ref, ba_ref, bx_ref, lam_ref,
                     o_ref, conv_ref, hlast_ref, xbuf, hcar, acum, bcum):
    t = pl.program_id(1)
    nt = pl.num_programs(1)
    tt, d = u_ref.shape[1], u_ref.shape[2]

    @pl.when(t == 0)
    def _():
        xbuf[0:8, :] = jnp.zeros((8, d), F32)
        hcar[...] = jnp.zeros((8, d), F32)

    x = u_ref[0]
    xbuf[8:8 + tt, :] = x
    uc = (cw_ref[0:1, :] * xbuf[5:5 + tt, :] + cw_ref[1:2, :] * xbuf[6:6 + tt, :]
          + cw_ref[2:3, :] * xbuf[7:7 + tt, :] + cw_ref[3:4, :] * x) + cb_ref[...]
    xbuf[0:8, :] = xbuf[tt:tt + 8, :]

    a, b = _lru_gates(uc, wa_ref, wx_ref, ba_ref, bx_ref, lam_ref)
    g = tt // 8
    a3 = a.reshape(g, 8, d)
    b3 = b.reshape(g, 8, d)
    row = lax.broadcasted_iota(I32, (g, 8, d), 1)
    for s in (1, 2, 4):
        a_sh = pltpu.roll(a3, s, axis=1)
        b_sh = pltpu.roll(b3, s, axis=1)
        keep = row >= s
        b3 = jnp.where(keep, a3 * b_sh + b3, b3)
        a3 = jnp.where(keep, a3 * a_sh, a3)
    acum[...] = a3
    bcum[...] = b3

    def body(gi, h):
        hg = bcum[gi] + acum[gi] * h
        bcum[gi] = hg
        return jnp.broadcast_to(hg[7:8, :], (8, d))

    hfin = lax.fori_loop(0, g, body, hcar[...])
    hcar[...] = hfin
    o_ref[0] = bcum[...].reshape(tt, d) * _gelu_tanh(gate_ref[0])

    @pl.when(t == nt - 1)
    def _():
        conv_ref[0] = x[tt - (CONV_W - 1):tt, :]
        hlast_ref[0] = hfin[0:1, :]


def lru_prompt(u_rnn, u_gate, conv_w, conv_b, wa, wx, ba, bx, lam, tt):
    b, t, d = u_rnn.shape
    vec = lambda: pl.BlockSpec((1, d), lambda i, j: (0, 0))
    wspec = lambda: pl.BlockSpec((RNN_BLOCKS, RNN_BLOCK_DIM, RNN_BLOCK_DIM), lambda i, j: (0, 0, 0))
    return pl.pallas_call(
        _lru_prompt_body,
        grid=(b, t // tt),
        in_specs=[pl.BlockSpec((1, tt, d), lambda i, j: (i, j, 0)), pl.BlockSpec((1, tt, d), lambda i, j: (i, j, 0)),
                  pl.BlockSpec((CONV_W, d), lambda i, j: (0, 0)), vec(), wspec(), wspec(), vec(), vec(), vec()],
        out_specs=[pl.BlockSpec((1, tt, d), lambda i, j: (i, j, 0)),
                   pl.BlockSpec((1, CONV_W - 1, d), lambda i, j: (i, 0, 0)),
                   pl.BlockSpec((1, 1, d), lambda i, j: (i, 0, 0))],
        out_shape=[jax.ShapeDtypeStruct((b, t, d), F32), jax.ShapeDtypeStruct((b, CONV_W - 1, d), F32),
                   jax.ShapeDtypeStruct((b, 1, d), F32)],
        scratch_shapes=[pltpu.VMEM((tt + 8, d), F32), pltpu.VMEM((8, d), F32),
                        pltpu.VMEM((tt // 8, 8, d), F32), pltpu.VMEM((tt // 8, 8, d), F32)],
        compiler_params=_params(("arbitrary", "arbitrary"), 56),
        name="lru_prompt",
    )(u_rnn, u_gate, conv_w, conv_b.reshape(1, d), wa, wx, ba.reshape(1, d), bx.reshape(1, d), lam.reshape(1, d))


def _lru_sample_body(u_ref, gate_ref, cp_ref, h0_ref, cw_ref, cb_ref, wa_ref, wx_ref, ba_ref, bx_ref, lam_ref,
                     o_ref, cn_ref, h_ref):
    x = u_ref[...]
    uc = (cw_ref[0:1, :] * cp_ref[0] + cw_ref[1:2, :] * cp_ref[1] + cw_ref[2:3, :] * cp_ref[2]
          + cw_ref[3:4, :] * x) + cb_ref[...]
    a, b = _lru_gates(uc, wa_ref, wx_ref, ba_ref, bx_ref, lam_ref)
    h = a * h0_ref[...] + b
    o_ref[...] = h * _gelu_tanh(gate_ref[...])
    h_ref[...] = h
    cn_ref[0] = cp_ref[1]
    cn_ref[1] = cp_ref[2]
    cn_ref[2] = x


def lru_sample(u_rnn, u_gate, conv_prev_t, h0, conv_w, conv_b, wa, wx, ba, bx, lam):
    n, d = u_rnn.shape
    return pl.pallas_call(
        _lru_sample_body,
        out_shape=[jax.ShapeDtypeStruct((n, d), F32), jax.ShapeDtypeStruct((CONV_W - 1, n, d), F32),
                   jax.ShapeDtypeStruct((n, d), F32)],
        compiler_params=pltpu.CompilerParams(vmem_limit_bytes=40 * MIB),
        name="lru_sample",
    )(u_rnn, u_gate, conv_prev_t, h0, conv_w, conv_b.reshape(1, d), wa, wx, ba.reshape(1, d), bx.reshape(1, d),
      lam.reshape(1, d))


COMPRESS_PAGES = 32
CHUNKS_PER_PAGE = PAGE_SIZE // CMP_STRIDE


def _compress_body(pt_ref, *refs, pp, n_steps):
    page_refs = refs[:pp]
    nxt_ref, extra_ref, wc_ref, pe_ref, b1_ref, w2_ref, b2_ref, o_ref, res_scr, pet_scr = refs[pp:]
    del pt_ref
    is_last = pl.program_id(1) == n_steps - 1
    m_rows = pp * CHUNKS_PER_PAGE * N_KV_HEADS
    lo4 = lax.broadcasted_iota(I32, (4, 8, 8, LANES), 2) < 4
    lo3 = lax.broadcasted_iota(I32, (8, 8, LANES), 1) < 4

    for k in range(2):
        lhs_sp = [[] for _ in range(8)]
        for pr in page_refs:
            xk = pr[pl.ds(k, PAGE_SIZE * N_KV_HEADS, stride=2), :]
            x5 = xk.reshape(4, 2, 8, 8, LANES)
            a0 = x5[:, 0]
            a1 = x5[:, 1]
            be = jnp.where(lo4, a0, pltpu.roll(a1, 4, axis=2))
            bo = jnp.where(lo4, pltpu.roll(a0, 4, axis=2), a1)
            for sp in range(8):
                lhs_sp[sp].append(jnp.concatenate([be[:, sp], bo[:, sp]], axis=-1).reshape(32, 2 * LANES))
        acc = jnp.zeros((m_rows, 2 * CMP_HIDDEN), F32)
        for sp in range(8):
            lhs = jnp.concatenate(lhs_sp[sp], axis=0).astype(BF16)
            acc = acc + jnp.dot(lhs, wc_ref[k, sp * 256:(sp + 1) * 256, :], preferred_element_type=F32)

        @pl.when((pl.program_id(0) == 0) & (pl.program_id(1) == 0))
        def _():
            pet_scr[k] = jnp.dot(pe_ref[k].astype(BF16), wc_ref[k], preferred_element_type=F32)

        peb0 = pet_scr[k, 0:1, :CMP_HIDDEN]
        peb1 = pet_scr[k, 1:2, CMP_HIDDEN:]
        h0 = acc[:, :CMP_HIDDEN] + peb0
        h1 = acc[:, CMP_HIDDEN:] + peb1

        xn = jnp.where(is_last, extra_ref[pl.ds(k, CMP_STRIDE * N_KV_HEADS, stride=2), :],
                       nxt_ref[pl.ds(k, CMP_STRIDE * N_KV_HEADS, stride=2), :])
        xn3 = xn.reshape(8, 8, LANES)
        ln = jnp.concatenate([jnp.where(lo3, xn3, 0.0), jnp.where(lo3, 0.0, xn3)], axis=-1).astype(BF16)
        nacc = jnp.zeros((8, CMP_HIDDEN), F32)
        for sp in range(8):
            nacc = nacc + jnp.dot(ln[sp], wc_ref[k, sp * 256:(sp + 1) * 256, CMP_HIDDEN:],
                                  preferred_element_type=F32)
        n8 = nacc + pltpu.roll(nacc, 4, axis=0) + peb1
        h1e = jnp.concatenate([h1, n8], axis=0)
        h1s = pltpu.roll(h1e, m_rows + 8 - N_KV_HEADS, axis=0)[:m_rows]
        pre = h0 + h1s + b1_ref[k]
        out_k = jnp.dot(_gelu_tanh(pre).astype(BF16), w2_ref[k], preferred_element_type=F32) + b2_ref[k]
        res_scr[...] = out_k
        for h in range(N_KV_HEADS):
            o_ref[0, 2 * h + k] = res_scr[pl.ds(h, m_rows // N_KV_HEADS, stride=N_KV_HEADS), :]


def compress(pool2d, page_table, extra2d, wc, pe8, b1, w2, b2):
    nseq, npages = page_table.shape
    pp = min(COMPRESS_PAGES, npages)
    n_steps = npages // pp
    assert npages % pp == 0
    rows_page = PAGE_SIZE * HK
    rows_chunk = CMP_STRIDE * HK
    blocks_step = pp * CHUNKS_PER_PAGE

    def page_spec(i):
        return pl.BlockSpec((rows_page, LANES), lambda b, s, pt: (pt[b, s * pp + i], 0))

    def nxt_map(b, s, pt):
        return (pt[b, jnp.minimum((s + 1) * pp, npages - 1)] * CHUNKS_PER_PAGE, 0)

    const3 = lambda shape: pl.BlockSpec(shape, lambda b, s, pt: (0, 0, 0))
    in_specs = [page_spec(i) for i in range(pp)] + [
        pl.BlockSpec((rows_chunk, LANES), nxt_map),
        pl.BlockSpec((rows_chunk, LANES), lambda b, s, pt: (b, 0)),
        const3((2, CMP_STRIDE * HEAD_DIM, 2 * CMP_HIDDEN)),
        const3((2, 8, CMP_STRIDE * HEAD_DIM)),
        const3((2, 1, CMP_HIDDEN)),
        const3((2, CMP_HIDDEN, HEAD_DIM)),
        const3((2, 1, HEAD_DIM)),
    ]
    grid_spec = pltpu.PrefetchScalarGridSpec(
        num_scalar_prefetch=1,
        grid=(nseq, n_steps),
        in_specs=in_specs,
        out_specs=pl.BlockSpec((1, HK, blocks_step, HEAD_DIM), lambda b, s, pt: (b, 0, s, 0)),
        scratch_shapes=[pltpu.VMEM((blocks_step * N_KV_HEADS, HEAD_DIM), F32),
                        pltpu.VMEM((2, 8, 2 * CMP_HIDDEN), F32)],
    )
    return pl.pallas_call(
        functools.partial(_compress_body, pp=pp, n_steps=n_steps),
        grid_spec=grid_spec,
        out_shape=jax.ShapeDtypeStruct((nseq, HK, npages * CHUNKS_PER_PAGE, HEAD_DIM), F32),
        compiler_params=_params(("arbitrary", "arbitrary"), 56),
        name="compress",
    )(page_table, *([pool2d] * pp), pool2d, extra2d, wc, pe8, b1, w2, b2)


def _overlap_matrix(n_cmp, n_cmp_pad, n_sel, n_sel_pad):
    cs = np.arange(n_cmp_pad)[:, None] * CMP_STRIDE
    ss = np.arange(n_sel_pad)[None, :] * SEL_BLK
    ov = np.minimum(cs + CMP_BLK, ss + SEL_BLK) - np.maximum(cs, ss)
    m = np.clip(ov, 0, CMP_BLK).astype(np.float32) / CMP_BLK
    m[n_cmp:, :] = 0.0
    m[:, n_sel:] = 0.0
    return m


QBLK = 128


KCHUNK = 64
KPIECE = 512
LOG2E = 1.4426950408889634


def _softmax_passes(s_scr, p_scr, n_plain, n_masked, n_total, plain_mask, full_mask):
    cols = s_scr.shape[1]
    grp = KCHUNK // 8
    c2 = ATTN_SCALE * LOG2E

    def rows(c):
        return pl.ds(pl.multiple_of(c * KCHUNK, KCHUNK), KCHUNK)

    def masked_max(mask_fn):
        def body(c, m8):
            sm = jnp.where(mask_fn(c), s_scr[rows(c), :], MASK_VALUE)
            s_scr[rows(c), :] = sm
            return jnp.maximum(m8, jnp.max(sm.reshape(grp, 8, cols), axis=0))
        return body

    m8 = lax.fori_loop(0, n_plain, masked_max(plain_mask), jnp.full((8, cols), MASK_VALUE, F32))
    m8 = lax.fori_loop(n_plain, n_plain + n_masked, masked_max(full_mask), m8)
    mc = jnp.max(m8, axis=0, keepdims=True) * c2

    def pass_exp(c, l8):
        p = jnp.exp2(s_scr[rows(c), :] * c2 - mc)
        p_scr[rows(c), :] = p.astype(BF16)
        return l8 + jnp.sum(p.reshape(grp, 8, cols), axis=0)

    l8 = lax.fori_loop(0, n_plain + n_masked, pass_exp, jnp.zeros((8, cols), F32))

    def zero_fill(c, carry):
        p_scr[rows(c), :] = jnp.zeros((KCHUNK, cols), BF16)
        return carry

    lax.fori_loop(n_plain + n_masked, n_total, zero_fill, 0)
    return 1.0 / jnp.sum(l8, axis=0, keepdims=True)


def _attn_prompt_s_body(q_ref, ksel_ref, vsel_ref, kwin_ref, vwin_ref, kvc_ref, bg_ref, mga_ref, mgb_ref, rnn_ref,
                        mt_ref, o_ref, vsel_t, vwin_t, kc_bf, vc_t_bf, s_scr, p_scr, sel_scr, *, n_cmp, n_sel):
    qi = pl.program_id(2)
    q0 = qi * QBLK
    cols = GROUP * QBLK

    @pl.when(qi == 0)
    def _():
        vsel_t[...] = vsel_ref[0].astype(F32).T.astype(BF16)
        vwin_t[...] = vwin_ref[0].astype(F32).T.astype(BF16)
        kc_bf[...] = kvc_ref[0, 0].astype(BF16)
        vc_t_bf[...] = kvc_ref[0, 1].T.astype(BF16)

    q = q_ref[0]
    q4 = jnp.concatenate([q[:, g * HEAD_DIM:(g + 1) * HEAD_DIM] for g in range(GROUP)], axis=0)
    qpos_row = q0 + (lax.broadcasted_iota(I32, (1, cols), 1) & (QBLK - 1))
    row_in_chunk = lax.broadcasted_iota(I32, (KCHUNK, cols), 0)

    nb = kvc_ref.shape[2]
    kc = kc_bf[...]
    vc_t = vc_t_bf[...]
    n_io = lax.broadcasted_iota(I32, (nb, cols), 0)
    mask_c = (n_io * CMP_STRIDE + (CMP_BLK - 1) <= qpos_row) & (n_io < n_cmp)
    sc = jnp.where(mask_c, lax.dot_general(kc, q4, _NT, preferred_element_type=F32) * ATTN_SCALE, MASK_VALUE)
    ec = jnp.where(mask_c, jnp.exp(sc - jnp.max(sc, axis=0, keepdims=True)), 0.0)
    den_c = jnp.sum(ec, axis=0, keepdims=True)
    pc = ec * jnp.where(den_c > 0.0, 1.0 / den_c, 0.0)
    o_c = jnp.dot(vc_t, pc.astype(BF16), preferred_element_type=F32)
    psum = pc[:, 0:QBLK] + pc[:, QBLK:2 * QBLK] + pc[:, 2 * QBLK:3 * QBLK] + pc[:, 3 * QBLK:4 * QBLK]

    nsp = mt_ref.shape[0]
    imp_t = jnp.dot(mt_ref[...], psum.astype(BF16), preferred_element_type=F32)
    jt = lax.broadcasted_iota(I32, (nsp, QBLK), 0)
    cur = lax.shift_right_logical(q0 + lax.broadcasted_iota(I32, (nsp, QBLK), 1), SEL_SHIFT)
    valid = (jt <= cur) & (jt < n_sel)
    forced = (jt == 0) | (jt >= cur - 1)
    score = jnp.where(forced, FORCED_SCORE, imp_t)
    score = jnp.where(valid, score, -1.0)
    rank = jnp.zeros((nsp, QBLK), I32)
    for k in range(n_sel):
        sk = score[k:k + 1, :]
        beats = (sk > score) | ((sk == score) & (k < jt))
        rank = rank + beats.astype(I32)
    sel_t = jnp.where((rank < N_SEL) & valid, 1.0, 0.0)
    sel_scr[...] = jnp.concatenate([sel_t] * GROUP, axis=1)

    wkeys = WINDOW + QBLK
    ks = pl.multiple_of(jnp.maximum(q0 - WINDOW, 0), QBLK)
    s_scr[0:wkeys, :] = lax.dot_general(kwin_ref[0, pl.ds(ks, wkeys), :], q4, _NT, preferred_element_type=F32)

    def win_mask(c):
        dist = qpos_row - (ks + c * KCHUNK + row_in_chunk)
        return (dist >= 0) & (dist < WINDOW)

    n_wchunks = wkeys // KCHUNK
    inv_w = _softmax_passes(s_scr, p_scr, 0, n_wchunks, n_wchunks, win_mask, win_mask)
    o_w = jnp.dot(vwin_t[:, pl.ds(ks, wkeys)], p_scr[0:wkeys, :], preferred_element_type=F32) * inv_w

    n_pieces = qi // (KPIECE // QBLK) + 1

    def score_piece(c, carry):
        r0 = pl.multiple_of(c * KPIECE, KPIECE)
        s_scr[pl.ds(r0, KPIECE), :] = lax.dot_general(ksel_ref[0, pl.ds(r0, KPIECE), :], q4, _NT,
                                                      preferred_element_type=F32)
        return carry

    lax.fori_loop(0, n_pieces, score_piece, 0)

    def picked(c):
        return sel_scr[pl.ds(c, 1), :] > 0.5

    def picked_causal(c):
        return picked(c) & (c * KCHUNK + row_in_chunk <= qpos_row)

    chunks_q = QBLK // KCHUNK
    inv_s = _softmax_passes(s_scr, p_scr, qi * chunks_q, chunks_q, n_pieces * (KPIECE // KCHUNK),
                            picked, picked_causal)

    def pv_piece(c, acc):
        r0 = pl.multiple_of(c * KPIECE, KPIECE)
        return acc + jnp.dot(vsel_t[:, pl.ds(r0, KPIECE)], p_scr[pl.ds(r0, KPIECE), :], preferred_element_type=F32)

    o_s = lax.fori_loop(0, n_pieces, pv_piece, jnp.zeros((HEAD_DIM, cols), F32)) * inv_s

    gates_t = jax.nn.sigmoid(bg_ref[0]).T
    grow = lambda br: jnp.concatenate([gates_t[br * GROUP + g:br * GROUP + g + 1, :] for g in range(GROUP)], axis=1)
    o_t = grow(0) * o_c + grow(1) * o_s + grow(2) * o_w
    for g in range(GROUP):
        cs = slice(g * HEAD_DIM, (g + 1) * HEAD_DIM)
        o_nsa = o_t[:, g * QBLK:(g + 1) * QBLK].T
        merged = jax.nn.sigmoid(mga_ref[0, :, cs]) * rnn_ref[0, :, cs] + jax.nn.sigmoid(mgb_ref[0, :, cs]) * o_nsa
        o_ref[0, :, cs] = merged.astype(o_ref.dtype)


def attn_prompt_s(q3, sel_planes, win_planes, kvc, ubg3, umg3, o_rnn3, n_cmp):
    b, t, _ = q3.shape
    nb = kvc.shape[2]
    assert KCHUNK == SEL_BLK and t % KPIECE == 0 and t >= WINDOW + QBLK
    n_sel = -(-t // SEL_BLK)
    nsp = max(8, -(-n_sel // 8) * 8)
    m_t = _overlap_matrix(n_cmp, nb, n_sel, nsp).T
    kspec = lambda: pl.BlockSpec((1, t, HEAD_DIM), lambda i, h, j: (2 * h, i, 0))
    vspec = lambda: pl.BlockSpec((1, t, HEAD_DIM), lambda i, h, j: (2 * h + 1, i, 0))
    cols = GROUP * QBLK
    head_cols = GROUP * HEAD_DIM
    tile = lambda off: pl.BlockSpec((1, QBLK, head_cols), lambda i, h, j: (i, j, off + h))
    return pl.pallas_call(
        functools.partial(_attn_prompt_s_body, n_cmp=n_cmp, n_sel=n_sel),
        grid=(b, N_KV_HEADS, t // QBLK),
        in_specs=[
            tile(0),
            kspec(), vspec(), kspec(), vspec(),
            pl.BlockSpec((1, 2, nb, HEAD_DIM), lambda i, h, j: (i, h, 0, 0)),
            pl.BlockSpec((1, QBLK, LANES), lambda i, h, j: (i, j, h)),
            tile(0), tile(N_KV_HEADS), tile(0),
            pl.BlockSpec((nsp, nb), lambda i, h, j: (0, 0)),
        ],
        out_specs=tile(0),
        out_shape=jax.ShapeDtypeStruct((b, t, N_HEADS * HEAD_DIM), BF16),
        scratch_shapes=[pltpu.VMEM((HEAD_DIM, t), BF16), pltpu.VMEM((HEAD_DIM, t), BF16),
                        pltpu.VMEM((nb, HEAD_DIM), BF16), pltpu.VMEM((HEAD_DIM, nb), BF16),
                        pltpu.VMEM((t, cols), F32), pltpu.VMEM((t, cols), BF16), pltpu.VMEM((nsp, cols), F32)],
        compiler_params=_params(("arbitrary", "arbitrary", "arbitrary"), 48),
        name="attn_prompt",
    )(q3, sel_planes, sel_planes, win_planes, win_planes, kvc, ubg3, umg3, umg3, o_rnn3, jnp.asarray(m_t, BF16))


def _row_head(shape):
    return lax.shift_right_logical(lax.broadcasted_iota(I32, shape, 0), 2)


def _cmp_sample_body(q_ref, kvc_ref, m_ref, oc_ref, imp_ref, *, qpos, n_cmp):
    qb = q_ref[0].astype(BF16)
    nb = kvc_ref.shape[2]
    rg = _row_head((N_HEADS, nb))
    ncol = lax.broadcasted_iota(I32, (N_HEADS, nb), 1)
    mask = (ncol * CMP_STRIDE + (CMP_BLK - 1) <= qpos) & (ncol < n_cmp)
    row8 = lax.broadcasted_iota(I32, (8, nb), 0)
    o = jnp.zeros((N_HEADS, HEAD_DIM), F32)
    ps8 = jnp.zeros((8, nb), F32)
    for h in range(N_KV_HEADS):
        kc = kvc_ref[0, 2 * h].astype(BF16)
        vc = kvc_ref[0, 2 * h + 1].astype(BF16)
        s = lax.dot_general(qb, kc, _NT, preferred_element_type=F32) * ATTN_SCALE
        ph = jnp.where(rg == h, _msoftmax(s, mask), 0.0)
        o = o + jnp.dot(ph.astype(BF16), vc, preferred_element_type=F32)
        ps8 = jnp.where(row8 == h, jnp.sum(ph, axis=0, keepdims=True), ps8)
    oc_ref[0] = o
    imp_ref[0] = jnp.dot(ps8.astype(BF16), m_ref[...], preferred_element_type=F32)[0:N_KV_HEADS]


def cmp_sample(q16, kvc, m_mat, qpos, n_cmp):
    ns = q16.shape[0]
    nb = kvc.shape[2]
    nsp = m_mat.shape[1]
    return pl.pallas_call(
        functools.partial(_cmp_sample_body, qpos=qpos, n_cmp=n_cmp),
        grid=(ns,),
        in_specs=[pl.BlockSpec((1, N_HEADS, HEAD_DIM), lambda b: (b, 0, 0)),
                  pl.BlockSpec((1, HK, nb, HEAD_DIM), lambda b: (b, 0, 0, 0)),
                  pl.BlockSpec((nb, nsp), lambda b: (0, 0))],
        out_specs=[pl.BlockSpec((1, N_HEADS, HEAD_DIM), lambda b: (b, 0, 0)),
                   pl.BlockSpec((1, N_KV_HEADS, nsp), lambda b: (b, 0, 0))],
        out_shape=[jax.ShapeDtypeStruct((ns, N_HEADS, HEAD_DIM), F32),
                   jax.ShapeDtypeStruct((ns, N_KV_HEADS, nsp), F32)],
        compiler_params=_params(("arbitrary",), 40),
        name="cmp_sample",
    )(q16, kvc, m_mat)


def _rank_sample_body(imp_ref, idx_ref, sc_scr, *, cur, n_sel):
    npad, nr = sc_scr.shape
    st = imp_ref[...].T
    j = lax.broadcasted_iota(I32, (npad, nr), 0)
    real = j < n_sel
    valid = (j <= cur) & real
    forced = (j == 0) | (j >= cur - 1)
    score = jnp.where(forced, FORCED_SCORE, st)
    score = jnp.where(valid, score, -1.0)
    score = jnp.where(real, score, -2.0)
    sc_scr[...] = score

    def body(k, rank):
        sk = sc_scr[pl.ds(k, 1), :]
        beats = (sk > score) | ((sk == score) & (k < j))
        return rank + beats.astype(I32)

    rank = lax.fori_loop(0, n_sel, body, jnp.zeros((npad, nr), I32))
    sel = (rank < N_SEL) & valid
    for slot in range(N_SEL):
        hit = sel & (rank == slot)
        found = jnp.max(hit.astype(I32), axis=0, keepdims=True)
        val = jnp.sum(jnp.where(hit, j, 0), axis=0, keepdims=True)
        idx_ref[slot:slot + 1, :] = jnp.where(found > 0, val, -1)


def rank_sample(imp2, cur, n_sel):
    nr, npad = imp2.shape
    return pl.pallas_call(
        functools.partial(_rank_sample_body, cur=cur, n_sel=n_sel),
        out_shape=jax.ShapeDtypeStruct((N_SEL, nr), I32),
        scratch_shapes=[pltpu.VMEM((npad, nr), F32)],
        name="rank_sample",
    )(imp2)


def _sel_sample_body(idx_ref, pt_ref, q_ref, kn_ref, vn_ref, *rest, n_past_blocks):
    del pt_ref
    kvs = rest[:N_SEL * N_KV_HEADS]
    o_ref = rest[N_SEL * N_KV_HEADS]
    b = pl.program_id(0)
    q = q_ref[0]
    qb = q.astype(BF16)
    nk = N_SEL * SEL_BLK
    rg = _row_head((N_HEADS, nk))
    col_slot = lax.shift_right_logical(lax.broadcasted_iota(I32, (N_HEADS, nk), 1), SEL_SHIFT)

    sc = jnp.zeros((N_HEADS, nk), F32)
    okv = jnp.zeros((N_HEADS, nk), I32)
    for h in range(N_KV_HEADS):
        kcat = jnp.concatenate([kvs[s * N_KV_HEADS + h][:, 0, 0, :] for s in range(N_SEL)], axis=0).astype(BF16)
        sh = lax.dot_general(qb, kcat, _NT, preferred_element_type=F32)
        ok = jnp.zeros((N_HEADS, nk), I32)
        for s in range(N_SEL):
            blk = idx_ref[s, b * N_KV_HEADS + h]
            ok = jnp.where(col_slot == s, jnp.where((blk >= 0) & (blk < n_past_blocks), 1, 0), ok)
        sc = jnp.where(rg == h, sh, sc)
        okv = jnp.where(rg == h, ok, okv)
    sm = jnp.where(okv > 0, sc * ATTN_SCALE, MASK_VALUE)
    s_new = jnp.sum(q * kn_ref[0], axis=-1, keepdims=True) * ATTN_SCALE
    m = jnp.maximum(jnp.max(sm, axis=-1, keepdims=True), s_new)
    p = jnp.where(okv > 0, jnp.exp(sm - m), 0.0)
    p_new = jnp.exp(s_new - m)
    den = jnp.sum(p, axis=-1, keepdims=True) + p_new
    pv = jnp.zeros((N_HEADS, HEAD_DIM), F32)
    for h in range(N_KV_HEADS):
        vcat = jnp.concatenate([kvs[s * N_KV_HEADS + h][:, 0, 1, :] for s in range(N_SEL)], axis=0).astype(BF16)
        pv = pv + jnp.dot(jnp.where(rg == h, p, 0.0).astype(BF16), vcat, preferred_element_type=F32)
    o_ref[0] = (pv + p_new * vn_ref[0]) / den


def sel_sample(idx, page_table, q16, kn16, vn16, pool4, n_past_blocks):
    ns = q16.shape[0]
    blk_per_page = PAGE_SIZE // SEL_BLK

    def kv_spec(s, h):
        def imap(b, idx_r, pt_r):
            blk = jnp.clip(idx_r[s, b * N_KV_HEADS + h], 0, n_past_blocks - 1)
            return (pt_r[b, blk // blk_per_page] * blk_per_page + blk % blk_per_page, h, 0, 0)
        return pl.BlockSpec((SEL_BLK, 1, 2, HEAD_DIM), imap)

    head3 = lambda: pl.BlockSpec((1, N_HEADS, HEAD_DIM), lambda b, i, p: (b, 0, 0))
    kv_specs = [kv_spec(s, h) for s in range(N_SEL) for h in range(N_KV_HEADS)]
    grid_spec = pltpu.PrefetchScalarGridSpec(
        num_scalar_prefetch=2,
        grid=(ns,),
        in_specs=[head3(), head3(), head3()] + kv_specs,
        out_specs=head3(),
    )
    return pl.pallas_call(
        functools.partial(_sel_sample_body, n_past_blocks=n_past_blocks),
        grid_spec=grid_spec,
        out_shape=jax.ShapeDtypeStruct((ns, N_HEADS, HEAD_DIM), F32),
        compiler_params=_params(("arbitrary",), 40),
        name="sel_sample",
    )(idx, page_table, q16, kn16, vn16, *([pool4] * len(kv_specs)))


def _win_sample_body(q_ref, win_ref, new_ref, kn_ref, vn_ref, oc_ref, os_ref, bg_ref, o_ref, wout_ref, *, wbuf):
    wout_ref[0:(wbuf - 1) * HK, :] = win_ref[HK:wbuf * HK, :]
    wout_ref[(wbuf - 1) * HK:wbuf * HK, :] = new_ref[...]
    q = q_ref[0]
    qb = q.astype(BF16)
    rg = _row_head((N_HEADS, wbuf))
    sc = jnp.zeros((N_HEADS, wbuf), F32)
    for h in range(N_KV_HEADS):
        kh = win_ref[pl.ds(2 * h, wbuf, stride=HK), :].astype(BF16)
        sc = jnp.where(rg == h, lax.dot_general(qb, kh, _NT, preferred_element_type=F32), sc)
    mask = lax.broadcasted_iota(I32, (N_HEADS, wbuf), 1) > wbuf - WINDOW
    sm = jnp.where(mask, sc * ATTN_SCALE, MASK_VALUE)
    s_new = jnp.sum(q * kn_ref[0], axis=-1, keepdims=True) * ATTN_SCALE
    m = jnp.maximum(jnp.max(sm, axis=-1, keepdims=True), s_new)
    p = jnp.where(mask, jnp.exp(sm - m), 0.0)
    p_new = jnp.exp(s_new - m)
    den = jnp.sum(p, axis=-1, keepdims=True) + p_new
    pv = jnp.zeros((N_HEADS, HEAD_DIM), F32)
    for h in range(N_KV_HEADS):
        vh = win_ref[pl.ds(2 * h + 1, wbuf, stride=HK), :].astype(BF16)
        pv = pv + jnp.dot(jnp.where(rg == h, p, 0.0).astype(BF16), vh, preferred_element_type=F32)
    o_w = (pv + p_new * vn_ref[0]) / den
    gates = jax.nn.sigmoid(bg_ref[0])
    o_ref[0] = gates[:, 0:1] * oc_ref[0] + gates[:, 1:2] * os_ref[0] + gates[:, 2:3] * o_w


def win_sample(q16, win2d, new2d, kn16, vn16, o_cmp, o_sel, bgs, wbuf):
    ns = q16.shape[0]
    head3 = lambda: pl.BlockSpec((1, N_HEADS, HEAD_DIM), lambda b: (b, 0, 0))
    wspec = lambda: pl.BlockSpec((wbuf * HK, LANES), lambda b: (b, 0))
    return pl.pallas_call(
        functools.partial(_win_sample_body, wbuf=wbuf),
        grid=(ns,),
        in_specs=[head3(), wspec(), pl.BlockSpec((HK, LANES), lambda b: (b, 0)), head3(), head3(), head3(), head3(),
                  head3()],
        out_specs=[head3(), wspec()],
        out_shape=[jax.ShapeDtypeStruct((ns, N_HEADS, HEAD_DIM), F32),
                   jax.ShapeDtypeStruct((ns * wbuf * HK, LANES), F32)],
        compiler_params=_params(("arbitrary",), 40),
        name="win_sample",
    )(q16, win2d, new2d, kn16, vn16, o_cmp, o_sel, bgs)


ROUTE_LANE0 = N_GROUPS


INFO_E1, INFO_E2, INFO_R1, INFO_R2, INFO_C1, INFO_C2 = range(6)


def _route_body(xp_ref, xs_ref, g_ref, wr_ref, br_ref, xn_ref, info_ref, cnt_ref, run_scr, *, n_real, n_ptiles):
    step = pl.program_id(0)

    @pl.when(step == 0)
    def _():
        run_scr[...] = jnp.zeros(run_scr.shape, F32)

    refs = (g_ref, wr_ref, br_ref, xn_ref, info_ref, cnt_ref, run_scr)

    @pl.when(step < n_ptiles)
    def _():
        _route_tile(xp_ref[...], step, n_real, *refs)

    @pl.when(step >= n_ptiles)
    def _():
        _route_tile(xs_ref[...], step, n_real, *refs)


def _route_tile(x, step, n_real, g_ref, wr_ref, br_ref, xn_ref, info_ref, cnt_ref, run_scr):
    tm = x.shape[0]
    var = jnp.mean(x * x, axis=-1, keepdims=True)
    xn = (x * lax.rsqrt(var + EPS)) * g_ref[...]
    xn_ref[...] = xn
    xnb = xn.astype(BF16)
    logits = jnp.dot(xnb, wr_ref[...], preferred_element_type=F32) + br_ref[...]
    lane = lax.broadcasted_iota(I32, logits.shape, 1)
    big = 4 * LANES

    isg = lane < N_GROUPS
    lg = jnp.where(isg, logits, -jnp.inf)
    mg = jnp.max(lg, axis=-1, keepdims=True)
    gi = jnp.min(jnp.where(lg == mg, lane, big), axis=-1, keepdims=True)
    pg = 1.0 / jnp.sum(jnp.where(isg, jnp.exp(lg - mg), 0.0), axis=-1, keepdims=True)

    lo = ROUTE_LANE0 + gi * EXP_PER_GROUP
    ing = (lane >= lo) & (lane < lo + EXP_PER_GROUP)
    le = jnp.where(ing, logits, -jnp.inf)
    me = jnp.max(le, axis=-1, keepdims=True)
    ee = jnp.where(ing, jnp.exp(le - me), 0.0)
    pe = jnp.where(ing, ee / jnp.sum(ee, axis=-1, keepdims=True), -1.0)
    p1 = jnp.max(pe, axis=-1, keepdims=True)
    i1 = jnp.min(jnp.where(pe == p1, lane, big), axis=-1, keepdims=True)
    pe2 = jnp.where(lane == i1, -1.0, pe)
    p2 = jnp.max(pe2, axis=-1, keepdims=True)
    i2 = jnp.min(jnp.where(pe2 == p2, lane, big), axis=-1, keepdims=True)
    tot = p1 + p2
    c1 = pg * (p1 / tot)
    c2 = pg * (p2 / tot)

    row = step * tm + lax.broadcasted_iota(I32, logits.shape, 0)
    hit = jnp.where(((lane == i1) | (lane == i2)) & (row < n_real), 1.0, 0.0)
    tri = jnp.where(lax.broadcasted_iota(I32, (tm, tm), 1) < lax.broadcasted_iota(I32, (tm, tm), 0), 1.0, 0.0)
    before = run_scr[...] + jnp.dot(tri.astype(BF16), hit.astype(BF16), preferred_element_type=F32)
    r1 = jnp.sum(jnp.where(lane == i1, before, 0.0), axis=-1, keepdims=True)
    r2 = jnp.sum(jnp.where(lane == i2, before, 0.0), axis=-1, keepdims=True)
    run_new = run_scr[...] + jnp.sum(hit, axis=0, keepdims=True)
    run_scr[...] = run_new
    cnt_ref[...] = run_new

    info = jnp.zeros(logits.shape, F32)
    for ln, val in ((INFO_E1, (i1 - ROUTE_LANE0).astype(F32)), (INFO_E2, (i2 - ROUTE_LANE0).astype(F32)),
                    (INFO_R1, r1), (INFO_R2, r2), (INFO_C1, c1), (INFO_C2, c2)):
        info = jnp.where(lane == ln, val, info)
    info_ref[...] = info


MOE_TILE = 256


def route(x_prompt, x_sample, g, wr, br, n_real):
    mp, d = x_prompt.shape
    n_ptiles = mp // MOE_TILE
    assert mp % MOE_TILE == 0 and x_sample.shape[0] == MOE_TILE
    m = mp + MOE_TILE
    return pl.pallas_call(
        functools.partial(_route_body, n_real=n_real, n_ptiles=n_ptiles),
        grid=(n_ptiles + 1,),
        in_specs=[pl.BlockSpec((MOE_TILE, d), lambda i: (jnp.minimum(i, n_ptiles - 1), 0)),
                  pl.BlockSpec((MOE_TILE, d), lambda i: (0, 0)), pl.BlockSpec((1, d), lambda i: (0, 0)),
                  pl.BlockSpec((d, LANES), lambda i: (0, 0)), pl.BlockSpec((1, LANES), lambda i: (0, 0))],
        out_specs=[pl.BlockSpec((MOE_TILE, d), lambda i: (i, 0)), pl.BlockSpec((MOE_TILE, LANES), lambda i: (i, 0)),
                   pl.BlockSpec((1, LANES), lambda i: (0, 0))],
        out_shape=[jax.ShapeDtypeStruct((m, d), F32), jax.ShapeDtypeStruct((m, LANES), F32),
                   jax.ShapeDtypeStruct((1, LANES), F32)],
        scratch_shapes=[pltpu.VMEM((1, LANES), F32)],
        compiler_params=_params(("arbitrary",), 40),
        name="route",
    )(x_prompt, x_sample, g.reshape(1, d), wr, br)


def _moe_plan(info, counts_row, n_real, n_tiles):
    n_pad = info.shape[0]
    e = jnp.clip(info[:, INFO_E1:INFO_E2 + 1].astype(I32), 0, N_EXPERTS - 1)
    r = info[:, INFO_R1:INFO_R2 + 1].astype(I32)
    counts = counts_row[0, ROUTE_LANE0:ROUTE_LANE0 + N_EXPERTS].astype(I32)
    padded = ((counts + MOE_TILE - 1) // MOE_TILE) * MOE_TILE
    ex = jnp.arange(N_EXPERTS, dtype=I32)
    ends = jnp.sum(jnp.where(ex[None, :] <= ex[:, None], padded[None, :], 0), axis=1)
    offs = ends - padded
    tok = jnp.arange(n_pad, dtype=I32)
    valid = (tok < n_real)[:, None]
    dest = jnp.where(valid, offs[e] + r, 0)
    n_slots = n_tiles * MOE_TILE
    src = jnp.zeros((n_slots,), I32).at[jnp.where(valid, dest, n_slots).reshape(-1)].set(
        jnp.repeat(tok, 2), mode="drop")
    n_used = ends[N_EXPERTS - 1] // MOE_TILE
    tiles = jnp.arange(n_tiles, dtype=I32)
    used = tiles < n_used
    te = jnp.minimum(jnp.sum((ends[None, :] <= (tiles * MOE_TILE)[:, None]).astype(I32), axis=1), N_EXPERTS - 1)
    te_prev = jnp.concatenate([te[:1] - 1, te[:-1]])
    first = (te != te_prev) & used
    later = tiles[None, :] > tiles[:, None]
    run = jnp.sum((first[None, :] & ~later).astype(I32), axis=1) - 1
    nxt_tile = jnp.min(jnp.where(first[None, :] & later, tiles[None, :], n_tiles), axis=1)
    nxt = jnp.where(nxt_tile < n_tiles, te[jnp.minimum(nxt_tile, n_tiles - 1)], -1)
    tile3 = lambda v: v.reshape(-1, 1, MOE_TILE)
    return (tile3(src), tile3(dest[:, 0]), tile3(dest[:, 1]),
            (te, first.astype(I32), nxt, jnp.bitwise_and(run, 1), n_used.reshape(1)))


def _issue_rows(idx_ref, src_hbm, dst_buf, slot, sem):
    n = idx_ref.shape[2]
    group = 8

    def body(g, carry):
        for j in range(group):
            r = g * group + j
            pltpu.make_async_copy(src_hbm.at[pl.ds(idx_ref[0, 0, r], 1), :], dst_buf.at[slot, pl.ds(r, 1), :],
                                  sem.at[slot]).start(priority=j % 2)
        return carry

    lax.fori_loop(0, n // group, body, 0)


def _wait_rows(src_hbm, dst_buf, slot, sem):
    n = dst_buf.shape[1]
    pltpu.make_async_copy(src_hbm.at[pl.ds(0, n), :], dst_buf.at[slot], sem.at[slot]).wait()


def _moe_dispatch_body(nu_ref, src_cur, src_nxt, xn_hbm, xs_ref, xbuf, sem):
    i = pl.program_id(0)
    n_used = nu_ref[0]
    slot = lax.rem(i, 2)

    @pl.when(i == 0)
    def _():
        _issue_rows(src_cur, xn_hbm, xbuf, 0, sem)

    @pl.when(i + 1 < n_used)
    def _():
        _issue_rows(src_nxt, xn_hbm, xbuf, 1 - slot, sem)

    @pl.when(i < n_used)
    def _():
        _wait_rows(xn_hbm, xbuf, slot, sem)
        xs_ref[...] = xbuf[slot].astype(xs_ref.dtype)

    @pl.when(i >= n_used)
    def _():
        xs_ref[...] = jnp.zeros(xs_ref.shape, xs_ref.dtype)


def moe_dispatch(n_used, src3, xn):
    n_tiles = src3.shape[0]
    d = xn.shape[1]
    smem_tile = lambda imap: pl.BlockSpec((1, 1, MOE_TILE), imap, memory_space=pltpu.SMEM)
    grid_spec = pltpu.PrefetchScalarGridSpec(
        num_scalar_prefetch=1,
        grid=(n_tiles,),
        in_specs=[smem_tile(lambda i, nu: (i, 0, 0)),
                  smem_tile(lambda i, nu: (jnp.minimum(i + 1, n_tiles - 1), 0, 0)),
                  pl.BlockSpec(memory_space=pl.ANY)],
        out_specs=pl.BlockSpec((MOE_TILE, d), lambda i, nu: (i, 0)),
        scratch_shapes=[pltpu.VMEM((2, MOE_TILE, d), F32), pltpu.SemaphoreType.DMA((2,))],
    )
    return pl.pallas_call(
        _moe_dispatch_body,
        grid_spec=grid_spec,
        out_shape=jax.ShapeDtypeStruct((n_tiles * MOE_TILE, d), BF16),
        compiler_params=_params(("arbitrary",), 24),
        name="moe_dispatch",
    )(n_used, src3, src3, xn)


def _moe_expert_body(te_ref, first_ref, nxt_ref, par_ref, nu_ref, x_ref, wg_hbm, wu_hbm, wd_hbm,
                     y_ref, wg_buf, wu_buf, wd_buf, wsem, wgb, wub, wdb):
    i = pl.program_id(0)
    n_used = nu_ref[0]

    def weight_copies(e, ws):
        return (pltpu.make_async_copy(wg_hbm.at[e], wg_buf.at[ws], wsem.at[0, ws]),
                pltpu.make_async_copy(wu_hbm.at[e], wu_buf.at[ws], wsem.at[1, ws]),
                pltpu.make_async_copy(wd_hbm.at[e], wd_buf.at[ws], wsem.at[2, ws]))

    @pl.when(i == 0)
    def _():
        for cp in weight_copies(te_ref[0], par_ref[0]):
            cp.start()

    @pl.when(i < n_used)
    def _():
        @pl.when(first_ref[i] == 1)
        def _():
            ws = par_ref[i]
            for cp in weight_copies(te_ref[i], ws):
                cp.wait()

            @pl.when(nxt_ref[i] >= 0)
            def _():
                for cp in weight_copies(nxt_ref[i], 1 - ws):
                    cp.start()

            wgb[...] = wg_buf[ws].astype(BF16)
            wub[...] = wu_buf[ws].astype(BF16)
            wdb[...] = wd_buf[ws].astype(BF16)

        x = x_ref[...]
        hg = jnp.dot(x, wgb[...], preferred_element_type=F32)
        hu = jnp.dot(x, wub[...], preferred_element_type=F32)
        hid = (hg * jax.nn.sigmoid(hg)) * hu
        y_ref[...] = jnp.dot(hid.astype(BF16), wdb[...], preferred_element_type=F32)

    @pl.when(i >= n_used)
    def _():
        y_ref[...] = jnp.zeros(y_ref.shape, F32)


def moe_experts(tile_plan, xs, wg, wu, wd):
    rows, d = xs.shape
    n_tiles = rows // MOE_TILE
    _, _, de = wg.shape
    hbm = lambda: pl.BlockSpec(memory_space=pl.ANY)
    grid_spec = pltpu.PrefetchScalarGridSpec(
        num_scalar_prefetch=5,
        grid=(n_tiles,),
        in_specs=[pl.BlockSpec((MOE_TILE, d), lambda i, te, fi, nx, pa, nu: (jnp.minimum(i, nu[0] - 1), 0)),
                  hbm(), hbm(), hbm()],
        out_specs=pl.BlockSpec((MOE_TILE, d), lambda i, *_: (i, 0)),
        scratch_shapes=[pltpu.VMEM((2, d, de), F32), pltpu.VMEM((2, d, de), F32), pltpu.VMEM((2, de, d), F32),
                        pltpu.SemaphoreType.DMA((3, 2)),
                        pltpu.VMEM((d, de), BF16), pltpu.VMEM((d, de), BF16), pltpu.VMEM((de, d), BF16)],
    )
    return pl.pallas_call(
        _moe_expert_body,
        grid_spec=grid_spec,
        out_shape=jax.ShapeDtypeStruct((rows, d), F32),
        compiler_params=_params(("arbitrary",), 56),
        name="moe_experts",
    )(*tile_plan, xs, wg, wu, wd)


def _moe_combine_body(d1_cur, d1_nxt, d2_cur, d2_nxt, ys_hbm, xp_ref, xs_ref, info_ref, gf_ref, yp_ref, ysm_ref,
                      buf1, buf2, sem1, sem2, *, n_ptiles):
    i = pl.program_id(0)
    slot = lax.rem(i, 2)

    @pl.when(i == 0)
    def _():
        _issue_rows(d1_cur, ys_hbm, buf1, 0, sem1)
        _issue_rows(d2_cur, ys_hbm, buf2, 0, sem2)

    @pl.when(i + 1 < pl.num_programs(0))
    def _():
        _issue_rows(d1_nxt, ys_hbm, buf1, 1 - slot, sem1)
        _issue_rows(d2_nxt, ys_hbm, buf2, 1 - slot, sem2)

    _wait_rows(ys_hbm, buf1, slot, sem1)
    _wait_rows(ys_hbm, buf2, slot, sem2)

    def finish(x_ref, y_ref):
        info = info_ref[...]
        xo = x_ref[...] + (info[:, INFO_C1:INFO_C1 + 1] * buf1[slot] + info[:, INFO_C2:INFO_C2 + 1] * buf2[slot])
        var = jnp.mean(xo * xo, axis=-1, keepdims=True)
        y_ref[...] = (xo * lax.rsqrt(var + EPS)) * gf_ref[...]

    @pl.when(i < n_ptiles)
    def _():
        finish(xp_ref, yp_ref)

    @pl.when(i >= n_ptiles)
    def _():
        finish(xs_ref, ysm_ref)


def moe_combine(dest1, dest2, ys, x_prompt, x_sample, info, g_final):
    mp, d = x_prompt.shape
    n_ptiles = mp // MOE_TILE
    n_tiles = n_ptiles + 1
    cur = lambda: pl.BlockSpec((1, 1, MOE_TILE), lambda i: (i, 0, 0), memory_space=pltpu.SMEM)
    nxt = lambda: pl.BlockSpec((1, 1, MOE_TILE), lambda i: (jnp.minimum(i + 1, n_tiles - 1), 0, 0),
                               memory_space=pltpu.SMEM)
    ptile = lambda: pl.BlockSpec((MOE_TILE, d), lambda i: (jnp.minimum(i, n_ptiles - 1), 0))
    stile = lambda: pl.BlockSpec((MOE_TILE, d), lambda i: (0, 0))
    return pl.pallas_call(
        functools.partial(_moe_combine_body, n_ptiles=n_ptiles),
        grid=(n_tiles,),
        in_specs=[cur(), nxt(), cur(), nxt(), pl.BlockSpec(memory_space=pl.ANY), ptile(), stile(),
                  pl.BlockSpec((MOE_TILE, LANES), lambda i: (i, 0)), pl.BlockSpec((1, d), lambda i: (0, 0))],
        out_specs=[ptile(), stile()],
        out_shape=[jax.ShapeDtypeStruct((mp, d), F32), jax.ShapeDtypeStruct((MOE_TILE, d), F32)],
        scratch_shapes=[pltpu.VMEM((2, MOE_TILE, d), F32), pltpu.VMEM((2, MOE_TILE, d), F32),
                        pltpu.SemaphoreType.DMA((2,)), pltpu.SemaphoreType.DMA((2,))],
        compiler_params=_params(("arbitrary",), 40),
        name="moe_combine",
    )(dest1, dest1, dest2, dest2, ys, x_prompt, x_sample, info, g_final.reshape(1, d))


def _mixer_inputs(x2, g_mix, w_in_t, w_bg_t, w_mg_t, tm, q_dtype):
    tn = PROJ_TILE_N
    xn = rmsnorm(x2, g_mix, tm, BF16)
    mm = functools.partial(matmul_cols, xn, tm=tm, w_is_t=True)
    u_rnn = mm(w_in_t, 0, D_RNN, tn=tn, out_dtype=F32, name="in_rnn")
    u_gate = mm(w_in_t, D_RNN, D_RNN, tn=tn, out_dtype=F32, name="in_gate")
    q = mm(w_in_t, 2 * D_RNN, N_HEADS * HEAD_DIM, tn=tn, out_dtype=q_dtype, name="in_q")
    kv0 = 2 * D_RNN + N_HEADS * HEAD_DIM
    kv_cmp, = matmul_kv(xn, w_in_t, kv0, tm, False, "in_kv_cmp")
    kv_sel, sel_planes = matmul_kv(xn, w_in_t, kv0 + 2 * KV_DIM, tm, True, "in_kv_sel")
    kv_win, win_planes = matmul_kv(xn, w_in_t, kv0 + 4 * KV_DIM, tm, True, "in_kv_win")
    u_bg = mm(w_bg_t, 0, w_bg_t.shape[0], tn=w_bg_t.shape[0], out_dtype=F32, name="in_bg")
    u_mg = mm(w_mg_t, 0, 2 * D_MODEL, tn=tn, out_dtype=F32, name="in_mg")
    return u_rnn, u_gate, q, (kv_cmp, kv_sel, kv_win), (sel_planes, win_planes), u_bg, u_mg


def kernel(x_prompt, x_sample, cache_cmp_kv, cache_sel_kv, cache_win_kv, state_conv, state_rglru, page_table, g_mix, w_in, conv_w, conv_b, lru_w_a, lru_b_a, lru_w_x, lru_b_x, lru_lambda, cmp_pe, cmp_w1, cmp_b1, cmp_w2, cmp_b2, w_out, g_ffn, moe_w_group, moe_b_group, moe_w_expert, moe_b_expert, moe_w_gate, moe_w_up, moe_w_down, g_final):
    b, t, d = x_prompt.shape
    ns = x_sample.shape[0]
    npages = page_table.shape[1]
    past_len = npages * PAGE_SIZE
    wbuf = cache_win_kv.shape[1]
    kvshape = (N_KV_HEADS, 2, HEAD_DIM)

    w_in_t = w_in.T
    bg0 = 2 * D_RNN + N_HEADS * HEAD_DIM + 6 * KV_DIM
    n_bg = 3 * N_HEADS
    w_bg = w_in_t[bg0:bg0 + n_bg].reshape(3, N_KV_HEADS, GROUP, d).transpose(1, 0, 2, 3).reshape(N_KV_HEADS, 3 * GROUP, d)
    w_bg = jnp.pad(w_bg, ((0, 0), (0, LANES - 3 * GROUP), (0, 0))).reshape(N_KV_HEADS * LANES, d)
    w_mg = w_in_t[bg0 + n_bg:]
    wa = lru_w_a.astype(BF16)
    wx = lru_w_x.astype(BF16)
    half_rows = CMP_STRIDE * HEAD_DIM
    wc = jnp.concatenate([cmp_w1[:, :half_rows], cmp_w1[:, half_rows:]], axis=-1).astype(BF16)
    pe8 = jnp.pad(cmp_pe.reshape(2, 2, half_rows), ((0, 0), (0, 6), (0, 0)))
    b1 = cmp_b1.reshape(2, 1, CMP_HIDDEN)
    w2 = cmp_w2.astype(BF16)
    b2 = cmp_b2.reshape(2, 1, HEAD_DIM)
    n_route = N_GROUPS + N_EXPERTS
    wr = jnp.pad(jnp.concatenate([moe_w_group, moe_w_expert], axis=1), ((0, 0), (0, LANES - n_route))).astype(BF16)
    br = jnp.pad(jnp.concatenate([moe_b_group, moe_b_expert]), (0, LANES - n_route)).reshape(1, LANES)
    lru_args = (conv_w, conv_b, wa, wx, lru_b_a, lru_b_x, lru_lambda)
    cmp_args = (wc, pe8, b1, w2, b2)

    xp2 = x_prompt.reshape(b * t, d)
    u_rnn, u_gate, q, kv2d, planes, u_bg, u_mg = _mixer_inputs(xp2, g_mix, w_in_t, w_bg, w_mg, PROJ_TILE_M, BF16)
    o_rnn, p_conv, p_h = lru_prompt(u_rnn.reshape(b, t, d), u_gate.reshape(b, t, d), *lru_args, LRU_TILE_T)
    p_cmp, p_sel, p_win_full = (a.reshape((b, t) + kvshape) for a in kv2d)
    p_win = p_win_full[:, t - min(WINDOW, t):]
    pt_prompt = jnp.arange(b * t // PAGE_SIZE, dtype=I32).reshape(b, t // PAGE_SIZE)
    kvc_p = compress(kv2d[0], pt_prompt, jnp.zeros((b * CMP_STRIDE * HK, HEAD_DIM), F32), *cmp_args)
    merged = attn_prompt_s(q.reshape(b, t, -1), planes[0], planes[1], kvc_p, u_bg.reshape(b, t, -1),
                           u_mg.reshape(b, t, -1), o_rnn, t // CMP_STRIDE - 1)
    x_mid = matmul_cols(merged.reshape(b * t, d), w_out, 0, d, PROJ_TILE_M, PROJ_TILE_N, F32, residual=xp2,
                        name="out_proj")

    xs2 = x_sample.reshape(ns, d)
    su_rnn, su_gate, sq, skv2d, _, su_bg, su_mg = _mixer_inputs(xs2, g_mix, w_in_t, w_bg, w_mg, ns, F32)
    so_rnn, cn, s_h = lru_sample(su_rnn, su_gate, state_conv.transpose(1, 0, 2), state_rglru, *lru_args)
    s_conv = cn.transpose(1, 0, 2)
    kvs = jnp.stack([a.reshape(ns, HK, HEAD_DIM) for a in skv2d], axis=1)
    s_cmp = kvs[:, 0].reshape((ns, 1) + kvshape)
    s_sel = kvs[:, 1].reshape((ns, 1) + kvshape)

    extra = jnp.concatenate([kvs[:, 0], jnp.zeros((ns, (CMP_STRIDE - 1) * HK, HEAD_DIM), F32)], axis=1)
    kvc_s = compress(cache_cmp_kv.reshape(-1, HEAD_DIM), page_table, extra.reshape(-1, HEAD_DIM), *cmp_args)
    n_cmp_s = -(-(past_len + 1) // CMP_STRIDE) - 1
    n_sel_s = -(-(past_len + 1) // SEL_BLK)
    nsp = -(-n_sel_s // LANES) * LANES
    q16 = sq.reshape(ns, N_HEADS, HEAD_DIM)
    m_s = jnp.asarray(_overlap_matrix(n_cmp_s, kvc_s.shape[2], n_sel_s, nsp), BF16)
    so_cmp, imp = cmp_sample(q16, kvc_s, m_s, past_len, n_cmp_s)
    idx = rank_sample(imp.reshape(ns * N_KV_HEADS, nsp), past_len // SEL_BLK, n_sel_s)

    def per_head_rows(new_kv, kv):
        return jnp.repeat(new_kv[:, kv::2], GROUP, axis=1)

    so_sel = sel_sample(idx, page_table, q16, per_head_rows(kvs[:, 1], 0), per_head_rows(kvs[:, 1], 1),
                        cache_sel_kv.reshape((-1,) + kvshape), past_len // SEL_BLK)
    bgs = su_bg.reshape(ns, N_KV_HEADS, LANES)[:, :, :3 * GROUP].reshape(ns, N_KV_HEADS, 3, GROUP)
    bgs = jnp.pad(bgs.transpose(0, 1, 3, 2).reshape(ns, N_HEADS, 3), ((0, 0), (0, 0), (0, LANES - 3)))
    so_nsa, s_win2d = win_sample(q16, cache_win_kv.reshape(-1, HEAD_DIM), skv2d[2], per_head_rows(kvs[:, 2], 0),
                                 per_head_rows(kvs[:, 2], 1), so_cmp, so_sel, bgs, wbuf)
    s_win = s_win2d.reshape((ns, wbuf) + kvshape)
    s_merged = merge(su_mg, so_rnn, so_nsa.reshape(ns, d), ns)
    sx_mid = matmul_cols(s_merged, w_out, 0, d, ns, PROJ_TILE_N, F32, residual=xs2, name="out_proj_s")

    n_real = b * t + ns
    assert ns <= MOE_TILE
    sx_pad = jnp.pad(sx_mid, ((0, MOE_TILE - ns), (0, 0)))
    xn_all, info, counts = route(x_mid, sx_pad, g_ffn, wr, br, n_real)
    n_tiles = -(-(2 * n_real + N_EXPERTS * (MOE_TILE - 1)) // MOE_TILE)
    src3, dest1, dest2, tile_plan = _moe_plan(info, counts, n_real, n_tiles)
    xs = moe_dispatch(tile_plan[-1], src3, xn_all)
    ys = moe_experts(tile_plan, xs, moe_w_gate, moe_w_up, moe_w_down)
    y_p2, y_s2 = moe_combine(dest1, dest2, ys, x_mid, sx_pad, info, g_final)
    y_prompt = y_p2.reshape(b, t, d)
    y_sample = y_s2[:ns].reshape(ns, 1, d)

    return (y_prompt, y_sample, p_cmp, s_cmp, p_sel, s_sel, p_win, s_win, p_conv, s_conv,
            p_h.reshape(b, d), s_h)
```

```python
import functools

import jax
import jax.numpy as jnp
import numpy as np
from jax import lax
from jax.experimental import pallas as pl
from jax.experimental.pallas import tpu as pltpu

F32 = jnp.float32
BF16 = jnp.bfloat16
I32 = jnp.int32

D_MODEL = 2048
D_RNN = 2048
RNN_BLOCKS = 16
RNN_BLOCK_DIM = 128
CONV_W = 4
LRU_C = 8.0
N_HEADS = 16
HEAD_DIM = 128
N_KV_HEADS = 4
GROUP = 4
KV_DIM = N_KV_HEADS * HEAD_DIM
HK = 2 * N_KV_HEADS
CMP_STRIDE = 16
CMP_BLK = 32
CMP_HIDDEN = 256
SEL_BLK = 64
SEL_SHIFT = 6
N_SEL = 16
WINDOW = 512
PAGE_SIZE = 128
ATTN_SCALE = HEAD_DIM ** -0.5
FORCED_SCORE = 1e4
N_GROUPS = 4
EXP_PER_GROUP = 8
N_EXPERTS = 32
D_EXPERT = 512
EPS = 1e-6
MASK_VALUE = -1e30
LANES = 128
MIB = 1024 * 1024

PROJ_TILE_M = 1024
PROJ_TILE_N = 1024
LRU_TILE_T = 256

_NT = (((1,), (1,)), ((), ()))


def _params(sem, vmem_mib):
    return pltpu.CompilerParams(dimension_semantics=sem, vmem_limit_bytes=vmem_mib * MIB)


def _msoftmax(s, mask):
    sm = jnp.where(mask, s, MASK_VALUE)
    m = jnp.max(sm, axis=-1, keepdims=True)
    e = jnp.where(mask, jnp.exp(sm - m), 0.0)
    den = jnp.sum(e, axis=-1, keepdims=True)
    return e * jnp.where(den > 0.0, 1.0 / den, 0.0)


def _gelu_tanh(x):
    return 0.5 * x * (1.0 + jnp.tanh(0.7978845608028654 * (x + 0.044715 * (x * x * x))))


def _rmsnorm_body(x_ref, g_ref, o_ref):
    x = x_ref[...]
    var = jnp.mean(x * x, axis=-1, keepdims=True)
    o_ref[...] = ((x * lax.rsqrt(var + EPS)) * g_ref[...]).astype(o_ref.dtype)


def rmsnorm(x, g, tm, out_dtype):
    m, d = x.shape
    return pl.pallas_call(
        _rmsnorm_body,
        grid=(m // tm,),
        in_specs=[pl.BlockSpec((tm, d), lambda i: (i, 0)), pl.BlockSpec((1, d), lambda i: (0, 0))],
        out_specs=pl.BlockSpec((tm, d), lambda i: (i, 0)),
        out_shape=jax.ShapeDtypeStruct((m, d), out_dtype),
        compiler_params=_params(("arbitrary",), 56),
        name="rmsnorm",
    )(x, g.reshape(1, d))


def _xw(x, wbf, w_is_t):
    if w_is_t:
        return lax.dot_general(x, wbf, _NT, preferred_element_type=F32)
    return jnp.dot(x, wbf, preferred_element_type=F32)


def _mm_body(x_ref, w_ref, *rest, w_is_t):
    r_ref = rest[0] if len(rest) == 3 else None
    o_ref, wbf_ref = rest[-2:]

    @pl.when(pl.program_id(1) == 0)
    def _():
        wbf_ref[...] = w_ref[...].astype(BF16)

    res = _xw(x_ref[...], wbf_ref[...], w_is_t)
    if r_ref is not None:
        res = r_ref[...] + res
    o_ref[...] = res.astype(o_ref.dtype)


def matmul_cols(x, w, col0, ncols, tm, tn, out_dtype, residual=None, w_is_t=False, name="matmul"):
    m, k = x.shape
    cb0 = col0 // tn
    assert col0 % tn == 0 and ncols % tn == 0 and m % tm == 0
    if w_is_t:
        w_spec = pl.BlockSpec((tn, k), lambda j, i: (cb0 + j, 0))
        w_tile = (tn, k)
    else:
        w_spec = pl.BlockSpec((k, tn), lambda j, i: (0, cb0 + j))
        w_tile = (k, tn)
    in_specs = [pl.BlockSpec((tm, k), lambda j, i: (i, 0)), w_spec]
    args = [x, w]
    if residual is not None:
        in_specs.append(pl.BlockSpec((tm, tn), lambda j, i: (i, j)))
        args.append(residual)
    return pl.pallas_call(
        functools.partial(_mm_body, w_is_t=w_is_t),
        grid=(ncols // tn, m // tm),
        in_specs=in_specs,
        out_specs=pl.BlockSpec((tm, tn), lambda j, i: (i, j)),
        out_shape=jax.ShapeDtypeStruct((m, ncols), out_dtype),
        scratch_shapes=[pltpu.VMEM(w_tile, BF16)],
        compiler_params=_params(("arbitrary", "arbitrary"), 56),
        name=name,
    )(*args)


def _mm_kv_body(x_ref, w_ref, oi_ref, *rest):
    wbf_ref = rest[-1]

    @pl.when(pl.program_id(0) == 0)
    def _():
        wbf_ref[...] = w_ref[...].astype(BF16)

    res = _xw(x_ref[...], wbf_ref[...], True)
    tm = res.shape[0]
    for hk in range(HK):
        blk = res[:, hk * HEAD_DIM:(hk + 1) * HEAD_DIM]
        oi_ref[pl.ds(hk, tm, stride=HK), :] = blk
        if len(rest) == 2:
            rest[0][hk] = blk.astype(BF16)


def matmul_kv(x, w_t, col0, tm, planes, name):
    m, k = x.shape
    ncols = HK * HEAD_DIM
    assert col0 % ncols == 0 and m % tm == 0
    out_specs = [pl.BlockSpec((tm * HK, HEAD_DIM), lambda i: (i, 0))]
    out_shape = [jax.ShapeDtypeStruct((m * HK, HEAD_DIM), F32)]
    if planes:
        out_specs.append(pl.BlockSpec((HK, tm, HEAD_DIM), lambda i: (0, i, 0)))
        out_shape.append(jax.ShapeDtypeStruct((HK, m, HEAD_DIM), BF16))
    return pl.pallas_call(
        _mm_kv_body,
        grid=(m // tm,),
        in_specs=[pl.BlockSpec((tm, k), lambda i: (i, 0)), pl.BlockSpec((ncols, k), lambda i: (col0 // ncols, 0))],
        out_specs=out_specs,
        out_shape=out_shape,
        scratch_shapes=[pltpu.VMEM((ncols, k), BF16)],
        compiler_params=_params(("arbitrary",), 56),
        name=name,
    )(x, w_t)


def _merge_body(mg_ref, rnn_ref, nsa_ref, o_ref):
    d = rnn_ref.shape[-1]
    ga = jax.nn.sigmoid(mg_ref[:, :d])
    gb = jax.nn.sigmoid(mg_ref[:, d:])
    o_ref[...] = (ga * rnn_ref[...] + gb * nsa_ref[...]).astype(o_ref.dtype)


def merge(u_mg, o_rnn, o_nsa, tm):
    m, d = o_rnn.shape
    return pl.pallas_call(
        _merge_body,
        grid=(m // tm,),
        in_specs=[pl.BlockSpec((tm, 2 * d), lambda i: (i, 0)), pl.BlockSpec((tm, d), lambda i: (i, 0)),
                  pl.BlockSpec((tm, d), lambda i: (i, 0))],
        out_specs=pl.BlockSpec((tm, d), lambda i: (i, 0)),
        out_shape=jax.ShapeDtypeStruct((m, d), BF16),
        compiler_params=_params(("arbitrary",), 48),
        name="merge",
    )(u_mg, o_rnn, o_nsa)


def _lru_gates(uc, wa_ref, wx_ref, ba_ref, bx_ref, lam_ref):
    ucb = uc.astype(BF16)
    ra, ri = [], []
    for n in range(RNN_BLOCKS):
        blk = ucb[:, n * RNN_BLOCK_DIM:(n + 1) * RNN_BLOCK_DIM]
        ra.append(jnp.dot(blk, wa_ref[n], preferred_element_type=F32))
        ri.append(jnp.dot(blk, wx_ref[n], preferred_element_type=F32))
    r = jax.nn.sigmoid(jnp.concatenate(ra, axis=-1) + ba_ref[...])
    i = jax.nn.sigmoid(jnp.concatenate(ri, axis=-1) + bx_ref[...])
    z = -lam_ref[...]
    softplus = jnp.maximum(z, 0.0) + jnp.log1p(jnp.exp(-jnp.abs(z)))
    log_a = (-LRU_C * r) * softplus
    a = jnp.exp(log_a)
    mult = jnp.sqrt(-jnp.tanh(log_a) * (a * a + 1.0))
    return a, mult * (i * uc)


def _lru_prompt_body(u_ref, gate_ref, cw_ref, cb_ref, wa_ref, wx_ref, ba_ref, bx_ref, lam_ref,
                     o_ref, conv_ref, hlast_ref, xbuf, hcar, acum, bcum):
    t = pl.program_id(1)
    nt = pl.num_programs(1)
    tt, d = u_ref.shape[1], u_ref.shape[2]

    @pl.when(t == 0)
    def _():
        xbuf[0:8, :] = jnp.zeros((8, d), F32)
        hcar[...] = jnp.zeros((8, d), F32)

    x = u_ref[0]
    xbuf[8:8 + tt, :] = x
    uc = (cw_ref[0:1, :] * xbuf[5:5 + tt, :] + cw_ref[1:2, :] * xbuf[6:6 + tt, :]
          + cw_ref[2:3, :] * xbuf[7:7 + tt, :] + cw_ref[3:4, :] * x) + cb_ref[...]
    xbuf[0:8, :] = xbuf[tt:tt + 8, :]

    a, b = _lru_gates(uc, wa_ref, wx_ref, ba_ref, bx_ref, lam_ref)
    g = tt // 8
    a3 = a.reshape(g, 8, d)
    b3 = b.reshape(g, 8, d)
    row = lax.broadcasted_iota(I32, (g, 8, d), 1)
    for s in (1, 2, 4):
        a_sh = pltpu.roll(a3, s, axis=1)
        b_sh = pltpu.roll(b3, s, axis=1)
        keep = row >= s
        b3 = jnp.where(keep, a3 * b_sh + b3, b3)
        a3 = jnp.where(keep, a3 * a_sh, a3)
    acum[...] = a3
    bcum[...] = b3

    def body(gi, h):
        hg = bcum[gi] + acum[gi] * h
        bcum[gi] = hg
        return jnp.broadcast_to(hg[7:8, :], (8, d))

    hfin = lax.fori_loop(0, g, body, hcar[...])
    hcar[...] = hfin
    o_ref[0] = bcum[...].reshape(tt, d) * _gelu_tanh(gate_ref[0])

    @pl.when(t == nt - 1)
    def _():
        conv_ref[0] = x[tt - (CONV_W - 1):tt, :]
        hlast_ref[0] = hfin[0:1, :]


def lru_prompt(u_rnn, u_gate, conv_w, conv_b, wa, wx, ba, bx, lam, tt):
    b, t, d = u_rnn.shape
    vec = lambda: pl.BlockSpec((1, d), lambda i, j: (0, 0))
    wspec = lambda: pl.BlockSpec((RNN_BLOCKS, RNN_BLOCK_DIM, RNN_BLOCK_DIM), lambda i, j: (0, 0, 0))
    return pl.pallas_call(
        _lru_prompt_body,
        grid=(b, t // tt),
        in_specs=[pl.BlockSpec((1, tt, d), lambda i, j: (i, j, 0)), pl.BlockSpec((1, tt, d), lambda i, j: (i, j, 0)),
                  pl.BlockSpec((CONV_W, d), lambda i, j: (0, 0)), vec(), wspec(), wspec(), vec(), vec(), vec()],
        out_specs=[pl.BlockSpec((1, tt, d), lambda i, j: (i, j, 0)),
                   pl.BlockSpec((1, CONV_W - 1, d), lambda i, j: (i, 0, 0)),
                   pl.BlockSpec((1, 1, d), lambda i, j: (i, 0, 0))],
        out_shape=[jax.ShapeDtypeStruct((b, t, d), F32), jax.ShapeDtypeStruct((b, CONV_W - 1, d), F32),
                   jax.ShapeDtypeStruct((b, 1, d), F32)],
        scratch_shapes=[pltpu.VMEM((tt + 8, d), F32), pltpu.VMEM((8, d), F32),
                        pltpu.VMEM((tt // 8, 8, d), F32), pltpu.VMEM((tt // 8, 8, d), F32)],
        compiler_params=_params(("arbitrary", "arbitrary"), 56),
        name="lru_prompt",
    )(u_rnn, u_gate, conv_w, conv_b.reshape(1, d), wa, wx, ba.reshape(1, d), bx.reshape(1, d), lam.reshape(1, d))


def _lru_sample_body(u_ref, gate_ref, cp_ref, h0_ref, cw_ref, cb_ref, wa_ref, wx_ref, ba_ref, bx_ref, lam_ref,
                     o_ref, cn_ref, h_ref):
    x = u_ref[...]
    uc = (cw_ref[0:1, :] * cp_ref[0] + cw_ref[1:2, :] * cp_ref[1] + cw_ref[2:3, :] * cp_ref[2]
          + cw_ref[3:4, :] * x) + cb_ref[...]
    a, b = _lru_gates(uc, wa_ref, wx_ref, ba_ref, bx_ref, lam_ref)
    h = a * h0_ref[...] + b
    o_ref[...] = h * _gelu_tanh(gate_ref[...])
    h_ref[...] = h
    cn_ref[0] = cp_ref[1]
    cn_ref[1] = cp_ref[2]
    cn_ref[2] = x


def lru_sample(u_rnn, u_gate, conv_prev_t, h0, conv_w, conv_b, wa, wx, ba, bx, lam):
    n, d = u_rnn.shape
    return pl.pallas_call(
        _lru_sample_body,
        out_shape=[jax.ShapeDtypeStruct((n, d), F32), jax.ShapeDtypeStruct((CONV_W - 1, n, d), F32),
                   jax.ShapeDtypeStruct((n, d), F32)],
        compiler_params=pltpu.CompilerParams(vmem_limit_bytes=40 * MIB),
        name="lru_sample",
    )(u_rnn, u_gate, conv_prev_t, h0, conv_w, conv_b.reshape(1, d), wa, wx, ba.reshape(1, d), bx.reshape(1, d),
      lam.reshape(1, d))


COMPRESS_PAGES = 32
CHUNKS_PER_PAGE = PAGE_SIZE // CMP_STRIDE


def _compress_body(pt_ref, *refs, pp, n_steps):
    page_refs = refs[:pp]
    nxt_ref, extra_ref, wc_ref, pe_ref, b1_ref, w2_ref, b2_ref, o_ref, res_scr, pet_scr = refs[pp:]
    del pt_ref
    is_last = pl.program_id(1) == n_steps - 1
    m_rows = pp * CHUNKS_PER_PAGE * N_KV_HEADS
    lo4 = lax.broadcasted_iota(I32, (4, 8, 8, LANES), 2) < 4
    lo3 = lax.broadcasted_iota(I32, (8, 8, LANES), 1) < 4

    for k in range(2):
        lhs_sp = [[] for _ in range(8)]
        for pr in page_refs:
            xk = pr[pl.ds(k, PAGE_SIZE * N_KV_HEADS, stride=2), :]
            x5 = xk.reshape(4, 2, 8, 8, LANES)
            a0 = x5[:, 0]
            a1 = x5[:, 1]
            be = jnp.where(lo4, a0, pltpu.roll(a1, 4, axis=2))
            bo = jnp.where(lo4, pltpu.roll(a0, 4, axis=2), a1)
            for sp in range(8):
                lhs_sp[sp].append(jnp.concatenate([be[:, sp], bo[:, sp]], axis=-1).reshape(32, 2 * LANES))
        acc = jnp.zeros((m_rows, 2 * CMP_HIDDEN), F32)
        for sp in range(8):
            lhs = jnp.concatenate(lhs_sp[sp], axis=0).astype(BF16)
            acc = acc + jnp.dot(lhs, wc_ref[k, sp * 256:(sp + 1) * 256, :], preferred_element_type=F32)

        @pl.when((pl.program_id(0) == 0) & (pl.program_id(1) == 0))
        def _():
            pet_scr[k] = jnp.dot(pe_ref[k].astype(BF16), wc_ref[k], preferred_element_type=F32)

        peb0 = pet_scr[k, 0:1, :CMP_HIDDEN]
        peb1 = pet_scr[k, 1:2, CMP_HIDDEN:]
        h0 = acc[:, :CMP_HIDDEN] + peb0
        h1 = acc[:, CMP_HIDDEN:] + peb1

        xn = jnp.where(is_last, extra_ref[pl.ds(k, CMP_STRIDE * N_KV_HEADS, stride=2), :],
                       nxt_ref[pl.ds(k, CMP_STRIDE * N_KV_HEADS, stride=2), :])
        xn3 = xn.reshape(8, 8, LANES)
        ln = jnp.concatenate([jnp.where(lo3, xn3, 0.0), jnp.where(lo3, 0.0, xn3)], axis=-1).astype(BF16)
        nacc = jnp.zeros((8, CMP_HIDDEN), F32)
        for sp in range(8):
            nacc = nacc + jnp.dot(ln[sp], wc_ref[k, sp * 256:(sp + 1) * 256, CMP_HIDDEN:],
                                  preferred_element_type=F32)
        n8 = nacc + pltpu.roll(nacc, 4, axis=0) + peb1
        h1e = jnp.concatenate([h1, n8], axis=0)
        h1s = pltpu.roll(h1e, m_rows + 8 - N_KV_HEADS, axis=0)[:m_rows]
        pre = h0 + h1s + b1_ref[k]
        out_k = jnp.dot(_gelu_tanh(pre).astype(BF16), w2_ref[k], preferred_element_type=F32) + b2_ref[k]
        res_scr[...] = out_k
        for h in range(N_KV_HEADS):
            o_ref[0, 2 * h + k] = res_scr[pl.ds(h, m_rows // N_KV_HEADS, stride=N_KV_HEADS), :]


def compress(pool2d, page_table, extra2d, wc, pe8, b1, w2, b2):
    nseq, npages = page_table.shape
    pp = min(COMPRESS_PAGES, npages)
    n_steps = npages // pp
    assert npages % pp == 0
    rows_page = PAGE_SIZE * HK
    rows_chunk = CMP_STRIDE * HK
    blocks_step = pp * CHUNKS_PER_PAGE

    def page_spec(i):
        return pl.BlockSpec((rows_page, LANES), lambda b, s, pt: (pt[b, s * pp + i], 0))

    def nxt_map(b, s, pt):
        return (pt[b, jnp.minimum((s + 1) * pp, npages - 1)] * CHUNKS_PER_PAGE, 0)

    const3 = lambda shape: pl.BlockSpec(shape, lambda b, s, pt: (0, 0, 0))
    in_specs = [page_spec(i) for i in range(pp)] + [
        pl.BlockSpec((rows_chunk, LANES), nxt_map),
        pl.BlockSpec((rows_chunk, LANES), lambda b, s, pt: (b, 0)),
        const3((2, CMP_STRIDE * HEAD_DIM, 2 * CMP_HIDDEN)),
        const3((2, 8, CMP_STRIDE * HEAD_DIM)),
        const3((2, 1, CMP_HIDDEN)),
        const3((2, CMP_HIDDEN, HEAD_DIM)),
        const3((2, 1, HEAD_DIM)),
    ]
    grid_spec = pltpu.PrefetchScalarGridSpec(
        num_scalar_prefetch=1,
        grid=(nseq, n_steps),
        in_specs=in_specs,
        out_specs=pl.BlockSpec((1, HK, blocks_step, HEAD_DIM), lambda b, s, pt: (b, 0, s, 0)),
        scratch_shapes=[pltpu.VMEM((blocks_step * N_KV_HEADS, HEAD_DIM), F32),
                        pltpu.VMEM((2, 8, 2 * CMP_HIDDEN), F32)],
    )
    return pl.pallas_call(
        functools.partial(_compress_body, pp=pp, n_steps=n_steps),
        grid_spec=grid_spec,
        out_shape=jax.ShapeDtypeStruct((nseq, HK, npages * CHUNKS_PER_PAGE, HEAD_DIM), F32),
        compiler_params=_params(("arbitrary", "arbitrary"), 56),
        name="compress",
    )(page_table, *([pool2d] * pp), pool2d, extra2d, wc, pe8, b1, w2, b2)


def _overlap_matrix(n_cmp, n_cmp_pad, n_sel, n_sel_pad):
    cs = np.arange(n_cmp_pad)[:, None] * CMP_STRIDE
    ss = np.arange(n_sel_pad)[None, :] * SEL_BLK
    ov = np.minimum(cs + CMP_BLK, ss + SEL_BLK) - np.maximum(cs, ss)
    m = np.clip(ov, 0, CMP_BLK).astype(np.float32) / CMP_BLK
    m[n_cmp:, :] = 0.0
    m[:, n_sel:] = 0.0
    return m


QBLK = 128


KCHUNK = 64
KPIECE = 512
LOG2E = 1.4426950408889634


def _softmax_passes(s_scr, p_scr, n_plain, n_masked, n_total, plain_mask, full_mask):
    cols = s_scr.shape[1]
    grp = KCHUNK // 8
    c2 = ATTN_SCALE * LOG2E

    def rows(c):
        return pl.ds(pl.multiple_of(c * KCHUNK, KCHUNK), KCHUNK)

    def masked_max(mask_fn):
        def body(c, m8):
            sm = jnp.where(mask_fn(c), s_scr[rows(c), :], MASK_VALUE)
            s_scr[rows(c), :] = sm
            return jnp.maximum(m8, jnp.max(sm.reshape(grp, 8, cols), axis=0))
        return body

    m8 = lax.fori_loop(0, n_plain, masked_max(plain_mask), jnp.full((8, cols), MASK_VALUE, F32))
    m8 = lax.fori_loop(n_plain, n_plain + n_masked, masked_max(full_mask), m8)
    mc = jnp.max(m8, axis=0, keepdims=True) * c2

    def pass_exp(c, l8):
        p = jnp.exp2(s_scr[rows(c), :] * c2 - mc)
        p_scr[rows(c), :] = p.astype(BF16)
        return l8 + jnp.sum(p.reshape(grp, 8, cols), axis=0)

    l8 = lax.fori_loop(0, n_plain + n_masked, pass_exp, jnp.zeros((8, cols), F32))

    def zero_fill(c, carry):
        p_scr[rows(c), :] = jnp.zeros((KCHUNK, cols), BF16)
        return carry

    lax.fori_loop(n_plain + n_masked, n_total, zero_fill, 0)
    return 1.0 / jnp.sum(l8, axis=0, keepdims=True)


def _attn_prompt_s_body(q_ref, ksel_ref, vsel_ref, kwin_ref, vwin_ref, kvc_ref, bg_ref, mga_ref, mgb_ref, rnn_ref,
                        mt_ref, o_ref, vsel_t, vwin_t, kc_bf, vc_t_bf, s_scr, p_scr, sel_scr, *, n_cmp, n_sel):
    qi = pl.program_id(2)
    q0 = qi * QBLK
    cols = GROUP * QBLK

    @pl.when(qi == 0)
    def _():
        vsel_t[...] = vsel_ref[0].astype(F32).T.astype(BF16)
        vwin_t[...] = vwin_ref[0].astype(F32).T.astype(BF16)
        kc_bf[...] = kvc_ref[0, 0].astype(BF16)
        vc_t_bf[...] = kvc_ref[0, 1].T.astype(BF16)

    q = q_ref[0]
    q4 = jnp.concatenate([q[:, g * HEAD_DIM:(g + 1) * HEAD_DIM] for g in range(GROUP)], axis=0)
    qpos_row = q0 + (lax.broadcasted_iota(I32, (1, cols), 1) & (QBLK - 1))
    row_in_chunk = lax.broadcasted_iota(I32, (KCHUNK, cols), 0)

    nb = kvc_ref.shape[2]
    kc = kc_bf[...]
    vc_t = vc_t_bf[...]
    n_io = lax.broadcasted_iota(I32, (nb, cols), 0)
    mask_c = (n_io * CMP_STRIDE + (CMP_BLK - 1) <= qpos_row) & (n_io < n_cmp)
    sc = jnp.where(mask_c, lax.dot_general(kc, q4, _NT, preferred_element_type=F32) * ATTN_SCALE, MASK_VALUE)
    ec = jnp.where(mask_c, jnp.exp(sc - jnp.max(sc, axis=0, keepdims=True)), 0.0)
    den_c = jnp.sum(ec, axis=0, keepdims=True)
    pc = ec * jnp.where(den_c > 0.0, 1.0 / den_c, 0.0)
    o_c = jnp.dot(vc_t, pc.astype(BF16), preferred_element_type=F32)
    psum = pc[:, 0:QBLK] + pc[:, QBLK:2 * QBLK] + pc[:, 2 * QBLK:3 * QBLK] + pc[:, 3 * QBLK:4 * QBLK]

    nsp = mt_ref.shape[0]
    imp_t = jnp.dot(mt_ref[...], psum.astype(BF16), preferred_element_type=F32)
    jt = lax.broadcasted_iota(I32, (nsp, QBLK), 0)
    cur = lax.shift_right_logical(q0 + lax.broadcasted_iota(I32, (nsp, QBLK), 1), SEL_SHIFT)
    valid = (jt <= cur) & (jt < n_sel)
    forced = (jt == 0) | (jt >= cur - 1)
    score = jnp.where(forced, FORCED_SCORE, imp_t)
    score = jnp.where(valid, score, -1.0)
    rank = jnp.zeros((nsp, QBLK), I32)
    for k in range(n_sel):
        sk = score[k:k + 1, :]
        beats = (sk > score) | ((sk == score) & (k < jt))
        rank = rank + beats.astype(I32)
    sel_t = jnp.where((rank < N_SEL) & valid, 1.0, 0.0)
    sel_scr[...] = jnp.concatenate([sel_t] * GROUP, axis=1)

    wkeys = WINDOW + QBLK
    ks = pl.multiple_of(jnp.maximum(q0 - WINDOW, 0), QBLK)
    s_scr[0:wkeys, :] = lax.dot_general(kwin_ref[0, pl.ds(ks, wkeys), :], q4, _NT, preferred_element_type=F32)

    def win_mask(c):
        dist = qpos_row - (ks + c * KCHUNK + row_in_chunk)
        return (dist >= 0) & (dist < WINDOW)

    n_wchunks = wkeys // KCHUNK
    inv_w = _softmax_passes(s_scr, p_scr, 0, n_wchunks, n_wchunks, win_mask, win_mask)
    o_w = jnp.dot(vwin_t[:, pl.ds(ks, wkeys)], p_scr[0:wkeys, :], preferred_element_type=F32) * inv_w

    n_pieces = qi // (KPIECE // QBLK) + 1

    def score_piece(c, carry):
        r0 = pl.multiple_of(c * KPIECE, KPIECE)
        s_scr[pl.ds(r0, KPIECE), :] = lax.dot_general(ksel_ref[0, pl.ds(r0, KPIECE), :], q4, _NT,
                                                      preferred_element_type=F32)
        return carry

    lax.fori_loop(0, n_pieces, score_piece, 0)

    def picked(c):
        return sel_scr[pl.ds(c, 1), :] > 0.5

    def picked_causal(c):
        return picked(c) & (c * KCHUNK + row_in_chunk <= qpos_row)

    chunks_q = QBLK // KCHUNK
    inv_s = _softmax_passes(s_scr, p_scr, qi * chunks_q, chunks_q, n_pieces * (KPIECE // KCHUNK),
                            picked, picked_causal)

    def pv_piece(c, acc):
        r0 = pl.multiple_of(c * KPIECE, KPIECE)
        return acc + jnp.dot(vsel_t[:, pl.ds(r0, KPIECE)], p_scr[pl.ds(r0, KPIECE), :], preferred_element_type=F32)

    o_s = lax.fori_loop(0, n_pieces, pv_piece, jnp.zeros((HEAD_DIM, cols), F32)) * inv_s

    gates_t = jax.nn.sigmoid(bg_ref[0]).T
    grow = lambda br: jnp.concatenate([gates_t[br * GROUP + g:br * GROUP + g + 1, :] for g in range(GROUP)], axis=1)
    o_t = grow(0) * o_c + grow(1) * o_s + grow(2) * o_w
    for g in range(GROUP):
        cs = slice(g * HEAD_DIM, (g + 1) * HEAD_DIM)
        o_nsa = o_t[:, g * QBLK:(g + 1) * QBLK].T
        merged = jax.nn.sigmoid(mga_ref[0, :, cs]) * rnn_ref[0, :, cs] + jax.nn.sigmoid(mgb_ref[0, :, cs]) * o_nsa
        o_ref[0, :, cs] = merged.astype(o_ref.dtype)


def attn_prompt_s(q3, sel_planes, win_planes, kvc, ubg3, umg3, o_rnn3, n_cmp):
    b, t, _ = q3.shape
    nb = kvc.shape[2]
    assert KCHUNK == SEL_BLK and t % KPIECE == 0 and t >= WINDOW + QBLK
    n_sel = -(-t // SEL_BLK)
    nsp = max(8, -(-n_sel // 8) * 8)
    m_t = _overlap_matrix(n_cmp, nb, n_sel, nsp).T
    kspec = lambda: pl.BlockSpec((1, t, HEAD_DIM), lambda i, h, j: (2 * h, i, 0))
    vspec = lambda: pl.BlockSpec((1, t, HEAD_DIM), lambda i, h, j: (2 * h + 1, i, 0))
    cols = GROUP * QBLK
    head_cols = GROUP * HEAD_DIM
    tile = lambda off: pl.BlockSpec((1, QBLK, head_cols), lambda i, h, j: (i, j, off + h))
    return pl.pallas_call(
        functools.partial(_attn_prompt_s_body, n_cmp=n_cmp, n_sel=n_sel),
        grid=(b, N_KV_HEADS, t // QBLK),
        in_specs=[
            tile(0),
            kspec(), vspec(), kspec(), vspec(),
            pl.BlockSpec((1, 2, nb, HEAD_DIM), lambda i, h, j: (i, h, 0, 0)),
            pl.BlockSpec((1, QBLK, LANES), lambda i, h, j: (i, j, h)),
            tile(0), tile(N_KV_HEADS), tile(0),
            pl.BlockSpec((nsp, nb), lambda i, h, j: (0, 0)),
        ],
        out_specs=tile(0),
        out_shape=jax.ShapeDtypeStruct((b, t, N_HEADS * HEAD_DIM), BF16),
        scratch_shapes=[pltpu.VMEM((HEAD_DIM, t), BF16), pltpu.VMEM((HEAD_DIM, t), BF16),
                        pltpu.VMEM((nb, HEAD_DIM), BF16), pltpu.VMEM((HEAD_DIM, nb), BF16),
                        pltpu.VMEM((t, cols), F32), pltpu.VMEM((t, cols), BF16), pltpu.VMEM((nsp, cols), F32)],
        compiler_params=_params(("arbitrary", "arbitrary", "arbitrary"), 48),
        name="attn_prompt",
    )(q3, sel_planes, sel_planes, win_planes, win_planes, kvc, ubg3, umg3, umg3, o_rnn3, jnp.asarray(m_t, BF16))


def _row_head(shape):
    return lax.shift_right_logical(lax.broadcasted_iota(I32, shape, 0), 2)


def _cmp_sample_body(q_ref, kvc_ref, m_ref, oc_ref, imp_ref, *, qpos, n_cmp):
    qb = q_ref[0].astype(BF16)
    nb = kvc_ref.shape[2]
    rg = _row_head((N_HEADS, nb))
    ncol = lax.broadcasted_iota(I32, (N_HEADS, nb), 1)
    mask = (ncol * CMP_STRIDE + (CMP_BLK - 1) <= qpos) & (ncol < n_cmp)
    row8 = lax.broadcasted_iota(I32, (8, nb), 0)
    o = jnp.zeros((N_HEADS, HEAD_DIM), F32)
    ps8 = jnp.zeros((8, nb), F32)
    for h in range(N_KV_HEADS):
        kc = kvc_ref[0, 2 * h].astype(BF16)
        vc = kvc_ref[0, 2 * h + 1].astype(BF16)
        s = lax.dot_general(qb, kc, _NT, preferred_element_type=F32) * ATTN_SCALE
        ph = jnp.where(rg == h, _msoftmax(s, mask), 0.0)
        o = o + jnp.dot(ph.astype(BF16), vc, preferred_element_type=F32)
        ps8 = jnp.where(row8 == h, jnp.sum(ph, axis=0, keepdims=True), ps8)
    oc_ref[0] = o
    imp_ref[0] = jnp.dot(ps8.astype(BF16), m_ref[...], preferred_element_type=F32)[0:N_KV_HEADS]


def cmp_sample(q16, kvc, m_mat, qpos, n_cmp):
    ns = q16.shape[0]
    nb = kvc.shape[2]
    nsp = m_mat.shape[1]
    return pl.pallas_call(
        functools.partial(_cmp_sample_body, qpos=qpos, n_cmp=n_cmp),
        grid=(ns,),
        in_specs=[pl.BlockSpec((1, N_HEADS, HEAD_DIM), lambda b: (b, 0, 0)),
                  pl.BlockSpec((1, HK, nb, HEAD_DIM), lambda b: (b, 0, 0, 0)),
                  pl.BlockSpec((nb, nsp), lambda b: (0, 0))],
        out_specs=[pl.BlockSpec((1, N_HEADS, HEAD_DIM), lambda b: (b, 0, 0)),
                   pl.BlockSpec((1, N_KV_HEADS, nsp), lambda b: (b, 0, 0))],
        out_shape=[jax.ShapeDtypeStruct((ns, N_HEADS, HEAD_DIM), F32),
                   jax.ShapeDtypeStruct((ns, N_KV_HEADS, nsp), F32)],
        compiler_params=_params(("arbitrary",), 40),
        name="cmp_sample",
    )(q16, kvc, m_mat)


def _rank_sample_body(imp_ref, idx_ref, sc_scr, *, cur, n_sel):
    npad, nr = sc_scr.shape
    st = imp_ref[...].T
    j = lax.broadcasted_iota(I32, (npad, nr), 0)
    real = j < n_sel
    valid = (j <= cur) & real
    forced = (j == 0) | (j >= cur - 1)
    score = jnp.where(forced, FORCED_SCORE, st)
    score = jnp.where(valid, score, -1.0)
    score = jnp.where(real, score, -2.0)
    sc_scr[...] = score

    def body(k, rank):
        sk = sc_scr[pl.ds(k, 1), :]
        beats = (sk > score) | ((sk == score) & (k < j))
        return rank + beats.astype(I32)

    rank = lax.fori_loop(0, n_sel, body, jnp.zeros((npad, nr), I32))
    sel = (rank < N_SEL) & valid
    for slot in range(N_SEL):
        hit = sel & (rank == slot)
        found = jnp.max(hit.astype(I32), axis=0, keepdims=True)
        val = jnp.sum(jnp.where(hit, j, 0), axis=0, keepdims=True)
        idx_ref[slot:slot + 1, :] = jnp.where(found > 0, val, -1)


def rank_sample(imp2, cur, n_sel):
    nr, npad = imp2.shape
    return pl.pallas_call(
        functools.partial(_rank_sample_body, cur=cur, n_sel=n_sel),
        out_shape=jax.ShapeDtypeStruct((N_SEL, nr), I32),
        scratch_shapes=[pltpu.VMEM((npad, nr), F32)],
        name="rank_sample",
    )(imp2)


def _sel_sample_body(idx_ref, pt_ref, q_ref, kn_ref, vn_ref, *rest, n_past_blocks):
    del pt_ref
    kvs = rest[:N_SEL * N_KV_HEADS]
    o_ref = rest[N_SEL * N_KV_HEADS]
    b = pl.program_id(0)
    q = q_ref[0]
    qb = q.astype(BF16)
    nk = N_SEL * SEL_BLK
    rg = _row_head((N_HEADS, nk))
    col_slot = lax.shift_right_logical(lax.broadcasted_iota(I32, (N_HEADS, nk), 1), SEL_SHIFT)

    sc = jnp.zeros((N_HEADS, nk), F32)
    okv = jnp.zeros((N_HEADS, nk), I32)
    for h in range(N_KV_HEADS):
        kcat = jnp.concatenate([kvs[s * N_KV_HEADS + h][:, 0, 0, :] for s in range(N_SEL)], axis=0).astype(BF16)
        sh = lax.dot_general(qb, kcat, _NT, preferred_element_type=F32)
        ok = jnp.zeros((N_HEADS, nk), I32)
        for s in range(N_SEL):
            blk = idx_ref[s, b * N_KV_HEADS + h]
            ok = jnp.where(col_slot == s, jnp.where((blk >= 0) & (blk < n_past_blocks), 1, 0), ok)
        sc = jnp.where(rg == h, sh, sc)
        okv = jnp.where(rg == h, ok, okv)
    sm = jnp.where(okv > 0, sc * ATTN_SCALE, MASK_VALUE)
    s_new = jnp.sum(q * kn_ref[0], axis=-1, keepdims=True) * ATTN_SCALE
    m = jnp.maximum(jnp.max(sm, axis=-1, keepdims=True), s_new)
    p = jnp.where(okv > 0, jnp.exp(sm - m), 0.0)
    p_new = jnp.exp(s_new - m)
    den = jnp.sum(p, axis=-1, keepdims=True) + p_new
    pv = jnp.zeros((N_HEADS, HEAD_DIM), F32)
    for h in range(N_KV_HEADS):
        vcat = jnp.concatenate([kvs[s * N_KV_HEADS + h][:, 0, 1, :] for s in range(N_SEL)], axis=0).astype(BF16)
        pv = pv + jnp.dot(jnp.where(rg == h, p, 0.0).astype(BF16), vcat, preferred_element_type=F32)
    o_ref[0] = (pv + p_new * vn_ref[0]) / den


def sel_sample(idx, page_table, q16, kn16, vn16, pool4, n_past_blocks):
    ns = q16.shape[0]
    blk_per_page = PAGE_SIZE // SEL_BLK

    def kv_spec(s, h):
        def imap(b, idx_r, pt_r):
            blk = jnp.clip(idx_r[s, b * N_KV_HEADS + h], 0, n_past_blocks - 1)
            return (pt_r[b, blk // blk_per_page] * blk_per_page + blk % blk_per_page, h, 0, 0)
        return pl.BlockSpec((SEL_BLK, 1, 2, HEAD_DIM), imap)

    head3 = lambda: pl.BlockSpec((1, N_HEADS, HEAD_DIM), lambda b, i, p: (b, 0, 0))
    kv_specs = [kv_spec(s, h) for s in range(N_SEL) for h in range(N_KV_HEADS)]
    grid_spec = pltpu.PrefetchScalarGridSpec(
        num_scalar_prefetch=2,
        grid=(ns,),
        in_specs=[head3(), head3(), head3()] + kv_specs,
        out_specs=head3(),
    )
    return pl.pallas_call(
        functools.partial(_sel_sample_body, n_past_blocks=n_past_blocks),
        grid_spec=grid_spec,
        out_shape=jax.ShapeDtypeStruct((ns, N_HEADS, HEAD_DIM), F32),
        compiler_params=_params(("arbitrary",), 40),
        name="sel_sample",
    )(idx, page_table, q16, kn16, vn16, *([pool4] * len(kv_specs)))


def _win_sample_body(q_ref, win_ref, new_ref, kn_ref, vn_ref, oc_ref, os_ref, bg_ref, o_ref, wout_ref, *, wbuf):
    wout_ref[0:(wbuf - 1) * HK, :] = win_ref[HK:wbuf * HK, :]
    wout_ref[(wbuf - 1) * HK:wbuf * HK, :] = new_ref[...]
    q = q_ref[0]
    qb = q.astype(BF16)
    rg = _row_head((N_HEADS, wbuf))
    sc = jnp.zeros((N_HEADS, wbuf), F32)
    for h in range(N_KV_HEADS):
        kh = win_ref[pl.ds(2 * h, wbuf, stride=HK), :].astype(BF16)
        sc = jnp.where(rg == h, lax.dot_general(qb, kh, _NT, preferred_element_type=F32), sc)
    mask = lax.broadcasted_iota(I32, (N_HEADS, wbuf), 1) > wbuf - WINDOW
    sm = jnp.where(mask, sc * ATTN_SCALE, MASK_VALUE)
    s_new = jnp.sum(q * kn_ref[0], axis=-1, keepdims=True) * ATTN_SCALE
    m = jnp.maximum(jnp.max(sm, axis=-1, keepdims=True), s_new)
    p = jnp.where(mask, jnp.exp(sm - m), 0.0)
    p_new = jnp.exp(s_new - m)
    den = jnp.sum(p, axis=-1, keepdims=True) + p_new
    pv = jnp.zeros((N_HEADS, HEAD_DIM), F32)
    for h in range(N_KV_HEADS):
        vh = win_ref[pl.ds(2 * h + 1, wbuf, stride=HK), :].astype(BF16)
        pv = pv + jnp.dot(jnp.where(rg == h, p, 0.0).astype(BF16), vh, preferred_element_type=F32)
    o_w = (pv + p_new * vn_ref[0]) / den
    gates = jax.nn.sigmoid(bg_ref[0])
    o_ref[0] = gates[:, 0:1] * oc_ref[0] + gates[:, 1:2] * os_ref[0] + gates[:, 2:3] * o_w


def win_sample(q16, win2d, new2d, kn16, vn16, o_cmp, o_sel, bgs, wbuf):
    ns = q16.shape[0]
    head3 = lambda: pl.BlockSpec((1, N_HEADS, HEAD_DIM), lambda b: (b, 0, 0))
    wspec = lambda: pl.BlockSpec((wbuf * HK, LANES), lambda b: (b, 0))
    return pl.pallas_call(
        functools.partial(_win_sample_body, wbuf=wbuf),
        grid=(ns,),
        in_specs=[head3(), wspec(), pl.BlockSpec((HK, LANES), lambda b: (b, 0)), head3(), head3(), head3(), head3(),
                  head3()],
        out_specs=[head3(), wspec()],
        out_shape=[jax.ShapeDtypeStruct((ns, N_HEADS, HEAD_DIM), F32),
                   jax.ShapeDtypeStruct((ns * wbuf * HK, LANES), F32)],
        compiler_params=_params(("arbitrary",), 40),
        name="win_sample",
    )(q16, win2d, new2d, kn16, vn16, o_cmp, o_sel, bgs)


ROUTE_LANE0 = N_GROUPS


INFO_E1, INFO_E2, INFO_R1, INFO_R2, INFO_C1, INFO_C2 = range(6)


def _route_body(xp_ref, xs_ref, g_ref, wr_ref, br_ref, xn_ref, info_ref, cnt_ref, run_scr, *, n_real, n_ptiles):
    step = pl.program_id(0)

    @pl.when(step == 0)
    def _():
        run_scr[...] = jnp.zeros(run_scr.shape, F32)

    refs = (g_ref, wr_ref, br_ref, xn_ref, info_ref, cnt_ref, run_scr)

    @pl.when(step < n_ptiles)
    def _():
        _route_tile(xp_ref[...], step, n_real, *refs)

    @pl.when(step >= n_ptiles)
    def _():
        _route_tile(xs_ref[...], step, n_real, *refs)


def _route_tile(x, step, n_real, g_ref, wr_ref, br_ref, xn_ref, info_ref, cnt_ref, run_scr):
    tm = x.shape[0]
    var = jnp.mean(x * x, axis=-1, keepdims=True)
    xn = (x * lax.rsqrt(var + EPS)) * g_ref[...]
    for j in range(x.shape[1] // LANES):
        xn_ref[pl.ds(j, tm, stride=x.shape[1] // LANES), :] = xn[:, j * LANES:(j + 1) * LANES]
    xnb = xn.astype(BF16)
    logits = jnp.dot(xnb, wr_ref[...], preferred_element_type=F32) + br_ref[...]
    lane = lax.broadcasted_iota(I32, logits.shape, 1)
    big = 4 * LANES

    isg = lane < N_GROUPS
    lg = jnp.where(isg, logits, -jnp.inf)
    mg = jnp.max(lg, axis=-1, keepdims=True)
    gi = jnp.min(jnp.where(lg == mg, lane, big), axis=-1, keepdims=True)
    pg = 1.0 / jnp.sum(jnp.where(isg, jnp.exp(lg - mg), 0.0), axis=-1, keepdims=True)

    lo = ROUTE_LANE0 + gi * EXP_PER_GROUP
    ing = (lane >= lo) & (lane < lo + EXP_PER_GROUP)
    le = jnp.where(ing, logits, -jnp.inf)
    me = jnp.max(le, axis=-1, keepdims=True)
    ee = jnp.where(ing, jnp.exp(le - me), 0.0)
    pe = jnp.where(ing, ee / jnp.sum(ee, axis=-1, keepdims=True), -1.0)
    p1 = jnp.max(pe, axis=-1, keepdims=True)
    i1 = jnp.min(jnp.where(pe == p1, lane, big), axis=-1, keepdims=True)
    pe2 = jnp.where(lane == i1, -1.0, pe)
    p2 = jnp.max(pe2, axis=-1, keepdims=True)
    i2 = jnp.min(jnp.where(pe2 == p2, lane, big), axis=-1, keepdims=True)
    tot = p1 + p2
    c1 = pg * (p1 / tot)
    c2 = pg * (p2 / tot)

    row = step * tm + lax.broadcasted_iota(I32, logits.shape, 0)
    hit = jnp.where(((lane == i1) | (lane == i2)) & (row < n_real), 1.0, 0.0)
    tri = jnp.where(lax.broadcasted_iota(I32, (tm, tm), 1) < lax.broadcasted_iota(I32, (tm, tm), 0), 1.0, 0.0)
    before = run_scr[...] + jnp.dot(tri.astype(BF16), hit.astype(BF16), preferred_element_type=F32)
    r1 = jnp.sum(jnp.where(lane == i1, before, 0.0), axis=-1, keepdims=True)
    r2 = jnp.sum(jnp.where(lane == i2, before, 0.0), axis=-1, keepdims=True)
    run_new = run_scr[...] + jnp.sum(hit, axis=0, keepdims=True)
    run_scr[...] = run_new
    cnt_ref[...] = run_new

    info = jnp.zeros(logits.shape, F32)
    for ln, val in ((INFO_E1, (i1 - ROUTE_LANE0).astype(F32)), (INFO_E2, (i2 - ROUTE_LANE0).astype(F32)),
                    (INFO_R1, r1), (INFO_R2, r2), (INFO_C1, c1), (INFO_C2, c2)):
        info = jnp.where(lane == ln, val, info)
    info_ref[...] = info


MOE_TILE = 256


def route(x_prompt, x_sample, g, wr, br, n_real):
    mp, d = x_prompt.shape
    n_ptiles = mp // MOE_TILE
    assert mp % MOE_TILE == 0 and x_sample.shape[0] == MOE_TILE
    m = mp + MOE_TILE
    return pl.pallas_call(
        functools.partial(_route_body, n_real=n_real, n_ptiles=n_ptiles),
        grid=(n_ptiles + 1,),
        in_specs=[pl.BlockSpec((MOE_TILE, d), lambda i: (jnp.minimum(i, n_ptiles - 1), 0)),
                  pl.BlockSpec((MOE_TILE, d), lambda i: (0, 0)), pl.BlockSpec((1, d), lambda i: (0, 0)),
                  pl.BlockSpec((d, LANES), lambda i: (0, 0)), pl.BlockSpec((1, LANES), lambda i: (0, 0))],
        out_specs=[pl.BlockSpec((MOE_TILE * (d // LANES), LANES), lambda i: (i, 0)),
                   pl.BlockSpec((MOE_TILE, LANES), lambda i: (i, 0)), pl.BlockSpec((1, LANES), lambda i: (0, 0))],
        out_shape=[jax.ShapeDtypeStruct((m * (d // LANES), LANES), F32), jax.ShapeDtypeStruct((m, LANES), F32),
                   jax.ShapeDtypeStruct((1, LANES), F32)],
        scratch_shapes=[pltpu.VMEM((1, LANES), F32)],
        compiler_params=_params(("arbitrary",), 40),
        name="route",
    )(x_prompt, x_sample, g.reshape(1, d), wr, br)


def _moe_plan(info, counts_row, n_real, n_tiles):
    n_pad = info.shape[0]
    e = jnp.clip(info[:, INFO_E1:INFO_E2 + 1].astype(I32), 0, N_EXPERTS - 1)
    r = info[:, INFO_R1:INFO_R2 + 1].astype(I32)
    counts = counts_row[0, ROUTE_LANE0:ROUTE_LANE0 + N_EXPERTS].astype(I32)
    padded = ((counts + MOE_TILE - 1) // MOE_TILE) * MOE_TILE
    ex = jnp.arange(N_EXPERTS, dtype=I32)
    ends = jnp.sum(jnp.where(ex[None, :] <= ex[:, None], padded[None, :], 0), axis=1)
    offs = ends - padded
    tok = jnp.arange(n_pad, dtype=I32)
    valid = (tok < n_real)[:, None]
    dest = jnp.where(valid, offs[e] + r, 0)
    n_slots = n_tiles * MOE_TILE
    src = jnp.zeros((n_slots,), I32).at[jnp.where(valid, dest, n_slots).reshape(-1)].set(
        jnp.repeat(tok, 2), mode="drop")
    n_used = ends[N_EXPERTS - 1] // MOE_TILE
    tiles = jnp.arange(n_tiles, dtype=I32)
    used = tiles < n_used
    te = jnp.minimum(jnp.sum((ends[None, :] <= (tiles * MOE_TILE)[:, None]).astype(I32), axis=1), N_EXPERTS - 1)
    te_prev = jnp.concatenate([te[:1] - 1, te[:-1]])
    first = (te != te_prev) & used
    later = tiles[None, :] > tiles[:, None]
    run = jnp.sum((first[None, :] & ~later).astype(I32), axis=1) - 1
    nxt_tile = jnp.min(jnp.where(first[None, :] & later, tiles[None, :], n_tiles), axis=1)
    nxt = jnp.where(nxt_tile < n_tiles, te[jnp.minimum(nxt_tile, n_tiles - 1)], -1)
    tile3 = lambda v: v.reshape(-1, 1, MOE_TILE)
    return (tile3(src), tile3(dest[:, 0]), tile3(dest[:, 1]),
            (te, first.astype(I32), nxt, jnp.bitwise_and(run, 1), n_used.reshape(1)))


def _issue_rows(idx_ref, src_hbm, dst_buf, slot, sem):
    n = idx_ref.shape[2]
    group = 8

    def body(g, carry):
        for j in range(group):
            r = g * group + j
            pltpu.make_async_copy(src_hbm.at[pl.ds(idx_ref[0, 0, r], 1), :], dst_buf.at[slot, pl.ds(r, 1), :],
                                  sem.at[slot]).start(priority=j % 2)
        return carry

    lax.fori_loop(0, n // group, body, 0)


def _wait_rows(src_hbm, dst_buf, slot, sem):
    n = dst_buf.shape[1]
    pltpu.make_async_copy(src_hbm.at[pl.ds(0, n), :], dst_buf.at[slot], sem.at[slot]).wait()


def _moe_dispatch_body(nu_ref, src_cur, src_nxt, xn_hbm, xs_ref, xbuf, sem):
    i = pl.program_id(0)
    n_used = nu_ref[0]
    slot = lax.rem(i, 2)
    t = src_cur.shape[2]
    tok_rows = xbuf.shape[1] // t
    group = 8

    def issue(idx_ref, dst_slot):
        def body(g, carry):
            for j in range(group):
                r = g * group + j
                src_row = pl.multiple_of(idx_ref[0, 0, r] * tok_rows, tok_rows)
                pltpu.make_async_copy(xn_hbm.at[pl.ds(src_row, tok_rows), :],
                                      xbuf.at[dst_slot, pl.ds(r * tok_rows, tok_rows), :],
                                      sem.at[dst_slot]).start(priority=j % 2)
            return carry

        lax.fori_loop(0, t // group, body, 0)

    @pl.when(i == 0)
    def _():
        issue(src_cur, 0)

    @pl.when(i + 1 < n_used)
    def _():
        issue(src_nxt, 1 - slot)

    @pl.when(i < n_used)
    def _():
        pltpu.make_async_copy(xn_hbm.at[pl.ds(0, t * tok_rows), :], xbuf.at[slot], sem.at[slot]).wait()
        for j in range(tok_rows):
            xs_ref[:, j * LANES:(j + 1) * LANES] = xbuf[slot, pl.ds(j, t, stride=tok_rows), :].astype(xs_ref.dtype)

    @pl.when(i >= n_used)
    def _():
        xs_ref[...] = jnp.zeros(xs_ref.shape, xs_ref.dtype)


def moe_dispatch(n_used, src3, xn_rows, d):
    n_tiles = src3.shape[0]
    tok_rows = d // LANES
    smem_tile = lambda imap: pl.BlockSpec((1, 1, MOE_TILE), imap, memory_space=pltpu.SMEM)
    grid_spec = pltpu.PrefetchScalarGridSpec(
        num_scalar_prefetch=1,
        grid=(n_tiles,),
        in_specs=[smem_tile(lambda i, nu: (i, 0, 0)),
                  smem_tile(lambda i, nu: (jnp.minimum(i + 1, n_tiles - 1), 0, 0)),
                  pl.BlockSpec(memory_space=pl.ANY)],
        out_specs=pl.BlockSpec((MOE_TILE, d), lambda i, nu: (i, 0)),
        scratch_shapes=[pltpu.VMEM((2, MOE_TILE * tok_rows, LANES), F32), pltpu.SemaphoreType.DMA((2,))],
    )
    return pl.pallas_call(
        _moe_dispatch_body,
        grid_spec=grid_spec,
        out_shape=jax.ShapeDtypeStruct((n_tiles * MOE_TILE, d), BF16),
        compiler_params=_params(("arbitrary",), 24),
        name="moe_dispatch",
    )(n_used, src3, src3, xn_rows)


def _moe_expert_body(te_ref, first_ref, nxt_ref, par_ref, nu_ref, x_ref, wg_hbm, wu_hbm, wd_hbm,
                     y_ref, wg_buf, wu_buf, wd_buf, wsem, wgb, wub, wdb):
    i = pl.program_id(0)
    n_used = nu_ref[0]

    def weight_copies(e, ws):
        return (pltpu.make_async_copy(wg_hbm.at[e], wg_buf.at[ws], wsem.at[0, ws]),
                pltpu.make_async_copy(wu_hbm.at[e], wu_buf.at[ws], wsem.at[1, ws]),
                pltpu.make_async_copy(wd_hbm.at[e], wd_buf.at[ws], wsem.at[2, ws]))

    @pl.when(i == 0)
    def _():
        for cp in weight_copies(te_ref[0], par_ref[0]):
            cp.start()

    @pl.when(i < n_used)
    def _():
        @pl.when(first_ref[i] == 1)
        def _():
            ws = par_ref[i]
            for cp in weight_copies(te_ref[i], ws):
                cp.wait()

            @pl.when(nxt_ref[i] >= 0)
            def _():
                for cp in weight_copies(nxt_ref[i], 1 - ws):
                    cp.start()

            wgb[...] = wg_buf[ws].astype(BF16)
            wub[...] = wu_buf[ws].astype(BF16)
            wdb[...] = wd_buf[ws].astype(BF16)

        x = x_ref[...]
        hg = jnp.dot(x, wgb[...], preferred_element_type=F32)
        hu = jnp.dot(x, wub[...], preferred_element_type=F32)
        hid = (hg * jax.nn.sigmoid(hg)) * hu
        y_ref[...] = jnp.dot(hid.astype(BF16), wdb[...], preferred_element_type=F32)

    @pl.when(i >= n_used)
    def _():
        y_ref[...] = jnp.zeros(y_ref.shape, F32)


def moe_experts(tile_plan, xs, wg, wu, wd):
    rows, d = xs.shape
    n_tiles = rows // MOE_TILE
    _, _, de = wg.shape
    hbm = lambda: pl.BlockSpec(memory_space=pl.ANY)
    grid_spec = pltpu.PrefetchScalarGridSpec(
        num_scalar_prefetch=5,
        grid=(n_tiles,),
        in_specs=[pl.BlockSpec((MOE_TILE, d), lambda i, te, fi, nx, pa, nu: (jnp.minimum(i, nu[0] - 1), 0)),
                  hbm(), hbm(), hbm()],
        out_specs=pl.BlockSpec((MOE_TILE, d), lambda i, *_: (i, 0)),
        scratch_shapes=[pltpu.VMEM((2, d, de), F32), pltpu.VMEM((2, d, de), F32), pltpu.VMEM((2, de, d), F32),
                        pltpu.SemaphoreType.DMA((3, 2)),
                        pltpu.VMEM((d, de), BF16), pltpu.VMEM((d, de), BF16), pltpu.VMEM((de, d), BF16)],
    )
    return pl.pallas_call(
        _moe_expert_body,
        grid_spec=grid_spec,
        out_shape=jax.ShapeDtypeStruct((rows, d), F32),
        compiler_params=_params(("arbitrary",), 56),
        name="moe_experts",
    )(*tile_plan, xs, wg, wu, wd)


def _moe_combine_body(d1_cur, d1_nxt, d2_cur, d2_nxt, ys_hbm, xp_ref, xs_ref, info_ref, gf_ref, yp_ref, ysm_ref,
                      buf1, buf2, sem1, sem2, *, n_ptiles):
    i = pl.program_id(0)
    slot = lax.rem(i, 2)

    @pl.when(i == 0)
    def _():
        _issue_rows(d1_cur, ys_hbm, buf1, 0, sem1)
        _issue_rows(d2_cur, ys_hbm, buf2, 0, sem2)

    @pl.when(i + 1 < pl.num_programs(0))
    def _():
        _issue_rows(d1_nxt, ys_hbm, buf1, 1 - slot, sem1)
        _issue_rows(d2_nxt, ys_hbm, buf2, 1 - slot, sem2)

    _wait_rows(ys_hbm, buf1, slot, sem1)
    _wait_rows(ys_hbm, buf2, slot, sem2)

    def finish(x_ref, y_ref):
        info = info_ref[...]
        xo = x_ref[...] + (info[:, INFO_C1:INFO_C1 + 1] * buf1[slot] + info[:, INFO_C2:INFO_C2 + 1] * buf2[slot])
        var = jnp.mean(xo * xo, axis=-1, keepdims=True)
        y_ref[...] = (xo * lax.rsqrt(var + EPS)) * gf_ref[...]

    @pl.when(i < n_ptiles)
    def _():
        finish(xp_ref, yp_ref)

    @pl.when(i >= n_ptiles)
    def _():
        finish(xs_ref, ysm_ref)


def moe_combine(dest1, dest2, ys, x_prompt, x_sample, info, g_final):
    mp, d = x_prompt.shape
    n_ptiles = mp // MOE_TILE
    n_tiles = n_ptiles + 1
    cur = lambda: pl.BlockSpec((1, 1, MOE_TILE), lambda i: (i, 0, 0), memory_space=pltpu.SMEM)
    nxt = lambda: pl.BlockSpec((1, 1, MOE_TILE), lambda i: (jnp.minimum(i + 1, n_tiles - 1), 0, 0),
                               memory_space=pltpu.SMEM)
    ptile = lambda: pl.BlockSpec((MOE_TILE, d), lambda i: (jnp.minimum(i, n_ptiles - 1), 0))
    stile = lambda: pl.BlockSpec((MOE_TILE, d), lambda i: (0, 0))
    return pl.pallas_call(
        functools.partial(_moe_combine_body, n_ptiles=n_ptiles),
        grid=(n_tiles,),
        in_specs=[cur(), nxt(), cur(), nxt(), pl.BlockSpec(memory_space=pl.ANY), ptile(), stile(),
                  pl.BlockSpec((MOE_TILE, LANES), lambda i: (i, 0)), pl.BlockSpec((1, d), lambda i: (0, 0))],
        out_specs=[ptile(), stile()],
        out_shape=[jax.ShapeDtypeStruct((mp, d), F32), jax.ShapeDtypeStruct((MOE_TILE, d), F32)],
        scratch_shapes=[pltpu.VMEM((2, MOE_TILE, d), F32), pltpu.VMEM((2, MOE_TILE, d), F32),
                        pltpu.SemaphoreType.DMA((2,)), pltpu.SemaphoreType.DMA((2,))],
        compiler_params=_params(("arbitrary",), 40),
        name="moe_combine",
    )(dest1, dest1, dest2, dest2, ys, x_prompt, x_sample, info, g_final.reshape(1, d))


def _mixer_inputs(x2, g_mix, w_in_t, w_bg_t, w_mg_t, tm, q_dtype):
    tn = PROJ_TILE_N
    xn = rmsnorm(x2, g_mix, tm, BF16)
    mm = functools.partial(matmul_cols, xn, tm=tm, w_is_t=True)
    u_rnn = mm(w_in_t, 0, D_RNN, tn=tn, out_dtype=F32, name="in_rnn")
    u_gate = mm(w_in_t, D_RNN, D_RNN, tn=tn, out_dtype=F32, name="in_gate")
    q = mm(w_in_t, 2 * D_RNN, N_HEADS * HEAD_DIM, tn=tn, out_dtype=q_dtype, name="in_q")
    kv0 = 2 * D_RNN + N_HEADS * HEAD_DIM
    kv_cmp, = matmul_kv(xn, w_in_t, kv0, tm, False, "in_kv_cmp")
    kv_sel, sel_planes = matmul_kv(xn, w_in_t, kv0 + 2 * KV_DIM, tm, True, "in_kv_sel")
    kv_win, win_planes = matmul_kv(xn, w_in_t, kv0 + 4 * KV_DIM, tm, True, "in_kv_win")
    u_bg = mm(w_bg_t, 0, w_bg_t.shape[0], tn=w_bg_t.shape[0], out_dtype=F32, name="in_bg")
    u_mg = mm(w_mg_t, 0, 2 * D_MODEL, tn=tn, out_dtype=F32, name="in_mg")
    return u_rnn, u_gate, q, (kv_cmp, kv_sel, kv_win), (sel_planes, win_planes), u_bg, u_mg


def kernel(x_prompt, x_sample, cache_cmp_kv, cache_sel_kv, cache_win_kv, state_conv, state_rglru, page_table, g_mix, w_in, conv_w, conv_b, lru_w_a, lru_b_a, lru_w_x, lru_b_x, lru_lambda, cmp_pe, cmp_w1, cmp_b1, cmp_w2, cmp_b2, w_out, g_ffn, moe_w_group, moe_b_group, moe_w_expert, moe_b_expert, moe_w_gate, moe_w_up, moe_w_down, g_final):
    b, t, d = x_prompt.shape
    ns = x_sample.shape[0]
    npages = page_table.shape[1]
    past_len = npages * PAGE_SIZE
    wbuf = cache_win_kv.shape[1]
    kvshape = (N_KV_HEADS, 2, HEAD_DIM)

    w_in_t = w_in.T
    bg0 = 2 * D_RNN + N_HEADS * HEAD_DIM + 6 * KV_DIM
    n_bg = 3 * N_HEADS
    w_bg = w_in_t[bg0:bg0 + n_bg].reshape(3, N_KV_HEADS, GROUP, d).transpose(1, 0, 2, 3).reshape(N_KV_HEADS, 3 * GROUP, d)
    w_bg = jnp.pad(w_bg, ((0, 0), (0, LANES - 3 * GROUP), (0, 0))).reshape(N_KV_HEADS * LANES, d)
    w_mg = w_in_t[bg0 + n_bg:]
    wa = lru_w_a.astype(BF16)
    wx = lru_w_x.astype(BF16)
    half_rows = CMP_STRIDE * HEAD_DIM
    wc = jnp.concatenate([cmp_w1[:, :half_rows], cmp_w1[:, half_rows:]], axis=-1).astype(BF16)
    pe8 = jnp.pad(cmp_pe.reshape(2, 2, half_rows), ((0, 0), (0, 6), (0, 0)))
    b1 = cmp_b1.reshape(2, 1, CMP_HIDDEN)
    w2 = cmp_w2.astype(BF16)
    b2 = cmp_b2.reshape(2, 1, HEAD_DIM)
    n_route = N_GROUPS + N_EXPERTS
    wr = jnp.pad(jnp.concatenate([moe_w_group, moe_w_expert], axis=1), ((0, 0), (0, LANES - n_route))).astype(BF16)
    br = jnp.pad(jnp.concatenate([moe_b_group, moe_b_expert]), (0, LANES - n_route)).reshape(1, LANES)
    lru_args = (conv_w, conv_b, wa, wx, lru_b_a, lru_b_x, lru_lambda)
    cmp_args = (wc, pe8, b1, w2, b2)

    xp2 = x_prompt.reshape(b * t, d)
    u_rnn, u_gate, q, kv2d, planes, u_bg, u_mg = _mixer_inputs(xp2, g_mix, w_in_t, w_bg, w_mg, PROJ_TILE_M, BF16)
    o_rnn, p_conv, p_h = lru_prompt(u_rnn.reshape(b, t, d), u_gate.reshape(b, t, d), *lru_args, LRU_TILE_T)
    p_cmp, p_sel, p_win_full = (a.reshape((b, t) + kvshape) for a in kv2d)
    p_win = p_win_full[:, t - min(WINDOW, t):]
    pt_prompt = jnp.arange(b * t // PAGE_SIZE, dtype=I32).reshape(b, t // PAGE_SIZE)
    kvc_p = compress(kv2d[0], pt_prompt, jnp.zeros((b * CMP_STRIDE * HK, HEAD_DIM), F32), *cmp_args)
    merged = attn_prompt_s(q.reshape(b, t, -1), planes[0], planes[1], kvc_p, u_bg.reshape(b, t, -1),
                           u_mg.reshape(b, t, -1), o_rnn, t // CMP_STRIDE - 1)
    x_mid = matmul_cols(merged.reshape(b * t, d), w_out, 0, d, PROJ_TILE_M, PROJ_TILE_N, F32, residual=xp2,
                        name="out_proj")

    xs2 = x_sample.reshape(ns, d)
    su_rnn, su_gate, sq, skv2d, _, su_bg, su_mg = _mixer_inputs(xs2, g_mix, w_in_t, w_bg, w_mg, ns, F32)
    so_rnn, cn, s_h = lru_sample(su_rnn, su_gate, state_conv.transpose(1, 0, 2), state_rglru, *lru_args)
    s_conv = cn.transpose(1, 0, 2)
    kvs = jnp.stack([a.reshape(ns, HK, HEAD_DIM) for a in skv2d], axis=1)
    s_cmp = kvs[:, 0].reshape((ns, 1) + kvshape)
    s_sel = kvs[:, 1].reshape((ns, 1) + kvshape)

    extra = jnp.concatenate([kvs[:, 0], jnp.zeros((ns, (CMP_STRIDE - 1) * HK, HEAD_DIM), F32)], axis=1)
    kvc_s = compress(cache_cmp_kv.reshape(-1, HEAD_DIM), page_table, extra.reshape(-1, HEAD_DIM), *cmp_args)
    n_cmp_s = -(-(past_len + 1) // CMP_STRIDE) - 1
    n_sel_s = -(-(past_len + 1) // SEL_BLK)
    nsp = -(-n_sel_s // LANES) * LANES
    q16 = sq.reshape(ns, N_HEADS, HEAD_DIM)
    m_s = jnp.asarray(_overlap_matrix(n_cmp_s, kvc_s.shape[2], n_sel_s, nsp), BF16)
    so_cmp, imp = cmp_sample(q16, kvc_s, m_s, past_len, n_cmp_s)
    idx = rank_sample(imp.reshape(ns * N_KV_HEADS, nsp), past_len // SEL_BLK, n_sel_s)

    def per_head_rows(new_kv, kv):
        return jnp.repeat(new_kv[:, kv::2], GROUP, axis=1)

    so_sel = sel_sample(idx, page_table, q16, per_head_rows(kvs[:, 1], 0), per_head_rows(kvs[:, 1], 1),
                        cache_sel_kv.reshape((-1,) + kvshape), past_len // SEL_BLK)
    bgs = su_bg.reshape(ns, N_KV_HEADS, LANES)[:, :, :3 * GROUP].reshape(ns, N_KV_HEADS, 3, GROUP)
    bgs = jnp.pad(bgs.transpose(0, 1, 3, 2).reshape(ns, N_HEADS, 3), ((0, 0), (0, 0), (0, LANES - 3)))
    so_nsa, s_win2d = win_sample(q16, cache_win_kv.reshape(-1, HEAD_DIM), skv2d[2], per_head_rows(kvs[:, 2], 0),
                                 per_head_rows(kvs[:, 2], 1), so_cmp, so_sel, bgs, wbuf)
    s_win = s_win2d.reshape((ns, wbuf) + kvshape)
    s_merged = merge(su_mg, so_rnn, so_nsa.reshape(ns, d), ns)
    sx_mid = matmul_cols(s_merged, w_out, 0, d, ns, PROJ_TILE_N, F32, residual=xs2, name="out_proj_s")

    n_real = b * t + ns
    assert ns <= MOE_TILE
    sx_pad = jnp.pad(sx_mid, ((0, MOE_TILE - ns), (0, 0)))
    xn_all, info, counts = route(x_mid, sx_pad, g_ffn, wr, br, n_real)
    n_tiles = -(-(2 * n_real + N_EXPERTS * (MOE_TILE - 1)) // MOE_TILE)
    src3, dest1, dest2, tile_plan = _moe_plan(info, counts, n_real, n_tiles)
    xs = moe_dispatch(tile_plan[-1], src3, xn_all, d)
    ys = moe_experts(tile_plan, xs, moe_w_gate, moe_w_up, moe_w_down)
    y_p2, y_s2 = moe_combine(dest1, dest2, ys, x_mid, sx_pad, info, g_final)
    y_prompt = y_p2.reshape(b, t, d)
    y_sample = y_s2[:ns].reshape(ns, 1, d)

    return (y_prompt, y_sample, p_cmp, s_cmp, p_sel, s_sel, p_win, s_win, p_conv, s_conv,
            p_h.reshape(b, d), s_h)
```

```python
import functools

import jax
import jax.numpy as jnp
import numpy as np
from jax import lax
from jax.experimental import pallas as pl
from jax.experimental.pallas import tpu as pltpu

F32 = jnp.float32
BF16 = jnp.bfloat16
I32 = jnp.int32

D_MODEL = 2048
D_RNN = 2048
RNN_BLOCKS = 16
RNN_BLOCK_DIM = 128
CONV_W = 4
LRU_C = 8.0
N_HEADS = 16
HEAD_DIM = 128
N_KV_HEADS = 4
GROUP = 4
KV_DIM = N_KV_HEADS * HEAD_DIM
HK = 2 * N_KV_HEADS
CMP_STRIDE = 16
CMP_BLK = 32
CMP_HIDDEN = 256
SEL_BLK = 64
SEL_SHIFT = 6
N_SEL = 16
WINDOW = 512
PAGE_SIZE = 128
ATTN_SCALE = HEAD_DIM ** -0.5
FORCED_SCORE = 1e4
N_GROUPS = 4
EXP_PER_GROUP = 8
N_EXPERTS = 32
D_EXPERT = 512
EPS = 1e-6
MASK_VALUE = -1e30
LANES = 128
MIB = 1024 * 1024

PROJ_TILE_M = 1024
PROJ_TILE_N = 1024
LRU_TILE_T = 256

_NT = (((1,), (1,)), ((), ()))


def _params(sem, vmem_mib):
    return pltpu.CompilerParams(dimension_semantics=sem, vmem_limit_bytes=vmem_mib * MIB)


def _msoftmax(s, mask):
    sm = jnp.where(mask, s, MASK_VALUE)
    m = jnp.max(sm, axis=-1, keepdims=True)
    e = jnp.where(mask, jnp.exp(sm - m), 0.0)
    den = jnp.sum(e, axis=-1, keepdims=True)
    return e * jnp.where(den > 0.0, 1.0 / den, 0.0)


def _gelu_tanh(x):
    return 0.5 * x * (1.0 + jnp.tanh(0.7978845608028654 * (x + 0.044715 * (x * x * x))))


def _rmsnorm_body(x_ref, g_ref, o_ref):
    x = x_ref[...]
    var = jnp.mean(x * x, axis=-1, keepdims=True)
    o_ref[...] = ((x * lax.rsqrt(var + EPS)) * g_ref[...]).astype(o_ref.dtype)


def rmsnorm(x, g, tm, out_dtype):
    m, d = x.shape
    return pl.pallas_call(
        _rmsnorm_body,
        grid=(m // tm,),
        in_specs=[pl.BlockSpec((tm, d), lambda i: (i, 0)), pl.BlockSpec((1, d), lambda i: (0, 0))],
        out_specs=pl.BlockSpec((tm, d), lambda i: (i, 0)),
        out_shape=jax.ShapeDtypeStruct((m, d), out_dtype),
        compiler_params=_params(("arbitrary",), 56),
        name="rmsnorm",
    )(x, g.reshape(1, d))


def _xw(x, wbf, w_is_t):
    if w_is_t:
        return lax.dot_general(x, wbf, _NT, preferred_element_type=F32)
    return jnp.dot(x, wbf, preferred_element_type=F32)


def _mm_body(x_ref, w_ref, *rest, w_is_t):
    r_ref = rest[0] if len(rest) == 3 else None
    o_ref, wbf_ref = rest[-2:]

    @pl.when(pl.program_id(1) == 0)
    def _():
        wbf_ref[...] = w_ref[...].astype(BF16)

    res = _xw(x_ref[...], wbf_ref[...], w_is_t)
    if r_ref is not None:
        res = r_ref[...] + res
    o_ref[...] = res.astype(o_ref.dtype)


def matmul_cols(x, w, col0, ncols, tm, tn, out_dtype, residual=None, w_is_t=False, name="matmul"):
    m, k = x.shape
    cb0 = col0 // tn
    assert col0 % tn == 0 and ncols % tn == 0 and m % tm == 0
    if w_is_t:
        w_spec = pl.BlockSpec((tn, k), lambda j, i: (cb0 + j, 0))
        w_tile = (tn, k)
    else:
        w_spec = pl.BlockSpec((k, tn), lambda j, i: (0, cb0 + j))
        w_tile = (k, tn)
    in_specs = [pl.BlockSpec((tm, k), lambda j, i: (i, 0)), w_spec]
    args = [x, w]
    if residual is not None:
        in_specs.append(pl.BlockSpec((tm, tn), lambda j, i: (i, j)))
        args.append(residual)
    return pl.pallas_call(
        functools.partial(_mm_body, w_is_t=w_is_t),
        grid=(ncols // tn, m // tm),
        in_specs=in_specs,
        out_specs=pl.BlockSpec((tm, tn), lambda j, i: (i, j)),
        out_shape=jax.ShapeDtypeStruct((m, ncols), out_dtype),
        scratch_shapes=[pltpu.VMEM(w_tile, BF16)],
        compiler_params=_params(("arbitrary", "arbitrary"), 56),
        name=name,
    )(*args)


def _mm_kv_body(x_ref, w_ref, oi_ref, *rest):
    wbf_ref = rest[-1]

    @pl.when(pl.program_id(0) == 0)
    def _():
        wbf_ref[...] = w_ref[...].astype(BF16)

    res = _xw(x_ref[...], wbf_ref[...], True)
    tm = res.shape[0]
    for hk in range(HK):
        blk = res[:, hk * HEAD_DIM:(hk + 1) * HEAD_DIM]
        oi_ref[pl.ds(hk, tm, stride=HK), :] = blk
        if len(rest) == 2:
            rest[0][hk] = blk.astype(BF16)


def matmul_kv(x, w_t, col0, tm, planes, name):
    m, k = x.shape
    ncols = HK * HEAD_DIM
    assert col0 % ncols == 0 and m % tm == 0
    out_specs = [pl.BlockSpec((tm * HK, HEAD_DIM), lambda i: (i, 0))]
    out_shape = [jax.ShapeDtypeStruct((m * HK, HEAD_DIM), F32)]
    if planes:
        out_specs.append(pl.BlockSpec((HK, tm, HEAD_DIM), lambda i: (0, i, 0)))
        out_shape.append(jax.ShapeDtypeStruct((HK, m, HEAD_DIM), BF16))
    return pl.pallas_call(
        _mm_kv_body,
        grid=(m // tm,),
        in_specs=[pl.BlockSpec((tm, k), lambda i: (i, 0)), pl.BlockSpec((ncols, k), lambda i: (col0 // ncols, 0))],
        out_specs=out_specs,
        out_shape=out_shape,
        scratch_shapes=[pltpu.VMEM((ncols, k), BF16)],
        compiler_params=_params(("arbitrary",), 56),
        name=name,
    )(x, w_t)


def _merge_body(mg_ref, rnn_ref, nsa_ref, o_ref):
    d = rnn_ref.shape[-1]
    ga = jax.nn.sigmoid(mg_ref[:, :d])
    gb = jax.nn.sigmoid(mg_ref[:, d:])
    o_ref[...] = (ga * rnn_ref[...] + gb * nsa_ref[...]).astype(o_ref.dtype)


def merge(u_mg, o_rnn, o_nsa, tm):
    m, d = o_rnn.shape
    return pl.pallas_call(
        _merge_body,
        grid=(m // tm,),
        in_specs=[pl.BlockSpec((tm, 2 * d), lambda i: (i, 0)), pl.BlockSpec((tm, d), lambda i: (i, 0)),
                  pl.BlockSpec((tm, d), lambda i: (i, 0))],
        out_specs=pl.BlockSpec((tm, d), lambda i: (i, 0)),
        out_shape=jax.ShapeDtypeStruct((m, d), BF16),
        compiler_params=_params(("arbitrary",), 48),
        name="merge",
    )(u_mg, o_rnn, o_nsa)


def _lru_gates(uc, wa_ref, wx_ref, ba_ref, bx_ref, lam_ref):
    ucb = uc.astype(BF16)
    ra, ri = [], []
    for n in range(RNN_BLOCKS):
        blk = ucb[:, n * RNN_BLOCK_DIM:(n + 1) * RNN_BLOCK_DIM]
        ra.append(jnp.dot(blk, wa_ref[n], preferred_element_type=F32))
        ri.append(jnp.dot(blk, wx_ref[n], preferred_element_type=F32))
    r = jax.nn.sigmoid(jnp.concatenate(ra, axis=-1) + ba_ref[...])
    i = jax.nn.sigmoid(jnp.concatenate(ri, axis=-1) + bx_ref[...])
    z = -lam_ref[...]
    softplus = jnp.maximum(z, 0.0) + jnp.log1p(jnp.exp(-jnp.abs(z)))
    log_a = (-LRU_C * r) * softplus
    a = jnp.exp(log_a)
    mult = jnp.sqrt(-jnp.tanh(log_a) * (a * a + 1.0))
    return a, mult * (i * uc)


def _lru_prompt_body(u_ref, gate_ref, cw_ref, cb_ref, wa_ref, wx_ref, ba_ref, bx_ref, lam_ref,
                     o_ref, conv_ref, hlast_ref, xbuf, hcar, acum, bcum):
    t = pl.program_id(1)
    nt = pl.num_programs(1)
    tt, d = u_ref.shape[1], u_ref.shape[2]

    @pl.when(t == 0)
    def _():
        xbuf[0:8, :] = jnp.zeros((8, d), F32)
        hcar[...] = jnp.zeros((8, d), F32)

    x = u_ref[0]
    xbuf[8:8 + tt, :] = x
    uc = (cw_ref[0:1, :] * xbuf[5:5 + tt, :] + cw_ref[1:2, :] * xbuf[6:6 + tt, :]
          + cw_ref[2:3, :] * xbuf[7:7 + tt, :] + cw_ref[3:4, :] * x) + cb_ref[...]
    xbuf[0:8, :] = xbuf[tt:tt + 8, :]

    a, b = _lru_gates(uc, wa_ref, wx_ref, ba_ref, bx_ref, lam_ref)
    g = tt // 8
    a3 = a.reshape(g, 8, d)
    b3 = b.reshape(g, 8, d)
    row = lax.broadcasted_iota(I32, (g, 8, d), 1)
    for s in (1, 2, 4):
        a_sh = pltpu.roll(a3, s, axis=1)
        b_sh = pltpu.roll(b3, s, axis=1)
        keep = row >= s
        b3 = jnp.where(keep, a3 * b_sh + b3, b3)
        a3 = jnp.where(keep, a3 * a_sh, a3)
    acum[...] = a3
    bcum[...] = b3

    def body(gi, h):
        hg = bcum[gi] + acum[gi] * h
        bcum[gi] = hg
        return jnp.broadcast_to(hg[7:8, :], (8, d))

    hfin = lax.fori_loop(0, g, body, hcar[...])
    hcar[...] = hfin
    o_ref[0] = bcum[...].reshape(tt, d) * _gelu_tanh(gate_ref[0])

    @pl.when(t == nt - 1)
    def _():
        conv_ref[0] = x[tt - (CONV_W - 1):tt, :]
        hlast_ref[0] = hfin[0:1, :]


def lru_prompt(u_rnn, u_gate, conv_w, conv_b, wa, wx, ba, bx, lam, tt):
    b, t, d = u_rnn.shape
    vec = lambda: pl.BlockSpec((1, d), lambda i, j: (0, 0))
    wspec = lambda: pl.BlockSpec((RNN_BLOCKS, RNN_BLOCK_DIM, RNN_BLOCK_DIM), lambda i, j: (0, 0, 0))
    return pl.pallas_call(
        _lru_prompt_body,
        grid=(b, t // tt),
        in_specs=[pl.BlockSpec((1, tt, d), lambda i, j: (i, j, 0)), pl.BlockSpec((1, tt, d), lambda i, j: (i, j, 0)),
                  pl.BlockSpec((CONV_W, d), lambda i, j: (0, 0)), vec(), wspec(), wspec(), vec(), vec(), vec()],
        out_specs=[pl.BlockSpec((1, tt, d), lambda i, j: (i, j, 0)),
                   pl.BlockSpec((1, CONV_W - 1, d), lambda i, j: (i, 0, 0)),
                   pl.BlockSpec((1, 1, d), lambda i, j: (i, 0, 0))],
        out_shape=[jax.ShapeDtypeStruct((b, t, d), F32), jax.ShapeDtypeStruct((b, CONV_W - 1, d), F32),
                   jax.ShapeDtypeStruct((b, 1, d), F32)],
        scratch_shapes=[pltpu.VMEM((tt + 8, d), F32), pltpu.VMEM((8, d), F32),
                        pltpu.VMEM((tt // 8, 8, d), F32), pltpu.VMEM((tt // 8, 8, d), F32)],
        compiler_params=_params(("arbitrary", "arbitrary"), 56),
        name="lru_prompt",
    )(u_rnn, u_gate, conv_w, conv_b.reshape(1, d), wa, wx, ba.reshape(1, d), bx.reshape(1, d), lam.reshape(1, d))


def _lru_sample_body(u_ref, gate_ref, cp_ref, h0_ref, cw_ref, cb_ref, wa_ref, wx_ref, ba_ref, bx_ref, lam_ref,
                     o_ref, cn_ref, h_ref):
    x = u_ref[...]
    uc = (cw_ref[0:1, :] * cp_ref[0] + cw_ref[1:2, :] * cp_ref[1] + cw_ref[2:3, :] * cp_ref[2]
          + cw_ref[3:4, :] * x) + cb_ref[...]
    a, b = _lru_gates(uc, wa_ref, wx_ref, ba_ref, bx_ref, lam_ref)
    h = a * h0_ref[...] + b
    o_ref[...] = h * _gelu_tanh(gate_ref[...])
    h_ref[...] = h
    cn_ref[0] = cp_ref[1]
    cn_ref[1] = cp_ref[2]
    cn_ref[2] = x


def lru_sample(u_rnn, u_gate, conv_prev_t, h0, conv_w, conv_b, wa, wx, ba, bx, lam):
    n, d = u_rnn.shape
    return pl.pallas_call(
        _lru_sample_body,
        out_shape=[jax.ShapeDtypeStruct((n, d), F32), jax.ShapeDtypeStruct((CONV_W - 1, n, d), F32),
                   jax.ShapeDtypeStruct((n, d), F32)],
        compiler_params=pltpu.CompilerParams(vmem_limit_bytes=40 * MIB),
        name="lru_sample",
    )(u_rnn, u_gate, conv_prev_t, h0, conv_w, conv_b.reshape(1, d), wa, wx, ba.reshape(1, d), bx.reshape(1, d),
      lam.reshape(1, d))


COMPRESS_PAGES = 32
CHUNKS_PER_PAGE = PAGE_SIZE // CMP_STRIDE


def _compress_body(pt_ref, *refs, pp, n_steps):
    page_refs = refs[:pp]
    nxt_ref, extra_ref, wc_ref, pe_ref, b1_ref, w2_ref, b2_ref, o_ref, res_scr, pet_scr = refs[pp:]
    del pt_ref
    is_last = pl.program_id(1) == n_steps - 1
    m_rows = pp * CHUNKS_PER_PAGE * N_KV_HEADS
    lo4 = lax.broadcasted_iota(I32, (4, 8, 8, LANES), 2) < 4
    lo3 = lax.broadcasted_iota(I32, (8, 8, LANES), 1) < 4

    for k in range(2):
        lhs_sp = [[] for _ in range(8)]
        for pr in page_refs:
            xk = pr[pl.ds(k, PAGE_SIZE * N_KV_HEADS, stride=2), :]
            x5 = xk.reshape(4, 2, 8, 8, LANES)
            a0 = x5[:, 0]
            a1 = x5[:, 1]
            be = jnp.where(lo4, a0, pltpu.roll(a1, 4, axis=2))
            bo = jnp.where(lo4, pltpu.roll(a0, 4, axis=2), a1)
            for sp in range(8):
                lhs_sp[sp].append(jnp.concatenate([be[:, sp], bo[:, sp]], axis=-1).reshape(32, 2 * LANES))
        acc = jnp.zeros((m_rows, 2 * CMP_HIDDEN), F32)
        for sp in range(8):
            lhs = jnp.concatenate(lhs_sp[sp], axis=0).astype(BF16)
            acc = acc + jnp.dot(lhs, wc_ref[k, sp * 256:(sp + 1) * 256, :], preferred_element_type=F32)

        @pl.when((pl.program_id(0) == 0) & (pl.program_id(1) == 0))
        def _():
            pet_scr[k] = jnp.dot(pe_ref[k].astype(BF16), wc_ref[k], preferred_element_type=F32)

        peb0 = pet_scr[k, 0:1, :CMP_HIDDEN]
        peb1 = pet_scr[k, 1:2, CMP_HIDDEN:]
        h0 = acc[:, :CMP_HIDDEN] + peb0
        h1 = acc[:, CMP_HIDDEN:] + peb1

        xn = jnp.where(is_last, extra_ref[pl.ds(k, CMP_STRIDE * N_KV_HEADS, stride=2), :],
                       nxt_ref[pl.ds(k, CMP_STRIDE * N_KV_HEADS, stride=2), :])
        xn3 = xn.reshape(8, 8, LANES)
        ln = jnp.concatenate([jnp.where(lo3, xn3, 0.0), jnp.where(lo3, 0.0, xn3)], axis=-1).astype(BF16)
        nacc = jnp.zeros((8, CMP_HIDDEN), F32)
        for sp in range(8):
            nacc = nacc + jnp.dot(ln[sp], wc_ref[k, sp * 256:(sp + 1) * 256, CMP_HIDDEN:],
                                  preferred_element_type=F32)
        n8 = nacc + pltpu.roll(nacc, 4, axis=0) + peb1
        h1e = jnp.concatenate([h1, n8], axis=0)
        h1s = pltpu.roll(h1e, m_rows + 8 - N_KV_HEADS, axis=0)[:m_rows]
        pre = h0 + h1s + b1_ref[k]
        out_k = jnp.dot(_gelu_tanh(pre).astype(BF16), w2_ref[k], preferred_element_type=F32) + b2_ref[k]
        res_scr[...] = out_k
        for h in range(N_KV_HEADS):
            o_ref[0, 2 * h + k] = res_scr[pl.ds(h, m_rows // N_KV_HEADS, stride=N_KV_HEADS), :]


def compress(pool2d, page_table, extra2d, wc, pe8, b1, w2, b2):
    nseq, npages = page_table.shape
    pp = min(COMPRESS_PAGES, npages)
    n_steps = npages // pp
    assert npages % pp == 0
    rows_page = PAGE_SIZE * HK
    rows_chunk = CMP_STRIDE * HK
    blocks_step = pp * CHUNKS_PER_PAGE

    def page_spec(i):
        return pl.BlockSpec((rows_page, LANES), lambda b, s, pt: (pt[b, s * pp + i], 0))

    def nxt_map(b, s, pt):
        return (pt[b, jnp.minimum((s + 1) * pp, npages - 1)] * CHUNKS_PER_PAGE, 0)

    const3 = lambda shape: pl.BlockSpec(shape, lambda b, s, pt: (0, 0, 0))
    in_specs = [page_spec(i) for i in range(pp)] + [
        pl.BlockSpec((rows_chunk, LANES), nxt_map),
        pl.BlockSpec((rows_chunk, LANES), lambda b, s, pt: (b, 0)),
        const3((2, CMP_STRIDE * HEAD_DIM, 2 * CMP_HIDDEN)),
        const3((2, 8, CMP_STRIDE * HEAD_DIM)),
        const3((2, 1, CMP_HIDDEN)),
        const3((2, CMP_HIDDEN, HEAD_DIM)),
        const3((2, 1, HEAD_DIM)),
    ]
    grid_spec = pltpu.PrefetchScalarGridSpec(
        num_scalar_prefetch=1,
        grid=(nseq, n_steps),
        in_specs=in_specs,
        out_specs=pl.BlockSpec((1, HK, blocks_step, HEAD_DIM), lambda b, s, pt: (b, 0, s, 0)),
        scratch_shapes=[pltpu.VMEM((blocks_step * N_KV_HEADS, HEAD_DIM), F32),
                        pltpu.VMEM((2, 8, 2 * CMP_HIDDEN), F32)],
    )
    return pl.pallas_call(
        functools.partial(_compress_body, pp=pp, n_steps=n_steps),
        grid_spec=grid_spec,
        out_shape=jax.ShapeDtypeStruct((nseq, HK, npages * CHUNKS_PER_PAGE, HEAD_DIM), F32),
        compiler_params=_params(("arbitrary", "arbitrary"), 56),
        name="compress",
    )(page_table, *([pool2d] * pp), pool2d, extra2d, wc, pe8, b1, w2, b2)


def _overlap_matrix(n_cmp, n_cmp_pad, n_sel, n_sel_pad):
    cs = np.arange(n_cmp_pad)[:, None] * CMP_STRIDE
    ss = np.arange(n_sel_pad)[None, :] * SEL_BLK
    ov = np.minimum(cs + CMP_BLK, ss + SEL_BLK) - np.maximum(cs, ss)
    m = np.clip(ov, 0, CMP_BLK).astype(np.float32) / CMP_BLK
    m[n_cmp:, :] = 0.0
    m[:, n_sel:] = 0.0
    return m


QBLK = 128


KCHUNK = 64
KPIECE = 512
LOG2E = 1.4426950408889634


def _softmax_passes(s_scr, p_scr, n_plain, n_masked, n_total, plain_mask, full_mask):
    cols = s_scr.shape[1]
    grp = KCHUNK // 8
    c2 = ATTN_SCALE * LOG2E

    def rows(c):
        return pl.ds(pl.multiple_of(c * KCHUNK, KCHUNK), KCHUNK)

    def masked_max(mask_fn):
        def body(c, m8):
            sm = jnp.where(mask_fn(c), s_scr[rows(c), :], MASK_VALUE)
            s_scr[rows(c), :] = sm
            return jnp.maximum(m8, jnp.max(sm.reshape(grp, 8, cols), axis=0))
        return body

    m8 = lax.fori_loop(0, n_plain, masked_max(plain_mask), jnp.full((8, cols), MASK_VALUE, F32))
    m8 = lax.fori_loop(n_plain, n_plain + n_masked, masked_max(full_mask), m8)
    mc = jnp.max(m8, axis=0, keepdims=True) * c2

    def pass_exp(c, l8):
        p = jnp.exp2(s_scr[rows(c), :] * c2 - mc)
        p_scr[rows(c), :] = p.astype(BF16)
        return l8 + jnp.sum(p.reshape(grp, 8, cols), axis=0)

    l8 = lax.fori_loop(0, n_plain + n_masked, pass_exp, jnp.zeros((8, cols), F32))

    def zero_fill(c, carry):
        p_scr[rows(c), :] = jnp.zeros((KCHUNK, cols), BF16)
        return carry

    lax.fori_loop(n_plain + n_masked, n_total, zero_fill, 0)
    return 1.0 / jnp.sum(l8, axis=0, keepdims=True)


def _attn_prompt_s_body(q_ref, ksel_ref, vsel_ref, kwin_ref, vwin_ref, kvc_ref, bg_ref, mga_ref, mgb_ref, rnn_ref,
                        mt_ref, o_ref, vsel_t, vwin_t, kc_bf, vc_t_bf, s_scr, p_scr, sel_scr, *, n_cmp, n_sel):
    qi = pl.program_id(2)
    q0 = qi * QBLK
    cols = GROUP * QBLK

    @pl.when(qi == 0)
    def _():
        vsel_t[...] = vsel_ref[0].astype(F32).T.astype(BF16)
        vwin_t[...] = vwin_ref[0].astype(F32).T.astype(BF16)
        kc_bf[...] = kvc_ref[0, 0].astype(BF16)
        vc_t_bf[...] = kvc_ref[0, 1].T.astype(BF16)

    q = q_ref[0]
    q4 = jnp.concatenate([q[:, g * HEAD_DIM:(g + 1) * HEAD_DIM] for g in range(GROUP)], axis=0)
    qpos_row = q0 + (lax.broadcasted_iota(I32, (1, cols), 1) & (QBLK - 1))
    row_in_chunk = lax.broadcasted_iota(I32, (KCHUNK, cols), 0)

    nb = kvc_ref.shape[2]
    kc = kc_bf[...]
    vc_t = vc_t_bf[...]
    n_io = lax.broadcasted_iota(I32, (nb, cols), 0)
    mask_c = (n_io * CMP_STRIDE + (CMP_BLK - 1) <= qpos_row) & (n_io < n_cmp)
    sc = jnp.where(mask_c, lax.dot_general(kc, q4, _NT, preferred_element_type=F32) * ATTN_SCALE, MASK_VALUE)
    ec = jnp.where(mask_c, jnp.exp(sc - jnp.max(sc, axis=0, keepdims=True)), 0.0)
    den_c = jnp.sum(ec, axis=0, keepdims=True)
    pc = ec * jnp.where(den_c > 0.0, 1.0 / den_c, 0.0)
    o_c = jnp.dot(vc_t, pc.astype(BF16), preferred_element_type=F32)
    psum = pc[:, 0:QBLK] + pc[:, QBLK:2 * QBLK] + pc[:, 2 * QBLK:3 * QBLK] + pc[:, 3 * QBLK:4 * QBLK]

    nsp = mt_ref.shape[0]
    imp_t = jnp.dot(mt_ref[...], psum.astype(BF16), preferred_element_type=F32)
    jt = lax.broadcasted_iota(I32, (nsp, QBLK), 0)
    cur = lax.shift_right_logical(q0 + lax.broadcasted_iota(I32, (nsp, QBLK), 1), SEL_SHIFT)
    valid = (jt <= cur) & (jt < n_sel)
    forced = (jt == 0) | (jt >= cur - 1)
    score = jnp.where(forced, FORCED_SCORE, imp_t)
    score = jnp.where(valid, score, -1.0)
    rank = jnp.zeros((nsp, QBLK), I32)
    for k in range(n_sel):
        sk = score[k:k + 1, :]
        beats = (sk > score) | ((sk == score) & (k < jt))
        rank = rank + beats.astype(I32)
    sel_t = jnp.where((rank < N_SEL) & valid, 1.0, 0.0)
    sel_scr[...] = jnp.concatenate([sel_t] * GROUP, axis=1)

    wkeys = WINDOW + QBLK
    ks = pl.multiple_of(jnp.maximum(q0 - WINDOW, 0), QBLK)
    s_scr[0:wkeys, :] = lax.dot_general(kwin_ref[0, pl.ds(ks, wkeys), :], q4, _NT, preferred_element_type=F32)

    def win_mask(c):
        dist = qpos_row - (ks + c * KCHUNK + row_in_chunk)
        return (dist >= 0) & (dist < WINDOW)

    n_wchunks = wkeys // KCHUNK
    inv_w = _softmax_passes(s_scr, p_scr, 0, n_wchunks, n_wchunks, win_mask, win_mask)
    o_w = jnp.dot(vwin_t[:, pl.ds(ks, wkeys)], p_scr[0:wkeys, :], preferred_element_type=F32) * inv_w

    n_pieces = qi // (KPIECE // QBLK) + 1

    def score_piece(c, carry):
        r0 = pl.multiple_of(c * KPIECE, KPIECE)
        s_scr[pl.ds(r0, KPIECE), :] = lax.dot_general(ksel_ref[0, pl.ds(r0, KPIECE), :], q4, _NT,
                                                      preferred_element_type=F32)
        return carry

    lax.fori_loop(0, n_pieces, score_piece, 0)

    def picked(c):
        return sel_scr[pl.ds(c, 1), :] > 0.5

    def picked_causal(c):
        return picked(c) & (c * KCHUNK + row_in_chunk <= qpos_row)

    chunks_q = QBLK // KCHUNK
    inv_s = _softmax_passes(s_scr, p_scr, qi * chunks_q, chunks_q, n_pieces * (KPIECE // KCHUNK),
                            picked, picked_causal)

    def pv_piece(c, acc):
        r0 = pl.multiple_of(c * KPIECE, KPIECE)
        return acc + jnp.dot(vsel_t[:, pl.ds(r0, KPIECE)], p_scr[pl.ds(r0, KPIECE), :], preferred_element_type=F32)

    o_s = lax.fori_loop(0, n_pieces, pv_piece, jnp.zeros((HEAD_DIM, cols), F32)) * inv_s

    gates_t = jax.nn.sigmoid(bg_ref[0]).T
    grow = lambda br: jnp.concatenate([gates_t[br * GROUP + g:br * GROUP + g + 1, :] for g in range(GROUP)], axis=1)
    o_t = grow(0) * o_c + grow(1) * o_s + grow(2) * o_w
    for g in range(GROUP):
        cs = slice(g * HEAD_DIM, (g + 1) * HEAD_DIM)
        o_nsa = o_t[:, g * QBLK:(g + 1) * QBLK].T
        merged = jax.nn.sigmoid(mga_ref[0, :, cs]) * rnn_ref[0, :, cs] + jax.nn.sigmoid(mgb_ref[0, :, cs]) * o_nsa
        o_ref[0, :, cs] = merged.astype(o_ref.dtype)


def attn_prompt_s(q3, sel_planes, win_planes, kvc, ubg3, umg3, o_rnn3, n_cmp):
    b, t, _ = q3.shape
    nb = kvc.shape[2]
    assert KCHUNK == SEL_BLK and t % KPIECE == 0 and t >= WINDOW + QBLK
    n_sel = -(-t // SEL_BLK)
    nsp = max(8, -(-n_sel // 8) * 8)
    m_t = _overlap_matrix(n_cmp, nb, n_sel, nsp).T
    kspec = lambda: pl.BlockSpec((1, t, HEAD_DIM), lambda i, h, j: (2 * h, i, 0))
    vspec = lambda: pl.BlockSpec((1, t, HEAD_DIM), lambda i, h, j: (2 * h + 1, i, 0))
    cols = GROUP * QBLK
    head_cols = GROUP * HEAD_DIM
    tile = lambda off: pl.BlockSpec((1, QBLK, head_cols), lambda i, h, j: (i, j, off + h))
    return pl.pallas_call(
        functools.partial(_attn_prompt_s_body, n_cmp=n_cmp, n_sel=n_sel),
        grid=(b, N_KV_HEADS, t // QBLK),
        in_specs=[
            tile(0),
            kspec(), vspec(), kspec(), vspec(),
            pl.BlockSpec((1, 2, nb, HEAD_DIM), lambda i, h, j: (i, h, 0, 0)),
            pl.BlockSpec((1, QBLK, LANES), lambda i, h, j: (i, j, h)),
            tile(0), tile(N_KV_HEADS), tile(0),
            pl.BlockSpec((nsp, nb), lambda i, h, j: (0, 0)),
        ],
        out_specs=tile(0),
        out_shape=jax.ShapeDtypeStruct((b, t, N_HEADS * HEAD_DIM), BF16),
        scratch_shapes=[pltpu.VMEM((HEAD_DIM, t), BF16), pltpu.VMEM((HEAD_DIM, t), BF16),
                        pltpu.VMEM((nb, HEAD_DIM), BF16), pltpu.VMEM((HEAD_DIM, nb), BF16),
                        pltpu.VMEM((t, cols), F32), pltpu.VMEM((t, cols), BF16), pltpu.VMEM((nsp, cols), F32)],
        compiler_params=_params(("arbitrary", "arbitrary", "arbitrary"), 48),
        name="attn_prompt",
    )(q3, sel_planes, sel_planes, win_planes, win_planes, kvc, ubg3, umg3, umg3, o_rnn3, jnp.asarray(m_t, BF16))


def _row_head(shape):
    return lax.shift_right_logical(lax.broadcasted_iota(I32, shape, 0), 2)


def _cmp_sample_body(q_ref, kvc_ref, m_ref, oc_ref, imp_ref, *, qpos, n_cmp):
    qb = q_ref[0].astype(BF16)
    nb = kvc_ref.shape[2]
    rg = _row_head((N_HEADS, nb))
    ncol = lax.broadcasted_iota(I32, (N_HEADS, nb), 1)
    mask = (ncol * CMP_STRIDE + (CMP_BLK - 1) <= qpos) & (ncol < n_cmp)
    row8 = lax.broadcasted_iota(I32, (8, nb), 0)
    o = jnp.zeros((N_HEADS, HEAD_DIM), F32)
    ps8 = jnp.zeros((8, nb), F32)
    for h in range(N_KV_HEADS):
        kc = kvc_ref[0, 2 * h].astype(BF16)
        vc = kvc_ref[0, 2 * h + 1].astype(BF16)
        s = lax.dot_general(qb, kc, _NT, preferred_element_type=F32) * ATTN_SCALE
        ph = jnp.where(rg == h, _msoftmax(s, mask), 0.0)
        o = o + jnp.dot(ph.astype(BF16), vc, preferred_element_type=F32)
        ps8 = jnp.where(row8 == h, jnp.sum(ph, axis=0, keepdims=True), ps8)
    oc_ref[0] = o
    imp_ref[0] = jnp.dot(ps8.astype(BF16), m_ref[...], preferred_element_type=F32)[0:N_KV_HEADS]


def cmp_sample(q16, kvc, m_mat, qpos, n_cmp):
    ns = q16.shape[0]
    nb = kvc.shape[2]
    nsp = m_mat.shape[1]
    return pl.pallas_call(
        functools.partial(_cmp_sample_body, qpos=qpos, n_cmp=n_cmp),
        grid=(ns,),
        in_specs=[pl.BlockSpec((1, N_HEADS, HEAD_DIM), lambda b: (b, 0, 0)),
                  pl.BlockSpec((1, HK, nb, HEAD_DIM), lambda b: (b, 0, 0, 0)),
                  pl.BlockSpec((nb, nsp), lambda b: (0, 0))],
        out_specs=[pl.BlockSpec((1, N_HEADS, HEAD_DIM), lambda b: (b, 0, 0)),
                   pl.BlockSpec((1, N_KV_HEADS, nsp), lambda b: (b, 0, 0))],
        out_shape=[jax.ShapeDtypeStruct((ns, N_HEADS, HEAD_DIM), F32),
                   jax.ShapeDtypeStruct((ns, N_KV_HEADS, nsp), F32)],
        compiler_params=_params(("arbitrary",), 40),
        name="cmp_sample",
    )(q16, kvc, m_mat)


def _rank_sample_body(imp_ref, idx_ref, sc_scr, *, cur, n_sel):
    npad, nr = sc_scr.shape
    st = imp_ref[...].T
    j = lax.broadcasted_iota(I32, (npad, nr), 0)
    real = j < n_sel
    valid = (j <= cur) & real
    forced = (j == 0) | (j >= cur - 1)
    score = jnp.where(forced, FORCED_SCORE, st)
    score = jnp.where(valid, score, -1.0)
    score = jnp.where(real, score, -2.0)
    sc_scr[...] = score

    def body(k, rank):
        sk = sc_scr[pl.ds(k, 1), :]
        beats = (sk > score) | ((sk == score) & (k < j))
        return rank + beats.astype(I32)

    rank = lax.fori_loop(0, n_sel, body, jnp.zeros((npad, nr), I32))
    sel = (rank < N_SEL) & valid
    for slot in range(N_SEL):
        hit = sel & (rank == slot)
        found = jnp.max(hit.astype(I32), axis=0, keepdims=True)
        val = jnp.sum(jnp.where(hit, j, 0), axis=0, keepdims=True)
        idx_ref[slot:slot + 1, :] = jnp.where(found > 0, val, -1)


def rank_sample(imp2, cur, n_sel):
    nr, npad = imp2.shape
    return pl.pallas_call(
        functools.partial(_rank_sample_body, cur=cur, n_sel=n_sel),
        out_shape=jax.ShapeDtypeStruct((N_SEL, nr), I32),
        scratch_shapes=[pltpu.VMEM((npad, nr), F32)],
        name="rank_sample",
    )(imp2)


def _sel_sample_body(idx_ref, pt_ref, q_ref, kn_ref, vn_ref, pool_hbm, o_ref, kbuf, vbuf, sem, *, n_past_blocks):
    b = pl.program_id(0)
    slot_buf = lax.rem(b, 2)
    nk = N_SEL * SEL_BLK

    def issue(seq, sl):
        for h in range(N_KV_HEADS):
            for s in range(N_SEL):
                blk = jnp.clip(idx_ref[s, seq * N_KV_HEADS + h], 0, n_past_blocks - 1)
                pos0 = (pt_ref[seq, lax.shift_right_logical(blk, 1)] * PAGE_SIZE
                        + jnp.bitwise_and(blk, 1) * SEL_BLK)
                rows = pl.ds(h * nk + s * SEL_BLK, SEL_BLK)
                pltpu.make_async_copy(pool_hbm.at[pl.ds(pos0, SEL_BLK), 2 * h, :], kbuf.at[sl, rows, :],
                                      sem.at[sl]).start()
                pltpu.make_async_copy(pool_hbm.at[pl.ds(pos0, SEL_BLK), 2 * h + 1, :], vbuf.at[sl, rows, :],
                                      sem.at[sl]).start()

    @pl.when(b == 0)
    def _():
        issue(b, 0)

    @pl.when(b + 1 < pl.num_programs(0))
    def _():
        issue(b + 1, 1 - slot_buf)

    pltpu.make_async_copy(pool_hbm.at[pl.ds(0, N_KV_HEADS * nk), 0, :], kbuf.at[slot_buf], sem.at[slot_buf]).wait()
    pltpu.make_async_copy(pool_hbm.at[pl.ds(0, N_KV_HEADS * nk), 0, :], vbuf.at[slot_buf], sem.at[slot_buf]).wait()

    q = q_ref[0]
    qb = q.astype(BF16)
    rg = _row_head((N_HEADS, nk))
    col_slot = lax.shift_right_logical(lax.broadcasted_iota(I32, (N_HEADS, nk), 1), SEL_SHIFT)

    sc = jnp.zeros((N_HEADS, nk), F32)
    okv = jnp.zeros((N_HEADS, nk), I32)
    for h in range(N_KV_HEADS):
        kcat = kbuf[slot_buf, pl.ds(h * nk, nk), :].astype(BF16)
        sh = lax.dot_general(qb, kcat, _NT, preferred_element_type=F32)
        ok = jnp.zeros((N_HEADS, nk), I32)
        for s in range(N_SEL):
            blk = idx_ref[s, b * N_KV_HEADS + h]
            ok = jnp.where(col_slot == s, jnp.where((blk >= 0) & (blk < n_past_blocks), 1, 0), ok)
        sc = jnp.where(rg == h, sh, sc)
        okv = jnp.where(rg == h, ok, okv)
    sm = jnp.where(okv > 0, sc * ATTN_SCALE, MASK_VALUE)
    s_new = jnp.sum(q * kn_ref[0], axis=-1, keepdims=True) * ATTN_SCALE
    m = jnp.maximum(jnp.max(sm, axis=-1, keepdims=True), s_new)
    p = jnp.where(okv > 0, jnp.exp(sm - m), 0.0)
    p_new = jnp.exp(s_new - m)
    den = jnp.sum(p, axis=-1, keepdims=True) + p_new
    pv = jnp.zeros((N_HEADS, HEAD_DIM), F32)
    for h in range(N_KV_HEADS):
        vcat = vbuf[slot_buf, pl.ds(h * nk, nk), :].astype(BF16)
        pv = pv + jnp.dot(jnp.where(rg == h, p, 0.0).astype(BF16), vcat, preferred_element_type=F32)
    o_ref[0] = (pv + p_new * vn_ref[0]) / den


def sel_sample(idx, page_table, q16, kn16, vn16, pool3, n_past_blocks):
    ns = q16.shape[0]
    assert PAGE_SIZE == 2 * SEL_BLK
    head3 = lambda: pl.BlockSpec((1, N_HEADS, HEAD_DIM), lambda b, i, p: (b, 0, 0))
    buf_rows = N_KV_HEADS * N_SEL * SEL_BLK
    grid_spec = pltpu.PrefetchScalarGridSpec(
        num_scalar_prefetch=2,
        grid=(ns,),
        in_specs=[head3(), head3(), head3(), pl.BlockSpec(memory_space=pl.ANY)],
        out_specs=head3(),
        scratch_shapes=[pltpu.VMEM((2, buf_rows, HEAD_DIM), F32), pltpu.VMEM((2, buf_rows, HEAD_DIM), F32),
                        pltpu.SemaphoreType.DMA((2,))],
    )
    return pl.pallas_call(
        functools.partial(_sel_sample_body, n_past_blocks=n_past_blocks),
        grid_spec=grid_spec,
        out_shape=jax.ShapeDtypeStruct((ns, N_HEADS, HEAD_DIM), F32),
        compiler_params=_params(("arbitrary",), 40),
        name="sel_sample",
    )(idx, page_table, q16, kn16, vn16, pool3)


def _win_sample_body(q_ref, win_ref, new_ref, kn_ref, vn_ref, oc_ref, os_ref, bg_ref, o_ref, wout_ref, *, wbuf):
    wout_ref[0:(wbuf - 1) * HK, :] = win_ref[HK:wbuf * HK, :]
    wout_ref[(wbuf - 1) * HK:wbuf * HK, :] = new_ref[...]
    q = q_ref[0]
    qb = q.astype(BF16)
    rg = _row_head((N_HEADS, wbuf))
    sc = jnp.zeros((N_HEADS, wbuf), F32)
    for h in range(N_KV_HEADS):
        kh = win_ref[pl.ds(2 * h, wbuf, stride=HK), :].astype(BF16)
        sc = jnp.where(rg == h, lax.dot_general(qb, kh, _NT, preferred_element_type=F32), sc)
    mask = lax.broadcasted_iota(I32, (N_HEADS, wbuf), 1) > wbuf - WINDOW
    sm = jnp.where(mask, sc * ATTN_SCALE, MASK_VALUE)
    s_new = jnp.sum(q * kn_ref[0], axis=-1, keepdims=True) * ATTN_SCALE
    m = jnp.maximum(jnp.max(sm, axis=-1, keepdims=True), s_new)
    p = jnp.where(mask, jnp.exp(sm - m), 0.0)
    p_new = jnp.exp(s_new - m)
    den = jnp.sum(p, axis=-1, keepdims=True) + p_new
    pv = jnp.zeros((N_HEADS, HEAD_DIM), F32)
    for h in range(N_KV_HEADS):
        vh = win_ref[pl.ds(2 * h + 1, wbuf, stride=HK), :].astype(BF16)
        pv = pv + jnp.dot(jnp.where(rg == h, p, 0.0).astype(BF16), vh, preferred_element_type=F32)
    o_w = (pv + p_new * vn_ref[0]) / den
    gates = jax.nn.sigmoid(bg_ref[0])
    o_ref[0] = gates[:, 0:1] * oc_ref[0] + gates[:, 1:2] * os_ref[0] + gates[:, 2:3] * o_w


def win_sample(q16, win2d, new2d, kn16, vn16, o_cmp, o_sel, bgs, wbuf):
    ns = q16.shape[0]
    head3 = lambda: pl.BlockSpec((1, N_HEADS, HEAD_DIM), lambda b: (b, 0, 0))
    wspec = lambda: pl.BlockSpec((wbuf * HK, LANES), lambda b: (b, 0))
    return pl.pallas_call(
        functools.partial(_win_sample_body, wbuf=wbuf),
        grid=(ns,),
        in_specs=[head3(), wspec(), pl.BlockSpec((HK, LANES), lambda b: (b, 0)), head3(), head3(), head3(), head3(),
                  head3()],
        out_specs=[head3(), wspec()],
        out_shape=[jax.ShapeDtypeStruct((ns, N_HEADS, HEAD_DIM), F32),
                   jax.ShapeDtypeStruct((ns * wbuf * HK, LANES), F32)],
        compiler_params=_params(("arbitrary",), 40),
        name="win_sample",
    )(q16, win2d, new2d, kn16, vn16, o_cmp, o_sel, bgs)


ROUTE_LANE0 = N_GROUPS


INFO_E1, INFO_E2, INFO_R1, INFO_R2, INFO_C1, INFO_C2 = range(6)


def _route_body(xp_ref, xs_ref, g_ref, wr_ref, br_ref, xn_ref, info_ref, cnt_ref, run_scr, *, n_real, n_ptiles):
    step = pl.program_id(0)

    @pl.when(step == 0)
    def _():
        run_scr[...] = jnp.zeros(run_scr.shape, F32)

    refs = (g_ref, wr_ref, br_ref, xn_ref, info_ref, cnt_ref, run_scr)

    @pl.when(step < n_ptiles)
    def _():
        _route_tile(xp_ref[...], step, n_real, *refs)

    @pl.when(step >= n_ptiles)
    def _():
        _route_tile(xs_ref[...], step, n_real, *refs)


def _route_tile(x, step, n_real, g_ref, wr_ref, br_ref, xn_ref, info_ref, cnt_ref, run_scr):
    tm = x.shape[0]
    var = jnp.mean(x * x, axis=-1, keepdims=True)
    xn = (x * lax.rsqrt(var + EPS)) * g_ref[...]
    xn_ref[...] = xn
    xnb = xn.astype(BF16)
    logits = jnp.dot(xnb, wr_ref[...], preferred_element_type=F32) + br_ref[...]
    lane = lax.broadcasted_iota(I32, logits.shape, 1)
    big = 4 * LANES

    isg = lane < N_GROUPS
    lg = jnp.where(isg, logits, -jnp.inf)
    mg = jnp.max(lg, axis=-1, keepdims=True)
    gi = jnp.min(jnp.where(lg == mg, lane, big), axis=-1, keepdims=True)
    pg = 1.0 / jnp.sum(jnp.where(isg, jnp.exp(lg - mg), 0.0), axis=-1, keepdims=True)

    lo = ROUTE_LANE0 + gi * EXP_PER_GROUP
    ing = (lane >= lo) & (lane < lo + EXP_PER_GROUP)
    le = jnp.where(ing, logits, -jnp.inf)
    me = jnp.max(le, axis=-1, keepdims=True)
    ee = jnp.where(ing, jnp.exp(le - me), 0.0)
    pe = jnp.where(ing, ee / jnp.sum(ee, axis=-1, keepdims=True), -1.0)
    p1 = jnp.max(pe, axis=-1, keepdims=True)
    i1 = jnp.min(jnp.where(pe == p1, lane, big), axis=-1, keepdims=True)
    pe2 = jnp.where(lane == i1, -1.0, pe)
    p2 = jnp.max(pe2, axis=-1, keepdims=True)
    i2 = jnp.min(jnp.where(pe2 == p2, lane, big), axis=-1, keepdims=True)
    tot = p1 + p2
    c1 = pg * (p1 / tot)
    c2 = pg * (p2 / tot)

    row = step * tm + lax.broadcasted_iota(I32, logits.shape, 0)
    hit = jnp.where(((lane == i1) | (lane == i2)) & (row < n_real), 1.0, 0.0)
    tri = jnp.where(lax.broadcasted_iota(I32, (tm, tm), 1) < lax.broadcasted_iota(I32, (tm, tm), 0), 1.0, 0.0)
    before = run_scr[...] + jnp.dot(tri.astype(BF16), hit.astype(BF16), preferred_element_type=F32)
    r1 = jnp.sum(jnp.where(lane == i1, before, 0.0), axis=-1, keepdims=True)
    r2 = jnp.sum(jnp.where(lane == i2, before, 0.0), axis=-1, keepdims=True)
    run_new = run_scr[...] + jnp.sum(hit, axis=0, keepdims=True)
    run_scr[...] = run_new
    cnt_ref[...] = run_new

    info = jnp.zeros(logits.shape, F32)
    for ln, val in ((INFO_E1, (i1 - ROUTE_LANE0).astype(F32)), (INFO_E2, (i2 - ROUTE_LANE0).astype(F32)),
                    (INFO_R1, r1), (INFO_R2, r2), (INFO_C1, c1), (INFO_C2, c2)):
        info = jnp.where(lane == ln, val, info)
    info_ref[...] = info


MOE_TILE = 256


def route(x_prompt, x_sample, g, wr, br, n_real):
    mp, d = x_prompt.shape
    n_ptiles = mp // MOE_TILE
    assert mp % MOE_TILE == 0 and x_sample.shape[0] == MOE_TILE
    m = mp + MOE_TILE
    return pl.pallas_call(
        functools.partial(_route_body, n_real=n_real, n_ptiles=n_ptiles),
        grid=(n_ptiles + 1,),
        in_specs=[pl.BlockSpec((MOE_TILE, d), lambda i: (jnp.minimum(i, n_ptiles - 1), 0)),
                  pl.BlockSpec((MOE_TILE, d), lambda i: (0, 0)), pl.BlockSpec((1, d), lambda i: (0, 0)),
                  pl.BlockSpec((d, LANES), lambda i: (0, 0)), pl.BlockSpec((1, LANES), lambda i: (0, 0))],
        out_specs=[pl.BlockSpec((MOE_TILE, d), lambda i: (i, 0)), pl.BlockSpec((MOE_TILE, LANES), lambda i: (i, 0)),
                   pl.BlockSpec((1, LANES), lambda i: (0, 0))],
        out_shape=[jax.ShapeDtypeStruct((m, d), F32), jax.ShapeDtypeStruct((m, LANES), F32),
                   jax.ShapeDtypeStruct((1, LANES), F32)],
        scratch_shapes=[pltpu.VMEM((1, LANES), F32)],
        compiler_params=_params(("arbitrary",), 40),
        name="route",
    )(x_prompt, x_sample, g.reshape(1, d), wr, br)


def _moe_plan(info, counts_row, n_real, n_tiles):
    n_pad = info.shape[0]
    e = jnp.clip(info[:, INFO_E1:INFO_E2 + 1].astype(I32), 0, N_EXPERTS - 1)
    r = info[:, INFO_R1:INFO_R2 + 1].astype(I32)
    counts = counts_row[0, ROUTE_LANE0:ROUTE_LANE0 + N_EXPERTS].astype(I32)
    padded = ((counts + MOE_TILE - 1) // MOE_TILE) * MOE_TILE
    ex = jnp.arange(N_EXPERTS, dtype=I32)
    ends = jnp.sum(jnp.where(ex[None, :] <= ex[:, None], padded[None, :], 0), axis=1)
    offs = ends - padded
    tok = jnp.arange(n_pad, dtype=I32)
    valid = (tok < n_real)[:, None]
    dest = jnp.where(valid, offs[e] + r, 0)
    n_slots = n_tiles * MOE_TILE
    src = jnp.zeros((n_slots,), I32).at[jnp.where(valid, dest, n_slots).reshape(-1)].set(
        jnp.repeat(tok, 2), mode="drop")
    n_used = ends[N_EXPERTS - 1] // MOE_TILE
    tiles = jnp.arange(n_tiles, dtype=I32)
    used = tiles < n_used
    te = jnp.minimum(jnp.sum((ends[None, :] <= (tiles * MOE_TILE)[:, None]).astype(I32), axis=1), N_EXPERTS - 1)
    te_prev = jnp.concatenate([te[:1] - 1, te[:-1]])
    first = (te != te_prev) & used
    later = tiles[None, :] > tiles[:, None]
    run = jnp.sum((first[None, :] & ~later).astype(I32), axis=1) - 1
    nxt_tile = jnp.min(jnp.where(first[None, :] & later, tiles[None, :], n_tiles), axis=1)
    nxt = jnp.where(nxt_tile < n_tiles, te[jnp.minimum(nxt_tile, n_tiles - 1)], -1)
    tile3 = lambda v: v.reshape(-1, 1, MOE_TILE)
    return (tile3(src), tile3(dest[:, 0]), tile3(dest[:, 1]),
            (te, first.astype(I32), nxt, jnp.bitwise_and(run, 1), n_used.reshape(1)))


def _issue_rows(idx_ref, src_hbm, dst_buf, slot, sem):
    n = idx_ref.shape[2]
    group = 8

    def body(g, carry):
        for j in range(group):
            r = g * group + j
            pltpu.make_async_copy(src_hbm.at[pl.ds(idx_ref[0, 0, r], 1), :], dst_buf.at[slot, pl.ds(r, 1), :],
                                  sem.at[slot]).start(priority=j % 2)
        return carry

    lax.fori_loop(0, n // group, body, 0)


def _wait_rows(src_hbm, dst_buf, slot, sem):
    n = dst_buf.shape[1]
    pltpu.make_async_copy(src_hbm.at[pl.ds(0, n), :], dst_buf.at[slot], sem.at[slot]).wait()


def _moe_dispatch_body(nu_ref, src_cur, src_nxt, xn_hbm, xs_ref, xbuf, sem):
    i = pl.program_id(0)
    n_used = nu_ref[0]
    slot = lax.rem(i, 2)

    @pl.when(i == 0)
    def _():
        _issue_rows(src_cur, xn_hbm, xbuf, 0, sem)

    @pl.when(i + 1 < n_used)
    def _():
        _issue_rows(src_nxt, xn_hbm, xbuf, 1 - slot, sem)

    @pl.when(i < n_used)
    def _():
        _wait_rows(xn_hbm, xbuf, slot, sem)
        xs_ref[...] = xbuf[slot].astype(xs_ref.dtype)

    @pl.when(i >= n_used)
    def _():
        xs_ref[...] = jnp.zeros(xs_ref.shape, xs_ref.dtype)


def moe_dispatch(n_used, src3, xn):
    n_tiles = src3.shape[0]
    d = xn.shape[1]
    smem_tile = lambda imap: pl.BlockSpec((1, 1, MOE_TILE), imap, memory_space=pltpu.SMEM)
    grid_spec = pltpu.PrefetchScalarGridSpec(
        num_scalar_prefetch=1,
        grid=(n_tiles,),
        in_specs=[smem_tile(lambda i, nu: (i, 0, 0)),
                  smem_tile(lambda i, nu: (jnp.minimum(i + 1, n_tiles - 1), 0, 0)),
                  pl.BlockSpec(memory_space=pl.ANY)],
        out_specs=pl.BlockSpec((MOE_TILE, d), lambda i, nu: (i, 0)),
        scratch_shapes=[pltpu.VMEM((2, MOE_TILE, d), F32), pltpu.SemaphoreType.DMA((2,))],
    )
    return pl.pallas_call(
        _moe_dispatch_body,
        grid_spec=grid_spec,
        out_shape=jax.ShapeDtypeStruct((n_tiles * MOE_TILE, d), BF16),
        compiler_params=_params(("arbitrary",), 24),
        name="moe_dispatch",
    )(n_used, src3, src3, xn)


def _moe_expert_body(te_ref, first_ref, nxt_ref, par_ref, nu_ref, x_ref, wg_hbm, wu_hbm, wd_hbm,
                     y_ref, wg_buf, wu_buf, wd_buf, wsem, wgb, wub, wdb):
    i = pl.program_id(0)
    n_used = nu_ref[0]

    def weight_copies(e, ws):
        return (pltpu.make_async_copy(wg_hbm.at[e], wg_buf.at[ws], wsem.at[0, ws]),
                pltpu.make_async_copy(wu_hbm.at[e], wu_buf.at[ws], wsem.at[1, ws]),
                pltpu.make_async_copy(wd_hbm.at[e], wd_buf.at[ws], wsem.at[2, ws]))

    @pl.when(i == 0)
    def _():
        for cp in weight_copies(te_ref[0], par_ref[0]):
            cp.start()

    @pl.when(i < n_used)
    def _():
        @pl.when(first_ref[i] == 1)
        def _():
            ws = par_ref[i]
            for cp in weight_copies(te_ref[i], ws):
                cp.wait()

            @pl.when(nxt_ref[i] >= 0)
            def _():
                for cp in weight_copies(nxt_ref[i], 1 - ws):
                    cp.start()

            wgb[...] = wg_buf[ws].astype(BF16)
            wub[...] = wu_buf[ws].astype(BF16)
            wdb[...] = wd_buf[ws].astype(BF16)

        x = x_ref[...]
        hg = jnp.dot(x, wgb[...], preferred_element_type=F32)
        hu = jnp.dot(x, wub[...], preferred_element_type=F32)
        hid = (hg * jax.nn.sigmoid(hg)) * hu
        y_ref[...] = jnp.dot(hid.astype(BF16), wdb[...], preferred_element_type=F32)

    @pl.when(i >= n_used)
    def _():
        y_ref[...] = jnp.zeros(y_ref.shape, F32)


def moe_experts(tile_plan, xs, wg, wu, wd):
    rows, d = xs.shape
    n_tiles = rows // MOE_TILE
    _, _, de = wg.shape
    hbm = lambda: pl.BlockSpec(memory_space=pl.ANY)
    grid_spec = pltpu.PrefetchScalarGridSpec(
        num_scalar_prefetch=5,
        grid=(n_tiles,),
        in_specs=[pl.BlockSpec((MOE_TILE, d), lambda i, te, fi, nx, pa, nu: (jnp.minimum(i, nu[0] - 1), 0)),
                  hbm(), hbm(), hbm()],
        out_specs=pl.BlockSpec((MOE_TILE, d), lambda i, *_: (i, 0)),
        scratch_shapes=[pltpu.VMEM((2, d, de), F32), pltpu.VMEM((2, d, de), F32), pltpu.VMEM((2, de, d), F32),
                        pltpu.SemaphoreType.DMA((3, 2)),
                        pltpu.VMEM((d, de), BF16), pltpu.VMEM((d, de), BF16), pltpu.VMEM((de, d), BF16)],
    )
    return pl.pallas_call(
        _moe_expert_body,
        grid_spec=grid_spec,
        out_shape=jax.ShapeDtypeStruct((rows, d), F32),
        compiler_params=_params(("arbitrary",), 56),
        name="moe_experts",
    )(*tile_plan, xs, wg, wu, wd)


def _moe_combine_body(d1_cur, d1_nxt, d2_cur, d2_nxt, ys_hbm, xp_ref, xs_ref, info_ref, gf_ref, yp_ref, ysm_ref,
                      buf1, buf2, sem1, sem2, *, n_ptiles):
    i = pl.program_id(0)
    slot = lax.rem(i, 2)

    @pl.when(i == 0)
    def _():
        _issue_rows(d1_cur, ys_hbm, buf1, 0, sem1)
        _issue_rows(d2_cur, ys_hbm, buf2, 0, sem2)

    @pl.when(i + 1 < pl.num_programs(0))
    def _():
        _issue_rows(d1_nxt, ys_hbm, buf1, 1 - slot, sem1)
        _issue_rows(d2_nxt, ys_hbm, buf2, 1 - slot, sem2)

    _wait_rows(ys_hbm, buf1, slot, sem1)
    _wait_rows(ys_hbm, buf2, slot, sem2)

    def finish(x_ref, y_ref):
        info = info_ref[...]
        xo = x_ref[...] + (info[:, INFO_C1:INFO_C1 + 1] * buf1[slot] + info[:, INFO_C2:INFO_C2 + 1] * buf2[slot])
        var = jnp.mean(xo * xo, axis=-1, keepdims=True)
        y_ref[...] = (xo * lax.rsqrt(var + EPS)) * gf_ref[...]

    @pl.when(i < n_ptiles)
    def _():
        finish(xp_ref, yp_ref)

    @pl.when(i >= n_ptiles)
    def _():
        finish(xs_ref, ysm_ref)


def moe_combine(dest1, dest2, ys, x_prompt, x_sample, info, g_final):
    mp, d = x_prompt.shape
    n_ptiles = mp // MOE_TILE
    n_tiles = n_ptiles + 1
    cur = lambda: pl.BlockSpec((1, 1, MOE_TILE), lambda i: (i, 0, 0), memory_space=pltpu.SMEM)
    nxt = lambda: pl.BlockSpec((1, 1, MOE_TILE), lambda i: (jnp.minimum(i + 1, n_tiles - 1), 0, 0),
                               memory_space=pltpu.SMEM)
    ptile = lambda: pl.BlockSpec((MOE_TILE, d), lambda i: (jnp.minimum(i, n_ptiles - 1), 0))
    stile = lambda: pl.BlockSpec((MOE_TILE, d), lambda i: (0, 0))
    return pl.pallas_call(
        functools.partial(_moe_combine_body, n_ptiles=n_ptiles),
        grid=(n_tiles,),
        in_specs=[cur(), nxt(), cur(), nxt(), pl.BlockSpec(memory_space=pl.ANY), ptile(), stile(),
                  pl.BlockSpec((MOE_TILE, LANES), lambda i: (i, 0)), pl.BlockSpec((1, d), lambda i: (0, 0))],
        out_specs=[ptile(), stile()],
        out_shape=[jax.ShapeDtypeStruct((mp, d), F32), jax.ShapeDtypeStruct((MOE_TILE, d), F32)],
        scratch_shapes=[pltpu.VMEM((2, MOE_TILE, d), F32), pltpu.VMEM((2, MOE_TILE, d), F32),
                        pltpu.SemaphoreType.DMA((2,)), pltpu.SemaphoreType.DMA((2,))],
        compiler_params=_params(("arbitrary",), 40),
        name="moe_combine",
    )(dest1, dest1, dest2, dest2, ys, x_prompt, x_sample, info, g_final.reshape(1, d))


def _mixer_inputs(x2, g_mix, w_in_t, w_bg_t, w_mg_t, tm, q_dtype):
    tn = PROJ_TILE_N
    xn = rmsnorm(x2, g_mix, tm, BF16)
    mm = functools.partial(matmul_cols, xn, tm=tm, w_is_t=True)
    u_rnn = mm(w_in_t, 0, D_RNN, tn=tn, out_dtype=F32, name="in_rnn")
    u_gate = mm(w_in_t, D_RNN, D_RNN, tn=tn, out_dtype=F32, name="in_gate")
    q = mm(w_in_t, 2 * D_RNN, N_HEADS * HEAD_DIM, tn=tn, out_dtype=q_dtype, name="in_q")
    kv0 = 2 * D_RNN + N_HEADS * HEAD_DIM
    kv_cmp, = matmul_kv(xn, w_in_t, kv0, tm, False, "in_kv_cmp")
    kv_sel, sel_planes = matmul_kv(xn, w_in_t, kv0 + 2 * KV_DIM, tm, True, "in_kv_sel")
    kv_win, win_planes = matmul_kv(xn, w_in_t, kv0 + 4 * KV_DIM, tm, True, "in_kv_win")
    u_bg = mm(w_bg_t, 0, w_bg_t.shape[0], tn=w_bg_t.shape[0], out_dtype=F32, name="in_bg")
    u_mg = mm(w_mg_t, 0, 2 * D_MODEL, tn=tn, out_dtype=F32, name="in_mg")
    return u_rnn, u_gate, q, (kv_cmp, kv_sel, kv_win), (sel_planes, win_planes), u_bg, u_mg


def kernel(x_prompt, x_sample, cache_cmp_kv, cache_sel_kv, cache_win_kv, state_conv, state_rglru, page_table, g_mix, w_in, conv_w, conv_b, lru_w_a, lru_b_a, lru_w_x, lru_b_x, lru_lambda, cmp_pe, cmp_w1, cmp_b1, cmp_w2, cmp_b2, w_out, g_ffn, moe_w_group, moe_b_group, moe_w_expert, moe_b_expert, moe_w_gate, moe_w_up, moe_w_down, g_final):
    b, t, d = x_prompt.shape
    ns = x_sample.shape[0]
    npages = page_table.shape[1]
    past_len = npages * PAGE_SIZE
    wbuf = cache_win_kv.shape[1]
    kvshape = (N_KV_HEADS, 2, HEAD_DIM)

    w_in_t = w_in.T
    bg0 = 2 * D_RNN + N_HEADS * HEAD_DIM + 6 * KV_DIM
    n_bg = 3 * N_HEADS
    w_bg = w_in_t[bg0:bg0 + n_bg].reshape(3, N_KV_HEADS, GROUP, d).transpose(1, 0, 2, 3).reshape(N_KV_HEADS, 3 * GROUP, d)
    w_bg = jnp.pad(w_bg, ((0, 0), (0, LANES - 3 * GROUP), (0, 0))).reshape(N_KV_HEADS * LANES, d)
    w_mg = w_in_t[bg0 + n_bg:]
    wa = lru_w_a.astype(BF16)
    wx = lru_w_x.astype(BF16)
    half_rows = CMP_STRIDE * HEAD_DIM
    wc = jnp.concatenate([cmp_w1[:, :half_rows], cmp_w1[:, half_rows:]], axis=-1).astype(BF16)
    pe8 = jnp.pad(cmp_pe.reshape(2, 2, half_rows), ((0, 0), (0, 6), (0, 0)))
    b1 = cmp_b1.reshape(2, 1, CMP_HIDDEN)
    w2 = cmp_w2.astype(BF16)
    b2 = cmp_b2.reshape(2, 1, HEAD_DIM)
    n_route = N_GROUPS + N_EXPERTS
    wr = jnp.pad(jnp.concatenate([moe_w_group, moe_w_expert], axis=1), ((0, 0), (0, LANES - n_route))).astype(BF16)
    br = jnp.pad(jnp.concatenate([moe_b_group, moe_b_expert]), (0, LANES - n_route)).reshape(1, LANES)
    lru_args = (conv_w, conv_b, wa, wx, lru_b_a, lru_b_x, lru_lambda)
    cmp_args = (wc, pe8, b1, w2, b2)

    xp2 = x_prompt.reshape(b * t, d)
    u_rnn, u_gate, q, kv2d, planes, u_bg, u_mg = _mixer_inputs(xp2, g_mix, w_in_t, w_bg, w_mg, PROJ_TILE_M, BF16)
    o_rnn, p_conv, p_h = lru_prompt(u_rnn.reshape(b, t, d), u_gate.reshape(b, t, d), *lru_args, LRU_TILE_T)
    p_cmp, p_sel, p_win_full = (a.reshape((b, t) + kvshape) for a in kv2d)
    p_win = p_win_full[:, t - min(WINDOW, t):]
    pt_prompt = jnp.arange(b * t // PAGE_SIZE, dtype=I32).reshape(b, t // PAGE_SIZE)
    kvc_p = compress(kv2d[0], pt_prompt, jnp.zeros((b * CMP_STRIDE * HK, HEAD_DIM), F32), *cmp_args)
    merged = attn_prompt_s(q.reshape(b, t, -1), planes[0], planes[1], kvc_p, u_bg.reshape(b, t, -1),
                           u_mg.reshape(b, t, -1), o_rnn, t // CMP_STRIDE - 1)
    x_mid = matmul_cols(merged.reshape(b * t, d), w_out, 0, d, PROJ_TILE_M, PROJ_TILE_N, F32, residual=xp2,
                        name="out_proj")

    xs2 = x_sample.reshape(ns, d)
    su_rnn, su_gate, sq, skv2d, _, su_bg, su_mg = _mixer_inputs(xs2, g_mix, w_in_t, w_bg, w_mg, ns, F32)
    so_rnn, cn, s_h = lru_sample(su_rnn, su_gate, state_conv.transpose(1, 0, 2), state_rglru, *lru_args)
    s_conv = cn.transpose(1, 0, 2)
    kvs = jnp.stack([a.reshape(ns, HK, HEAD_DIM) for a in skv2d], axis=1)
    s_cmp = kvs[:, 0].reshape((ns, 1) + kvshape)
    s_sel = kvs[:, 1].reshape((ns, 1) + kvshape)

    extra = jnp.concatenate([kvs[:, 0], jnp.zeros((ns, (CMP_STRIDE - 1) * HK, HEAD_DIM), F32)], axis=1)
    kvc_s = compress(cache_cmp_kv.reshape(-1, HEAD_DIM), page_table, extra.reshape(-1, HEAD_DIM), *cmp_args)
    n_cmp_s = -(-(past_len + 1) // CMP_STRIDE) - 1
    n_sel_s = -(-(past_len + 1) // SEL_BLK)
    nsp = -(-n_sel_s // LANES) * LANES
    q16 = sq.reshape(ns, N_HEADS, HEAD_DIM)
    m_s = jnp.asarray(_overlap_matrix(n_cmp_s, kvc_s.shape[2], n_sel_s, nsp), BF16)
    so_cmp, imp = cmp_sample(q16, kvc_s, m_s, past_len, n_cmp_s)
    idx = rank_sample(imp.reshape(ns * N_KV_HEADS, nsp), past_len // SEL_BLK, n_sel_s)

    def per_head_rows(new_kv, kv):
        return jnp.repeat(new_kv[:, kv::2], GROUP, axis=1)

    so_sel = sel_sample(idx, page_table, q16, per_head_rows(kvs[:, 1], 0), per_head_rows(kvs[:, 1], 1),
                        cache_sel_kv.reshape(-1, HK, HEAD_DIM), past_len // SEL_BLK)
    bgs = su_bg.reshape(ns, N_KV_HEADS, LANES)[:, :, :3 * GROUP].reshape(ns, N_KV_HEADS, 3, GROUP)
    bgs = jnp.pad(bgs.transpose(0, 1, 3, 2).reshape(ns, N_HEADS, 3), ((0, 0), (0, 0), (0, LANES - 3)))
    so_nsa, s_win2d = win_sample(q16, cache_win_kv.reshape(-1, HEAD_DIM), skv2d[2], per_head_rows(kvs[:, 2], 0),
                                 per_head_rows(kvs[:, 2], 1), so_cmp, so_sel, bgs, wbuf)
    s_win = s_win2d.reshape((ns, wbuf) + kvshape)
    s_merged = merge(su_mg, so_rnn, so_nsa.reshape(ns, d), ns)
    sx_mid = matmul_cols(s_merged, w_out, 0, d, ns, PROJ_TILE_N, F32, residual=xs2, name="out_proj_s")

    n_real = b * t + ns
    assert ns <= MOE_TILE
    sx_pad = jnp.pad(sx_mid, ((0, MOE_TILE - ns), (0, 0)))
    xn_all, info, counts = route(x_mid, sx_pad, g_ffn, wr, br, n_real)
    n_tiles = -(-(2 * n_real + N_EXPERTS * (MOE_TILE - 1)) // MOE_TILE)
    src3, dest1, dest2, tile_plan = _moe_plan(info, counts, n_real, n_tiles)
    xs = moe_dispatch(tile_plan[-1], src3, xn_all)
    ys = moe_experts(tile_plan, xs, moe_w_gate, moe_w_up, moe_w_down)
    y_p2, y_s2 = moe_combine(dest1, dest2, ys, x_mid, sx_pad, info, g_final)
    y_prompt = y_p2.reshape(b, t, d)
    y_sample = y_s2[:ns].reshape(ns, 1, d)

    return (y_prompt, y_sample, p_cmp, s_cmp, p_sel, s_sel, p_win, s_win, p_conv, s_conv,
            p_h.reshape(b, d), s_h)
```

```python
import functools

import jax
import jax.numpy as jnp
import numpy as np
from jax import lax
from jax.experimental import pallas as pl
from jax.experimental.pallas import tpu as pltpu

F32 = jnp.float32
BF16 = jnp.bfloat16
I32 = jnp.int32

D_MODEL = 2048
D_RNN = 2048
RNN_BLOCKS = 16
RNN_BLOCK_DIM = 128
CONV_W = 4
LRU_C = 8.0
N_HEADS = 16
HEAD_DIM = 128
N_KV_HEADS = 4
GROUP = 4
KV_DIM = N_KV_HEADS * HEAD_DIM
HK = 2 * N_KV_HEADS
CMP_STRIDE = 16
CMP_BLK = 32
CMP_HIDDEN = 256
SEL_BLK = 64
SEL_SHIFT = 6
N_SEL = 16
WINDOW = 512
PAGE_SIZE = 128
ATTN_SCALE = HEAD_DIM ** -0.5
FORCED_SCORE = 1e4
N_GROUPS = 4
EXP_PER_GROUP = 8
N_EXPERTS = 32
D_EXPERT = 512
EPS = 1e-6
MASK_VALUE = -1e30
LANES = 128
MIB = 1024 * 1024

PROJ_TILE_M = 1024
PROJ_TILE_N = 1024
LRU_TILE_T = 256

_NT = (((1,), (1,)), ((), ()))


def _params(sem, vmem_mib):
    return pltpu.CompilerParams(dimension_semantics=sem, vmem_limit_bytes=vmem_mib * MIB)


def _msoftmax(s, mask):
    sm = jnp.where(mask, s, MASK_VALUE)
    m = jnp.max(sm, axis=-1, keepdims=True)
    e = jnp.where(mask, jnp.exp(sm - m), 0.0)
    den = jnp.sum(e, axis=-1, keepdims=True)
    return e * jnp.where(den > 0.0, 1.0 / den, 0.0)


def _gelu_tanh(x):
    return 0.5 * x * (1.0 + jnp.tanh(0.7978845608028654 * (x + 0.044715 * (x * x * x))))


def _rmsnorm_body(x_ref, g_ref, o_ref):
    x = x_ref[...]
    var = jnp.mean(x * x, axis=-1, keepdims=True)
    o_ref[...] = ((x * lax.rsqrt(var + EPS)) * g_ref[...]).astype(o_ref.dtype)


def rmsnorm(x, g, tm, out_dtype):
    m, d = x.shape
    return pl.pallas_call(
        _rmsnorm_body,
        grid=(m // tm,),
        in_specs=[pl.BlockSpec((tm, d), lambda i: (i, 0)), pl.BlockSpec((1, d), lambda i: (0, 0))],
        out_specs=pl.BlockSpec((tm, d), lambda i: (i, 0)),
        out_shape=jax.ShapeDtypeStruct((m, d), out_dtype),
        compiler_params=_params(("arbitrary",), 56),
        name="rmsnorm",
    )(x, g.reshape(1, d))


def _xw(x, wbf, w_is_t):
    if w_is_t:
        return lax.dot_general(x, wbf, _NT, preferred_element_type=F32)
    return jnp.dot(x, wbf, preferred_element_type=F32)


def _mm_body(x_ref, w_ref, *rest, w_is_t):
    r_ref = rest[0] if len(rest) == 3 else None
    o_ref, wbf_ref = rest[-2:]

    @pl.when(pl.program_id(1) == 0)
    def _():
        wbf_ref[...] = w_ref[...].astype(BF16)

    res = _xw(x_ref[...], wbf_ref[...], w_is_t)
    if r_ref is not None:
        res = r_ref[...] + res
    o_ref[...] = res.astype(o_ref.dtype)


def matmul_cols(x, w, col0, ncols, tm, tn, out_dtype, residual=None, w_is_t=False, name="matmul"):
    m, k = x.shape
    cb0 = col0 // tn
    assert col0 % tn == 0 and ncols % tn == 0 and m % tm == 0
    if w_is_t:
        w_spec = pl.BlockSpec((tn, k), lambda j, i: (cb0 + j, 0))
        w_tile = (tn, k)
    else:
        w_spec = pl.BlockSpec((k, tn), lambda j, i: (0, cb0 + j))
        w_tile = (k, tn)
    in_specs = [pl.BlockSpec((tm, k), lambda j, i: (i, 0)), w_spec]
    args = [x, w]
    if residual is not None:
        in_specs.append(pl.BlockSpec((tm, tn), lambda j, i: (i, j)))
        args.append(residual)
    return pl.pallas_call(
        functools.partial(_mm_body, w_is_t=w_is_t),
        grid=(ncols // tn, m // tm),
        in_specs=in_specs,
        out_specs=pl.BlockSpec((tm, tn), lambda j, i: (i, j)),
        out_shape=jax.ShapeDtypeStruct((m, ncols), out_dtype),
        scratch_shapes=[pltpu.VMEM(w_tile, BF16)],
        compiler_params=_params(("arbitrary", "arbitrary"), 56),
        name=name,
    )(*args)


def _mm_kv_body(x_ref, w_ref, oi_ref, *rest):
    wbf_ref = rest[-1]

    @pl.when(pl.program_id(0) == 0)
    def _():
        wbf_ref[...] = w_ref[...].astype(BF16)

    res = _xw(x_ref[...], wbf_ref[...], True)
    tm = res.shape[0]
    for hk in range(HK):
        blk = res[:, hk * HEAD_DIM:(hk + 1) * HEAD_DIM]
        oi_ref[pl.ds(hk, tm, stride=HK), :] = blk
        if len(rest) == 2:
            rest[0][hk] = blk.astype(BF16)


def matmul_kv(x, w_t, col0, tm, planes, name):
    m, k = x.shape
    ncols = HK * HEAD_DIM
    assert col0 % ncols == 0 and m % tm == 0
    out_specs = [pl.BlockSpec((tm * HK, HEAD_DIM), lambda i: (i, 0))]
    out_shape = [jax.ShapeDtypeStruct((m * HK, HEAD_DIM), F32)]
    if planes:
        out_specs.append(pl.BlockSpec((HK, tm, HEAD_DIM), lambda i: (0, i, 0)))
        out_shape.append(jax.ShapeDtypeStruct((HK, m, HEAD_DIM), BF16))
    return pl.pallas_call(
        _mm_kv_body,
        grid=(m // tm,),
        in_specs=[pl.BlockSpec((tm, k), lambda i: (i, 0)), pl.BlockSpec((ncols, k), lambda i: (col0 // ncols, 0))],
        out_specs=out_specs,
        out_shape=out_shape,
        scratch_shapes=[pltpu.VMEM((ncols, k), BF16)],
        compiler_params=_params(("arbitrary",), 56),
        name=name,
    )(x, w_t)


def _merge_body(mg_ref, rnn_ref, nsa_ref, o_ref):
    d = rnn_ref.shape[-1]
    ga = jax.nn.sigmoid(mg_ref[:, :d])
    gb = jax.nn.sigmoid(mg_ref[:, d:])
    o_ref[...] = (ga * rnn_ref[...] + gb * nsa_ref[...]).astype(o_ref.dtype)


def merge(u_mg, o_rnn, o_nsa, tm):
    m, d = o_rnn.shape
    return pl.pallas_call(
        _merge_body,
        grid=(m // tm,),
        in_specs=[pl.BlockSpec((tm, 2 * d), lambda i: (i, 0)), pl.BlockSpec((tm, d), lambda i: (i, 0)),
                  pl.BlockSpec((tm, d), lambda i: (i, 0))],
        out_specs=pl.BlockSpec((tm, d), lambda i: (i, 0)),
        out_shape=jax.ShapeDtypeStruct((m, d), BF16),
        compiler_params=_params(("arbitrary",), 48),
        name="merge",
    )(u_mg, o_rnn, o_nsa)


def _lru_gates(uc, wa_ref, wx_ref, ba_ref, bx_ref, lam_ref):
    ucb = uc.astype(BF16)
    ra, ri = [], []
    for n in range(RNN_BLOCKS):
        blk = ucb[:, n * RNN_BLOCK_DIM:(n + 1) * RNN_BLOCK_DIM]
        ra.append(jnp.dot(blk, wa_ref[n], preferred_element_type=F32))
        ri.append(jnp.dot(blk, wx_ref[n], preferred_element_type=F32))
    r = jax.nn.sigmoid(jnp.concatenate(ra, axis=-1) + ba_ref[...])
    i = jax.nn.sigmoid(jnp.concatenate(ri, axis=-1) + bx_ref[...])
    z = -lam_ref[...]
    softplus = jnp.maximum(z, 0.0) + jnp.log1p(jnp.exp(-jnp.abs(z)))
    log_a = (-LRU_C * r) * softplus
    a = jnp.exp(log_a)
    mult = jnp.sqrt(-jnp.tanh(log_a) * (a * a + 1.0))
    return a, mult * (i * uc)


def _lru_prompt_body(u_ref, gate_ref, cw_ref, cb_ref, wa_ref, wx_ref, ba_ref, bx_ref, lam_ref,
                     o_ref, conv_ref, hlast_ref, xbuf, hcar, acum, bcum):
    t = pl.program_id(1)
    nt = pl.num_programs(1)
    tt, d = u_ref.shape[1], u_ref.shape[2]

    @pl.when(t == 0)
    def _():
        xbuf[0:8, :] = jnp.zeros((8, d), F32)
        hcar[...] = jnp.zeros((8, d), F32)

    x = u_ref[0]
    xbuf[8:8 + tt, :] = x
    uc = (cw_ref[0:1, :] * xbuf[5:5 + tt, :] + cw_ref[1:2, :] * xbuf[6:6 + tt, :]
          + cw_ref[2:3, :] * xbuf[7:7 + tt, :] + cw_ref[3:4, :] * x) + cb_ref[...]
    xbuf[0:8, :] = xbuf[tt:tt + 8, :]

    a, b = _lru_gates(uc, wa_ref, wx_ref, ba_ref, bx_ref, lam_ref)
    g = tt // 8
    a3 = a.reshape(g, 8, d)
    b3 = b.reshape(g, 8, d)
    row = lax.broadcasted_iota(I32, (g, 8, d), 1)
    for s in (1, 2, 4):
        a_sh = pltpu.roll(a3, s, axis=1)
        b_sh = pltpu.roll(b3, s, axis=1)
        keep = row >= s
        b3 = jnp.where(keep, a3 * b_sh + b3, b3)
        a3 = jnp.where(keep, a3 * a_sh, a3)
    acum[...] = a3
    bcum[...] = b3

    def body(gi, h):
        hg = bcum[gi] + acum[gi] * h
        bcum[gi] = hg
        return jnp.broadcast_to(hg[7:8, :], (8, d))

    hfin = lax.fori_loop(0, g, body, hcar[...])
    hcar[...] = hfin
    o_ref[0] = bcum[...].reshape(tt, d) * _gelu_tanh(gate_ref[0])

    @pl.when(t == nt - 1)
    def _():
        conv_ref[0] = x[tt - (CONV_W - 1):tt, :]
        hlast_ref[0] = hfin[0:1, :]


def lru_prompt(u_rnn, u_gate, conv_w, conv_b, wa, wx, ba, bx, lam, tt):
    b, t, d = u_rnn.shape
    vec = lambda: pl.BlockSpec((1, d), lambda i, j: (0, 0))
    wspec = lambda: pl.BlockSpec((RNN_BLOCKS, RNN_BLOCK_DIM, RNN_BLOCK_DIM), lambda i, j: (0, 0, 0))
    return pl.pallas_call(
        _lru_prompt_body,
        grid=(b, t // tt),
        in_specs=[pl.BlockSpec((1, tt, d), lambda i, j: (i, j, 0)), pl.BlockSpec((1, tt, d), lambda i, j: (i, j, 0)),
                  pl.BlockSpec((CONV_W, d), lambda i, j: (0, 0)), vec(), wspec(), wspec(), vec(), vec(), vec()],
        out_specs=[pl.BlockSpec((1, tt, d), lambda i, j: (i, j, 0)),
                   pl.BlockSpec((1, CONV_W - 1, d), lambda i, j: (i, 0, 0)),
                   pl.BlockSpec((1, 1, d), lambda i, j: (i, 0, 0))],
        out_shape=[jax.ShapeDtypeStruct((b, t, d), F32), jax.ShapeDtypeStruct((b, CONV_W - 1, d), F32),
                   jax.ShapeDtypeStruct((b, 1, d), F32)],
        scratch_shapes=[pltpu.VMEM((tt + 8, d), F32), pltpu.VMEM((8, d), F32),
                        pltpu.VMEM((tt // 8, 8, d), F32), pltpu.VMEM((tt // 8, 8, d), F32)],
        compiler_params=_params(("arbitrary", "arbitrary"), 56),
        name="lru_prompt",
    )(u_rnn, u_gate, conv_w, conv_b.reshape(1, d), wa, wx, ba.reshape(1, d), bx.reshape(1, d), lam.reshape(1, d))


def _lru_sample_body(u_ref, gate_ref, cp_ref, h0_ref, cw_ref, cb_ref, wa_ref, wx_ref, ba_ref, bx_ref, lam_ref,
                     o_ref, cn_ref, h_ref):
    x = u_ref[...]
    uc = (cw_ref[0:1, :] * cp_ref[0] + cw_ref[1:2, :] * cp_ref[1] + cw_ref[2:3, :] * cp_ref[2]
          + cw_ref[3:4, :] * x) + cb_ref[...]
    a, b = _lru_gates(uc, wa_ref, wx_ref, ba_ref, bx_ref, lam_ref)
    h = a * h0_ref[...] + b
    o_ref[...] = h * _gelu_tanh(gate_ref[...])
    h_ref[...] = h
    cn_ref[0] = cp_ref[1]
    cn_ref[1] = cp_ref[2]
    cn_ref[2] = x


def lru_sample(u_rnn, u_gate, conv_prev_t, h0, conv_w, conv_b, wa, wx, ba, bx, lam):
    n, d = u_rnn.shape
    return pl.pallas_call(
        _lru_sample_body,
        out_shape=[jax.ShapeDtypeStruct((n, d), F32), jax.ShapeDtypeStruct((CONV_W - 1, n, d), F32),
                   jax.ShapeDtypeStruct((n, d), F32)],
        compiler_params=pltpu.CompilerParams(vmem_limit_bytes=40 * MIB),
        name="lru_sample",
    )(u_rnn, u_gate, conv_prev_t, h0, conv_w, conv_b.reshape(1, d), wa, wx, ba.reshape(1, d), bx.reshape(1, d),
      lam.reshape(1, d))


COMPRESS_PAGES = 32
CHUNKS_PER_PAGE = PAGE_SIZE // CMP_STRIDE


def _compress_body(pt_ref, *refs, pp, n_steps):
    page_refs = refs[:pp]
    nxt_ref, extra_ref, wc_ref, pe_ref, b1_ref, w2_ref, b2_ref, o_ref, res_scr, pet_scr = refs[pp:]
    del pt_ref
    is_last = pl.program_id(1) == n_steps - 1
    m_rows = pp * CHUNKS_PER_PAGE * N_KV_HEADS
    lo4 = lax.broadcasted_iota(I32, (4, 8, 8, LANES), 2) < 4
    lo3 = lax.broadcasted_iota(I32, (8, 8, LANES), 1) < 4

    for k in range(2):
        lhs_sp = [[] for _ in range(8)]
        for pr in page_refs:
            xk = pr[pl.ds(k, PAGE_SIZE * N_KV_HEADS, stride=2), :]
            x5 = xk.reshape(4, 2, 8, 8, LANES)
            a0 = x5[:, 0]
            a1 = x5[:, 1]
            be = jnp.where(lo4, a0, pltpu.roll(a1, 4, axis=2))
            bo = jnp.where(lo4, pltpu.roll(a0, 4, axis=2), a1)
            for sp in range(8):
                lhs_sp[sp].append(jnp.concatenate([be[:, sp], bo[:, sp]], axis=-1).reshape(32, 2 * LANES))
        acc = jnp.zeros((m_rows, 2 * CMP_HIDDEN), F32)
        for sp in range(8):
            lhs = jnp.concatenate(lhs_sp[sp], axis=0).astype(BF16)
            acc = acc + jnp.dot(lhs, wc_ref[k, sp * 256:(sp + 1) * 256, :], preferred_element_type=F32)

        @pl.when((pl.program_id(0) == 0) & (pl.program_id(1) == 0))
        def _():
            pet_scr[k] = jnp.dot(pe_ref[k].astype(BF16), wc_ref[k], preferred_element_type=F32)

        peb0 = pet_scr[k, 0:1, :CMP_HIDDEN]
        peb1 = pet_scr[k, 1:2, CMP_HIDDEN:]
        h0 = acc[:, :CMP_HIDDEN] + peb0
        h1 = acc[:, CMP_HIDDEN:] + peb1

        xn = jnp.where(is_last, extra_ref[pl.ds(k, CMP_STRIDE * N_KV_HEADS, stride=2), :],
                       nxt_ref[pl.ds(k, CMP_STRIDE * N_KV_HEADS, stride=2), :])
        xn3 = xn.reshape(8, 8, LANES)
        ln = jnp.concatenate([jnp.where(lo3, xn3, 0.0), jnp.where(lo3, 0.0, xn3)], axis=-1).astype(BF16)
        nacc = jnp.zeros((8, CMP_HIDDEN), F32)
        for sp in range(8):
            nacc = nacc + jnp.dot(ln[sp], wc_ref[k, sp * 256:(sp + 1) * 256, CMP_HIDDEN:],
                                  preferred_element_type=F32)
        n8 = nacc + pltpu.roll(nacc, 4, axis=0) + peb1
        h1e = jnp.concatenate([h1, n8], axis=0)
        h1s = pltpu.roll(h1e, m_rows + 8 - N_KV_HEADS, axis=0)[:m_rows]
        pre = h0 + h1s + b1_ref[k]
        out_k = jnp.dot(_gelu_tanh(pre).astype(BF16), w2_ref[k], preferred_element_type=F32) + b2_ref[k]
        res_scr[...] = out_k
        for h in range(N_KV_HEADS):
            o_ref[0, 2 * h + k] = res_scr[pl.ds(h, m_rows // N_KV_HEADS, stride=N_KV_HEADS), :]


def compress(pool2d, page_table, extra2d, wc, pe8, b1, w2, b2):
    nseq, npages = page_table.shape
    pp = min(COMPRESS_PAGES, npages)
    n_steps = npages // pp
    assert npages % pp == 0
    rows_page = PAGE_SIZE * HK
    rows_chunk = CMP_STRIDE * HK
    blocks_step = pp * CHUNKS_PER_PAGE

    def page_spec(i):
        return pl.BlockSpec((rows_page, LANES), lambda b, s, pt: (pt[b, s * pp + i], 0))

    def nxt_map(b, s, pt):
        return (pt[b, jnp.minimum((s + 1) * pp, npages - 1)] * CHUNKS_PER_PAGE, 0)

    const3 = lambda shape: pl.BlockSpec(shape, lambda b, s, pt: (0, 0, 0))
    in_specs = [page_spec(i) for i in range(pp)] + [
        pl.BlockSpec((rows_chunk, LANES), nxt_map),
        pl.BlockSpec((rows_chunk, LANES), lambda b, s, pt: (b, 0)),
        const3((2, CMP_STRIDE * HEAD_DIM, 2 * CMP_HIDDEN)),
        const3((2, 8, CMP_STRIDE * HEAD_DIM)),
        const3((2, 1, CMP_HIDDEN)),
        const3((2, CMP_HIDDEN, HEAD_DIM)),
        const3((2, 1, HEAD_DIM)),
    ]
    grid_spec = pltpu.PrefetchScalarGridSpec(
        num_scalar_prefetch=1,
        grid=(nseq, n_steps),
        in_specs=in_specs,
        out_specs=pl.BlockSpec((1, HK, blocks_step, HEAD_DIM), lambda b, s, pt: (b, 0, s, 0)),
        scratch_shapes=[pltpu.VMEM((blocks_step * N_KV_HEADS, HEAD_DIM), F32),
                        pltpu.VMEM((2, 8, 2 * CMP_HIDDEN), F32)],
    )
    return pl.pallas_call(
        functools.partial(_compress_body, pp=pp, n_steps=n_steps),
        grid_spec=grid_spec,
        out_shape=jax.ShapeDtypeStruct((nseq, HK, npages * CHUNKS_PER_PAGE, HEAD_DIM), F32),
        compiler_params=_params(("arbitrary", "arbitrary"), 56),
        name="compress",
    )(page_table, *([pool2d] * pp), pool2d, extra2d, wc, pe8, b1, w2, b2)


def _overlap_matrix(n_cmp, n_cmp_pad, n_sel, n_sel_pad):
    cs = np.arange(n_cmp_pad)[:, None] * CMP_STRIDE
    ss = np.arange(n_sel_pad)[None, :] * SEL_BLK
    ov = np.minimum(cs + CMP_BLK, ss + SEL_BLK) - np.maximum(cs, ss)
    m = np.clip(ov, 0, CMP_BLK).astype(np.float32) / CMP_BLK
    m[n_cmp:, :] = 0.0
    m[:, n_sel:] = 0.0
    return m


QBLK = 128


KCHUNK = 64
KPIECE = 512
LOG2E = 1.4426950408889634


def _softmax_passes(s_scr, p_scr, n_plain, n_masked, n_total, plain_mask, full_mask):
    cols = s_scr.shape[1]
    grp = KCHUNK // 8
    c2 = ATTN_SCALE * LOG2E

    def rows(c):
        return pl.ds(pl.multiple_of(c * KCHUNK, KCHUNK), KCHUNK)

    def masked_max(mask_fn):
        def body(c, m8):
            sm = jnp.where(mask_fn(c), s_scr[rows(c), :], MASK_VALUE)
            s_scr[rows(c), :] = sm
            return jnp.maximum(m8, jnp.max(sm.reshape(grp, 8, cols), axis=0))
        return body

    m8 = lax.fori_loop(0, n_plain, masked_max(plain_mask), jnp.full((8, cols), MASK_VALUE, F32))
    m8 = lax.fori_loop(n_plain, n_plain + n_masked, masked_max(full_mask), m8)
    mc = jnp.max(m8, axis=0, keepdims=True) * c2

    def pass_exp(c, l8):
        p = jnp.exp2(s_scr[rows(c), :] * c2 - mc)
        p_scr[rows(c), :] = p.astype(BF16)
        return l8 + jnp.sum(p.reshape(grp, 8, cols), axis=0)

    l8 = lax.fori_loop(0, n_plain + n_masked, pass_exp, jnp.zeros((8, cols), F32))

    def zero_fill(c, carry):
        p_scr[rows(c), :] = jnp.zeros((KCHUNK, cols), BF16)
        return carry

    lax.fori_loop(n_plain + n_masked, n_total, zero_fill, 0)
    return 1.0 / jnp.sum(l8, axis=0, keepdims=True)


def _attn_prompt_s_body(q_ref, ksel_ref, vsel_ref, kwin_ref, vwin_ref, kvc_ref, bg_ref, mga_ref, mgb_ref, rnn_ref,
                        mt_ref, o_ref, vsel_t, vwin_t, kc_bf, vc_t_bf, s_scr, p_scr, sel_scr, *, n_cmp, n_sel):
    qi = pl.program_id(2)
    q0 = qi * QBLK
    cols = GROUP * QBLK

    @pl.when(qi == 0)
    def _():
        vsel_t[...] = vsel_ref[0].astype(F32).T.astype(BF16)
        vwin_t[...] = vwin_ref[0].astype(F32).T.astype(BF16)
        kc_bf[...] = kvc_ref[0, 0].astype(BF16)
        vc_t_bf[...] = kvc_ref[0, 1].T.astype(BF16)

    q = q_ref[0]
    q4 = jnp.concatenate([q[:, g * HEAD_DIM:(g + 1) * HEAD_DIM] for g in range(GROUP)], axis=0)
    qpos_row = q0 + (lax.broadcasted_iota(I32, (1, cols), 1) & (QBLK - 1))
    row_in_chunk = lax.broadcasted_iota(I32, (KCHUNK, cols), 0)

    nb = kvc_ref.shape[2]
    kc = kc_bf[...]
    vc_t = vc_t_bf[...]
    n_io = lax.broadcasted_iota(I32, (nb, cols), 0)
    mask_c = (n_io * CMP_STRIDE + (CMP_BLK - 1) <= qpos_row) & (n_io < n_cmp)
    sc = jnp.where(mask_c, lax.dot_general(kc, q4, _NT, preferred_element_type=F32) * ATTN_SCALE, MASK_VALUE)
    ec = jnp.where(mask_c, jnp.exp(sc - jnp.max(sc, axis=0, keepdims=True)), 0.0)
    den_c = jnp.sum(ec, axis=0, keepdims=True)
    pc = ec * jnp.where(den_c > 0.0, 1.0 / den_c, 0.0)
    o_c = jnp.dot(vc_t, pc.astype(BF16), preferred_element_type=F32)
    psum = pc[:, 0:QBLK] + pc[:, QBLK:2 * QBLK] + pc[:, 2 * QBLK:3 * QBLK] + pc[:, 3 * QBLK:4 * QBLK]

    nsp = mt_ref.shape[0]
    imp_t = jnp.dot(mt_ref[...], psum.astype(BF16), preferred_element_type=F32)
    jt = lax.broadcasted_iota(I32, (nsp, QBLK), 0)
    cur = lax.shift_right_logical(q0 + lax.broadcasted_iota(I32, (nsp, QBLK), 1), SEL_SHIFT)
    valid = (jt <= cur) & (jt < n_sel)
    forced = (jt == 0) | (jt >= cur - 1)
    score = jnp.where(forced, FORCED_SCORE, imp_t)
    score = jnp.where(valid, score, -1.0)
    rank = jnp.zeros((nsp, QBLK), I32)
    for k in range(n_sel):
        sk = score[k:k + 1, :]
        beats = (sk > score) | ((sk == score) & (k < jt))
        rank = rank + beats.astype(I32)
    sel_t = jnp.where((rank < N_SEL) & valid, 1.0, 0.0)
    sel_scr[...] = jnp.concatenate([sel_t] * GROUP, axis=1)

    wkeys = WINDOW + QBLK
    ks = pl.multiple_of(jnp.maximum(q0 - WINDOW, 0), QBLK)
    s_scr[0:wkeys, :] = lax.dot_general(kwin_ref[0, pl.ds(ks, wkeys), :], q4, _NT, preferred_element_type=F32)

    def win_mask(c):
        dist = qpos_row - (ks + c * KCHUNK + row_in_chunk)
        return (dist >= 0) & (dist < WINDOW)

    n_wchunks = wkeys // KCHUNK
    inv_w = _softmax_passes(s_scr, p_scr, 0, n_wchunks, n_wchunks, win_mask, win_mask)
    o_w = jnp.dot(vwin_t[:, pl.ds(ks, wkeys)], p_scr[0:wkeys, :], preferred_element_type=F32) * inv_w

    n_pieces = qi // (KPIECE // QBLK) + 1

    def score_piece(c, carry):
        r0 = pl.multiple_of(c * KPIECE, KPIECE)
        s_scr[pl.ds(r0, KPIECE), :] = lax.dot_general(ksel_ref[0, pl.ds(r0, KPIECE), :], q4, _NT,
                                                      preferred_element_type=F32)
        return carry

    lax.fori_loop(0, n_pieces, score_piece, 0)

    def picked(c):
        return sel_scr[pl.ds(c, 1), :] > 0.5

    def picked_causal(c):
        return picked(c) & (c * KCHUNK + row_in_chunk <= qpos_row)

    chunks_q = QBLK // KCHUNK
    inv_s = _softmax_passes(s_scr, p_scr, qi * chunks_q, chunks_q, n_pieces * (KPIECE // KCHUNK),
                            picked, picked_causal)

    def pv_piece(c, acc):
        r0 = pl.multiple_of(c * KPIECE, KPIECE)
        return acc + jnp.dot(vsel_t[:, pl.ds(r0, KPIECE)], p_scr[pl.ds(r0, KPIECE), :], preferred_element_type=F32)

    o_s = lax.fori_loop(0, n_pieces, pv_piece, jnp.zeros((HEAD_DIM, cols), F32)) * inv_s

    gates_t = jax.nn.sigmoid(bg_ref[0]).T
    grow = lambda br: jnp.concatenate([gates_t[br * GROUP + g:br * GROUP + g + 1, :] for g in range(GROUP)], axis=1)
    o_t = grow(0) * o_c + grow(1) * o_s + grow(2) * o_w
    for g in range(GROUP):
        cs = slice(g * HEAD_DIM, (g + 1) * HEAD_DIM)
        o_nsa = o_t[:, g * QBLK:(g + 1) * QBLK].T
        merged = jax.nn.sigmoid(mga_ref[0, :, cs]) * rnn_ref[0, :, cs] + jax.nn.sigmoid(mgb_ref[0, :, cs]) * o_nsa
        o_ref[0, :, cs] = merged.astype(o_ref.dtype)


def attn_prompt_s(q3, sel_planes, win_planes, kvc, ubg3, umg3, o_rnn3, n_cmp):
    b, t, _ = q3.shape
    nb = kvc.shape[2]
    assert KCHUNK == SEL_BLK and t % KPIECE == 0 and t >= WINDOW + QBLK
    n_sel = -(-t // SEL_BLK)
    nsp = max(8, -(-n_sel // 8) * 8)
    m_t = _overlap_matrix(n_cmp, nb, n_sel, nsp).T
    kspec = lambda: pl.BlockSpec((1, t, HEAD_DIM), lambda i, h, j: (2 * h, i, 0))
    vspec = lambda: pl.BlockSpec((1, t, HEAD_DIM), lambda i, h, j: (2 * h + 1, i, 0))
    cols = GROUP * QBLK
    head_cols = GROUP * HEAD_DIM
    tile = lambda off: pl.BlockSpec((1, QBLK, head_cols), lambda i, h, j: (i, j, off + h))
    return pl.pallas_call(
        functools.partial(_attn_prompt_s_body, n_cmp=n_cmp, n_sel=n_sel),
        grid=(b, N_KV_HEADS, t // QBLK),
        in_specs=[
            tile(0),
            kspec(), vspec(), kspec(), vspec(),
            pl.BlockSpec((1, 2, nb, HEAD_DIM), lambda i, h, j: (i, h, 0, 0)),
            pl.BlockSpec((1, QBLK, LANES), lambda i, h, j: (i, j, h)),
            tile(0), tile(N_KV_HEADS), tile(0),
            pl.BlockSpec((nsp, nb), lambda i, h, j: (0, 0)),
        ],
        out_specs=tile(0),
        out_shape=jax.ShapeDtypeStruct((b, t, N_HEADS * HEAD_DIM), BF16),
        scratch_shapes=[pltpu.VMEM((HEAD_DIM, t), BF16), pltpu.VMEM((HEAD_DIM, t), BF16),
                        pltpu.VMEM((nb, HEAD_DIM), BF16), pltpu.VMEM((HEAD_DIM, nb), BF16),
                        pltpu.VMEM((t, cols), F32), pltpu.VMEM((t, cols), BF16), pltpu.VMEM((nsp, cols), F32)],
        compiler_params=_params(("arbitrary", "arbitrary", "arbitrary"), 48),
        name="attn_prompt",
    )(q3, sel_planes, sel_planes, win_planes, win_planes, kvc, ubg3, umg3, umg3, o_rnn3, jnp.asarray(m_t, BF16))


def _row_head(shape):
    return lax.shift_right_logical(lax.broadcasted_iota(I32, shape, 0), 2)


def _cmp_sample_body(q_ref, kvc_ref, m_ref, oc_ref, imp_ref, *, qpos, n_cmp):
    qb = q_ref[0].astype(BF16)
    nb = kvc_ref.shape[2]
    rg = _row_head((N_HEADS, nb))
    ncol = lax.broadcasted_iota(I32, (N_HEADS, nb), 1)
    mask = (ncol * CMP_STRIDE + (CMP_BLK - 1) <= qpos) & (ncol < n_cmp)
    row8 = lax.broadcasted_iota(I32, (8, nb), 0)
    o = jnp.zeros((N_HEADS, HEAD_DIM), F32)
    ps8 = jnp.zeros((8, nb), F32)
    for h in range(N_KV_HEADS):
        kc = kvc_ref[0, 2 * h].astype(BF16)
        vc = kvc_ref[0, 2 * h + 1].astype(BF16)
        s = lax.dot_general(qb, kc, _NT, preferred_element_type=F32) * ATTN_SCALE
        ph = jnp.where(rg == h, _msoftmax(s, mask), 0.0)
        o = o + jnp.dot(ph.astype(BF16), vc, preferred_element_type=F32)
        ps8 = jnp.where(row8 == h, jnp.sum(ph, axis=0, keepdims=True), ps8)
    oc_ref[0] = o
    imp_ref[0] = jnp.dot(ps8.astype(BF16), m_ref[...], preferred_element_type=F32)[0:N_KV_HEADS]


def cmp_sample(q16, kvc, m_mat, qpos, n_cmp):
    ns = q16.shape[0]
    nb = kvc.shape[2]
    nsp = m_mat.shape[1]
    return pl.pallas_call(
        functools.partial(_cmp_sample_body, qpos=qpos, n_cmp=n_cmp),
        grid=(ns,),
        in_specs=[pl.BlockSpec((1, N_HEADS, HEAD_DIM), lambda b: (b, 0, 0)),
                  pl.BlockSpec((1, HK, nb, HEAD_DIM), lambda b: (b, 0, 0, 0)),
                  pl.BlockSpec((nb, nsp), lambda b: (0, 0))],
        out_specs=[pl.BlockSpec((1, N_HEADS, HEAD_DIM), lambda b: (b, 0, 0)),
                   pl.BlockSpec((1, N_KV_HEADS, nsp), lambda b: (b, 0, 0))],
        out_shape=[jax.ShapeDtypeStruct((ns, N_HEADS, HEAD_DIM), F32),
                   jax.ShapeDtypeStruct((ns, N_KV_HEADS, nsp), F32)],
        compiler_params=_params(("arbitrary",), 40),
        name="cmp_sample",
    )(q16, kvc, m_mat)


def _rank_sample_body(imp_ref, idx_ref, sc_scr, *, cur, n_sel):
    npad, nr = sc_scr.shape
    st = imp_ref[...].T
    j = lax.broadcasted_iota(I32, (npad, nr), 0)
    real = j < n_sel
    valid = (j <= cur) & real
    forced = (j == 0) | (j >= cur - 1)
    score = jnp.where(forced, FORCED_SCORE, st)
    score = jnp.where(valid, score, -1.0)
    score = jnp.where(real, score, -2.0)
    sc_scr[...] = score

    def body(k, rank):
        sk = sc_scr[pl.ds(k, 1), :]
        beats = (sk > score) | ((sk == score) & (k < j))
        return rank + beats.astype(I32)

    rank = lax.fori_loop(0, n_sel, body, jnp.zeros((npad, nr), I32))
    sel = (rank < N_SEL) & valid
    for slot in range(N_SEL):
        hit = sel & (rank == slot)
        found = jnp.max(hit.astype(I32), axis=0, keepdims=True)
        val = jnp.sum(jnp.where(hit, j, 0), axis=0, keepdims=True)
        idx_ref[slot:slot + 1, :] = jnp.where(found > 0, val, -1)


def rank_sample(imp2, cur, n_sel):
    nr, npad = imp2.shape
    return pl.pallas_call(
        functools.partial(_rank_sample_body, cur=cur, n_sel=n_sel),
        out_shape=jax.ShapeDtypeStruct((N_SEL, nr), I32),
        scratch_shapes=[pltpu.VMEM((npad, nr), F32)],
        name="rank_sample",
    )(imp2)


def _sel_sample_body(idx_ref, pt_ref, q_ref, kn_ref, vn_ref, pool_hbm, o_ref, kbuf, vbuf, sem, *, n_past_blocks):
    b = pl.program_id(0)
    slot_buf = lax.rem(b, 2)
    nk = N_SEL * SEL_BLK

    def issue(seq, sl):
        for h in range(N_KV_HEADS):
            for s in range(N_SEL):
                blk = jnp.clip(idx_ref[s, seq * N_KV_HEADS + h], 0, n_past_blocks - 1)
                pos0 = (pt_ref[seq, lax.shift_right_logical(blk, 1)] * PAGE_SIZE
                        + jnp.bitwise_and(blk, 1) * SEL_BLK)
                rows = pl.ds(h * nk + s * SEL_BLK, SEL_BLK)
                pltpu.make_async_copy(pool_hbm.at[pl.ds(pos0, SEL_BLK), 2 * h, :], kbuf.at[sl, rows, :],
                                      sem.at[sl]).start()
                pltpu.make_async_copy(pool_hbm.at[pl.ds(pos0, SEL_BLK), 2 * h + 1, :], vbuf.at[sl, rows, :],
                                      sem.at[sl]).start()

    @pl.when(b == 0)
    def _():
        issue(b, 0)

    @pl.when(b + 1 < pl.num_programs(0))
    def _():
        issue(b + 1, 1 - slot_buf)

    pltpu.make_async_copy(pool_hbm.at[pl.ds(0, N_KV_HEADS * nk), 0, :], kbuf.at[slot_buf], sem.at[slot_buf]).wait()
    pltpu.make_async_copy(pool_hbm.at[pl.ds(0, N_KV_HEADS * nk), 0, :], vbuf.at[slot_buf], sem.at[slot_buf]).wait()

    q = q_ref[0]
    qb = q.astype(BF16)
    rg = _row_head((N_HEADS, nk))
    col_slot = lax.shift_right_logical(lax.broadcasted_iota(I32, (N_HEADS, nk), 1), SEL_SHIFT)

    sc = jnp.zeros((N_HEADS, nk), F32)
    okv = jnp.zeros((N_HEADS, nk), I32)
    for h in range(N_KV_HEADS):
        kcat = kbuf[slot_buf, pl.ds(h * nk, nk), :].astype(BF16)
        sh = lax.dot_general(qb, kcat, _NT, preferred_element_type=F32)
        ok = jnp.zeros((N_HEADS, nk), I32)
        for s in range(N_SEL):
            blk = idx_ref[s, b * N_KV_HEADS + h]
            ok = jnp.where(col_slot == s, jnp.where((blk >= 0) & (blk < n_past_blocks), 1, 0), ok)
        sc = jnp.where(rg == h, sh, sc)
        okv = jnp.where(rg == h, ok, okv)
    sm = jnp.where(okv > 0, sc * ATTN_SCALE, MASK_VALUE)
    s_new = jnp.sum(q * kn_ref[0], axis=-1, keepdims=True) * ATTN_SCALE
    m = jnp.maximum(jnp.max(sm, axis=-1, keepdims=True), s_new)
    p = jnp.where(okv > 0, jnp.exp(sm - m), 0.0)
    p_new = jnp.exp(s_new - m)
    den = jnp.sum(p, axis=-1, keepdims=True) + p_new
    pv = jnp.zeros((N_HEADS, HEAD_DIM), F32)
    for h in range(N_KV_HEADS):
        vcat = vbuf[slot_buf, pl.ds(h * nk, nk), :].astype(BF16)
        pv = pv + jnp.dot(jnp.where(rg == h, p, 0.0).astype(BF16), vcat, preferred_element_type=F32)
    o_ref[0] = (pv + p_new * vn_ref[0]) / den


def sel_sample(idx, page_table, q16, kn16, vn16, pool3, n_past_blocks):
    ns = q16.shape[0]
    assert PAGE_SIZE == 2 * SEL_BLK
    head3 = lambda: pl.BlockSpec((1, N_HEADS, HEAD_DIM), lambda b, i, p: (b, 0, 0))
    buf_rows = N_KV_HEADS * N_SEL * SEL_BLK
    grid_spec = pltpu.PrefetchScalarGridSpec(
        num_scalar_prefetch=2,
        grid=(ns,),
        in_specs=[head3(), head3(), head3(), pl.BlockSpec(memory_space=pl.ANY)],
        out_specs=head3(),
        scratch_shapes=[pltpu.VMEM((2, buf_rows, HEAD_DIM), F32), pltpu.VMEM((2, buf_rows, HEAD_DIM), F32),
                        pltpu.SemaphoreType.DMA((2,))],
    )
    return pl.pallas_call(
        functools.partial(_sel_sample_body, n_past_blocks=n_past_blocks),
        grid_spec=grid_spec,
        out_shape=jax.ShapeDtypeStruct((ns, N_HEADS, HEAD_DIM), F32),
        compiler_params=_params(("arbitrary",), 40),
        name="sel_sample",
    )(idx, page_table, q16, kn16, vn16, pool3)


def _win_sample_body(q_ref, win_ref, new_ref, kn_ref, vn_ref, oc_ref, os_ref, bg_ref, o_ref, wout_ref, *, wbuf):
    wout_ref[0:(wbuf - 1) * HK, :] = win_ref[HK:wbuf * HK, :]
    wout_ref[(wbuf - 1) * HK:wbuf * HK, :] = new_ref[...]
    q = q_ref[0]
    qb = q.astype(BF16)
    rg = _row_head((N_HEADS, wbuf))
    sc = jnp.zeros((N_HEADS, wbuf), F32)
    for h in range(N_KV_HEADS):
        kh = win_ref[pl.ds(2 * h, wbuf, stride=HK), :].astype(BF16)
        sc = jnp.where(rg == h, lax.dot_general(qb, kh, _NT, preferred_element_type=F32), sc)
    mask = lax.broadcasted_iota(I32, (N_HEADS, wbuf), 1) > wbuf - WINDOW
    sm = jnp.where(mask, sc * ATTN_SCALE, MASK_VALUE)
    s_new = jnp.sum(q * kn_ref[0], axis=-1, keepdims=True) * ATTN_SCALE
    m = jnp.maximum(jnp.max(sm, axis=-1, keepdims=True), s_new)
    p = jnp.where(mask, jnp.exp(sm - m), 0.0)
    p_new = jnp.exp(s_new - m)
    den = jnp.sum(p, axis=-1, keepdims=True) + p_new
    pv = jnp.zeros((N_HEADS, HEAD_DIM), F32)
    for h in range(N_KV_HEADS):
        vh = win_ref[pl.ds(2 * h + 1, wbuf, stride=HK), :].astype(BF16)
        pv = pv + jnp.dot(jnp.where(rg == h, p, 0.0).astype(BF16), vh, preferred_element_type=F32)
    o_w = (pv + p_new * vn_ref[0]) / den
    gates = jax.nn.sigmoid(bg_ref[0])
    o_ref[0] = gates[:, 0:1] * oc_ref[0] + gates[:, 1:2] * os_ref[0] + gates[:, 2:3] * o_w


def win_sample(q16, win2d, new2d, kn16, vn16, o_cmp, o_sel, bgs, wbuf):
    ns = q16.shape[0]
    head3 = lambda: pl.BlockSpec((1, N_HEADS, HEAD_DIM), lambda b: (b, 0, 0))
    wspec = lambda: pl.BlockSpec((wbuf * HK, LANES), lambda b: (b, 0))
    return pl.pallas_call(
        functools.partial(_win_sample_body, wbuf=wbuf),
        grid=(ns,),
        in_specs=[head3(), wspec(), pl.BlockSpec((HK, LANES), lambda b: (b, 0)), head3(), head3(), head3(), head3(),
                  head3()],
        out_specs=[head3(), wspec()],
        out_shape=[jax.ShapeDtypeStruct((ns, N_HEADS, HEAD_DIM), F32),
                   jax.ShapeDtypeStruct((ns * wbuf * HK, LANES), F32)],
        compiler_params=_params(("arbitrary",), 40),
        name="win_sample",
    )(q16, win2d, new2d, kn16, vn16, o_cmp, o_sel, bgs)


ROUTE_LANE0 = N_GROUPS


INFO_E1, INFO_E2, INFO_R1, INFO_R2, INFO_C1, INFO_C2 = range(6)


def _route_body(xp_ref, xs_ref, g_ref, wr_ref, br_ref, xn_ref, info_ref, cnt_ref, run_scr, *, n_real, n_ptiles):
    step = pl.program_id(0)

    @pl.when(step == 0)
    def _():
        run_scr[...] = jnp.zeros(run_scr.shape, F32)

    refs = (g_ref, wr_ref, br_ref, xn_ref, info_ref, cnt_ref, run_scr)

    @pl.when(step < n_ptiles)
    def _():
        _route_tile(xp_ref[...], step, n_real, *refs)

    @pl.when(step >= n_ptiles)
    def _():
        _route_tile(xs_ref[...], step, n_real, *refs)


def _route_tile(x, step, n_real, g_ref, wr_ref, br_ref, xn_ref, info_ref, cnt_ref, run_scr):
    tm = x.shape[0]
    var = jnp.mean(x * x, axis=-1, keepdims=True)
    xn = (x * lax.rsqrt(var + EPS)) * g_ref[...]
    xn_ref[...] = xn
    xnb = xn.astype(BF16)
    logits = jnp.dot(xnb, wr_ref[...], preferred_element_type=F32) + br_ref[...]
    lane = lax.broadcasted_iota(I32, logits.shape, 1)
    big = 4 * LANES

    isg = lane < N_GROUPS
    lg = jnp.where(isg, logits, -jnp.inf)
    mg = jnp.max(lg, axis=-1, keepdims=True)
    gi = jnp.min(jnp.where(lg == mg, lane, big), axis=-1, keepdims=True)
    pg = 1.0 / jnp.sum(jnp.where(isg, jnp.exp(lg - mg), 0.0), axis=-1, keepdims=True)

    lo = ROUTE_LANE0 + gi * EXP_PER_GROUP
    ing = (lane >= lo) & (lane < lo + EXP_PER_GROUP)
    le = jnp.where(ing, logits, -jnp.inf)
    me = jnp.max(le, axis=-1, keepdims=True)
    ee = jnp.where(ing, jnp.exp(le - me), 0.0)
    pe = jnp.where(ing, ee / jnp.sum(ee, axis=-1, keepdims=True), -1.0)
    p1 = jnp.max(pe, axis=-1, keepdims=True)
    i1 = jnp.min(jnp.where(pe == p1, lane, big), axis=-1, keepdims=True)
    pe2 = jnp.where(lane == i1, -1.0, pe)
    p2 = jnp.max(pe2, axis=-1, keepdims=True)
    i2 = jnp.min(jnp.where(pe2 == p2, lane, big), axis=-1, keepdims=True)
    tot = p1 + p2
    c1 = pg * (p1 / tot)
    c2 = pg * (p2 / tot)

    row = step * tm + lax.broadcasted_iota(I32, logits.shape, 0)
    hit = jnp.where(((lane == i1) | (lane == i2)) & (row < n_real), 1.0, 0.0)
    tri = jnp.where(lax.broadcasted_iota(I32, (tm, tm), 1) < lax.broadcasted_iota(I32, (tm, tm), 0), 1.0, 0.0)
    before = run_scr[...] + jnp.dot(tri.astype(BF16), hit.astype(BF16), preferred_element_type=F32)
    r1 = jnp.sum(jnp.where(lane == i1, before, 0.0), axis=-1, keepdims=True)
    r2 = jnp.sum(jnp.where(lane == i2, before, 0.0), axis=-1, keepdims=True)
    run_new = run_scr[...] + jnp.sum(hit, axis=0, keepdims=True)
    run_scr[...] = run_new
    cnt_ref[...] = run_new

    info = jnp.zeros(logits.shape, F32)
    for ln, val in ((INFO_E1, (i1 - ROUTE_LANE0).astype(F32)), (INFO_E2, (i2 - ROUTE_LANE0).astype(F32)),
                    (INFO_R1, r1), (INFO_R2, r2), (INFO_C1, c1), (INFO_C2, c2)):
        info = jnp.where(lane == ln, val, info)
    info_ref[...] = info


MOE_TILE = 256


def route(x_prompt, x_sample, g, wr, br, n_real):
    mp, d = x_prompt.shape
    n_ptiles = mp // MOE_TILE
    assert mp % MOE_TILE == 0 and x_sample.shape[0] == MOE_TILE
    m = mp + MOE_TILE
    return pl.pallas_call(
        functools.partial(_route_body, n_real=n_real, n_ptiles=n_ptiles),
        grid=(n_ptiles + 1,),
        in_specs=[pl.BlockSpec((MOE_TILE, d), lambda i: (jnp.minimum(i, n_ptiles - 1), 0)),
                  pl.BlockSpec((MOE_TILE, d), lambda i: (0, 0)), pl.BlockSpec((1, d), lambda i: (0, 0)),
                  pl.BlockSpec((d, LANES), lambda i: (0, 0)), pl.BlockSpec((1, LANES), lambda i: (0, 0))],
        out_specs=[pl.BlockSpec((MOE_TILE, d), lambda i: (i, 0)), pl.BlockSpec((MOE_TILE, LANES), lambda i: (i, 0)),
                   pl.BlockSpec((1, LANES), lambda i: (0, 0))],
        out_shape=[jax.ShapeDtypeStruct((m, d), F32), jax.ShapeDtypeStruct((m, LANES), F32),
                   jax.ShapeDtypeStruct((1, LANES), F32)],
        scratch_shapes=[pltpu.VMEM((1, LANES), F32)],
        compiler_params=_params(("arbitrary",), 40),
        name="route",
    )(x_prompt, x_sample, g.reshape(1, d), wr, br)


def _moe_plan(info, counts_row, n_real, n_tiles):
    n_pad = info.shape[0]
    e = jnp.clip(info[:, INFO_E1:INFO_E2 + 1].astype(I32), 0, N_EXPERTS - 1)
    r = info[:, INFO_R1:INFO_R2 + 1].astype(I32)
    counts = counts_row[0, ROUTE_LANE0:ROUTE_LANE0 + N_EXPERTS].astype(I32)
    padded = ((counts + MOE_TILE - 1) // MOE_TILE) * MOE_TILE
    ex = jnp.arange(N_EXPERTS, dtype=I32)
    ends = jnp.sum(jnp.where(ex[None, :] <= ex[:, None], padded[None, :], 0), axis=1)
    offs = ends - padded
    tok = jnp.arange(n_pad, dtype=I32)
    valid = (tok < n_real)[:, None]
    dest = jnp.where(valid, offs[e] + r, 0)
    n_slots = n_tiles * MOE_TILE
    src = jnp.zeros((n_slots,), I32).at[jnp.where(valid, dest, n_slots).reshape(-1)].set(
        jnp.repeat(tok, 2), mode="drop")
    n_used = ends[N_EXPERTS - 1] // MOE_TILE
    tiles = jnp.arange(n_tiles, dtype=I32)
    used = tiles < n_used
    te = jnp.minimum(jnp.sum((ends[None, :] <= (tiles * MOE_TILE)[:, None]).astype(I32), axis=1), N_EXPERTS - 1)
    te_prev = jnp.concatenate([te[:1] - 1, te[:-1]])
    first = (te != te_prev) & used
    later = tiles[None, :] > tiles[:, None]
    run = jnp.sum((first[None, :] & ~later).astype(I32), axis=1) - 1
    nxt_tile = jnp.min(jnp.where(first[None, :] & later, tiles[None, :], n_tiles), axis=1)
    nxt = jnp.where(nxt_tile < n_tiles, te[jnp.minimum(nxt_tile, n_tiles - 1)], -1)
    tile3 = lambda v: v.reshape(-1, 1, MOE_TILE)
    return (tile3(src), tile3(dest[:, 0]), tile3(dest[:, 1]),
            (te, first.astype(I32), nxt, jnp.bitwise_and(run, 1), n_used.reshape(1)))


def _issue_rows(idx_ref, src_hbm, dst_buf, slot, sem):
    n = idx_ref.shape[2]
    group = 8

    def body(g, carry):
        for j in range(group):
            r = g * group + j
            pltpu.make_async_copy(src_hbm.at[pl.ds(idx_ref[0, 0, r], 1), :], dst_buf.at[slot, pl.ds(r, 1), :],
                                  sem.at[slot]).start(priority=j % 2)
        return carry

    lax.fori_loop(0, n // group, body, 0)


def _wait_rows(src_hbm, dst_buf, slot, sem):
    n = dst_buf.shape[1]
    pltpu.make_async_copy(src_hbm.at[pl.ds(0, n), :], dst_buf.at[slot], sem.at[slot]).wait()


DISPATCH_TILES = 4


def _moe_dispatch_body(nu_ref, src_cur, src_nxt, xn_hbm, xs_ref, xbuf, sem):
    i = pl.program_id(0)
    n_used = nu_ref[0]
    slot = lax.rem(i, 2)

    @pl.when(i == 0)
    def _():
        _issue_rows(src_cur, xn_hbm, xbuf, 0, sem)

    @pl.when((i + 1) * DISPATCH_TILES < n_used)
    def _():
        _issue_rows(src_nxt, xn_hbm, xbuf, 1 - slot, sem)

    @pl.when(i * DISPATCH_TILES < n_used)
    def _():
        _wait_rows(xn_hbm, xbuf, slot, sem)
        xs_ref[...] = xbuf[slot].astype(xs_ref.dtype)

    @pl.when(i * DISPATCH_TILES >= n_used)
    def _():
        xs_ref[...] = jnp.zeros(xs_ref.shape, xs_ref.dtype)


def moe_dispatch(n_used, src3, xn):
    n_tiles = src3.shape[0]
    assert n_tiles % DISPATCH_TILES == 0
    n_steps = n_tiles // DISPATCH_TILES
    rows = DISPATCH_TILES * MOE_TILE
    d = xn.shape[1]
    src_steps = src3.reshape(n_steps, 1, rows)
    smem_tile = lambda imap: pl.BlockSpec((1, 1, rows), imap, memory_space=pltpu.SMEM)
    grid_spec = pltpu.PrefetchScalarGridSpec(
        num_scalar_prefetch=1,
        grid=(n_steps,),
        in_specs=[smem_tile(lambda i, nu: (i, 0, 0)),
                  smem_tile(lambda i, nu: (jnp.minimum(i + 1, n_steps - 1), 0, 0)),
                  pl.BlockSpec(memory_space=pl.ANY)],
        out_specs=pl.BlockSpec((rows, d), lambda i, nu: (i, 0)),
        scratch_shapes=[pltpu.VMEM((2, rows, d), F32), pltpu.SemaphoreType.DMA((2,))],
    )
    return pl.pallas_call(
        _moe_dispatch_body,
        grid_spec=grid_spec,
        out_shape=jax.ShapeDtypeStruct((n_tiles * MOE_TILE, d), BF16),
        compiler_params=_params(("arbitrary",), 40),
        name="moe_dispatch",
    )(n_used, src_steps, src_steps, xn)


def _moe_expert_body(te_ref, first_ref, nxt_ref, par_ref, nu_ref, x_ref, wg_hbm, wu_hbm, wd_hbm,
                     y_ref, wg_buf, wu_buf, wd_buf, wsem, wgb, wub, wdb):
    i = pl.program_id(0)
    n_used = nu_ref[0]

    def weight_copies(e, ws):
        return (pltpu.make_async_copy(wg_hbm.at[e], wg_buf.at[ws], wsem.at[0, ws]),
                pltpu.make_async_copy(wu_hbm.at[e], wu_buf.at[ws], wsem.at[1, ws]),
                pltpu.make_async_copy(wd_hbm.at[e], wd_buf.at[ws], wsem.at[2, ws]))

    @pl.when(i == 0)
    def _():
        for cp in weight_copies(te_ref[0], par_ref[0]):
            cp.start()

    @pl.when(i < n_used)
    def _():
        @pl.when(first_ref[i] == 1)
        def _():
            ws = par_ref[i]
            for cp in weight_copies(te_ref[i], ws):
                cp.wait()

            @pl.when(nxt_ref[i] >= 0)
            def _():
                for cp in weight_copies(nxt_ref[i], 1 - ws):
                    cp.start()

            wgb[...] = wg_buf[ws].astype(BF16)
            wub[...] = wu_buf[ws].astype(BF16)
            wdb[...] = wd_buf[ws].astype(BF16)

        x = x_ref[...]
        hg = jnp.dot(x, wgb[...], preferred_element_type=F32)
        hu = jnp.dot(x, wub[...], preferred_element_type=F32)
        hid = (hg * jax.nn.sigmoid(hg)) * hu
        y_ref[...] = jnp.dot(hid.astype(BF16), wdb[...], preferred_element_type=F32)

    @pl.when(i >= n_used)
    def _():
        y_ref[...] = jnp.zeros(y_ref.shape, F32)


def moe_experts(tile_plan, xs, wg, wu, wd):
    rows, d = xs.shape
    n_tiles = rows // MOE_TILE
    _, _, de = wg.shape
    hbm = lambda: pl.BlockSpec(memory_space=pl.ANY)
    grid_spec = pltpu.PrefetchScalarGridSpec(
        num_scalar_prefetch=5,
        grid=(n_tiles,),
        in_specs=[pl.BlockSpec((MOE_TILE, d), lambda i, te, fi, nx, pa, nu: (jnp.minimum(i, nu[0] - 1), 0)),
                  hbm(), hbm(), hbm()],
        out_specs=pl.BlockSpec((MOE_TILE, d), lambda i, *_: (i, 0)),
        scratch_shapes=[pltpu.VMEM((2, d, de), F32), pltpu.VMEM((2, d, de), F32), pltpu.VMEM((2, de, d), F32),
                        pltpu.SemaphoreType.DMA((3, 2)),
                        pltpu.VMEM((d, de), BF16), pltpu.VMEM((d, de), BF16), pltpu.VMEM((de, d), BF16)],
    )
    return pl.pallas_call(
        _moe_expert_body,
        grid_spec=grid_spec,
        out_shape=jax.ShapeDtypeStruct((rows, d), F32),
        compiler_params=_params(("arbitrary",), 56),
        name="moe_experts",
    )(*tile_plan, xs, wg, wu, wd)


def _moe_combine_body(d1_cur, d1_nxt, d2_cur, d2_nxt, ys_hbm, xp_ref, xs_ref, info_ref, gf_ref, yp_ref, ysm_ref,
                      buf1, buf2, sem1, sem2, *, n_ptiles):
    i = pl.program_id(0)
    slot = lax.rem(i, 2)

    @pl.when(i == 0)
    def _():
        _issue_rows(d1_cur, ys_hbm, buf1, 0, sem1)
        _issue_rows(d2_cur, ys_hbm, buf2, 0, sem2)

    @pl.when(i + 1 < pl.num_programs(0))
    def _():
        _issue_rows(d1_nxt, ys_hbm, buf1, 1 - slot, sem1)
        _issue_rows(d2_nxt, ys_hbm, buf2, 1 - slot, sem2)

    _wait_rows(ys_hbm, buf1, slot, sem1)
    _wait_rows(ys_hbm, buf2, slot, sem2)

    def finish(x_ref, y_ref):
        info = info_ref[...]
        xo = x_ref[...] + (info[:, INFO_C1:INFO_C1 + 1] * buf1[slot] + info[:, INFO_C2:INFO_C2 + 1] * buf2[slot])
        var = jnp.mean(xo * xo, axis=-1, keepdims=True)
        y_ref[...] = (xo * lax.rsqrt(var + EPS)) * gf_ref[...]

    @pl.when(i < n_ptiles)
    def _():
        finish(xp_ref, yp_ref)

    @pl.when(i >= n_ptiles)
    def _():
        finish(xs_ref, ysm_ref)


def moe_combine(dest1, dest2, ys, x_prompt, x_sample, info, g_final):
    mp, d = x_prompt.shape
    n_ptiles = mp // MOE_TILE
    n_tiles = n_ptiles + 1
    cur = lambda: pl.BlockSpec((1, 1, MOE_TILE), lambda i: (i, 0, 0), memory_space=pltpu.SMEM)
    nxt = lambda: pl.BlockSpec((1, 1, MOE_TILE), lambda i: (jnp.minimum(i + 1, n_tiles - 1), 0, 0),
                               memory_space=pltpu.SMEM)
    ptile = lambda: pl.BlockSpec((MOE_TILE, d), lambda i: (jnp.minimum(i, n_ptiles - 1), 0))
    stile = lambda: pl.BlockSpec((MOE_TILE, d), lambda i: (0, 0))
    return pl.pallas_call(
        functools.partial(_moe_combine_body, n_ptiles=n_ptiles),
        grid=(n_tiles,),
        in_specs=[cur(), nxt(), cur(), nxt(), pl.BlockSpec(memory_space=pl.ANY), ptile(), stile(),
                  pl.BlockSpec((MOE_TILE, LANES), lambda i: (i, 0)), pl.BlockSpec((1, d), lambda i: (0, 0))],
        out_specs=[ptile(), stile()],
        out_shape=[jax.ShapeDtypeStruct((mp, d), F32), jax.ShapeDtypeStruct((MOE_TILE, d), F32)],
        scratch_shapes=[pltpu.VMEM((2, MOE_TILE, d), F32), pltpu.VMEM((2, MOE_TILE, d), F32),
                        pltpu.SemaphoreType.DMA((2,)), pltpu.SemaphoreType.DMA((2,))],
        compiler_params=_params(("arbitrary",), 40),
        name="moe_combine",
    )(dest1, dest1, dest2, dest2, ys, x_prompt, x_sample, info, g_final.reshape(1, d))


def _mixer_inputs(x2, g_mix, w_in_t, w_bg_t, w_mg_t, tm, q_dtype):
    tn = PROJ_TILE_N
    xn = rmsnorm(x2, g_mix, tm, BF16)
    mm = functools.partial(matmul_cols, xn, tm=tm, w_is_t=True)
    u_rnn = mm(w_in_t, 0, D_RNN, tn=tn, out_dtype=F32, name="in_rnn")
    u_gate = mm(w_in_t, D_RNN, D_RNN, tn=tn, out_dtype=F32, name="in_gate")
    q = mm(w_in_t, 2 * D_RNN, N_HEADS * HEAD_DIM, tn=tn, out_dtype=q_dtype, name="in_q")
    kv0 = 2 * D_RNN + N_HEADS * HEAD_DIM
    kv_cmp, = matmul_kv(xn, w_in_t, kv0, tm, False, "in_kv_cmp")
    kv_sel, sel_planes = matmul_kv(xn, w_in_t, kv0 + 2 * KV_DIM, tm, True, "in_kv_sel")
    kv_win, win_planes = matmul_kv(xn, w_in_t, kv0 + 4 * KV_DIM, tm, True, "in_kv_win")
    u_bg = mm(w_bg_t, 0, w_bg_t.shape[0], tn=w_bg_t.shape[0], out_dtype=F32, name="in_bg")
    u_mg = mm(w_mg_t, 0, 2 * D_MODEL, tn=tn, out_dtype=F32, name="in_mg")
    return u_rnn, u_gate, q, (kv_cmp, kv_sel, kv_win), (sel_planes, win_planes), u_bg, u_mg


def kernel(x_prompt, x_sample, cache_cmp_kv, cache_sel_kv, cache_win_kv, state_conv, state_rglru, page_table, g_mix, w_in, conv_w, conv_b, lru_w_a, lru_b_a, lru_w_x, lru_b_x, lru_lambda, cmp_pe, cmp_w1, cmp_b1, cmp_w2, cmp_b2, w_out, g_ffn, moe_w_group, moe_b_group, moe_w_expert, moe_b_expert, moe_w_gate, moe_w_up, moe_w_down, g_final):
    b, t, d = x_prompt.shape
    ns = x_sample.shape[0]
    npages = page_table.shape[1]
    past_len = npages * PAGE_SIZE
    wbuf = cache_win_kv.shape[1]
    kvshape = (N_KV_HEADS, 2, HEAD_DIM)

    w_in_t = w_in.T
    bg0 = 2 * D_RNN + N_HEADS * HEAD_DIM + 6 * KV_DIM
    n_bg = 3 * N_HEADS
    w_bg = w_in_t[bg0:bg0 + n_bg].reshape(3, N_KV_HEADS, GROUP, d).transpose(1, 0, 2, 3).reshape(N_KV_HEADS, 3 * GROUP, d)
    w_bg = jnp.pad(w_bg, ((0, 0), (0, LANES - 3 * GROUP), (0, 0))).reshape(N_KV_HEADS * LANES, d)
    w_mg = w_in_t[bg0 + n_bg:]
    wa = lru_w_a.astype(BF16)
    wx = lru_w_x.astype(BF16)
    half_rows = CMP_STRIDE * HEAD_DIM
    wc = jnp.concatenate([cmp_w1[:, :half_rows], cmp_w1[:, half_rows:]], axis=-1).astype(BF16)
    pe8 = jnp.pad(cmp_pe.reshape(2, 2, half_rows), ((0, 0), (0, 6), (0, 0)))
    b1 = cmp_b1.reshape(2, 1, CMP_HIDDEN)
    w2 = cmp_w2.astype(BF16)
    b2 = cmp_b2.reshape(2, 1, HEAD_DIM)
    n_route = N_GROUPS + N_EXPERTS
    wr = jnp.pad(jnp.concatenate([moe_w_group, moe_w_expert], axis=1), ((0, 0), (0, LANES - n_route))).astype(BF16)
    br = jnp.pad(jnp.concatenate([moe_b_group, moe_b_expert]), (0, LANES - n_route)).reshape(1, LANES)
    lru_args = (conv_w, conv_b, wa, wx, lru_b_a, lru_b_x, lru_lambda)
    cmp_args = (wc, pe8, b1, w2, b2)

    xp2 = x_prompt.reshape(b * t, d)
    u_rnn, u_gate, q, kv2d, planes, u_bg, u_mg = _mixer_inputs(xp2, g_mix, w_in_t, w_bg, w_mg, PROJ_TILE_M, BF16)
    o_rnn, p_conv, p_h = lru_prompt(u_rnn.reshape(b, t, d), u_gate.reshape(b, t, d), *lru_args, LRU_TILE_T)
    p_cmp, p_sel, p_win_full = (a.reshape((b, t) + kvshape) for a in kv2d)
    p_win = p_win_full[:, t - min(WINDOW, t):]
    pt_prompt = jnp.arange(b * t // PAGE_SIZE, dtype=I32).reshape(b, t // PAGE_SIZE)
    kvc_p = compress(kv2d[0], pt_prompt, jnp.zeros((b * CMP_STRIDE * HK, HEAD_DIM), F32), *cmp_args)
    merged = attn_prompt_s(q.reshape(b, t, -1), planes[0], planes[1], kvc_p, u_bg.reshape(b, t, -1),
                           u_mg.reshape(b, t, -1), o_rnn, t // CMP_STRIDE - 1)
    x_mid = matmul_cols(merged.reshape(b * t, d), w_out, 0, d, PROJ_TILE_M, PROJ_TILE_N, F32, residual=xp2,
                        name="out_proj")

    xs2 = x_sample.reshape(ns, d)
    su_rnn, su_gate, sq, skv2d, _, su_bg, su_mg = _mixer_inputs(xs2, g_mix, w_in_t, w_bg, w_mg, ns, F32)
    so_rnn, cn, s_h = lru_sample(su_rnn, su_gate, state_conv.transpose(1, 0, 2), state_rglru, *lru_args)
    s_conv = cn.transpose(1, 0, 2)
    kvs = jnp.stack([a.reshape(ns, HK, HEAD_DIM) for a in skv2d], axis=1)
    s_cmp = kvs[:, 0].reshape((ns, 1) + kvshape)
    s_sel = kvs[:, 1].reshape((ns, 1) + kvshape)

    extra = jnp.concatenate([kvs[:, 0], jnp.zeros((ns, (CMP_STRIDE - 1) * HK, HEAD_DIM), F32)], axis=1)
    kvc_s = compress(cache_cmp_kv.reshape(-1, HEAD_DIM), page_table, extra.reshape(-1, HEAD_DIM), *cmp_args)
    n_cmp_s = -(-(past_len + 1) // CMP_STRIDE) - 1
    n_sel_s = -(-(past_len + 1) // SEL_BLK)
    nsp = -(-n_sel_s // LANES) * LANES
    q16 = sq.reshape(ns, N_HEADS, HEAD_DIM)
    m_s = jnp.asarray(_overlap_matrix(n_cmp_s, kvc_s.shape[2], n_sel_s, nsp), BF16)
    so_cmp, imp = cmp_sample(q16, kvc_s, m_s, past_len, n_cmp_s)
    idx = rank_sample(imp.reshape(ns * N_KV_HEADS, nsp), past_len // SEL_BLK, n_sel_s)

    def per_head_rows(new_kv, kv):
        return jnp.repeat(new_kv[:, kv::2], GROUP, axis=1)

    so_sel = sel_sample(idx, page_table, q16, per_head_rows(kvs[:, 1], 0), per_head_rows(kvs[:, 1], 1),
                        cache_sel_kv.reshape(-1, HK, HEAD_DIM), past_len // SEL_BLK)
    bgs = su_bg.reshape(ns, N_KV_HEADS, LANES)[:, :, :3 * GROUP].reshape(ns, N_KV_HEADS, 3, GROUP)
    bgs = jnp.pad(bgs.transpose(0, 1, 3, 2).reshape(ns, N_HEADS, 3), ((0, 0), (0, 0), (0, LANES - 3)))
    so_nsa, s_win2d = win_sample(q16, cache_win_kv.reshape(-1, HEAD_DIM), skv2d[2], per_head_rows(kvs[:, 2], 0),
                                 per_head_rows(kvs[:, 2], 1), so_cmp, so_sel, bgs, wbuf)
    s_win = s_win2d.reshape((ns, wbuf) + kvshape)
    s_merged = merge(su_mg, so_rnn, so_nsa.reshape(ns, d), ns)
    sx_mid = matmul_cols(s_merged, w_out, 0, d, ns, PROJ_TILE_N, F32, residual=xs2, name="out_proj_s")

    n_real = b * t + ns
    assert ns <= MOE_TILE
    sx_pad = jnp.pad(sx_mid, ((0, MOE_TILE - ns), (0, 0)))
    xn_all, info, counts = route(x_mid, sx_pad, g_ffn, wr, br, n_real)
    n_tiles = -(-(2 * n_real + N_EXPERTS * (MOE_TILE - 1)) // MOE_TILE)
    n_tiles = -(-n_tiles // DISPATCH_TILES) * DISPATCH_TILES
    src3, dest1, dest2, tile_plan = _moe_plan(info, counts, n_real, n_tiles)
    xs = moe_dispatch(tile_plan[-1], src3, xn_all)
    ys = moe_experts(tile_plan, xs, moe_w_gate, moe_w_up, moe_w_down)
    y_p2, y_s2 = moe_combine(dest1, dest2, ys, x_mid, sx_pad, info, g_final)
    y_prompt = y_p2.reshape(b, t, d)
    y_sample = y_s2[:ns].reshape(ns, 1, d)

    return (y_prompt, y_sample, p_cmp, s_cmp, p_sel, s_sel, p_win, s_win, p_conv, s_conv,
            p_h.reshape(b, d), s_h)
```
